```python
import math
import jax
import jax.numpy as jnp
from jax import lax
import numpy as np

D_MODEL = 1024
BATCH = 4
SEQ = 4096
DEPTH = 2

GRID_W = 64
CTX_LEN = 256
NORM_EPS = 1e-6
N_MOD = 6

DN_HEADS = 8
DN_HEAD_DIM = 128
DN_WIDTH = DN_HEADS * DN_HEAD_DIM
DN_CHUNK = 64
DN_CONV = 4

HY_WIDTH = 1024
HY_CONV = 3
HY_EMB = 33
HY_BANDS = (HY_EMB - 1) // 2
HY_FILTER_HIDDEN = 64
HY_FAST_DECAY_PCT = 0.3
HY_SLOW_DECAY_PCT = 1.5
HY_DECAY_TARGET = 1e-2

LRU_WIDTH = 1024
LRU_BLOCKS = 8
LRU_BLOCK = LRU_WIDTH // LRU_BLOCKS
LRU_CONV = 4
LRU_C = 8.0

FFN_HIDDEN = 2816
FFN_CONV = 3

N_BRANCH = 3
IN_SIZES = (3 * DN_WIDTH, DN_WIDTH, 4 * DN_HEADS, 3 * HY_WIDTH, LRU_WIDTH, LRU_WIDTH, N_BRANCH * D_MODEL)
N_IN = 3 * DN_WIDTH + DN_WIDTH + 4 * DN_HEADS + 3 * HY_WIDTH + 2 * LRU_WIDTH + N_BRANCH * D_MODEL

kernel_name = 'hybrid_deltanet_hyena_rglru_diffusion_block'


def rms_norm(x, gain):
    xf = x.astype(jnp.float32)
    y = xf * lax.rsqrt(jnp.mean(xf * xf, axis=-1, keepdims=True) + NORM_EPS)
    return (y * gain.astype(jnp.float32)).astype(x.dtype)


def modulate(h, shift, scale):
    return h * (1.0 + scale) + shift


def l2_normalize(t):
    return t * lax.rsqrt(jnp.sum(t * t, axis=-1, keepdims=True) + 1e-6)


def dwconv1d(x, w):
    k = w.shape[0]
    n = x.shape[1]
    xp = jnp.pad(x, ((0, 0), ((k - 1) // 2, k // 2), (0, 0)))
    return sum(xp[:, j:j + n] * w[j] for j in range(k))


def dwconv_grid(u, w, rows, cols):
    bsz, n, ch = u.shape
    k = w.shape[0]
    p = k // 2
    img = jnp.pad(u.reshape(bsz, rows, cols, ch), ((0, 0), (p, p), (p, p), (0, 0)))
    out = sum(img[:, i:i + rows, j:j + cols] * w[i, j] for i in range(k) for j in range(k))
    return out.reshape(bsz, n, ch)


def block_diag_linear(x, w, b):
    bsz, n, _ = x.shape
    xb = x.reshape(bsz, n, LRU_BLOCKS, LRU_BLOCK)
    y = jnp.einsum('blgi,gij->blgj', xb, w.astype(x.dtype))
    return y.reshape(bsz, n, LRU_WIDTH) + b.astype(x.dtype)


def linear_scan(a, b, h0):
    def combine(e1, e2):
        a1, b1 = e1
        a2, b2 = e2
        return a1 * a2, a2 * b1 + b2
    a_cum, b_cum = lax.associative_scan(combine, (a, b), axis=1)
    h = b_cum + a_cum * h0[:, None, :]
    return h, h[:, -1]


def gdn_chunked(q, k, v, g, beta, s0):
    bsz, nh, n, dk = q.shape
    dv = v.shape[-1]
    cs = DN_CHUNK
    nc = n // cs

    def chunks(t):
        return t.reshape(bsz, nh, nc, cs, *t.shape[3:])

    q = chunks(q * dk ** -0.5)
    k = chunks(k)
    v = chunks(v)
    beta = chunks(beta)
    gc = jnp.cumsum(chunks(g), axis=-1)
    pos = jnp.arange(cs)
    lower = pos[:, None] >= pos[None, :]
    strict = pos[:, None] > pos[None, :]
    diff = gc[..., :, None] - gc[..., None, :]
    decay = jnp.where(lower, jnp.exp(jnp.where(lower, diff, 0.0)), 0.0)
    kb = k * beta[..., None]
    m = jnp.where(strict, jnp.einsum('bhnid,bhnjd->bhnij', kb, k) * decay, 0.0)
    eye = jnp.eye(cs, dtype=m.dtype)
    t_inv = lax.linalg.triangular_solve(eye + m, jnp.broadcast_to(eye, m.shape), left_side=True, lower=True)
    u = jnp.einsum('bhnij,bhnjd->bhnid', t_inv, v * beta[..., None])
    w = jnp.einsum('bhnij,bhnjd->bhnid', t_inv, kb * jnp.exp(gc)[..., None])
    attn = jnp.einsum('bhnid,bhnjd->bhnij', q, k) * decay
    q_dec = q * jnp.exp(gc)[..., None]
    k_dec = k * jnp.exp(gc[..., -1:] - gc)[..., None]
    chunk_decay = jnp.exp(gc[..., -1])
    xs = tuple(jnp.moveaxis(t, 2, 0) for t in (q_dec, k_dec, u, w, attn, chunk_decay))

    def step(state, inp):
        q_i, k_i, u_i, w_i, a_i, d_i = inp
        v_new = u_i - jnp.einsum('bhcd,bhde->bhce', w_i, state)
        o_i = jnp.einsum('bhcd,bhde->bhce', q_i, state) + jnp.einsum('bhcs,bhse->bhce', a_i, v_new)
        state = state * d_i[..., None, None] + jnp.einsum('bhcd,bhce->bhde', k_i, v_new)
        return state, o_i

    s_final, o = lax.scan(step, s0, xs)
    return jnp.moveaxis(o, 0, 2).reshape(bsz, nh, n, dv), s_final


def delta_net_branch(p_qkv, p_z, p_ab, s0_f, s0_b, conv_w, a_log, dt_bias, norm_g):
    f32 = jnp.float32
    bsz, n, _ = p_qkv.shape
    qkv = jax.nn.silu(dwconv1d(p_qkv, conv_w)).astype(f32)
    qkv = qkv.reshape(bsz, n, 3, DN_HEADS, DN_HEAD_DIM).transpose(2, 0, 3, 1, 4)
    q = l2_normalize(qkv[0])
    k = l2_normalize(qkv[1])
    v = qkv[2]
    ab = p_ab.astype(f32).reshape(bsz, n, 2, 2, DN_HEADS)
    g = -jnp.exp(a_log.astype(f32)) * jax.nn.softplus(ab[:, :, :, 0] + dt_bias.astype(f32))
    beta = jax.nn.sigmoid(ab[:, :, :, 1])
    g = g.transpose(2, 0, 3, 1)
    beta = beta.transpose(2, 0, 3, 1)

    def rev(t):
        return jnp.flip(t, axis=2)

    o_f, s_f = gdn_chunked(q, k, v, g[0], beta[0], s0_f)
    o_b, s_b = gdn_chunked(rev(q), rev(k), rev(v), rev(g[1]), rev(beta[1]), s0_b)
    o = (o_f + rev(o_b)).transpose(0, 2, 1, 3)
    z = p_z.astype(f32).reshape(bsz, n, DN_HEADS, DN_HEAD_DIM)
    o = rms_norm(o, norm_g) * jax.nn.silu(z)
    return o.reshape(bsz, n, DN_WIDTH).astype(p_qkv.dtype), s_f, s_b


def hyena_kernel(n, w1, b1, f1, w2, b2, f2, w3):
    f32 = jnp.float32
    t = jnp.linspace(0.0, 1.0, n, dtype=f32)[:, None]
    omega = 2.0 * math.pi * jnp.arange(n, dtype=f32)[:, None] / n
    bands = jnp.linspace(1e-4, HY_BANDS - 1, HY_BANDS, dtype=f32)[None, :]
    z = jnp.concatenate([t, jnp.cos(bands * omega), -jnp.sin(bands * omega)], axis=-1)
    hid = jnp.sin(f1.astype(f32) * (z @ w1.astype(f32) + b1.astype(f32)))
    hid = jnp.sin(f2.astype(f32) * (hid @ w2.astype(f32) + b2.astype(f32)))
    filt = (hid @ w3.astype(f32)).reshape(n, 2, HY_WIDTH)
    log_target = math.log(HY_DECAY_TARGET)
    deltas = jnp.abs(jnp.linspace(log_target / HY_SLOW_DECAY_PCT, log_target / HY_FAST_DECAY_PCT, HY_WIDTH, dtype=f32))
    filt = filt * jnp.exp(-t * deltas)[:, None, :]
    h_fwd = filt[:, 0]
    h_bwd = filt[:, 1]
    return jnp.concatenate([h_fwd.at[0].add(h_bwd[0]), jnp.zeros_like(h_fwd[:1]), h_bwd[:0:-1]], axis=0)


def long_conv(u, kern):
    n = u.shape[1]
    uf = jnp.fft.rfft(u, n=2 * n, axis=1)
    kf = jnp.fft.rfft(kern, axis=0)
    return jnp.fft.irfft(uf * kf[None], n=2 * n, axis=1)[:, :n]


def hyena_branch(p_hy, conv_w, conv_b, w1, b1, f1, w2, b2, f2, w3, bias):
    f32 = jnp.float32
    n = p_hy.shape[1]
    uc = (dwconv1d(p_hy, conv_w) + conv_b).astype(f32)
    x0, x1, v = jnp.split(uc, 3, axis=-1)
    kern = hyena_kernel(n, w1, b1, f1, w2, b2, f2, w3)
    zz = x1 * v
    y = long_conv(zz, kern) + zz * bias.astype(f32)
    return (x0 * y).astype(p_hy.dtype)


def rglru_direction(x, h0, w_a, b_a, w_x, b_x, lam):
    r = jax.nn.sigmoid(block_diag_linear(x, w_a, b_a))
    i = jax.nn.sigmoid(block_diag_linear(x, w_x, b_x))
    log_a = -LRU_C * r * jax.nn.softplus(-lam.astype(jnp.float32))
    b = jnp.sqrt(-jnp.expm1(2.0 * log_a)) * (i * x)
    return linear_scan(jnp.exp(log_a), b, h0)


def rglru_branch(p_x, p_y, h0_f, h0_b, conv_w, conv_b, w_a, b_a, w_x, b_x, lam):
    f32 = jnp.float32
    xs = (dwconv1d(p_x, conv_w) + conv_b).astype(f32)
    h_f, last_f = rglru_direction(xs, h0_f, w_a[0], b_a[0], w_x[0], b_x[0], lam[0])
    h_b, last_b = rglru_direction(jnp.flip(xs, 1), h0_b, w_a[1], b_a[1], w_x[1], b_x[1], lam[1])
    out = (h_f + jnp.flip(h_b, 1)) * jax.nn.gelu(p_y.astype(f32))
    return out.astype(p_x.dtype), last_f, last_b


def token_mixer(h, s_dn_f, s_dn_b, s_lru_f, s_lru_b, with_output,
                w_in, dn_conv_w, dn_a_log, dn_dt_bias, dn_norm_g,
                hy_conv_w, hy_conv_b, hy_w1, hy_b1, hy_f1, hy_w2, hy_b2, hy_f2, hy_w3, hy_bias,
                lru_conv_w, lru_conv_b, lru_w_a, lru_b_a, lru_w_x, lru_b_x, lru_lambda,
                w_proj_dn, w_proj_hy, w_proj_lru, w_out):
    proj = h @ w_in
    offsets = np.cumsum(IN_SIZES)[:-1].tolist()
    p_qkv, p_z, p_ab, p_hy, p_lx, p_ly, p_gate = jnp.split(proj, offsets, axis=-1)
    o_dn, s_dn_f, s_dn_b = delta_net_branch(p_qkv, p_z, p_ab, s_dn_f, s_dn_b, dn_conv_w, dn_a_log, dn_dt_bias, dn_norm_g)
    o_lru, s_lru_f, s_lru_b = rglru_branch(p_lx, p_ly, s_lru_f, s_lru_b, lru_conv_w, lru_conv_b,
                                          lru_w_a, lru_b_a, lru_w_x, lru_b_x, lru_lambda)
    states = (s_dn_f, s_dn_b, s_lru_f, s_lru_b)
    if not with_output:
        return None, states
    o_hy = hyena_branch(p_hy, hy_conv_w, hy_conv_b, hy_w1, hy_b1, hy_f1, hy_w2, hy_b2, hy_f2, hy_w3, hy_bias)
    g_dn, g_hy, g_lru = jnp.split(jax.nn.sigmoid(p_gate), N_BRANCH, axis=-1)
    merged = g_dn * (o_dn @ w_proj_dn) + g_hy * (o_hy @ w_proj_hy) + g_lru * (o_lru @ w_proj_lru)
    return merged @ w_out, states


def conv_ffn(h, rows, cols, w_up, conv_w, w_down):
    u = dwconv_grid(h @ w_up, conv_w, rows, cols)
    gate, val = jnp.split(u, 2, axis=-1)
    return (jax.nn.silu(gate) * val) @ w_down


def setup_inputs(seed: int = 0) -> dict:
    key = jax.random.key(seed)
    ks = iter(jax.random.split(key, 48))
    f32 = jnp.float32
    dp = DEPTH
    d = D_MODEL

    def nrm(shape, scale):
        return jax.random.normal(next(ks), shape, f32) * scale

    def gain(shape):
        return 1.0 + nrm(shape, 0.02)

    a_log = jnp.log(jax.random.uniform(next(ks), (dp, 2, DN_HEADS), f32, 1.0, 16.0))
    dt = jnp.exp(jax.random.uniform(next(ks), (dp, 2, DN_HEADS), f32, math.log(1e-3), math.log(1e-1)))
    dt_bias = dt + jnp.log(-jnp.expm1(-dt))
    a0 = jax.random.uniform(next(ks), (dp, 2, LRU_WIDTH), f32, 0.9, 0.999) ** (1.0 / LRU_C)
    lru_lambda = jnp.log(a0) - jnp.log1p(-a0)
    return {
        'x': nrm((BATCH, SEQ, d), 1.0),
        'c': nrm((BATCH, d), 1.0),
        'ctx': nrm((BATCH, CTX_LEN, d), 1.0),
        'c_ctx': nrm((d,), 1.0),
        'w_mod': nrm((dp, d, N_MOD * d), 0.5 * d ** -0.5),
        'b_mod': nrm((dp, N_MOD * d), 0.02),
        'norm1_g': gain((dp, d)),
        'norm2_g': gain((dp, d)),
        'w_in': nrm((dp, d, N_IN), d ** -0.5),
        'dn_conv_w': nrm((dp, DN_CONV, 3 * DN_WIDTH), DN_CONV ** -0.5),
        'dn_a_log': a_log,
        'dn_dt_bias': dt_bias,
        'dn_norm_g': gain((dp, DN_HEAD_DIM)),
        'hy_conv_w': nrm((dp, HY_CONV, 3 * HY_WIDTH), HY_CONV ** -0.5),
        'hy_conv_b': nrm((dp, 3 * HY_WIDTH), 0.02),
        'hy_w1': nrm((dp, HY_EMB, HY_FILTER_HIDDEN), HY_EMB ** -0.5),
        'hy_b1': nrm((dp, HY_FILTER_HIDDEN), 0.02),
        'hy_f1': gain((dp, HY_FILTER_HIDDEN)),
        'hy_w2': nrm((dp, HY_FILTER_HIDDEN, HY_FILTER_HIDDEN), HY_FILTER_HIDDEN ** -0.5),
        'hy_b2': nrm((dp, HY_FILTER_HIDDEN), 0.02),
        'hy_f2': gain((dp, HY_FILTER_HIDDEN)),
        'hy_w3': nrm((dp, HY_FILTER_HIDDEN, 2 * HY_WIDTH), 0.05 * HY_FILTER_HIDDEN ** -0.5),
        'hy_bias': nrm((dp, HY_WIDTH), 0.1),
        'lru_conv_w': nrm((dp, LRU_CONV, LRU_WIDTH), LRU_CONV ** -0.5),
        'lru_conv_b': nrm((dp, LRU_WIDTH), 0.02),
        'lru_w_a': nrm((dp, 2, LRU_BLOCKS, LRU_BLOCK, LRU_BLOCK), LRU_BLOCK ** -0.5),
        'lru_b_a': nrm((dp, 2, LRU_WIDTH), 0.02),
        'lru_w_x': nrm((dp, 2, LRU_BLOCKS, LRU_BLOCK, LRU_BLOCK), LRU_BLOCK ** -0.5),
        'lru_b_x': nrm((dp, 2, LRU_WIDTH), 0.02),
        'lru_lambda': lru_lambda,
        'w_proj_dn': nrm((dp, DN_WIDTH, d), DN_WIDTH ** -0.5),
        'w_proj_hy': nrm((dp, HY_WIDTH, d), HY_WIDTH ** -0.5),
        'w_proj_lru': nrm((dp, LRU_WIDTH, d), LRU_WIDTH ** -0.5),
        'w_out': nrm((dp, d, d), d ** -0.5),
        'ffn_up': nrm((dp, d, 2 * FFN_HIDDEN), d ** -0.5),
        'ffn_conv_w': nrm((dp, FFN_CONV, FFN_CONV, 2 * FFN_HIDDEN), 1.0 / FFN_CONV),
        'ffn_down': nrm((dp, FFN_HIDDEN, d), FFN_HIDDEN ** -0.5),
        'final_norm_g': gain((d,)),
    }


def reference(x, c, ctx, c_ctx, w_mod, b_mod, norm1_g, norm2_g, w_in, dn_conv_w, dn_a_log, dn_dt_bias,
              dn_norm_g, hy_conv_w, hy_conv_b, hy_w1, hy_b1, hy_f1, hy_w2, hy_b2, hy_f2, hy_w3, hy_bias,
              lru_conv_w, lru_conv_b, lru_w_a, lru_b_a, lru_w_x, lru_b_x, lru_lambda,
              w_proj_dn, w_proj_hy, w_proj_lru, w_out, ffn_up, ffn_conv_w, ffn_down, final_norm_g):
    n_lat = x.shape[1]
    rows = n_lat // GRID_W
    bsz, n_ctx = ctx.shape[0], ctx.shape[1]
    silu_c = jax.nn.silu(c)[:, None, :]
    silu_cc = jax.nn.silu(c_ctx)[None, None, :]
    zero_dn = jnp.zeros((bsz, DN_HEADS, DN_HEAD_DIM, DN_HEAD_DIM), jnp.float32)
    zero_lru = jnp.zeros((bsz, LRU_WIDTH), jnp.float32)
    xc = ctx
    for l in range(DEPTH):
        ctx_needed = l < DEPTH - 1
        mixer_params = (w_in[l], dn_conv_w[l], dn_a_log[l], dn_dt_bias[l], dn_norm_g[l],
                        hy_conv_w[l], hy_conv_b[l], hy_w1[l], hy_b1[l], hy_f1[l], hy_w2[l], hy_b2[l],
                        hy_f2[l], hy_w3[l], hy_bias[l],
                        lru_conv_w[l], lru_conv_b[l], lru_w_a[l], lru_b_a[l], lru_w_x[l], lru_b_x[l],
                        lru_lambda[l], w_proj_dn[l], w_proj_hy[l], w_proj_lru[l], w_out[l])
        lat_mod = jnp.split(silu_c @ w_mod[l] + b_mod[l], N_MOD, axis=-1)
        ctx_mod = jnp.split(silu_cc @ w_mod[l] + b_mod[l], N_MOD, axis=-1)
        hc = modulate(rms_norm(xc, norm1_g[l]), ctx_mod[0], ctx_mod[1])
        yc, ctx_states = token_mixer(hc, zero_dn, zero_dn, zero_lru, zero_lru, ctx_needed, *mixer_params)
        h = modulate(rms_norm(x, norm1_g[l]), lat_mod[0], lat_mod[1])
        y, _ = token_mixer(h, *ctx_states, True, *mixer_params)
        x = x + lat_mod[2] * y
        h = modulate(rms_norm(x, norm2_g[l]), lat_mod[3], lat_mod[4])
        x = x + lat_mod[5] * conv_ffn(h, rows, GRID_W, ffn_up[l], ffn_conv_w[l], ffn_down[l])
        if ctx_needed:
            xc = xc + ctx_mod[2] * yc
            hc = modulate(rms_norm(xc, norm2_g[l]), ctx_mod[3], ctx_mod[4])
            xc = xc + ctx_mod[5] * conv_ffn(hc, 1, n_ctx, ffn_up[l], ffn_conv_w[l], ffn_down[l])
    return rms_norm(x, final_norm_g)
```

```python
import functools
import math

import numpy as np
import jax
import jax.numpy as jnp
from jax import lax
from jax.experimental import pallas as pl
from jax.experimental.pallas import tpu as pltpu

F32 = jnp.float32
BF16 = jnp.bfloat16
HIGHEST = lax.Precision.HIGHEST

D_MODEL = 1024
DEPTH = 2
GRID_W = 64
NORM_EPS = 1e-6
N_MOD = 6

DN_HEADS = 8
DN_HEAD_DIM = 128
DN_WIDTH = DN_HEADS * DN_HEAD_DIM
HY_WIDTH = 1024
HY_EMB = 33
HY_BANDS = (HY_EMB - 1) // 2
HY_FILTER_HIDDEN = 64
HY_FAST_DECAY_PCT = 0.3
HY_SLOW_DECAY_PCT = 1.5
HY_DECAY_TARGET = 1e-2
LRU_WIDTH = 1024
LRU_BLOCKS = 8
LRU_BLOCK = LRU_WIDTH // LRU_BLOCKS
LRU_C = 8.0
FFN_HIDDEN = 2816

LANE = 128
SUBLANE = 8
TILE = 256
CHUNK = 128
LRU_ROWS = 256
MIB = 1024 * 1024

QKV_BLK = 0
Z_BLK = 24
HY_BLK = 32
LX_BLK = 56
LY_BLK = 64
GATE_BLK = 72
N_MAIN = 96 * LANE


def _cparams(sem, vmem_mib=48):
    return pltpu.CompilerParams(dimension_semantics=sem, vmem_limit_bytes=vmem_mib * MIB)


def _sigmoid(x):
    return 1.0 / (1.0 + jnp.exp(-x))


def _silu(x):
    return x * _sigmoid(x)


def _softplus(x):
    return jnp.maximum(x, 0.0) + jnp.log(1.0 + jnp.exp(-jnp.abs(x)))


def _row_iota(shape):
    return lax.broadcasted_iota(jnp.int32, shape, 0)


def _col_iota(shape):
    return lax.broadcasted_iota(jnp.int32, shape, 1)


def _div_pow2(x, k):
    assert k & (k - 1) == 0
    return x >> (k.bit_length() - 1)


def _mod_pow2(x, k):
    assert k & (k - 1) == 0
    return x & (k - 1)


def _bdot(a, b):
    return jnp.dot(a.astype(BF16), b.astype(BF16), preferred_element_type=F32)


def _mod_kernel(c_ref, w_ref, b_ref, o_ref):
    o_ref[...] = _bdot(_silu(c_ref[...]), w_ref[...]) + b_ref[...]


def _modulation(cvec, w_mod, b_mod):
    n = w_mod.shape[1]
    tn = 1024
    return pl.pallas_call(
        _mod_kernel,
        grid=(n // tn,),
        in_specs=[pl.BlockSpec((SUBLANE, D_MODEL), lambda j: (0, 0)),
                  pl.BlockSpec((D_MODEL, tn), lambda j: (0, j)),
                  pl.BlockSpec((1, tn), lambda j: (0, j))],
        out_specs=pl.BlockSpec((SUBLANE, tn), lambda j: (0, j)),
        out_shape=jax.ShapeDtypeStruct((SUBLANE, n), F32),
        compiler_params=_cparams(("parallel",)),
        name="modulation",
    )(cvec, w_mod, b_mod.reshape(1, n))


def _nmm_kernel(x_ref, g_ref, sh_ref, sc_ref, w_ref, o_ref, h_ref):
    @pl.when(pl.program_id(2) == 0)
    def _():
        x = x_ref[0]
        y = x * lax.rsqrt(jnp.mean(x * x, axis=-1, keepdims=True) + NORM_EPS) * g_ref[...]
        h_ref[...] = (y * (1.0 + sc_ref[0]) + sh_ref[0]).astype(BF16)

    o_ref[0] = jnp.dot(h_ref[...], w_ref[...], preferred_element_type=F32).astype(o_ref.dtype)


def _norm_mod_matmul(x, gain, shift, scale, w, out_dtype, tn):
    bsz, n, d = x.shape
    nout = w.shape[1]
    tm = min(n, 512)
    return pl.pallas_call(
        _nmm_kernel,
        grid=(bsz, n // tm, nout // tn),
        in_specs=[pl.BlockSpec((1, tm, d), lambda b, i, j: (b, i, 0)),
                  pl.BlockSpec((1, d), lambda b, i, j: (0, 0)),
                  pl.BlockSpec((1, 1, d), lambda b, i, j: (b, 0, 0)),
                  pl.BlockSpec((1, 1, d), lambda b, i, j: (b, 0, 0)),
                  pl.BlockSpec((d, tn), lambda b, i, j: (0, j))],
        out_specs=pl.BlockSpec((1, tm, tn), lambda b, i, j: (b, i, j)),
        out_shape=jax.ShapeDtypeStruct((bsz, n, nout), out_dtype),
        scratch_shapes=[pltpu.VMEM((tm, d), BF16)],
        compiler_params=_cparams(("parallel", "parallel", "arbitrary")),
        name="norm_mod_matmul",
    )(x, gain.reshape(1, d), shift, scale, w)


def _mm_kernel(a_ref, b_ref, o_ref):
    o_ref[...] = jnp.dot(a_ref[...], b_ref[...], preferred_element_type=F32).astype(o_ref.dtype)


def _matmul(a, b, out_dtype, tm, tn):
    m, k = a.shape
    n = b.shape[1]
    return pl.pallas_call(
        _mm_kernel,
        grid=(m // tm, n // tn),
        in_specs=[pl.BlockSpec((tm, k), lambda i, j: (i, 0)),
                  pl.BlockSpec((k, tn), lambda i, j: (0, j))],
        out_specs=pl.BlockSpec((tm, tn), lambda i, j: (i, j)),
        out_shape=jax.ShapeDtypeStruct((m, n), out_dtype),
        compiler_params=_cparams(("parallel", "parallel")),
        name="matmul",
    )(a, b)


def _mm_res_kernel(a_ref, b_ref, x_ref, g_ref, o_ref):
    y = jnp.dot(a_ref[0], b_ref[...], preferred_element_type=F32)
    o_ref[0] = x_ref[0] + g_ref[0] * y


def _matmul_residual(a, w, x, gate):
    bsz, n, k = a.shape
    d = w.shape[1]
    tm = min(n, 512)
    return pl.pallas_call(
        _mm_res_kernel,
        grid=(bsz, n // tm),
        in_specs=[pl.BlockSpec((1, tm, k), lambda b, i: (b, i, 0)),
                  pl.BlockSpec((k, d), lambda b, i: (0, 0)),
                  pl.BlockSpec((1, tm, d), lambda b, i: (b, i, 0)),
                  pl.BlockSpec((1, 1, d), lambda b, i: (b, 0, 0))],
        out_specs=pl.BlockSpec((1, tm, d), lambda b, i: (b, i, 0)),
        out_shape=jax.ShapeDtypeStruct((bsz, n, d), F32),
        compiler_params=_cparams(("parallel", "parallel")),
        name="matmul_residual",
    )(a, w, x, gate)


def _shift_rows(x, off):
    n = x.shape[0]
    if off == 0:
        return x
    rolled = pltpu.roll(x, (-off) % n, 0)
    t = _row_iota(x.shape)
    valid = (t + off >= 0) & (t + off < n)
    return jnp.where(valid, rolled, 0.0)


def _dwconv_rows(x, w_ref, k):
    left = (k - 1) // 2
    acc = None
    for j in range(k):
        term = _shift_rows(x, j - left) * w_ref[j:j + 1, :]
        acc = term if acc is None else acc + term
    return acc


def _dn_prep_kernel(p_ref, w_ref, rm_ref, tr_ref, *, n_tiles):
    c = pl.program_id(1)
    x = p_ref[0].astype(F32)
    y = _silu(_dwconv_rows(x, w_ref, 4))
    nrm = y * lax.rsqrt(jnp.sum(y * y, axis=-1, keepdims=True) + 1e-6)
    nrm = nrm * jnp.where(c < DN_HEADS, DN_HEAD_DIM ** -0.5, 1.0)
    y = jnp.where(c < 2 * DN_HEADS, nrm, y)
    rm_ref[0] = y.astype(BF16)
    for t in range(n_tiles):
        tr_ref[0, 0, t] = y[t * TILE:(t + 1) * TILE, :].T.astype(BF16)


def _dn_prep(proj, conv_w):
    bsz, n, _ = proj.shape
    nt = n // TILE
    nc = 3 * DN_HEADS
    return pl.pallas_call(
        functools.partial(_dn_prep_kernel, n_tiles=nt),
        grid=(bsz, nc),
        in_specs=[pl.BlockSpec((1, n, LANE), lambda b, c: (b, 0, QKV_BLK + c)),
                  pl.BlockSpec((4, LANE), lambda b, c: (0, c))],
        out_specs=[pl.BlockSpec((1, n, LANE), lambda b, c: (b, 0, c)),
                   pl.BlockSpec((1, 1, nt, LANE, TILE), lambda b, c: (b, c, 0, 0, 0))],
        out_shape=[jax.ShapeDtypeStruct((bsz, n, nc * LANE), BF16),
                   jax.ShapeDtypeStruct((bsz, nc, nt, LANE, TILE), BF16)],
        compiler_params=_cparams(("parallel", "parallel")),
        name="dn_prep",
    )(proj, conv_w)


def _dn_gate_kernel(ab_ref, alog_ref, dtb_ref, isdec_ref, o_ref):
    x = ab_ref[0]
    dec = -jnp.exp(alog_ref[...]) * _softplus(x + dtb_ref[...])
    e = jnp.where(isdec_ref[...] > 0.5, dec, _sigmoid(x))
    et = e.T
    s = _row_iota((TILE, TILE))
    t = _col_iota((TILE, TILE))
    same = _div_pow2(s, CHUNK) == _div_pow2(t, CHUNK)
    prefix = jnp.where(same & (s <= t), 1.0, 0.0)
    suffix = jnp.where(same & (s >= t), 1.0, 0.0)
    total = jnp.where(same, 1.0, 0.0)
    pre = jnp.dot(et, prefix, precision=HIGHEST, preferred_element_type=F32)
    suf = jnp.dot(et, suffix, precision=HIGHEST, preferred_element_type=F32)
    tot = jnp.dot(et, total, precision=HIGHEST, preferred_element_type=F32)
    slot = _mod_pow2(_row_iota((LANE, TILE)), SUBLANE)
    o_ref[0, 0] = jnp.where(slot == 0, pre, jnp.where(slot == 2, suf, jnp.where(slot >= 4, tot, et)))


def _dn_gates(ab, alog_c, dtb_c, isdec_c):
    bsz, n, _ = ab.shape
    nt = n // TILE
    vec = pl.BlockSpec((1, LANE), lambda b, i: (0, 0))
    return pl.pallas_call(
        _dn_gate_kernel,
        grid=(bsz, nt),
        in_specs=[pl.BlockSpec((1, TILE, LANE), lambda b, i: (b, i, 0)), vec, vec, vec],
        out_specs=pl.BlockSpec((1, 1, LANE, TILE), lambda b, i: (b, i, 0, 0)),
        out_shape=jax.ShapeDtypeStruct((bsz, nt, LANE, TILE), F32),
        compiler_params=_cparams(("parallel", "parallel")),
        name="dn_gates",
    )(ab, alog_c, dtb_c, isdec_c)


def _dn_tile(k_rm, qt, kt, vt, gc, beta, tot, st_ref, backward):
    a = _row_iota((TILE, TILE))
    b = _col_iota((TILE, TILE))
    same = _div_pow2(a, CHUNK) == _div_pow2(b, CHUNK)
    incl = same & ((a >= b) if backward else (a <= b))
    gcb = jnp.broadcast_to(gc, (TILE, TILE))
    diff = gcb - gcb.T
    decay = jnp.where(incl, jnp.exp(jnp.where(incl, diff, 0.0)), 0.0)
    kk = jnp.dot(k_rm, kt, preferred_element_type=F32)
    kq = jnp.dot(k_rm, qt, preferred_element_type=F32)
    attn_t = (kq * decay).astype(BF16)
    x = jnp.where(a == b, 0.0, -(kk * decay * beta))
    p = jnp.where(a == b, 1.0, jnp.where((a >> 1) == (b >> 1), x, 0.0))
    s = 2
    while s < CHUNK:
        lg = s.bit_length() - 1
        couple = ((a >> (lg + 1)) == (b >> (lg + 1))) & ((a >> lg) != (b >> lg))
        pb = p.astype(BF16)
        px = jnp.dot(pb, jnp.where(couple, x, 0.0).astype(BF16), preferred_element_type=F32)
        p = p + jnp.dot(px.astype(BF16), pb, preferred_element_type=F32)
        s *= 2
    t_inv = p.astype(BF16)
    egc = jnp.exp(gc)
    u_t = jnp.dot((vt.astype(F32) * beta).astype(BF16), t_inv, preferred_element_type=F32)
    w_t = jnp.dot((kt.astype(F32) * (beta * egc)).astype(BF16), t_inv, preferred_element_type=F32)
    qd_t = (qt.astype(F32) * egc).astype(BF16)
    kdec = jnp.exp(tot - gc)
    outs = [None] * (TILE // CHUNK)
    order = range(TILE // CHUNK - 1, -1, -1) if backward else range(TILE // CHUNK)
    for ci in order:
        lo, hi = ci * CHUNK, (ci + 1) * CHUNK
        st = st_ref[...]
        stb = st.astype(BF16)
        vn = u_t[:, lo:hi] - jnp.dot(stb, w_t[:, lo:hi].astype(BF16), preferred_element_type=F32)
        o_c = (jnp.dot(stb, qd_t[:, lo:hi], preferred_element_type=F32)
               + jnp.dot(vn.astype(BF16), attn_t[lo:hi, lo:hi], preferred_element_type=F32))
        upd = jnp.dot((vn * kdec[:, lo:hi]).astype(BF16), k_rm[lo:hi, :], preferred_element_type=F32)
        st_ref[...] = st * jnp.exp(tot[:, lo:hi]) + upd
        outs[ci] = o_c
    return jnp.concatenate(outs, axis=1)


def _dn_kernel(k_ref, qt_ref, kt_ref, vt_ref, g_ref, z_ref, ng_ref, s0f_ref, s0b_ref,
               o_ref, sf_ref, sb_ref, ot_ref, stf_ref, stb_ref, *, n_tiles, with_output):
    stf_ref[...] = s0f_ref[0, 0]
    stb_ref[...] = s0b_ref[0, 0]

    def body(i, carry):
        nf = i
        nb = n_tiles - 1 - i
        for backward, n in ((False, nf), (True, nb)):
            g = g_ref[0, n]
            base = 2 if backward else 0
            o_t = _dn_tile(k_ref[0, pl.ds(pl.multiple_of(n * TILE, TILE), TILE), :],
                           qt_ref[0, 0, n], kt_ref[0, 0, n], vt_ref[0, 0, n],
                           g[base:base + 1, :], g[base + 1:base + 2, :], g[4 + base // 2:5 + base // 2, :],
                           stb_ref if backward else stf_ref, backward)
            if with_output:
                ot_ref[n] = ot_ref[n] + o_t
        return carry

    if with_output:
        ot_ref[...] = jnp.zeros_like(ot_ref)
    lax.fori_loop(0, n_tiles, body, 0)
    sf_ref[0, 0] = stf_ref[...]
    sb_ref[0, 0] = stb_ref[...]
    if with_output:
        for t in range(n_tiles):
            o = ot_ref[t].T
            y = o * lax.rsqrt(jnp.mean(o * o, axis=-1, keepdims=True) + NORM_EPS) * ng_ref[...]
            z = z_ref[0, t * TILE:(t + 1) * TILE, :].astype(F32)
            o_ref[0, t * TILE:(t + 1) * TILE, :] = (y * _silu(z)).astype(BF16)
    else:
        o_ref[...] = jnp.zeros_like(o_ref)


def _delta_net(qkv_rm, qkv_tr, gates, proj, norm_g, s0f, s0b, with_output):
    bsz, n, _ = qkv_rm.shape
    nt = n // TILE
    h = DN_HEADS
    tr_spec = lambda off: pl.BlockSpec((1, 1, nt, LANE, TILE), lambda b, j: (b, off + j, 0, 0, 0))
    st_spec = pl.BlockSpec((1, 1, LANE, LANE), lambda b, j: (b, j, 0, 0))
    n_out = n if with_output else SUBLANE
    return pl.pallas_call(
        functools.partial(_dn_kernel, n_tiles=nt, with_output=with_output),
        grid=(bsz, h),
        in_specs=[pl.BlockSpec((1, n, LANE), lambda b, j: (b, 0, h + j)),
                  tr_spec(0), tr_spec(h), tr_spec(2 * h),
                  pl.BlockSpec((1, nt, SUBLANE, TILE), lambda b, j: (b, 0, j, 0)),
                  pl.BlockSpec((1, n, LANE), lambda b, j: (b, 0, Z_BLK + j)),
                  pl.BlockSpec((1, LANE), lambda b, j: (0, 0)),
                  st_spec, st_spec],
        out_specs=[pl.BlockSpec((1, n_out, LANE), lambda b, j: (b, 0, j)), st_spec, st_spec],
        out_shape=[jax.ShapeDtypeStruct((bsz, n_out, DN_WIDTH), BF16),
                   jax.ShapeDtypeStruct((bsz, h, LANE, LANE), F32),
                   jax.ShapeDtypeStruct((bsz, h, LANE, LANE), F32)],
        scratch_shapes=[pltpu.VMEM((nt, LANE, TILE), F32),
                        pltpu.VMEM((LANE, LANE), F32),
                        pltpu.VMEM((LANE, LANE), F32)],
        compiler_params=_cparams(("parallel", "parallel")),
        name="delta_net",
    )(qkv_rm, qkv_tr, qkv_tr, qkv_tr, gates, proj, norm_g.reshape(1, LANE), s0f, s0b)


def _lru_scan_block(x, wa, ba, wx, bx, spl, h_in, backward):
    rows = x.shape[0]
    xb = x.astype(BF16)
    r = _sigmoid(jnp.dot(xb, wa, preferred_element_type=F32) + ba)
    gi = _sigmoid(jnp.dot(xb, wx, preferred_element_type=F32) + bx)
    log_a = -LRU_C * r * spl
    a = jnp.exp(log_a)
    b = jnp.sqrt(1.0 - jnp.exp(2.0 * log_a)) * (gi * x)
    sub = _mod_pow2(_row_iota(x.shape), SUBLANE)
    s = 1
    while s < SUBLANE:
        if backward:
            keep = sub < SUBLANE - s
            a_sh = jnp.where(keep, pltpu.roll(a, rows - s, 0), 1.0)
            b_sh = jnp.where(keep, pltpu.roll(b, rows - s, 0), 0.0)
        else:
            keep = sub >= s
            a_sh = jnp.where(keep, pltpu.roll(a, s, 0), 1.0)
            b_sh = jnp.where(keep, pltpu.roll(b, s, 0), 0.0)
        b = a * b_sh + b
        a = a * a_sh
        s *= 2
    groups = rows // SUBLANE
    pieces = [None] * groups
    carry = h_in
    order = range(groups - 1, -1, -1) if backward else range(groups)
    edge = 0 if backward else SUBLANE - 1
    for gidx in order:
        lo = gidx * SUBLANE
        hgrp = b[lo:lo + SUBLANE, :] + a[lo:lo + SUBLANE, :] * carry
        pieces[gidx] = hgrp
        carry = hgrp[edge:edge + 1, :]
    return jnp.concatenate(pieces, axis=0), carry


def _lru_kernel(px_ref, py_ref, cw_ref, cb_ref, wa_ref, ba_ref, wx_ref, bx_ref, lam_ref, h0_ref,
                o_ref, last_ref, xs_ref, hs_ref, *, n_blocks, with_output):
    x = px_ref[0].astype(F32)
    xs_ref[...] = _dwconv_rows(x, cw_ref, 4) + cb_ref[...]
    spl = _softplus(-lam_ref[...])

    def body(i, carry):
        hf, hb = carry
        rf = pl.multiple_of(i * LRU_ROWS, LRU_ROWS)
        rb = pl.multiple_of((n_blocks - 1 - i) * LRU_ROWS, LRU_ROWS)
        h_f, hf = _lru_scan_block(xs_ref[pl.ds(rf, LRU_ROWS), :], wa_ref[0, 0], ba_ref[0:1, :],
                                  wx_ref[0, 0], bx_ref[0:1, :], spl[0:1, :], hf, False)
        h_b, hb = _lru_scan_block(xs_ref[pl.ds(rb, LRU_ROWS), :], wa_ref[1, 0], ba_ref[1:2, :],
                                  wx_ref[1, 0], bx_ref[1:2, :], spl[1:2, :], hb, True)
        if with_output:
            hs_ref[pl.ds(rf, LRU_ROWS), :] = hs_ref[pl.ds(rf, LRU_ROWS), :] + h_f
            hs_ref[pl.ds(rb, LRU_ROWS), :] = hs_ref[pl.ds(rb, LRU_ROWS), :] + h_b
        return hf, hb

    if with_output:
        hs_ref[...] = jnp.zeros_like(hs_ref)
    h0 = h0_ref[0]
    hf, hb = lax.fori_loop(0, n_blocks, body, (h0[0:1, :], h0[1:2, :]))
    last_ref[0] = jnp.concatenate([hf, hb], axis=0)
    if with_output:
        y = py_ref[0].astype(F32)
        gelu = 0.5 * y * (1.0 + jnp.tanh(math.sqrt(2.0 / math.pi) * (y + 0.044715 * (y * y * y))))
        o_ref[0] = (hs_ref[...] * gelu).astype(BF16)
    else:
        o_ref[...] = jnp.zeros_like(o_ref)


def _rglru(proj, conv_w, conv_b, w_a, b_a, w_x, b_x, lam, h0, with_output):
    bsz, n, _ = proj.shape
    rows = min(n, LRU_ROWS)
    nb = n // rows
    n_out = n if with_output else SUBLANE
    vec2 = pl.BlockSpec((2, LANE), lambda b, j: (0, j))
    wspec = pl.BlockSpec((2, 1, LRU_BLOCK, LRU_BLOCK), lambda b, j: (0, j, 0, 0))
    return pl.pallas_call(
        functools.partial(_lru_kernel, n_blocks=nb, with_output=with_output),
        grid=(bsz, LRU_BLOCKS),
        in_specs=[pl.BlockSpec((1, n, LANE), lambda b, j: (b, 0, LX_BLK + j)),
                  pl.BlockSpec((1, n, LANE), lambda b, j: (b, 0, LY_BLK + j)),
                  pl.BlockSpec((4, LANE), lambda b, j: (0, j)),
                  pl.BlockSpec((1, LANE), lambda b, j: (0, j)),
                  wspec, vec2, wspec, vec2, vec2,
                  pl.BlockSpec((1, 2, LANE), lambda b, j: (b, 0, j))],
        out_specs=[pl.BlockSpec((1, n_out, LANE), lambda b, j: (b, 0, j)),
                   pl.BlockSpec((1, 2, LANE), lambda b, j: (b, 0, j))],
        out_shape=[jax.ShapeDtypeStruct((bsz, n_out, LRU_WIDTH), BF16),
                   jax.ShapeDtypeStruct((bsz, 2, LRU_WIDTH), F32)],
        scratch_shapes=[pltpu.VMEM((n, LANE), F32), pltpu.VMEM((n, LANE), F32)],
        compiler_params=_cparams(("parallel", "parallel")),
        name="rglru",
    )(proj, proj, conv_w, conv_b.reshape(1, LRU_WIDTH), w_a, b_a, w_x, b_x, lam, h0)


def _hy_prep_kernel(p0_ref, p1_ref, pv_ref, w0_ref, w1_ref, wv_ref, b0_ref, b1_ref, bv_ref, x0_ref, zz_ref):
    x0 = _dwconv_rows(p0_ref[0].astype(F32), w0_ref, 3) + b0_ref[...]
    x1 = _dwconv_rows(p1_ref[0].astype(F32), w1_ref, 3) + b1_ref[...]
    v = _dwconv_rows(pv_ref[0].astype(F32), wv_ref, 3) + bv_ref[...]
    x0_ref[0] = x0.astype(BF16)
    zz_ref[0] = (x1 * v).astype(BF16)


def _hy_prep(proj, conv_w, conv_b):
    bsz, n, _ = proj.shape
    nblk = HY_WIDTH // LANE
    pspec = lambda off: pl.BlockSpec((1, n, LANE), lambda b, j: (b, 0, HY_BLK + off + j))
    wspec = lambda off: pl.BlockSpec((3, LANE), lambda b, j: (0, off + j))
    bspec = lambda off: pl.BlockSpec((1, LANE), lambda b, j: (0, off + j))
    ospec = pl.BlockSpec((1, n, LANE), lambda b, j: (b, 0, j))
    cb = conv_b.reshape(1, 3 * HY_WIDTH)
    return pl.pallas_call(
        _hy_prep_kernel,
        grid=(bsz, nblk),
        in_specs=[pspec(0), pspec(nblk), pspec(2 * nblk), wspec(0), wspec(nblk), wspec(2 * nblk),
                  bspec(0), bspec(nblk), bspec(2 * nblk)],
        out_specs=[ospec, ospec],
        out_shape=[jax.ShapeDtypeStruct((bsz, n, HY_WIDTH), BF16)] * 2,
        compiler_params=_cparams(("parallel", "parallel")),
        name="hy_prep",
    )(proj, proj, proj, conv_w, conv_w, conv_w, cb, cb, cb)


def _hy_filter_kernel(w1_ref, b1_ref, f1_ref, w2_ref, b2_ref, f2_ref, w3_ref, band_ref, delta_ref,
                      hs_ref, hd_ref, *, n, rows):
    i = (_row_iota((rows, LANE)) + pl.program_id(0) * rows).astype(F32)
    lane = _col_iota((rows, LANE))
    t = i * (1.0 / (n - 1))
    ang = band_ref[...] * (i * (2.0 * math.pi / n))
    feat = jnp.where(lane == 0, t,
                     jnp.where(lane <= HY_BANDS, jnp.cos(ang), jnp.where(lane < HY_EMB, -jnp.sin(ang), 0.0)))
    hid = jnp.sin(f1_ref[...] * (jnp.dot(feat, w1_ref[...], precision=HIGHEST,
                                         preferred_element_type=F32) + b1_ref[...]))
    hid = jnp.sin(f2_ref[...] * (jnp.dot(hid, w2_ref[...], precision=HIGHEST,
                                         preferred_element_type=F32) + b2_ref[...]))
    filt = jnp.dot(hid, w3_ref[...], precision=HIGHEST, preferred_element_type=F32)
    tw = (_row_iota((rows, HY_WIDTH)) + pl.program_id(0) * rows).astype(F32) * (1.0 / (n - 1))
    dec = jnp.exp(-tw * delta_ref[...])
    h_f = filt[:, :HY_WIDTH] * dec
    h_b = filt[:, HY_WIDTH:] * dec
    hs_ref[...] = (h_f + h_b).astype(BF16)
    hd_ref[...] = (h_f - h_b).astype(BF16)


def _hy_filter(n, w1, b1, f1, w2, b2, f2, w3):
    rows = min(n, 256)
    hid = HY_FILTER_HIDDEN
    w1p = jnp.zeros((LANE, hid), F32).at[:HY_EMB].set(w1)
    bands = np.zeros((1, LANE), np.float32)
    base = np.linspace(1e-4, HY_BANDS - 1, HY_BANDS, dtype=np.float32)
    bands[0, 1:1 + HY_BANDS] = base
    bands[0, 1 + HY_BANDS:HY_EMB] = base
    log_target = math.log(HY_DECAY_TARGET)
    deltas = np.abs(np.linspace(log_target / HY_SLOW_DECAY_PCT, log_target / HY_FAST_DECAY_PCT, HY_WIDTH,
                                dtype=np.float32)).reshape(1, HY_WIDTH)
    full = lambda shape: pl.BlockSpec(shape, lambda i: (0,) * len(shape))
    ospec = pl.BlockSpec((rows, HY_WIDTH), lambda i: (i, 0))
    return pl.pallas_call(
        functools.partial(_hy_filter_kernel, n=n, rows=rows),
        grid=(n // rows,),
        in_specs=[full((LANE, hid)), full((1, hid)), full((1, hid)), full((hid, hid)), full((1, hid)),
                  full((1, hid)), full((hid, 2 * HY_WIDTH)), full((1, LANE)), full((1, HY_WIDTH))],
        out_specs=[ospec, ospec],
        out_shape=[jax.ShapeDtypeStruct((n, HY_WIDTH), BF16)] * 2,
        compiler_params=_cparams(("parallel",)),
        name="hy_filter",
    )(w1p, b1.reshape(1, hid), f1.reshape(1, hid), w2, b2.reshape(1, hid), f2.reshape(1, hid), w3,
      jnp.asarray(bands), jnp.asarray(deltas))


DFT_GROUP = 64


def _dft_table_kernel(c_ref, s_ref, cb_ref, sb_ref, *, n, rows, transposed):
    period = 4 * n
    scale = 2.0 * math.pi / period
    col = _col_iota((1, n))

    @pl.when(pl.program_id(0) == 0)
    def _():
        r2 = _row_iota((DFT_GROUP, n))
        c2 = _col_iota((DFT_GROUP, n))
        ph = (r2 * (2 * c2 + 1)) if transposed else ((2 * r2 + 1) * c2)
        ang = (ph & (period - 1)).astype(F32) * scale
        cb_ref[...] = jnp.cos(ang)
        sb_ref[...] = jnp.sin(ang)

    for g in range(rows // DFT_GROUP):
        r1 = pl.program_id(0) * (rows // DFT_GROUP) + g
        ph = (DFT_GROUP * r1) * (2 * col + 1) if transposed else (2 * DFT_GROUP * r1) * col
        ang = (ph & (period - 1)).astype(F32) * scale
        ca = jnp.cos(ang)
        sa = jnp.sin(ang)
        cb = cb_ref[...]
        sb = sb_ref[...]
        c_ref[g * DFT_GROUP:(g + 1) * DFT_GROUP, :] = (ca * cb - sa * sb).astype(BF16)
        s_ref[g * DFT_GROUP:(g + 1) * DFT_GROUP, :] = (sa * cb + ca * sb).astype(BF16)


def _dft_tables(n, transposed):
    rows = min(n, 256)
    spec = pl.BlockSpec((rows, n), lambda i: (i, 0))
    return pl.pallas_call(
        functools.partial(_dft_table_kernel, n=n, rows=rows, transposed=transposed),
        grid=(n // rows,),
        in_specs=[],
        out_specs=[spec, spec],
        out_shape=[jax.ShapeDtypeStruct((n, n), BF16)] * 2,
        scratch_shapes=[pltpu.VMEM((DFT_GROUP, n), F32), pltpu.VMEM((DFT_GROUP, n), F32)],
        compiler_params=_cparams(("arbitrary",)),
        name="dft_tables",
    )()


def _hy_fwd_kernel(c_ref, s_ref, zz_ref, kc_ref, ks_ref, a_ref, b_ref):
    zz = zz_ref[0]
    uc = jnp.dot(c_ref[...], zz, preferred_element_type=F32)
    us = jnp.dot(s_ref[...], zz, preferred_element_type=F32)
    kc = kc_ref[...]
    ks = ks_ref[...]
    a_ref[0] = (uc * kc - us * ks).astype(BF16)
    b_ref[0] = (uc * ks + us * kc).astype(BF16)


def _hy_forward(ctab, stab, zz, kc, ks):
    bsz, n, w = zz.shape
    tf = min(n, 256)
    tab = pl.BlockSpec((tf, n), lambda b, i: (i, 0))
    kspec = pl.BlockSpec((tf, w), lambda b, i: (i, 0))
    ospec = pl.BlockSpec((1, tf, w), lambda b, i: (b, i, 0))
    return pl.pallas_call(
        _hy_fwd_kernel,
        grid=(bsz, n // tf),
        in_specs=[tab, tab, pl.BlockSpec((1, n, w), lambda b, i: (b, 0, 0)), kspec, kspec],
        out_specs=[ospec, ospec],
        out_shape=[jax.ShapeDtypeStruct((bsz, n, w), BF16)] * 2,
        compiler_params=_cparams(("parallel", "parallel")),
        name="hy_forward",
    )(ctab, stab, zz, kc, ks)


def _hy_inv_kernel(ct_ref, st_ref, a_ref, b_ref, x0_ref, zz_ref, bias_ref, o_ref, *, n):
    y = (jnp.dot(ct_ref[...], a_ref[0], preferred_element_type=F32)
         + jnp.dot(st_ref[...], b_ref[0], preferred_element_type=F32)) * (1.0 / n)
    zz = zz_ref[0].astype(F32)
    o_ref[0] = (x0_ref[0].astype(F32) * (y + zz * bias_ref[...])).astype(BF16)


def _hy_inverse(cttab, sttab, a, bq, x0, zz, bias):
    bsz, n, w = zz.shape
    tt = min(n, 256)
    tn = 512
    tab = pl.BlockSpec((tt, n), lambda b, j, i: (i, 0))
    full = pl.BlockSpec((1, n, tn), lambda b, j, i: (b, 0, j))
    tile = pl.BlockSpec((1, tt, tn), lambda b, j, i: (b, i, j))
    return pl.pallas_call(
        functools.partial(_hy_inv_kernel, n=n),
        grid=(bsz, w // tn, n // tt),
        in_specs=[tab, tab, full, full, tile, tile, pl.BlockSpec((1, tn), lambda b, j, i: (0, j))],
        out_specs=tile,
        out_shape=jax.ShapeDtypeStruct((bsz, n, w), BF16),
        compiler_params=_cparams(("parallel", "parallel", "parallel")),
        name="hy_inverse",
    )(cttab, sttab, a, bq, x0, zz, bias.reshape(1, w))


def _merge_kernel(odn_ref, ohy_ref, olru_ref, gdn_ref, ghy_ref, glru_ref, wdn_ref, why_ref, wlru_ref,
                  wout_ref, x_ref, gate_ref, o_ref):
    m = _sigmoid(gdn_ref[0].astype(F32)) * jnp.dot(odn_ref[0], wdn_ref[...], preferred_element_type=F32)
    m = m + _sigmoid(ghy_ref[0].astype(F32)) * jnp.dot(ohy_ref[0], why_ref[...], preferred_element_type=F32)
    m = m + _sigmoid(glru_ref[0].astype(F32)) * jnp.dot(olru_ref[0], wlru_ref[...], preferred_element_type=F32)
    y = jnp.dot(m.astype(BF16), wout_ref[...], preferred_element_type=F32)
    o_ref[0] = x_ref[0] + gate_ref[0] * y


def _merge(o_dn, o_hy, o_lru, proj, w_dn, w_hy, w_lru, w_out, x, gate):
    bsz, n, d = x.shape
    tm = min(n, 256)
    nblk = d // 1024
    act = pl.BlockSpec((1, tm, d), lambda b, i: (b, i, 0))
    gspec = lambda k: pl.BlockSpec((1, tm, d), lambda b, i: (b, i, GATE_BLK * LANE // d + k * nblk))
    wspec = pl.BlockSpec((d, d), lambda b, i: (0, 0))
    return pl.pallas_call(
        _merge_kernel,
        grid=(bsz, n // tm),
        in_specs=[act, act, act, gspec(0), gspec(1), gspec(2), wspec, wspec, wspec, wspec, act,
                  pl.BlockSpec((1, 1, d), lambda b, i: (b, 0, 0))],
        out_specs=act,
        out_shape=jax.ShapeDtypeStruct((bsz, n, d), F32),
        compiler_params=_cparams(("parallel", "parallel")),
        name="merge",
    )(o_dn, o_hy, o_lru, proj, proj, proj, w_dn, w_hy, w_lru, w_out, x, gate)


def _grid_conv(u, w_ref, rows, cols):
    n = u.shape[0]
    t = _row_iota(u.shape)
    c = _mod_pow2(t, cols)
    r = _div_pow2(t, cols)
    left = jnp.where(c >= 1, pltpu.roll(u, 1, 0), 0.0)
    right = jnp.where(c <= cols - 2, pltpu.roll(u, n - 1, 0), 0.0)
    acc = None
    for di in (-1, 0, 1):
        if rows == 1 and di != 0:
            continue
        k = 3 * (di + 1)
        inner = left * w_ref[k:k + 1, :] + u * w_ref[k + 1:k + 2, :] + right * w_ref[k + 2:k + 3, :]
        if di != 0:
            inner = jnp.where((r + di >= 0) & (r + di < rows), pltpu.roll(inner, (-di * cols) % n, 0), 0.0)
        acc = inner if acc is None else acc + inner
    return acc


def _ffn_act_kernel(ug_ref, uv_ref, wg_ref, wv_ref, o_ref, *, rows, cols):
    g = _grid_conv(ug_ref[0].astype(F32), wg_ref, rows, cols)
    v = _grid_conv(uv_ref[0].astype(F32), wv_ref, rows, cols)
    o_ref[0] = (_silu(g) * v).astype(BF16)


def _ffn_act(u, conv_w, rows, cols):
    bsz, n, _ = u.shape
    nblk = FFN_HIDDEN // LANE
    return pl.pallas_call(
        functools.partial(_ffn_act_kernel, rows=rows, cols=cols),
        grid=(bsz, nblk),
        in_specs=[pl.BlockSpec((1, n, LANE), lambda b, j: (b, 0, j)),
                  pl.BlockSpec((1, n, LANE), lambda b, j: (b, 0, nblk + j)),
                  pl.BlockSpec((9, LANE), lambda b, j: (0, j)),
                  pl.BlockSpec((9, LANE), lambda b, j: (0, nblk + j))],
        out_specs=pl.BlockSpec((1, n, LANE), lambda b, j: (b, 0, j)),
        out_shape=jax.ShapeDtypeStruct((bsz, n, FFN_HIDDEN), BF16),
        compiler_params=_cparams(("parallel", "parallel")),
        name="ffn_act",
    )(u, u, conv_w, conv_w)


def _final_norm_kernel(x_ref, g_ref, o_ref):
    x = x_ref[0]
    o_ref[0] = x * lax.rsqrt(jnp.mean(x * x, axis=-1, keepdims=True) + NORM_EPS) * g_ref[...]


def _final_norm(x, gain):
    bsz, n, d = x.shape
    tm = min(n, 512)
    spec = pl.BlockSpec((1, tm, d), lambda b, i: (b, i, 0))
    return pl.pallas_call(
        _final_norm_kernel,
        grid=(bsz, n // tm),
        in_specs=[spec, pl.BlockSpec((1, d), lambda b, i: (0, 0))],
        out_specs=spec,
        out_shape=jax.ShapeDtypeStruct((bsz, n, d), F32),
        compiler_params=_cparams(("parallel", "parallel")),
        name="final_norm",
    )(x, gain.reshape(1, d))


def _gate_column_map():
    src = np.full((LANE,), -1, np.int32)
    isdec = np.zeros((1, LANE), np.float32)
    dirs = np.zeros((LANE,), np.int32)
    for h in range(DN_HEADS):
        for slot, (d, kind) in enumerate(((0, 0), (0, 1), (1, 0), (1, 1), (0, 0), (1, 0))):
            src[h * SUBLANE + slot] = d * 2 * DN_HEADS + kind * DN_HEADS + h
            isdec[0, h * SUBLANE + slot] = 1.0 if kind == 0 else 0.0
            dirs[h * SUBLANE + slot] = d
    return src, isdec, dirs


def _split_in_proj(w_in):
    o = np.cumsum((3 * DN_WIDTH, DN_WIDTH, 4 * DN_HEADS, 3 * HY_WIDTH, LRU_WIDTH, LRU_WIDTH)).tolist()
    w_main = jnp.concatenate([w_in[:, :o[1]], w_in[:, o[2]:]], axis=1).astype(BF16)
    src, _, _ = _gate_column_map()
    w_ab = w_in[:, o[1]:o[2]]
    w_gate = jnp.where(jnp.asarray(src >= 0)[None, :], w_ab[:, np.maximum(src, 0)], 0.0).astype(BF16)
    return w_main, w_gate


def _gate_params(a_log, dt_bias):
    _, isdec, dirs = _gate_column_map()
    head = (np.arange(LANE) // SUBLANE).astype(np.int32)
    alog_c = a_log[dirs, head].reshape(1, LANE)
    dtb_c = dt_bias[dirs, head].reshape(1, LANE)
    return alog_c, dtb_c, jnp.asarray(isdec)


def _token_mixer(x, mods, lp, states, tables, with_output):
    bsz, n, _ = x.shape
    shift1, scale1, gate1 = mods[0], mods[1], mods[2]
    proj = _norm_mod_matmul(x, lp["norm1_g"], shift1, scale1, lp["w_main"], BF16, 1024)
    ab = _norm_mod_matmul(x, lp["norm1_g"], shift1, scale1, lp["w_gate"], F32, LANE)

    qkv_rm, qkv_tr = _dn_prep(proj, lp["dn_conv_w"])
    gates = _dn_gates(ab, *lp["gate_params"])
    o_dn, s_f, s_b = _delta_net(qkv_rm, qkv_tr, gates, proj, lp["dn_norm_g"], states[0], states[1], with_output)
    o_lru, h_last = _rglru(proj, lp["lru_conv_w"], lp["lru_conv_b"], lp["lru_w_a"], lp["lru_b_a"],
                           lp["lru_w_x"], lp["lru_b_x"], lp["lru_lambda"], states[2], with_output)
    new_states = (s_f, s_b, h_last)
    if not with_output:
        return None, new_states

    ctab, stab, cttab, sttab = tables
    hsum, hdiff = _hy_filter(n, lp["hy_w1"], lp["hy_b1"], lp["hy_f1"], lp["hy_w2"], lp["hy_b2"], lp["hy_f2"],
                             lp["hy_w3"])
    tk = min(n, 512)
    kc = _matmul(ctab, hsum, F32, tk, 512)
    ks = _matmul(stab, hdiff, F32, tk, 512)
    x0, zz = _hy_prep(proj, lp["hy_conv_w"], lp["hy_conv_b"])
    a, bq = _hy_forward(ctab, stab, zz, kc, ks)
    o_hy = _hy_inverse(cttab, sttab, a, bq, x0, zz, lp["hy_bias"])

    x = _merge(o_dn, o_hy, o_lru, proj, lp["w_proj_dn"], lp["w_proj_hy"], lp["w_proj_lru"], lp["w_out"], x, gate1)
    return x, new_states


def _conv_ffn(x, mods, lp, rows, cols):
    u = _norm_mod_matmul(x, lp["norm2_g"], mods[3], mods[4], lp["ffn_up"], BF16, 512)
    act = _ffn_act(u, lp["ffn_conv_w"], rows, cols)
    return _matmul_residual(act, lp["ffn_down"], x, mods[5])


def kernel(x, c, ctx, c_ctx, w_mod, b_mod, norm1_g, norm2_g, w_in, dn_conv_w, dn_a_log, dn_dt_bias, dn_norm_g,
           hy_conv_w, hy_conv_b, hy_w1, hy_b1, hy_f1, hy_w2, hy_b2, hy_f2, hy_w3, hy_bias,
           lru_conv_w, lru_conv_b, lru_w_a, lru_b_a, lru_w_x, lru_b_x, lru_lambda,
           w_proj_dn, w_proj_hy, w_proj_lru, w_out, ffn_up, ffn_conv_w, ffn_down, final_norm_g):
    bsz, n_lat, d = x.shape
    n_ctx = ctx.shape[1]
    depth = w_in.shape[0]
    rows = n_lat // GRID_W

    cvec = jnp.zeros((SUBLANE, d), F32).at[:bsz].set(c).at[bsz].set(c_ctx)
    lat_tables = _dft_tables(n_lat, False) + _dft_tables(n_lat, True)
    ctx_tables = _dft_tables(n_ctx, False) + _dft_tables(n_ctx, True)
    zero_states = (jnp.zeros((bsz, DN_HEADS, LANE, LANE), F32), jnp.zeros((bsz, DN_HEADS, LANE, LANE), F32),
                   jnp.zeros((bsz, 2, LRU_WIDTH), F32))

    xc = ctx
    for l in range(depth):
        ctx_needed = l < depth - 1
        w_main, w_gate = _split_in_proj(w_in[l])
        lp = dict(
            norm1_g=norm1_g[l], norm2_g=norm2_g[l], w_main=w_main, w_gate=w_gate,
            dn_conv_w=dn_conv_w[l], gate_params=_gate_params(dn_a_log[l], dn_dt_bias[l]), dn_norm_g=dn_norm_g[l],
            hy_conv_w=hy_conv_w[l], hy_conv_b=hy_conv_b[l], hy_w1=hy_w1[l], hy_b1=hy_b1[l], hy_f1=hy_f1[l],
            hy_w2=hy_w2[l], hy_b2=hy_b2[l], hy_f2=hy_f2[l], hy_w3=hy_w3[l], hy_bias=hy_bias[l],
            lru_conv_w=lru_conv_w[l], lru_conv_b=lru_conv_b[l], lru_w_a=lru_w_a[l].astype(BF16),
            lru_b_a=lru_b_a[l], lru_w_x=lru_w_x[l].astype(BF16), lru_b_x=lru_b_x[l], lru_lambda=lru_lambda[l],
            w_proj_dn=w_proj_dn[l].astype(BF16), w_proj_hy=w_proj_hy[l].astype(BF16),
            w_proj_lru=w_proj_lru[l].astype(BF16), w_out=w_out[l].astype(BF16),
            ffn_up=ffn_up[l].astype(BF16), ffn_conv_w=ffn_conv_w[l].reshape(9, 2 * FFN_HIDDEN),
            ffn_down=ffn_down[l].astype(BF16))
        mod = _modulation(cvec, w_mod[l].astype(BF16), b_mod[l])
        lat_mod = [mod[:bsz, k * d:(k + 1) * d].reshape(bsz, 1, d) for k in range(N_MOD)]
        ctx_mod = [jnp.broadcast_to(mod[bsz:bsz + 1, k * d:(k + 1) * d].reshape(1, 1, d), (bsz, 1, d))
                   for k in range(N_MOD)]

        xc_new, ctx_states = _token_mixer(xc, ctx_mod, lp, zero_states, ctx_tables, ctx_needed)
        x, _ = _token_mixer(x, lat_mod, lp, ctx_states, lat_tables, True)
        x = _conv_ffn(x, lat_mod, lp, rows, GRID_W)
        if ctx_needed:
            xc = _conv_ffn(xc_new, ctx_mod, lp, 1, n_ctx)
    return _final_norm(x, final_norm_g)
```

```python
import functools
import math

import numpy as np
import jax
import jax.numpy as jnp
from jax import lax
from jax.experimental import pallas as pl
from jax.experimental.pallas import tpu as pltpu

F32 = jnp.float32
BF16 = jnp.bfloat16
HIGHEST = lax.Precision.HIGHEST

D_MODEL = 1024
DEPTH = 2
GRID_W = 64
NORM_EPS = 1e-6
N_MOD = 6

DN_HEADS = 8
DN_HEAD_DIM = 128
DN_WIDTH = DN_HEADS * DN_HEAD_DIM
HY_WIDTH = 1024
HY_EMB = 33
HY_BANDS = (HY_EMB - 1) // 2
HY_FILTER_HIDDEN = 64
HY_FAST_DECAY_PCT = 0.3
HY_SLOW_DECAY_PCT = 1.5
HY_DECAY_TARGET = 1e-2
LRU_WIDTH = 1024
LRU_BLOCKS = 8
LRU_BLOCK = LRU_WIDTH // LRU_BLOCKS
LRU_C = 8.0
FFN_HIDDEN = 2816

LANE = 128
SUBLANE = 8
TILE = 256
CHUNK = 128
DN_HEADS_PER_STEP = 2
LRU_ROWS = 256
MIB = 1024 * 1024

QKV_BLK = 0
Z_BLK = 24
HY_BLK = 32
LX_BLK = 56
LY_BLK = 64
GATE_BLK = 72
N_MAIN = 96 * LANE


def _cparams(sem, vmem_mib=48):
    return pltpu.CompilerParams(dimension_semantics=sem, vmem_limit_bytes=vmem_mib * MIB)


def _sigmoid(x):
    return 1.0 / (1.0 + jnp.exp(-x))


def _silu(x):
    return x * _sigmoid(x)


def _softplus(x):
    return jnp.maximum(x, 0.0) + jnp.log(1.0 + jnp.exp(-jnp.abs(x)))


def _row_iota(shape):
    return lax.broadcasted_iota(jnp.int32, shape, 0)


def _col_iota(shape):
    return lax.broadcasted_iota(jnp.int32, shape, 1)


def _div_pow2(x, k):
    assert k & (k - 1) == 0
    return x >> (k.bit_length() - 1)


def _mod_pow2(x, k):
    assert k & (k - 1) == 0
    return x & (k - 1)


def _bdot(a, b):
    return jnp.dot(a.astype(BF16), b.astype(BF16), preferred_element_type=F32)


def _mod_kernel(c_ref, w_ref, b_ref, o_ref):
    o_ref[...] = _bdot(_silu(c_ref[...]), w_ref[...]) + b_ref[...]


def _modulation(cvec, w_mod, b_mod):
    n = w_mod.shape[1]
    tn = 1024
    return pl.pallas_call(
        _mod_kernel,
        grid=(n // tn,),
        in_specs=[pl.BlockSpec((SUBLANE, D_MODEL), lambda j: (0, 0)),
                  pl.BlockSpec((D_MODEL, tn), lambda j: (0, j)),
                  pl.BlockSpec((1, tn), lambda j: (0, j))],
        out_specs=pl.BlockSpec((SUBLANE, tn), lambda j: (0, j)),
        out_shape=jax.ShapeDtypeStruct((SUBLANE, n), F32),
        compiler_params=_cparams(("parallel",)),
        name="modulation",
    )(cvec, w_mod, b_mod.reshape(1, n))


def _nmm_kernel(x_ref, g_ref, sh_ref, sc_ref, w_ref, o_ref, h_ref):
    @pl.when(pl.program_id(2) == 0)
    def _():
        x = x_ref[0]
        y = x * lax.rsqrt(jnp.mean(x * x, axis=-1, keepdims=True) + NORM_EPS) * g_ref[...]
        h_ref[...] = (y * (1.0 + sc_ref[0]) + sh_ref[0]).astype(BF16)

    o_ref[0] = jnp.dot(h_ref[...], w_ref[...], preferred_element_type=F32).astype(o_ref.dtype)


def _norm_mod_matmul(x, gain, shift, scale, w, out_dtype, tm, tn):
    bsz, n, d = x.shape
    nout = w.shape[1]
    tm = min(n, tm)
    return pl.pallas_call(
        _nmm_kernel,
        grid=(bsz, n // tm, nout // tn),
        in_specs=[pl.BlockSpec((1, tm, d), lambda b, i, j: (b, i, 0)),
                  pl.BlockSpec((1, d), lambda b, i, j: (0, 0)),
                  pl.BlockSpec((1, 1, d), lambda b, i, j: (b, 0, 0)),
                  pl.BlockSpec((1, 1, d), lambda b, i, j: (b, 0, 0)),
                  pl.BlockSpec((d, tn), lambda b, i, j: (0, j))],
        out_specs=pl.BlockSpec((1, tm, tn), lambda b, i, j: (b, i, j)),
        out_shape=jax.ShapeDtypeStruct((bsz, n, nout), out_dtype),
        scratch_shapes=[pltpu.VMEM((tm, d), BF16)],
        compiler_params=_cparams(("parallel", "parallel", "arbitrary")),
        name="norm_mod_matmul",
    )(x, gain.reshape(1, d), shift, scale, w)


def _mm_kernel(a_ref, b_ref, o_ref):
    o_ref[...] = jnp.dot(a_ref[...], b_ref[...], preferred_element_type=F32).astype(o_ref.dtype)


def _matmul(a, b, out_dtype, tm, tn):
    m, k = a.shape
    n = b.shape[1]
    return pl.pallas_call(
        _mm_kernel,
        grid=(m // tm, n // tn),
        in_specs=[pl.BlockSpec((tm, k), lambda i, j: (i, 0)),
                  pl.BlockSpec((k, tn), lambda i, j: (0, j))],
        out_specs=pl.BlockSpec((tm, tn), lambda i, j: (i, j)),
        out_shape=jax.ShapeDtypeStruct((m, n), out_dtype),
        compiler_params=_cparams(("parallel", "parallel")),
        name="matmul",
    )(a, b)


def _mm_res_kernel(a_ref, b_ref, x_ref, g_ref, o_ref):
    y = jnp.dot(a_ref[0], b_ref[...], preferred_element_type=F32)
    o_ref[0] = x_ref[0] + g_ref[0] * y


def _matmul_residual(a, w, x, gate):
    bsz, n, k = a.shape
    d = w.shape[1]
    tm = min(n, 512)
    return pl.pallas_call(
        _mm_res_kernel,
        grid=(bsz, n // tm),
        in_specs=[pl.BlockSpec((1, tm, k), lambda b, i: (b, i, 0)),
                  pl.BlockSpec((k, d), lambda b, i: (0, 0)),
                  pl.BlockSpec((1, tm, d), lambda b, i: (b, i, 0)),
                  pl.BlockSpec((1, 1, d), lambda b, i: (b, 0, 0))],
        out_specs=pl.BlockSpec((1, tm, d), lambda b, i: (b, i, 0)),
        out_shape=jax.ShapeDtypeStruct((bsz, n, d), F32),
        compiler_params=_cparams(("parallel", "parallel")),
        name="matmul_residual",
    )(a, w, x, gate)


def _shift_rows(x, off):
    n = x.shape[0]
    if off == 0:
        return x
    rolled = pltpu.roll(x, (-off) % n, 0)
    t = _row_iota(x.shape)
    valid = (t + off >= 0) & (t + off < n)
    return jnp.where(valid, rolled, 0.0)


def _dwconv_rows(x, w_ref, k):
    left = (k - 1) // 2
    acc = None
    for j in range(k):
        term = _shift_rows(x, j - left) * w_ref[j:j + 1, :]
        acc = term if acc is None else acc + term
    return acc


def _dn_prep_kernel(p_ref, w_ref, rm_ref, tr_ref, *, n_tiles):
    c = pl.program_id(1)
    x = p_ref[0].astype(F32)
    y = _silu(_dwconv_rows(x, w_ref, 4))
    nrm = y * lax.rsqrt(jnp.sum(y * y, axis=-1, keepdims=True) + 1e-6)
    nrm = nrm * jnp.where(c < DN_HEADS, DN_HEAD_DIM ** -0.5, 1.0)
    y = jnp.where(c < 2 * DN_HEADS, nrm, y)
    rm_ref[0] = y.astype(BF16)
    for t in range(n_tiles):
        tr_ref[0, 0, t] = y[t * TILE:(t + 1) * TILE, :].T.astype(BF16)


def _dn_prep(proj, conv_w):
    bsz, n, _ = proj.shape
    nt = n // TILE
    nc = 3 * DN_HEADS
    return pl.pallas_call(
        functools.partial(_dn_prep_kernel, n_tiles=nt),
        grid=(bsz, nc),
        in_specs=[pl.BlockSpec((1, n, LANE), lambda b, c: (b, 0, QKV_BLK + c)),
                  pl.BlockSpec((4, LANE), lambda b, c: (0, c))],
        out_specs=[pl.BlockSpec((1, n, LANE), lambda b, c: (b, 0, c)),
                   pl.BlockSpec((1, 1, nt, LANE, TILE), lambda b, c: (b, c, 0, 0, 0))],
        out_shape=[jax.ShapeDtypeStruct((bsz, n, nc * LANE), BF16),
                   jax.ShapeDtypeStruct((bsz, nc, nt, LANE, TILE), BF16)],
        compiler_params=_cparams(("parallel", "parallel")),
        name="dn_prep",
    )(proj, conv_w)


def _dn_gate_kernel(ab_ref, alog_ref, dtb_ref, isdec_ref, o_ref):
    x = ab_ref[0]
    dec = -jnp.exp(alog_ref[...]) * _softplus(x + dtb_ref[...])
    e = jnp.where(isdec_ref[...] > 0.5, dec, _sigmoid(x))
    et = e.T
    s = _row_iota((TILE, TILE))
    t = _col_iota((TILE, TILE))
    same = _div_pow2(s, CHUNK) == _div_pow2(t, CHUNK)
    prefix = jnp.where(same & (s <= t), 1.0, 0.0)
    suffix = jnp.where(same & (s >= t), 1.0, 0.0)
    total = jnp.where(same, 1.0, 0.0)
    pre = jnp.dot(et, prefix, precision=HIGHEST, preferred_element_type=F32)
    suf = jnp.dot(et, suffix, precision=HIGHEST, preferred_element_type=F32)
    tot = jnp.dot(et, total, precision=HIGHEST, preferred_element_type=F32)
    slot = _mod_pow2(_row_iota((LANE, TILE)), SUBLANE)
    o_ref[0, 0] = jnp.where(slot == 0, pre, jnp.where(slot == 2, suf, jnp.where(slot >= 4, tot, et)))


def _dn_gates(ab, alog_c, dtb_c, isdec_c):
    bsz, n, _ = ab.shape
    nt = n // TILE
    vec = pl.BlockSpec((1, LANE), lambda b, i: (0, 0))
    return pl.pallas_call(
        _dn_gate_kernel,
        grid=(bsz, nt),
        in_specs=[pl.BlockSpec((1, TILE, LANE), lambda b, i: (b, i, 0)), vec, vec, vec],
        out_specs=pl.BlockSpec((1, 1, LANE, TILE), lambda b, i: (b, i, 0, 0)),
        out_shape=jax.ShapeDtypeStruct((bsz, nt, LANE, TILE), F32),
        compiler_params=_cparams(("parallel", "parallel")),
        name="dn_gates",
    )(ab, alog_c, dtb_c, isdec_c)


def _dn_tiles(chains):
    a = _row_iota((TILE, TILE))
    b = _col_iota((TILE, TILE))
    same = _div_pow2(a, CHUNK) == _div_pow2(b, CHUNK)
    apart = a ^ b
    n_chunks = TILE // CHUNK
    dot = functools.partial(jnp.dot, preferred_element_type=F32)

    kk = [dot(c["k_rm"], c["kt"]) for c in chains]
    kq = [dot(c["k_rm"], c["qt"]) for c in chains]
    xs, ps, attn = [], [], []
    for c, kk_c, kq_c in zip(chains, kk, kq):
        incl = same & ((a >= b) if c["backward"] else (a <= b))
        gcb = jnp.broadcast_to(c["gc"], (TILE, TILE))
        diff = gcb - gcb.T
        decay = jnp.where(incl, jnp.exp(jnp.where(incl, diff, 0.0)), 0.0)
        attn.append((kq_c * decay).astype(BF16))
        x = jnp.where(a == b, 0.0, -(kk_c * decay * c["beta"]))
        xs.append(x)
        ps.append(jnp.where(a == b, 1.0, jnp.where(apart == 1, x, 0.0)))
    s = 2
    while s < CHUNK:
        couple = (apart >> (s.bit_length() - 1)) == 1
        pb = [p.astype(BF16) for p in ps]
        px = [dot(pb_c, jnp.where(couple, x, 0.0).astype(BF16)) for pb_c, x in zip(pb, xs)]
        ps = [p + dot(px_c.astype(BF16), pb_c) for p, px_c, pb_c in zip(ps, px, pb)]
        s *= 2
    t_inv = [p.astype(BF16) for p in ps]
    egc = [jnp.exp(c["gc"]) for c in chains]
    u_t = [dot((c["vt"].astype(F32) * c["beta"]).astype(BF16), t) for c, t in zip(chains, t_inv)]
    w_t = [dot((c["kt"].astype(F32) * (c["beta"] * e)).astype(BF16), t).astype(BF16)
           for c, e, t in zip(chains, egc, t_inv)]
    qd_t = [(c["qt"].astype(F32) * e).astype(BF16) for c, e in zip(chains, egc)]
    kdec = [jnp.exp(c["tot"] - c["gc"]) for c in chains]
    outs = [[None] * n_chunks for _ in chains]
    for step in range(n_chunks):
        cis = [(n_chunks - 1 - step) if c["backward"] else step for c in chains]
        sl = [slice(ci * CHUNK, (ci + 1) * CHUNK) for ci in cis]
        st = [c["st_ref"][...] for c in chains]
        stb = [s_c.astype(BF16) for s_c in st]
        sw = [dot(stb_c, w_c[:, r]) for stb_c, w_c, r in zip(stb, w_t, sl)]
        sq = [dot(stb_c, q_c[:, r]) for stb_c, q_c, r in zip(stb, qd_t, sl)]
        vn = [u_c[:, r] - sw_c for u_c, sw_c, r in zip(u_t, sw, sl)]
        av = [dot(vn_c.astype(BF16), at_c[r, r]) for vn_c, at_c, r in zip(vn, attn, sl)]
        upd = [dot((vn_c * kd_c[:, r]).astype(BF16), c["k_rm"][r, :])
               for vn_c, kd_c, c, r in zip(vn, kdec, chains, sl)]
        for idx, c in enumerate(chains):
            c["st_ref"][...] = st[idx] * jnp.exp(c["tot"][:, sl[idx]]) + upd[idx]
            outs[idx][cis[idx]] = sq[idx] + av[idx]
    return [jnp.concatenate(o, axis=1) for o in outs]


def _dn_kernel(k_ref, qt_ref, kt_ref, vt_ref, g_ref, z_ref, ng_ref, s0f_ref, s0b_ref,
               o_ref, sf_ref, sb_ref, ot_ref, st_ref, *, n_tiles, with_output):
    hb = DN_HEADS_PER_STEP
    for hh in range(hb):
        st_ref[2 * hh] = s0f_ref[0, hh]
        st_ref[2 * hh + 1] = s0b_ref[0, hh]

    def body(i, carry):
        nf = i
        nb = n_tiles - 1 - i
        chains = []
        for hh in range(hb):
            for backward, n in ((False, nf), (True, nb)):
                g = g_ref[0, n, hh * SUBLANE:(hh + 1) * SUBLANE, :]
                base = 2 if backward else 0
                chains.append(dict(
                    k_rm=k_ref[0, pl.ds(pl.multiple_of(n * TILE, TILE), TILE), hh * LANE:(hh + 1) * LANE],
                    qt=qt_ref[0, hh, n], kt=kt_ref[0, hh, n], vt=vt_ref[0, hh, n],
                    gc=g[base:base + 1, :], beta=g[base + 1:base + 2, :], tot=g[4 + base // 2:5 + base // 2, :],
                    st_ref=st_ref.at[2 * hh + (1 if backward else 0)], backward=backward, hh=hh, n=n))
        o_t = _dn_tiles(chains)
        if with_output:
            for c, o_c in zip(chains, o_t):
                ot_ref[c["hh"], c["n"]] = ot_ref[c["hh"], c["n"]] + o_c
        return carry

    if with_output:
        ot_ref[...] = jnp.zeros_like(ot_ref)
    lax.fori_loop(0, n_tiles, body, 0)
    for hh in range(hb):
        sf_ref[0, hh] = st_ref[2 * hh]
        sb_ref[0, hh] = st_ref[2 * hh + 1]
    if with_output:
        for hh in range(hb):
            for t in range(n_tiles):
                o = ot_ref[hh, t].T
                y = o * lax.rsqrt(jnp.mean(o * o, axis=-1, keepdims=True) + NORM_EPS) * ng_ref[...]
                z = z_ref[0, t * TILE:(t + 1) * TILE, hh * LANE:(hh + 1) * LANE].astype(F32)
                o_ref[0, t * TILE:(t + 1) * TILE, hh * LANE:(hh + 1) * LANE] = (y * _silu(z)).astype(BF16)
    else:
        o_ref[...] = jnp.zeros_like(o_ref)


def _delta_net(qkv_rm, qkv_tr, gates, proj, norm_g, s0f, s0b, with_output):
    bsz, n, _ = qkv_rm.shape
    nt = n // TILE
    h = DN_HEADS
    hb = DN_HEADS_PER_STEP
    wide = hb * LANE
    tr_spec = lambda off: pl.BlockSpec((1, hb, nt, LANE, TILE), lambda b, j: (b, off + j, 0, 0, 0))
    st_spec = pl.BlockSpec((1, hb, LANE, LANE), lambda b, j: (b, j, 0, 0))
    n_out = n if with_output else SUBLANE
    return pl.pallas_call(
        functools.partial(_dn_kernel, n_tiles=nt, with_output=with_output),
        grid=(bsz, h // hb),
        in_specs=[pl.BlockSpec((1, n, wide), lambda b, j: (b, 0, h // hb + j)),
                  tr_spec(0), tr_spec(h // hb), tr_spec(2 * h // hb),
                  pl.BlockSpec((1, nt, hb * SUBLANE, TILE), lambda b, j: (b, 0, j, 0)),
                  pl.BlockSpec((1, n, wide), lambda b, j: (b, 0, Z_BLK // hb + j)),
                  pl.BlockSpec((1, LANE), lambda b, j: (0, 0)),
                  st_spec, st_spec],
        out_specs=[pl.BlockSpec((1, n_out, wide), lambda b, j: (b, 0, j)), st_spec, st_spec],
        out_shape=[jax.ShapeDtypeStruct((bsz, n_out, DN_WIDTH), BF16),
                   jax.ShapeDtypeStruct((bsz, h, LANE, LANE), F32),
                   jax.ShapeDtypeStruct((bsz, h, LANE, LANE), F32)],
        scratch_shapes=[pltpu.VMEM((hb, nt, LANE, TILE), F32),
                        pltpu.VMEM((2 * hb, LANE, LANE), F32)],
        compiler_params=_cparams(("parallel", "parallel")),
        name="delta_net",
    )(qkv_rm, qkv_tr, qkv_tr, qkv_tr, gates, proj, norm_g.reshape(1, LANE), s0f, s0b)


def _lru_scan_block(x, wa, ba, wx, bx, spl, h_in, backward):
    rows = x.shape[0]
    xb = x.astype(BF16)
    r = _sigmoid(jnp.dot(xb, wa, preferred_element_type=F32) + ba)
    gi = _sigmoid(jnp.dot(xb, wx, preferred_element_type=F32) + bx)
    log_a = -LRU_C * r * spl
    a = jnp.exp(log_a)
    b = jnp.sqrt(1.0 - jnp.exp(2.0 * log_a)) * (gi * x)
    sub = _mod_pow2(_row_iota(x.shape), SUBLANE)
    s = 1
    while s < SUBLANE:
        if backward:
            keep = sub < SUBLANE - s
            a_sh = jnp.where(keep, pltpu.roll(a, rows - s, 0), 1.0)
            b_sh = jnp.where(keep, pltpu.roll(b, rows - s, 0), 0.0)
        else:
            keep = sub >= s
            a_sh = jnp.where(keep, pltpu.roll(a, s, 0), 1.0)
            b_sh = jnp.where(keep, pltpu.roll(b, s, 0), 0.0)
        b = a * b_sh + b
        a = a * a_sh
        s *= 2
    groups = rows // SUBLANE
    pieces = [None] * groups
    carry = h_in
    order = range(groups - 1, -1, -1) if backward else range(groups)
    edge = 0 if backward else SUBLANE - 1
    for gidx in order:
        lo = gidx * SUBLANE
        hgrp = b[lo:lo + SUBLANE, :] + a[lo:lo + SUBLANE, :] * carry
        pieces[gidx] = hgrp
        carry = hgrp[edge:edge + 1, :]
    return jnp.concatenate(pieces, axis=0), carry


def _lru_kernel(px_ref, py_ref, cw_ref, cb_ref, wa_ref, ba_ref, wx_ref, bx_ref, lam_ref, h0_ref,
                o_ref, last_ref, xs_ref, hs_ref, *, n_blocks, with_output):
    x = px_ref[0].astype(F32)
    xs_ref[...] = _dwconv_rows(x, cw_ref, 4) + cb_ref[...]
    spl = _softplus(-lam_ref[...])

    def body(i, carry):
        hf, hb = carry
        rf = pl.multiple_of(i * LRU_ROWS, LRU_ROWS)
        rb = pl.multiple_of((n_blocks - 1 - i) * LRU_ROWS, LRU_ROWS)
        h_f, hf = _lru_scan_block(xs_ref[pl.ds(rf, LRU_ROWS), :], wa_ref[0, 0], ba_ref[0:1, :],
                                  wx_ref[0, 0], bx_ref[0:1, :], spl[0:1, :], hf, False)
        h_b, hb = _lru_scan_block(xs_ref[pl.ds(rb, LRU_ROWS), :], wa_ref[1, 0], ba_ref[1:2, :],
                                  wx_ref[1, 0], bx_ref[1:2, :], spl[1:2, :], hb, True)
        if with_output:
            hs_ref[pl.ds(rf, LRU_ROWS), :] = hs_ref[pl.ds(rf, LRU_ROWS), :] + h_f
            hs_ref[pl.ds(rb, LRU_ROWS), :] = hs_ref[pl.ds(rb, LRU_ROWS), :] + h_b
        return hf, hb

    if with_output:
        hs_ref[...] = jnp.zeros_like(hs_ref)
    h0 = h0_ref[0]
    hf, hb = lax.fori_loop(0, n_blocks, body, (h0[0:1, :], h0[1:2, :]))
    last_ref[0] = jnp.concatenate([hf, hb], axis=0)
    if with_output:
        y = py_ref[0].astype(F32)
        gelu = 0.5 * y * (1.0 + jnp.tanh(math.sqrt(2.0 / math.pi) * (y + 0.044715 * (y * y * y))))
        o_ref[0] = (hs_ref[...] * gelu).astype(BF16)
    else:
        o_ref[...] = jnp.zeros_like(o_ref)


def _rglru(proj, conv_w, conv_b, w_a, b_a, w_x, b_x, lam, h0, with_output):
    bsz, n, _ = proj.shape
    rows = min(n, LRU_ROWS)
    nb = n // rows
    n_out = n if with_output else SUBLANE
    vec2 = pl.BlockSpec((2, LANE), lambda b, j: (0, j))
    wspec = pl.BlockSpec((2, 1, LRU_BLOCK, LRU_BLOCK), lambda b, j: (0, j, 0, 0))
    return pl.pallas_call(
        functools.partial(_lru_kernel, n_blocks=nb, with_output=with_output),
        grid=(bsz, LRU_BLOCKS),
        in_specs=[pl.BlockSpec((1, n, LANE), lambda b, j: (b, 0, LX_BLK + j)),
                  pl.BlockSpec((1, n, LANE), lambda b, j: (b, 0, LY_BLK + j)),
                  pl.BlockSpec((4, LANE), lambda b, j: (0, j)),
                  pl.BlockSpec((1, LANE), lambda b, j: (0, j)),
                  wspec, vec2, wspec, vec2, vec2,
                  pl.BlockSpec((1, 2, LANE), lambda b, j: (b, 0, j))],
        out_specs=[pl.BlockSpec((1, n_out, LANE), lambda b, j: (b, 0, j)),
                   pl.BlockSpec((1, 2, LANE), lambda b, j: (b, 0, j))],
        out_shape=[jax.ShapeDtypeStruct((bsz, n_out, LRU_WIDTH), BF16),
                   jax.ShapeDtypeStruct((bsz, 2, LRU_WIDTH), F32)],
        scratch_shapes=[pltpu.VMEM((n, LANE), F32), pltpu.VMEM((n, LANE), F32)],
        compiler_params=_cparams(("parallel", "parallel")),
        name="rglru",
    )(proj, proj, conv_w, conv_b.reshape(1, LRU_WIDTH), w_a, b_a, w_x, b_x, lam, h0)


def _hy_prep_kernel(p0_ref, p1_ref, pv_ref, w0_ref, w1_ref, wv_ref, b0_ref, b1_ref, bv_ref, x0_ref, zz_ref):
    x0 = _dwconv_rows(p0_ref[0].astype(F32), w0_ref, 3) + b0_ref[...]
    x1 = _dwconv_rows(p1_ref[0].astype(F32), w1_ref, 3) + b1_ref[...]
    v = _dwconv_rows(pv_ref[0].astype(F32), wv_ref, 3) + bv_ref[...]
    x0_ref[0] = x0.astype(BF16)
    zz_ref[0] = (x1 * v).astype(BF16)


def _hy_prep(proj, conv_w, conv_b):
    bsz, n, _ = proj.shape
    nblk = HY_WIDTH // LANE
    pspec = lambda off: pl.BlockSpec((1, n, LANE), lambda b, j: (b, 0, HY_BLK + off + j))
    wspec = lambda off: pl.BlockSpec((3, LANE), lambda b, j: (0, off + j))
    bspec = lambda off: pl.BlockSpec((1, LANE), lambda b, j: (0, off + j))
    ospec = pl.BlockSpec((1, n, LANE), lambda b, j: (b, 0, j))
    cb = conv_b.reshape(1, 3 * HY_WIDTH)
    return pl.pallas_call(
        _hy_prep_kernel,
        grid=(bsz, nblk),
        in_specs=[pspec(0), pspec(nblk), pspec(2 * nblk), wspec(0), wspec(nblk), wspec(2 * nblk),
                  bspec(0), bspec(nblk), bspec(2 * nblk)],
        out_specs=[ospec, ospec],
        out_shape=[jax.ShapeDtypeStruct((bsz, n, HY_WIDTH), BF16)] * 2,
        compiler_params=_cparams(("parallel", "parallel")),
        name="hy_prep",
    )(proj, proj, proj, conv_w, conv_w, conv_w, cb, cb, cb)


def _hy_filter_kernel(w1_ref, b1_ref, f1_ref, w2_ref, b2_ref, f2_ref, w3_ref, band_ref, delta_ref,
                      hs_ref, hd_ref, *, n, rows):
    i = (_row_iota((rows, LANE)) + pl.program_id(0) * rows).astype(F32)
    lane = _col_iota((rows, LANE))
    t = i * (1.0 / (n - 1))
    ang = band_ref[...] * (i * (2.0 * math.pi / n))
    feat = jnp.where(lane == 0, t,
                     jnp.where(lane <= HY_BANDS, jnp.cos(ang), jnp.where(lane < HY_EMB, -jnp.sin(ang), 0.0)))
    hid = jnp.sin(f1_ref[...] * (jnp.dot(feat, w1_ref[...], precision=HIGHEST,
                                         preferred_element_type=F32) + b1_ref[...]))
    hid = jnp.sin(f2_ref[...] * (jnp.dot(hid, w2_ref[...], precision=HIGHEST,
                                         preferred_element_type=F32) + b2_ref[...]))
    filt = jnp.dot(hid, w3_ref[...], precision=HIGHEST, preferred_element_type=F32)
    tw = (_row_iota((rows, HY_WIDTH)) + pl.program_id(0) * rows).astype(F32) * (1.0 / (n - 1))
    dec = jnp.exp(-tw * delta_ref[...])
    h_f = filt[:, :HY_WIDTH] * dec
    h_b = filt[:, HY_WIDTH:] * dec
    hs_ref[...] = (h_f + h_b).astype(BF16)
    hd_ref[...] = (h_f - h_b).astype(BF16)


def _hy_filter(n, w1, b1, f1, w2, b2, f2, w3):
    rows = min(n, 256)
    hid = HY_FILTER_HIDDEN
    w1p = jnp.zeros((LANE, hid), F32).at[:HY_EMB].set(w1)
    bands = np.zeros((1, LANE), np.float32)
    base = np.linspace(1e-4, HY_BANDS - 1, HY_BANDS, dtype=np.float32)
    bands[0, 1:1 + HY_BANDS] = base
    bands[0, 1 + HY_BANDS:HY_EMB] = base
    log_target = math.log(HY_DECAY_TARGET)
    deltas = np.abs(np.linspace(log_target / HY_SLOW_DECAY_PCT, log_target / HY_FAST_DECAY_PCT, HY_WIDTH,
                                dtype=np.float32)).reshape(1, HY_WIDTH)
    full = lambda shape: pl.BlockSpec(shape, lambda i: (0,) * len(shape))
    ospec = pl.BlockSpec((rows, HY_WIDTH), lambda i: (i, 0))
    return pl.pallas_call(
        functools.partial(_hy_filter_kernel, n=n, rows=rows),
        grid=(n // rows,),
        in_specs=[full((LANE, hid)), full((1, hid)), full((1, hid)), full((hid, hid)), full((1, hid)),
                  full((1, hid)), full((hid, 2 * HY_WIDTH)), full((1, LANE)), full((1, HY_WIDTH))],
        out_specs=[ospec, ospec],
        out_shape=[jax.ShapeDtypeStruct((n, HY_WIDTH), BF16)] * 2,
        compiler_params=_cparams(("parallel",)),
        name="hy_filter",
    )(w1p, b1.reshape(1, hid), f1.reshape(1, hid), w2, b2.reshape(1, hid), f2.reshape(1, hid), w3,
      jnp.asarray(bands), jnp.asarray(deltas))


DFT_GROUP = 64


def _dft_table_kernel(c_ref, s_ref, cb_ref, sb_ref, *, n, rows, transposed):
    period = 4 * n
    scale = 2.0 * math.pi / period
    col = _col_iota((1, n))

    @pl.when(pl.program_id(0) == 0)
    def _():
        r2 = _row_iota((DFT_GROUP, n))
        c2 = _col_iota((DFT_GROUP, n))
        ph = (r2 * (2 * c2 + 1)) if transposed else ((2 * r2 + 1) * c2)
        ang = (ph & (period - 1)).astype(F32) * scale
        cb_ref[...] = jnp.cos(ang)
        sb_ref[...] = jnp.sin(ang)

    for g in range(rows // DFT_GROUP):
        r1 = pl.program_id(0) * (rows // DFT_GROUP) + g
        ph = (DFT_GROUP * r1) * (2 * col + 1) if transposed else (2 * DFT_GROUP * r1) * col
        ang = (ph & (period - 1)).astype(F32) * scale
        ca = jnp.cos(ang)
        sa = jnp.sin(ang)
        cb = cb_ref[...]
        sb = sb_ref[...]
        c_ref[g * DFT_GROUP:(g + 1) * DFT_GROUP, :] = (ca * cb - sa * sb).astype(BF16)
        s_ref[g * DFT_GROUP:(g + 1) * DFT_GROUP, :] = (sa * cb + ca * sb).astype(BF16)


def _dft_tables(n, transposed):
    rows = min(n, 256)
    spec = pl.BlockSpec((rows, n), lambda i: (i, 0))
    return pl.pallas_call(
        functools.partial(_dft_table_kernel, n=n, rows=rows, transposed=transposed),
        grid=(n // rows,),
        in_specs=[],
        out_specs=[spec, spec],
        out_shape=[jax.ShapeDtypeStruct((n, n), BF16)] * 2,
        scratch_shapes=[pltpu.VMEM((DFT_GROUP, n), F32), pltpu.VMEM((DFT_GROUP, n), F32)],
        compiler_params=_cparams(("arbitrary",)),
        name="dft_tables",
    )()


def _hy_fwd_kernel(c_ref, s_ref, zz_ref, kc_ref, ks_ref, a_ref, b_ref):
    zz = zz_ref[0]
    uc = jnp.dot(c_ref[...], zz, preferred_element_type=F32)
    us = jnp.dot(s_ref[...], zz, preferred_element_type=F32)
    kc = kc_ref[...]
    ks = ks_ref[...]
    a_ref[0] = (uc * kc - us * ks).astype(BF16)
    b_ref[0] = (uc * ks + us * kc).astype(BF16)


def _hy_forward(ctab, stab, zz, kc, ks):
    bsz, n, w = zz.shape
    tf = min(n, 256)
    tab = pl.BlockSpec((tf, n), lambda b, i: (i, 0))
    kspec = pl.BlockSpec((tf, w), lambda b, i: (i, 0))
    ospec = pl.BlockSpec((1, tf, w), lambda b, i: (b, i, 0))
    return pl.pallas_call(
        _hy_fwd_kernel,
        grid=(bsz, n // tf),
        in_specs=[tab, tab, pl.BlockSpec((1, n, w), lambda b, i: (b, 0, 0)), kspec, kspec],
        out_specs=[ospec, ospec],
        out_shape=[jax.ShapeDtypeStruct((bsz, n, w), BF16)] * 2,
        compiler_params=_cparams(("parallel", "parallel")),
        name="hy_forward",
    )(ctab, stab, zz, kc, ks)


def _hy_inv_kernel(ct_ref, st_ref, a_ref, b_ref, x0_ref, zz_ref, bias_ref, o_ref, *, n):
    y = (jnp.dot(ct_ref[...], a_ref[0], preferred_element_type=F32)
         + jnp.dot(st_ref[...], b_ref[0], preferred_element_type=F32)) * (1.0 / n)
    zz = zz_ref[0].astype(F32)
    o_ref[0] = (x0_ref[0].astype(F32) * (y + zz * bias_ref[...])).astype(BF16)


def _hy_inverse(cttab, sttab, a, bq, x0, zz, bias):
    bsz, n, w = zz.shape
    tt = min(n, 256)
    tn = 512
    tab = pl.BlockSpec((tt, n), lambda b, j, i: (i, 0))
    full = pl.BlockSpec((1, n, tn), lambda b, j, i: (b, 0, j))
    tile = pl.BlockSpec((1, tt, tn), lambda b, j, i: (b, i, j))
    return pl.pallas_call(
        functools.partial(_hy_inv_kernel, n=n),
        grid=(bsz, w // tn, n // tt),
        in_specs=[tab, tab, full, full, tile, tile, pl.BlockSpec((1, tn), lambda b, j, i: (0, j))],
        out_specs=tile,
        out_shape=jax.ShapeDtypeStruct((bsz, n, w), BF16),
        compiler_params=_cparams(("parallel", "parallel", "parallel")),
        name="hy_inverse",
    )(cttab, sttab, a, bq, x0, zz, bias.reshape(1, w))


def _merge_kernel(odn_ref, ohy_ref, olru_ref, gdn_ref, ghy_ref, glru_ref, wdn_ref, why_ref, wlru_ref,
                  wout_ref, x_ref, gate_ref, o_ref):
    m = _sigmoid(gdn_ref[0].astype(F32)) * jnp.dot(odn_ref[0], wdn_ref[...], preferred_element_type=F32)
    m = m + _sigmoid(ghy_ref[0].astype(F32)) * jnp.dot(ohy_ref[0], why_ref[...], preferred_element_type=F32)
    m = m + _sigmoid(glru_ref[0].astype(F32)) * jnp.dot(olru_ref[0], wlru_ref[...], preferred_element_type=F32)
    y = jnp.dot(m.astype(BF16), wout_ref[...], preferred_element_type=F32)
    o_ref[0] = x_ref[0] + gate_ref[0] * y


def _merge(o_dn, o_hy, o_lru, proj, w_dn, w_hy, w_lru, w_out, x, gate):
    bsz, n, d = x.shape
    tm = min(n, 256)
    nblk = d // 1024
    act = pl.BlockSpec((1, tm, d), lambda b, i: (b, i, 0))
    gspec = lambda k: pl.BlockSpec((1, tm, d), lambda b, i: (b, i, GATE_BLK * LANE // d + k * nblk))
    wspec = pl.BlockSpec((d, d), lambda b, i: (0, 0))
    return pl.pallas_call(
        _merge_kernel,
        grid=(bsz, n // tm),
        in_specs=[act, act, act, gspec(0), gspec(1), gspec(2), wspec, wspec, wspec, wspec, act,
                  pl.BlockSpec((1, 1, d), lambda b, i: (b, 0, 0))],
        out_specs=act,
        out_shape=jax.ShapeDtypeStruct((bsz, n, d), F32),
        compiler_params=_cparams(("parallel", "parallel")),
        name="merge",
    )(o_dn, o_hy, o_lru, proj, proj, proj, w_dn, w_hy, w_lru, w_out, x, gate)


def _grid_conv(u, w_ref, rows, cols):
    n = u.shape[0]
    t = _row_iota(u.shape)
    c = _mod_pow2(t, cols)
    r = _div_pow2(t, cols)
    left = jnp.where(c >= 1, pltpu.roll(u, 1, 0), 0.0)
    right = jnp.where(c <= cols - 2, pltpu.roll(u, n - 1, 0), 0.0)
    acc = None
    for di in (-1, 0, 1):
        if rows == 1 and di != 0:
            continue
        k = 3 * (di + 1)
        inner = left * w_ref[k:k + 1, :] + u * w_ref[k + 1:k + 2, :] + right * w_ref[k + 2:k + 3, :]
        if di != 0:
            inner = jnp.where((r + di >= 0) & (r + di < rows), pltpu.roll(inner, (-di * cols) % n, 0), 0.0)
        acc = inner if acc is None else acc + inner
    return acc


def _ffn_act_kernel(ug_ref, uv_ref, wg_ref, wv_ref, o_ref, *, rows, cols):
    g = _grid_conv(ug_ref[0].astype(F32), wg_ref, rows, cols)
    v = _grid_conv(uv_ref[0].astype(F32), wv_ref, rows, cols)
    o_ref[0] = (_silu(g) * v).astype(BF16)


def _ffn_act(u, conv_w, rows, cols):
    bsz, n, _ = u.shape
    nblk = FFN_HIDDEN // LANE
    return pl.pallas_call(
        functools.partial(_ffn_act_kernel, rows=rows, cols=cols),
        grid=(bsz, nblk),
        in_specs=[pl.BlockSpec((1, n, LANE), lambda b, j: (b, 0, j)),
                  pl.BlockSpec((1, n, LANE), lambda b, j: (b, 0, nblk + j)),
                  pl.BlockSpec((9, LANE), lambda b, j: (0, j)),
                  pl.BlockSpec((9, LANE), lambda b, j: (0, nblk + j))],
        out_specs=pl.BlockSpec((1, n, LANE), lambda b, j: (b, 0, j)),
        out_shape=jax.ShapeDtypeStruct((bsz, n, FFN_HIDDEN), BF16),
        compiler_params=_cparams(("parallel", "parallel")),
        name="ffn_act",
    )(u, u, conv_w, conv_w)


def _final_norm_kernel(x_ref, g_ref, o_ref):
    x = x_ref[0]
    o_ref[0] = x * lax.rsqrt(jnp.mean(x * x, axis=-1, keepdims=True) + NORM_EPS) * g_ref[...]


def _final_norm(x, gain):
    bsz, n, d = x.shape
    tm = min(n, 512)
    spec = pl.BlockSpec((1, tm, d), lambda b, i: (b, i, 0))
    return pl.pallas_call(
        _final_norm_kernel,
        grid=(bsz, n // tm),
        in_specs=[spec, pl.BlockSpec((1, d), lambda b, i: (0, 0))],
        out_specs=spec,
        out_shape=jax.ShapeDtypeStruct((bsz, n, d), F32),
        compiler_params=_cparams(("parallel", "parallel")),
        name="final_norm",
    )(x, gain.reshape(1, d))


def _gate_column_map():
    src = np.full((LANE,), -1, np.int32)
    isdec = np.zeros((1, LANE), np.float32)
    dirs = np.zeros((LANE,), np.int32)
    for h in range(DN_HEADS):
        for slot, (d, kind) in enumerate(((0, 0), (0, 1), (1, 0), (1, 1), (0, 0), (1, 0))):
            src[h * SUBLANE + slot] = d * 2 * DN_HEADS + kind * DN_HEADS + h
            isdec[0, h * SUBLANE + slot] = 1.0 if kind == 0 else 0.0
            dirs[h * SUBLANE + slot] = d
    return src, isdec, dirs


def _split_in_proj(w_in):
    o = np.cumsum((3 * DN_WIDTH, DN_WIDTH, 4 * DN_HEADS, 3 * HY_WIDTH, LRU_WIDTH, LRU_WIDTH)).tolist()
    w_main = jnp.concatenate([w_in[:, :o[1]], w_in[:, o[2]:]], axis=1).astype(BF16)
    src, _, _ = _gate_column_map()
    w_ab = w_in[:, o[1]:o[2]]
    w_gate = jnp.where(jnp.asarray(src >= 0)[None, :], w_ab[:, np.maximum(src, 0)], 0.0).astype(BF16)
    return w_main, w_gate


def _gate_params(a_log, dt_bias):
    _, isdec, dirs = _gate_column_map()
    head = (np.arange(LANE) // SUBLANE).astype(np.int32)
    alog_c = a_log[dirs, head].reshape(1, LANE)
    dtb_c = dt_bias[dirs, head].reshape(1, LANE)
    return alog_c, dtb_c, jnp.asarray(isdec)


def _token_mixer(x, mods, lp, states, tables, with_output):
    bsz, n, _ = x.shape
    shift1, scale1, gate1 = mods[0], mods[1], mods[2]
    proj = _norm_mod_matmul(x, lp["norm1_g"], shift1, scale1, lp["w_main"], BF16, 2048, 1024)
    ab = _norm_mod_matmul(x, lp["norm1_g"], shift1, scale1, lp["w_gate"], F32, 2048, LANE)

    qkv_rm, qkv_tr = _dn_prep(proj, lp["dn_conv_w"])
    gates = _dn_gates(ab, *lp["gate_params"])
    o_dn, s_f, s_b = _delta_net(qkv_rm, qkv_tr, gates, proj, lp["dn_norm_g"], states[0], states[1], with_output)
    o_lru, h_last = _rglru(proj, lp["lru_conv_w"], lp["lru_conv_b"], lp["lru_w_a"], lp["lru_b_a"],
                           lp["lru_w_x"], lp["lru_b_x"], lp["lru_lambda"], states[2], with_output)
    new_states = (s_f, s_b, h_last)
    if not with_output:
        return None, new_states

    ctab, stab, cttab, sttab = tables
    hsum, hdiff = _hy_filter(n, lp["hy_w1"], lp["hy_b1"], lp["hy_f1"], lp["hy_w2"], lp["hy_b2"], lp["hy_f2"],
                             lp["hy_w3"])
    tk = min(n, 512)
    kc = _matmul(ctab, hsum, F32, tk, 512)
    ks = _matmul(stab, hdiff, F32, tk, 512)
    x0, zz = _hy_prep(proj, lp["hy_conv_w"], lp["hy_conv_b"])
    a, bq = _hy_forward(ctab, stab, zz, kc, ks)
    o_hy = _hy_inverse(cttab, sttab, a, bq, x0, zz, lp["hy_bias"])

    x = _merge(o_dn, o_hy, o_lru, proj, lp["w_proj_dn"], lp["w_proj_hy"], lp["w_proj_lru"], lp["w_out"], x, gate1)
    return x, new_states


def _conv_ffn(x, mods, lp, rows, cols):
    u = _norm_mod_matmul(x, lp["norm2_g"], mods[3], mods[4], lp["ffn_up"], BF16, 1024, FFN_HIDDEN)
    act = _ffn_act(u, lp["ffn_conv_w"], rows, cols)
    return _matmul_residual(act, lp["ffn_down"], x, mods[5])


def kernel(x, c, ctx, c_ctx, w_mod, b_mod, norm1_g, norm2_g, w_in, dn_conv_w, dn_a_log, dn_dt_bias, dn_norm_g,
           hy_conv_w, hy_conv_b, hy_w1, hy_b1, hy_f1, hy_w2, hy_b2, hy_f2, hy_w3, hy_bias,
           lru_conv_w, lru_conv_b, lru_w_a, lru_b_a, lru_w_x, lru_b_x, lru_lambda,
           w_proj_dn, w_proj_hy, w_proj_lru, w_out, ffn_up, ffn_conv_w, ffn_down, final_norm_g):
    bsz, n_lat, d = x.shape
    n_ctx = ctx.shape[1]
    depth = w_in.shape[0]
    rows = n_lat // GRID_W

    cvec = jnp.zeros((SUBLANE, d), F32).at[:bsz].set(c).at[bsz].set(c_ctx)
    lat_tables = _dft_tables(n_lat, False) + _dft_tables(n_lat, True)
    ctx_tables = _dft_tables(n_ctx, False) + _dft_tables(n_ctx, True)
    zero_states = (jnp.zeros((bsz, DN_HEADS, LANE, LANE), F32), jnp.zeros((bsz, DN_HEADS, LANE, LANE), F32),
                   jnp.zeros((bsz, 2, LRU_WIDTH), F32))

    xc = ctx
    for l in range(depth):
        ctx_needed = l < depth - 1
        w_main, w_gate = _split_in_proj(w_in[l])
        lp = dict(
            norm1_g=norm1_g[l], norm2_g=norm2_g[l], w_main=w_main, w_gate=w_gate,
            dn_conv_w=dn_conv_w[l], gate_params=_gate_params(dn_a_log[l], dn_dt_bias[l]), dn_norm_g=dn_norm_g[l],
            hy_conv_w=hy_conv_w[l], hy_conv_b=hy_conv_b[l], hy_w1=hy_w1[l], hy_b1=hy_b1[l], hy_f1=hy_f1[l],
            hy_w2=hy_w2[l], hy_b2=hy_b2[l], hy_f2=hy_f2[l], hy_w3=hy_w3[l], hy_bias=hy_bias[l],
            lru_conv_w=lru_conv_w[l], lru_conv_b=lru_conv_b[l], lru_w_a=lru_w_a[l].astype(BF16),
            lru_b_a=lru_b_a[l], lru_w_x=lru_w_x[l].astype(BF16), lru_b_x=lru_b_x[l], lru_lambda=lru_lambda[l],
            w_proj_dn=w_proj_dn[l].astype(BF16), w_proj_hy=w_proj_hy[l].astype(BF16),
            w_proj_lru=w_proj_lru[l].astype(BF16), w_out=w_out[l].astype(BF16),
            ffn_up=ffn_up[l].astype(BF16), ffn_conv_w=ffn_conv_w[l].reshape(9, 2 * FFN_HIDDEN),
            ffn_down=ffn_down[l].astype(BF16))
        mod = _modulation(cvec, w_mod[l].astype(BF16), b_mod[l])
        lat_mod = [mod[:bsz, k * d:(k + 1) * d].reshape(bsz, 1, d) for k in range(N_MOD)]
        ctx_mod = [jnp.broadcast_to(mod[bsz:bsz + 1, k * d:(k + 1) * d].reshape(1, 1, d), (bsz, 1, d))
                   for k in range(N_MOD)]

        xc_new, ctx_states = _token_mixer(xc, ctx_mod, lp, zero_states, ctx_tables, ctx_needed)
        x, _ = _token_mixer(x, lat_mod, lp, ctx_states, lat_tables, True)
        x = _conv_ffn(x, lat_mod, lp, rows, GRID_W)
        if ctx_needed:
            xc = _conv_ffn(xc_new, ctx_mod, lp, 1, n_ctx)
    return _final_norm(x, final_norm_g)
```

```python
import functools
import math

import numpy as np
import jax
import jax.numpy as jnp
from jax import lax
from jax.experimental import pallas as pl
from jax.experimental.pallas import tpu as pltpu

F32 = jnp.float32
BF16 = jnp.bfloat16
HIGHEST = lax.Precision.HIGHEST

D_MODEL = 1024
DEPTH = 2
GRID_W = 64
NORM_EPS = 1e-6
N_MOD = 6

DN_HEADS = 8
DN_HEAD_DIM = 128
DN_WIDTH = DN_HEADS * DN_HEAD_DIM
HY_WIDTH = 1024
HY_EMB = 33
HY_BANDS = (HY_EMB - 1) // 2
HY_FILTER_HIDDEN = 64
HY_FAST_DECAY_PCT = 0.3
HY_SLOW_DECAY_PCT = 1.5
HY_DECAY_TARGET = 1e-2
LRU_WIDTH = 1024
LRU_BLOCKS = 8
LRU_BLOCK = LRU_WIDTH // LRU_BLOCKS
LRU_C = 8.0
FFN_HIDDEN = 2816

LANE = 128
SUBLANE = 8
TILE = 256
CHUNK = 128
DN_HEADS_PER_STEP = 2
HY_BLOCK = 1024
LRU_ROWS = 256
MIB = 1024 * 1024

QKV_BLK = 0
Z_BLK = 24
HY_BLK = 32
LX_BLK = 56
LY_BLK = 64
GATE_BLK = 72
N_MAIN = 96 * LANE


def _cparams(sem, vmem_mib=48):
    return pltpu.CompilerParams(dimension_semantics=sem, vmem_limit_bytes=vmem_mib * MIB)


def _sigmoid(x):
    return 1.0 / (1.0 + jnp.exp(-x))


def _silu(x):
    return x * _sigmoid(x)


def _softplus(x):
    return jnp.maximum(x, 0.0) + jnp.log(1.0 + jnp.exp(-jnp.abs(x)))


def _row_iota(shape):
    return lax.broadcasted_iota(jnp.int32, shape, 0)


def _col_iota(shape):
    return lax.broadcasted_iota(jnp.int32, shape, 1)


def _div_pow2(x, k):
    assert k & (k - 1) == 0
    return x >> (k.bit_length() - 1)


def _mod_pow2(x, k):
    assert k & (k - 1) == 0
    return x & (k - 1)


def _bdot(a, b):
    return jnp.dot(a.astype(BF16), b.astype(BF16), preferred_element_type=F32)


def _mod_kernel(c_ref, w_ref, b_ref, o_ref):
    o_ref[...] = _bdot(_silu(c_ref[...]), w_ref[...]) + b_ref[...]


def _modulation(cvec, w_mod, b_mod):
    n = w_mod.shape[1]
    tn = 1024
    return pl.pallas_call(
        _mod_kernel,
        grid=(n // tn,),
        in_specs=[pl.BlockSpec((SUBLANE, D_MODEL), lambda j: (0, 0)),
                  pl.BlockSpec((D_MODEL, tn), lambda j: (0, j)),
                  pl.BlockSpec((1, tn), lambda j: (0, j))],
        out_specs=pl.BlockSpec((SUBLANE, tn), lambda j: (0, j)),
        out_shape=jax.ShapeDtypeStruct((SUBLANE, n), F32),
        compiler_params=_cparams(("parallel",)),
        name="modulation",
    )(cvec, w_mod, b_mod.reshape(1, n))


def _nmm_kernel(x_ref, g_ref, sh_ref, sc_ref, w_ref, o_ref, h_ref):
    @pl.when(pl.program_id(2) == 0)
    def _():
        x = x_ref[0]
        y = x * lax.rsqrt(jnp.mean(x * x, axis=-1, keepdims=True) + NORM_EPS) * g_ref[...]
        h_ref[...] = (y * (1.0 + sc_ref[0]) + sh_ref[0]).astype(BF16)

    o_ref[0] = jnp.dot(h_ref[...], w_ref[...], preferred_element_type=F32).astype(o_ref.dtype)


def _norm_mod_matmul(x, gain, shift, scale, w, out_dtype, tm, tn):
    bsz, n, d = x.shape
    nout = w.shape[1]
    tm = min(n, tm)
    return pl.pallas_call(
        _nmm_kernel,
        grid=(bsz, n // tm, nout // tn),
        in_specs=[pl.BlockSpec((1, tm, d), lambda b, i, j: (b, i, 0)),
                  pl.BlockSpec((1, d), lambda b, i, j: (0, 0)),
                  pl.BlockSpec((1, 1, d), lambda b, i, j: (b, 0, 0)),
                  pl.BlockSpec((1, 1, d), lambda b, i, j: (b, 0, 0)),
                  pl.BlockSpec((d, tn), lambda b, i, j: (0, j))],
        out_specs=pl.BlockSpec((1, tm, tn), lambda b, i, j: (b, i, j)),
        out_shape=jax.ShapeDtypeStruct((bsz, n, nout), out_dtype),
        scratch_shapes=[pltpu.VMEM((tm, d), BF16)],
        compiler_params=_cparams(("parallel", "parallel", "arbitrary")),
        name="norm_mod_matmul",
    )(x, gain.reshape(1, d), shift, scale, w)


def _mm_kernel(a_ref, b_ref, o_ref):
    o_ref[...] = jnp.dot(a_ref[...], b_ref[...], preferred_element_type=F32).astype(o_ref.dtype)


def _matmul(a, b, out_dtype, tm, tn):
    m, k = a.shape
    n = b.shape[1]
    return pl.pallas_call(
        _mm_kernel,
        grid=(m // tm, n // tn),
        in_specs=[pl.BlockSpec((tm, k), lambda i, j: (i, 0)),
                  pl.BlockSpec((k, tn), lambda i, j: (0, j))],
        out_specs=pl.BlockSpec((tm, tn), lambda i, j: (i, j)),
        out_shape=jax.ShapeDtypeStruct((m, n), out_dtype),
        compiler_params=_cparams(("parallel", "parallel")),
        name="matmul",
    )(a, b)


def _mm_res_kernel(a_ref, b_ref, x_ref, g_ref, o_ref):
    y = jnp.dot(a_ref[0], b_ref[...], preferred_element_type=F32)
    o_ref[0] = x_ref[0] + g_ref[0] * y


def _matmul_residual(a, w, x, gate):
    bsz, n, k = a.shape
    d = w.shape[1]
    tm = min(n, 512)
    return pl.pallas_call(
        _mm_res_kernel,
        grid=(bsz, n // tm),
        in_specs=[pl.BlockSpec((1, tm, k), lambda b, i: (b, i, 0)),
                  pl.BlockSpec((k, d), lambda b, i: (0, 0)),
                  pl.BlockSpec((1, tm, d), lambda b, i: (b, i, 0)),
                  pl.BlockSpec((1, 1, d), lambda b, i: (b, 0, 0))],
        out_specs=pl.BlockSpec((1, tm, d), lambda b, i: (b, i, 0)),
        out_shape=jax.ShapeDtypeStruct((bsz, n, d), F32),
        compiler_params=_cparams(("parallel", "parallel")),
        name="matmul_residual",
    )(a, w, x, gate)


def _shift_rows(x, off):
    n = x.shape[0]
    if off == 0:
        return x
    rolled = pltpu.roll(x, (-off) % n, 0)
    t = _row_iota(x.shape)
    valid = (t + off >= 0) & (t + off < n)
    return jnp.where(valid, rolled, 0.0)


def _dwconv_rows(x, w_ref, k):
    left = (k - 1) // 2
    acc = None
    for j in range(k):
        term = _shift_rows(x, j - left) * w_ref[j:j + 1, :]
        acc = term if acc is None else acc + term
    return acc


def _dn_prep_kernel(p_ref, w_ref, rm_ref, tr_ref, *, n_tiles):
    c = pl.program_id(1)
    x = p_ref[0].astype(F32)
    y = _silu(_dwconv_rows(x, w_ref, 4))
    nrm = y * lax.rsqrt(jnp.sum(y * y, axis=-1, keepdims=True) + 1e-6)
    nrm = nrm * jnp.where(c < DN_HEADS, DN_HEAD_DIM ** -0.5, 1.0)
    y = jnp.where(c < 2 * DN_HEADS, nrm, y)
    rm_ref[0] = y.astype(BF16)
    for t in range(n_tiles):
        tr_ref[0, 0, t] = y[t * TILE:(t + 1) * TILE, :].T.astype(BF16)


def _dn_prep(proj, conv_w):
    bsz, n, _ = proj.shape
    nt = n // TILE
    nc = 3 * DN_HEADS
    return pl.pallas_call(
        functools.partial(_dn_prep_kernel, n_tiles=nt),
        grid=(bsz, nc),
        in_specs=[pl.BlockSpec((1, n, LANE), lambda b, c: (b, 0, QKV_BLK + c)),
                  pl.BlockSpec((4, LANE), lambda b, c: (0, c))],
        out_specs=[pl.BlockSpec((1, n, LANE), lambda b, c: (b, 0, c)),
                   pl.BlockSpec((1, 1, nt, LANE, TILE), lambda b, c: (b, c, 0, 0, 0))],
        out_shape=[jax.ShapeDtypeStruct((bsz, n, nc * LANE), BF16),
                   jax.ShapeDtypeStruct((bsz, nc, nt, LANE, TILE), BF16)],
        compiler_params=_cparams(("parallel", "parallel")),
        name="dn_prep",
    )(proj, conv_w)


def _dn_gate_kernel(ab_ref, alog_ref, dtb_ref, isdec_ref, o_ref):
    x = ab_ref[0]
    dec = -jnp.exp(alog_ref[...]) * _softplus(x + dtb_ref[...])
    e = jnp.where(isdec_ref[...] > 0.5, dec, _sigmoid(x))
    et = e.T
    s = _row_iota((TILE, TILE))
    t = _col_iota((TILE, TILE))
    same = _div_pow2(s, CHUNK) == _div_pow2(t, CHUNK)
    prefix = jnp.where(same & (s <= t), 1.0, 0.0)
    suffix = jnp.where(same & (s >= t), 1.0, 0.0)
    total = jnp.where(same, 1.0, 0.0)
    pre = jnp.dot(et, prefix, precision=HIGHEST, preferred_element_type=F32)
    suf = jnp.dot(et, suffix, precision=HIGHEST, preferred_element_type=F32)
    tot = jnp.dot(et, total, precision=HIGHEST, preferred_element_type=F32)
    slot = _mod_pow2(_row_iota((LANE, TILE)), SUBLANE)
    o_ref[0, 0] = jnp.where(slot == 0, pre, jnp.where(slot == 2, suf, jnp.where(slot >= 4, tot, et)))


def _dn_gates(ab, alog_c, dtb_c, isdec_c):
    bsz, n, _ = ab.shape
    nt = n // TILE
    vec = pl.BlockSpec((1, LANE), lambda b, i: (0, 0))
    return pl.pallas_call(
        _dn_gate_kernel,
        grid=(bsz, nt),
        in_specs=[pl.BlockSpec((1, TILE, LANE), lambda b, i: (b, i, 0)), vec, vec, vec],
        out_specs=pl.BlockSpec((1, 1, LANE, TILE), lambda b, i: (b, i, 0, 0)),
        out_shape=jax.ShapeDtypeStruct((bsz, nt, LANE, TILE), F32),
        compiler_params=_cparams(("parallel", "parallel")),
        name="dn_gates",
    )(ab, alog_c, dtb_c, isdec_c)


def _dn_tiles(chains):
    a = _row_iota((TILE, TILE))
    b = _col_iota((TILE, TILE))
    same = _div_pow2(a, CHUNK) == _div_pow2(b, CHUNK)
    apart = a ^ b
    n_chunks = TILE // CHUNK
    dot = functools.partial(jnp.dot, preferred_element_type=F32)

    kk = [dot(c["k_rm"], c["kt"]) for c in chains]
    kq = [dot(c["k_rm"], c["qt"]) for c in chains]
    xs, ps, attn = [], [], []
    for c, kk_c, kq_c in zip(chains, kk, kq):
        incl = same & ((a >= b) if c["backward"] else (a <= b))
        gcb = jnp.broadcast_to(c["gc"], (TILE, TILE))
        diff = gcb - gcb.T
        decay = jnp.where(incl, jnp.exp(jnp.where(incl, diff, 0.0)), 0.0)
        attn.append((kq_c * decay).astype(BF16))
        x = jnp.where(a == b, 0.0, -(kk_c * decay * c["beta"]))
        xs.append(x)
        ps.append(jnp.where(a == b, 1.0, jnp.where(apart == 1, x, 0.0)))
    s = 2
    while s < CHUNK:
        couple = (apart >> (s.bit_length() - 1)) == 1
        pb = [p.astype(BF16) for p in ps]
        px = [dot(pb_c, jnp.where(couple, x, 0.0).astype(BF16)) for pb_c, x in zip(pb, xs)]
        ps = [p + dot(px_c.astype(BF16), pb_c) for p, px_c, pb_c in zip(ps, px, pb)]
        s *= 2
    t_inv = [p.astype(BF16) for p in ps]
    egc = [jnp.exp(c["gc"]) for c in chains]
    u_t = [dot((c["vt"].astype(F32) * c["beta"]).astype(BF16), t) for c, t in zip(chains, t_inv)]
    w_t = [dot((c["kt"].astype(F32) * (c["beta"] * e)).astype(BF16), t).astype(BF16)
           for c, e, t in zip(chains, egc, t_inv)]
    qd_t = [(c["qt"].astype(F32) * e).astype(BF16) for c, e in zip(chains, egc)]
    kdec = [jnp.exp(c["tot"] - c["gc"]) for c in chains]
    outs = [[None] * n_chunks for _ in chains]
    for step in range(n_chunks):
        cis = [(n_chunks - 1 - step) if c["backward"] else step for c in chains]
        sl = [slice(ci * CHUNK, (ci + 1) * CHUNK) for ci in cis]
        st = [c["st_ref"][...] for c in chains]
        stb = [s_c.astype(BF16) for s_c in st]
        sw = [dot(stb_c, w_c[:, r]) for stb_c, w_c, r in zip(stb, w_t, sl)]
        sq = [dot(stb_c, q_c[:, r]) for stb_c, q_c, r in zip(stb, qd_t, sl)]
        vn = [u_c[:, r] - sw_c for u_c, sw_c, r in zip(u_t, sw, sl)]
        av = [dot(vn_c.astype(BF16), at_c[r, r]) for vn_c, at_c, r in zip(vn, attn, sl)]
        upd = [dot((vn_c * kd_c[:, r]).astype(BF16), c["k_rm"][r, :])
               for vn_c, kd_c, c, r in zip(vn, kdec, chains, sl)]
        for idx, c in enumerate(chains):
            c["st_ref"][...] = st[idx] * jnp.exp(c["tot"][:, sl[idx]]) + upd[idx]
            outs[idx][cis[idx]] = sq[idx] + av[idx]
    return [jnp.concatenate(o, axis=1) for o in outs]


def _dn_kernel(k_ref, qt_ref, kt_ref, vt_ref, g_ref, z_ref, ng_ref, s0f_ref, s0b_ref,
               o_ref, sf_ref, sb_ref, ot_ref, st_ref, *, n_tiles, with_output):
    hb = DN_HEADS_PER_STEP
    for hh in range(hb):
        st_ref[2 * hh] = s0f_ref[0, hh]
        st_ref[2 * hh + 1] = s0b_ref[0, hh]

    def body(i, carry):
        nf = i
        nb = n_tiles - 1 - i
        chains = []
        for hh in range(hb):
            for backward, n in ((False, nf), (True, nb)):
                g = g_ref[0, n, hh * SUBLANE:(hh + 1) * SUBLANE, :]
                base = 2 if backward else 0
                chains.append(dict(
                    k_rm=k_ref[0, pl.ds(pl.multiple_of(n * TILE, TILE), TILE), hh * LANE:(hh + 1) * LANE],
                    qt=qt_ref[0, hh, n], kt=kt_ref[0, hh, n], vt=vt_ref[0, hh, n],
                    gc=g[base:base + 1, :], beta=g[base + 1:base + 2, :], tot=g[4 + base // 2:5 + base // 2, :],
                    st_ref=st_ref.at[2 * hh + (1 if backward else 0)], backward=backward, hh=hh, n=n))
        o_t = _dn_tiles(chains)
        if with_output:
            for c, o_c in zip(chains, o_t):
                ot_ref[c["hh"], c["n"]] = ot_ref[c["hh"], c["n"]] + o_c
        return carry

    if with_output:
        ot_ref[...] = jnp.zeros_like(ot_ref)
    lax.fori_loop(0, n_tiles, body, 0)
    for hh in range(hb):
        sf_ref[0, hh] = st_ref[2 * hh]
        sb_ref[0, hh] = st_ref[2 * hh + 1]
    if with_output:
        for hh in range(hb):
            for t in range(n_tiles):
                o = ot_ref[hh, t].T
                y = o * lax.rsqrt(jnp.mean(o * o, axis=-1, keepdims=True) + NORM_EPS) * ng_ref[...]
                z = z_ref[0, t * TILE:(t + 1) * TILE, hh * LANE:(hh + 1) * LANE].astype(F32)
                o_ref[0, t * TILE:(t + 1) * TILE, hh * LANE:(hh + 1) * LANE] = (y * _silu(z)).astype(BF16)
    else:
        o_ref[...] = jnp.zeros_like(o_ref)


def _delta_net(qkv_rm, qkv_tr, gates, proj, norm_g, s0f, s0b, with_output):
    bsz, n, _ = qkv_rm.shape
    nt = n // TILE
    h = DN_HEADS
    hb = DN_HEADS_PER_STEP
    wide = hb * LANE
    tr_spec = lambda off: pl.BlockSpec((1, hb, nt, LANE, TILE), lambda b, j: (b, off + j, 0, 0, 0))
    st_spec = pl.BlockSpec((1, hb, LANE, LANE), lambda b, j: (b, j, 0, 0))
    n_out = n if with_output else SUBLANE
    return pl.pallas_call(
        functools.partial(_dn_kernel, n_tiles=nt, with_output=with_output),
        grid=(bsz, h // hb),
        in_specs=[pl.BlockSpec((1, n, wide), lambda b, j: (b, 0, h // hb + j)),
                  tr_spec(0), tr_spec(h // hb), tr_spec(2 * h // hb),
                  pl.BlockSpec((1, nt, hb * SUBLANE, TILE), lambda b, j: (b, 0, j, 0)),
                  pl.BlockSpec((1, n, wide), lambda b, j: (b, 0, Z_BLK // hb + j)),
                  pl.BlockSpec((1, LANE), lambda b, j: (0, 0)),
                  st_spec, st_spec],
        out_specs=[pl.BlockSpec((1, n_out, wide), lambda b, j: (b, 0, j)), st_spec, st_spec],
        out_shape=[jax.ShapeDtypeStruct((bsz, n_out, DN_WIDTH), BF16),
                   jax.ShapeDtypeStruct((bsz, h, LANE, LANE), F32),
                   jax.ShapeDtypeStruct((bsz, h, LANE, LANE), F32)],
        scratch_shapes=[pltpu.VMEM((hb, nt, LANE, TILE), F32),
                        pltpu.VMEM((2 * hb, LANE, LANE), F32)],
        compiler_params=_cparams(("parallel", "parallel")),
        name="delta_net",
    )(qkv_rm, qkv_tr, qkv_tr, qkv_tr, gates, proj, norm_g.reshape(1, LANE), s0f, s0b)


def _lru_scan_block(x, wa, ba, wx, bx, spl, h_in, backward):
    rows = x.shape[0]
    xb = x.astype(BF16)
    r = _sigmoid(jnp.dot(xb, wa, preferred_element_type=F32) + ba)
    gi = _sigmoid(jnp.dot(xb, wx, preferred_element_type=F32) + bx)
    log_a = -LRU_C * r * spl
    a = jnp.exp(log_a)
    b = jnp.sqrt(1.0 - jnp.exp(2.0 * log_a)) * (gi * x)
    groups = rows // SUBLANE
    a = a.reshape(groups, SUBLANE, LANE)
    b = b.reshape(groups, SUBLANE, LANE)
    sub = lax.broadcasted_iota(jnp.int32, a.shape, 1)
    s = 1
    while s < SUBLANE:
        if backward:
            keep = sub < SUBLANE - s
            a_sh = jnp.where(keep, pltpu.roll(a, SUBLANE - s, 1), 1.0)
            b_sh = jnp.where(keep, pltpu.roll(b, SUBLANE - s, 1), 0.0)
        else:
            keep = sub >= s
            a_sh = jnp.where(keep, pltpu.roll(a, s, 1), 1.0)
            b_sh = jnp.where(keep, pltpu.roll(b, s, 1), 0.0)
        b = a * b_sh + b
        a = a * a_sh
        s *= 2
    a = a.reshape(rows, LANE)
    b = b.reshape(rows, LANE)
    pieces = [None] * groups
    carry = h_in
    order = range(groups - 1, -1, -1) if backward else range(groups)
    edge = 0 if backward else SUBLANE - 1
    for gidx in order:
        lo = gidx * SUBLANE
        hgrp = b[lo:lo + SUBLANE, :] + a[lo:lo + SUBLANE, :] * carry
        pieces[gidx] = hgrp
        carry = hgrp[edge:edge + 1, :]
    return jnp.concatenate(pieces, axis=0), carry


def _lru_kernel(px_ref, py_ref, cw_ref, cb_ref, wa_ref, ba_ref, wx_ref, bx_ref, lam_ref, h0_ref,
                o_ref, last_ref, xs_ref, hs_ref, *, n_blocks, with_output):
    x = px_ref[0].astype(F32)
    xs_ref[...] = _dwconv_rows(x, cw_ref, 4) + cb_ref[...]
    spl = _softplus(-lam_ref[...])

    def body(i, carry):
        hf, hb = carry
        rf = pl.multiple_of(i * LRU_ROWS, LRU_ROWS)
        rb = pl.multiple_of((n_blocks - 1 - i) * LRU_ROWS, LRU_ROWS)
        h_f, hf = _lru_scan_block(xs_ref[pl.ds(rf, LRU_ROWS), :], wa_ref[0, 0], ba_ref[0:1, :],
                                  wx_ref[0, 0], bx_ref[0:1, :], spl[0:1, :], hf, False)
        h_b, hb = _lru_scan_block(xs_ref[pl.ds(rb, LRU_ROWS), :], wa_ref[1, 0], ba_ref[1:2, :],
                                  wx_ref[1, 0], bx_ref[1:2, :], spl[1:2, :], hb, True)
        if with_output:
            hs_ref[pl.ds(rf, LRU_ROWS), :] = hs_ref[pl.ds(rf, LRU_ROWS), :] + h_f
            hs_ref[pl.ds(rb, LRU_ROWS), :] = hs_ref[pl.ds(rb, LRU_ROWS), :] + h_b
        return hf, hb

    if with_output:
        hs_ref[...] = jnp.zeros_like(hs_ref)
    h0 = h0_ref[0]
    hf, hb = lax.fori_loop(0, n_blocks, body, (h0[0:1, :], h0[1:2, :]))
    last_ref[0] = jnp.concatenate([hf, hb], axis=0)
    if with_output:
        y = py_ref[0].astype(F32)
        gelu = 0.5 * y * (1.0 + jnp.tanh(math.sqrt(2.0 / math.pi) * (y + 0.044715 * (y * y * y))))
        o_ref[0] = (hs_ref[...] * gelu).astype(BF16)
    else:
        o_ref[...] = jnp.zeros_like(o_ref)


def _rglru(proj, conv_w, conv_b, w_a, b_a, w_x, b_x, lam, h0, with_output):
    bsz, n, _ = proj.shape
    rows = min(n, LRU_ROWS)
    nb = n // rows
    n_out = n if with_output else SUBLANE
    vec2 = pl.BlockSpec((2, LANE), lambda b, j: (0, j))
    wspec = pl.BlockSpec((2, 1, LRU_BLOCK, LRU_BLOCK), lambda b, j: (0, j, 0, 0))
    return pl.pallas_call(
        functools.partial(_lru_kernel, n_blocks=nb, with_output=with_output),
        grid=(bsz, LRU_BLOCKS),
        in_specs=[pl.BlockSpec((1, n, LANE), lambda b, j: (b, 0, LX_BLK + j)),
                  pl.BlockSpec((1, n, LANE), lambda b, j: (b, 0, LY_BLK + j)),
                  pl.BlockSpec((4, LANE), lambda b, j: (0, j)),
                  pl.BlockSpec((1, LANE), lambda b, j: (0, j)),
                  wspec, vec2, wspec, vec2, vec2,
                  pl.BlockSpec((1, 2, LANE), lambda b, j: (b, 0, j))],
        out_specs=[pl.BlockSpec((1, n_out, LANE), lambda b, j: (b, 0, j)),
                   pl.BlockSpec((1, 2, LANE), lambda b, j: (b, 0, j))],
        out_shape=[jax.ShapeDtypeStruct((bsz, n_out, LRU_WIDTH), BF16),
                   jax.ShapeDtypeStruct((bsz, 2, LRU_WIDTH), F32)],
        scratch_shapes=[pltpu.VMEM((n, LANE), F32), pltpu.VMEM((n, LANE), F32)],
        compiler_params=_cparams(("parallel", "parallel")),
        name="rglru",
    )(proj, proj, conv_w, conv_b.reshape(1, LRU_WIDTH), w_a, b_a, w_x, b_x, lam, h0)


def _hy_prep_kernel(p0_ref, p1_ref, pv_ref, w0_ref, w1_ref, wv_ref, b0_ref, b1_ref, bv_ref, x0_ref, zz_ref):
    x0 = _dwconv_rows(p0_ref[0].astype(F32), w0_ref, 3) + b0_ref[...]
    x1 = _dwconv_rows(p1_ref[0].astype(F32), w1_ref, 3) + b1_ref[...]
    v = _dwconv_rows(pv_ref[0].astype(F32), wv_ref, 3) + bv_ref[...]
    x0_ref[0] = x0.astype(BF16)
    zz_ref[0] = (x1 * v).astype(BF16)


def _hy_prep(proj, conv_w, conv_b):
    bsz, n, _ = proj.shape
    nblk = HY_WIDTH // LANE
    pspec = lambda off: pl.BlockSpec((1, n, LANE), lambda b, j: (b, 0, HY_BLK + off + j))
    wspec = lambda off: pl.BlockSpec((3, LANE), lambda b, j: (0, off + j))
    bspec = lambda off: pl.BlockSpec((1, LANE), lambda b, j: (0, off + j))
    ospec = pl.BlockSpec((1, n, LANE), lambda b, j: (b, 0, j))
    cb = conv_b.reshape(1, 3 * HY_WIDTH)
    return pl.pallas_call(
        _hy_prep_kernel,
        grid=(bsz, nblk),
        in_specs=[pspec(0), pspec(nblk), pspec(2 * nblk), wspec(0), wspec(nblk), wspec(2 * nblk),
                  bspec(0), bspec(nblk), bspec(2 * nblk)],
        out_specs=[ospec, ospec],
        out_shape=[jax.ShapeDtypeStruct((bsz, n, HY_WIDTH), BF16)] * 2,
        compiler_params=_cparams(("parallel", "parallel")),
        name="hy_prep",
    )(proj, proj, proj, conv_w, conv_w, conv_w, cb, cb, cb)


def _hy_filter_kernel(w1_ref, b1_ref, f1_ref, w2_ref, b2_ref, f2_ref, w3_ref, band_ref, delta_ref,
                      k_ref, *, n, rows):
    def lag(shape):
        p = _row_iota(shape) + (pl.program_id(0) * rows - n)
        return p, jnp.abs(p).astype(F32)

    _, i = lag((rows, LANE))
    lane = _col_iota((rows, LANE))
    t = i * (1.0 / (n - 1))
    ang = band_ref[...] * (i * (2.0 * math.pi / n))
    feat = jnp.where(lane == 0, t,
                     jnp.where(lane <= HY_BANDS, jnp.cos(ang), jnp.where(lane < HY_EMB, -jnp.sin(ang), 0.0)))
    hid = jnp.sin(f1_ref[...] * (jnp.dot(feat, w1_ref[...], precision=HIGHEST,
                                         preferred_element_type=F32) + b1_ref[...]))
    hid = jnp.sin(f2_ref[...] * (jnp.dot(hid, w2_ref[...], precision=HIGHEST,
                                         preferred_element_type=F32) + b2_ref[...]))
    filt = jnp.dot(hid, w3_ref[...], precision=HIGHEST, preferred_element_type=F32)
    p, iw = lag((rows, HY_WIDTH))
    dec = jnp.exp(-(iw * (1.0 / (n - 1))) * delta_ref[...])
    h_f = filt[:, :HY_WIDTH] * dec
    h_b = filt[:, HY_WIDTH:] * dec
    taps = jnp.where(p > 0, h_f, jnp.where(p < 0, h_b, h_f + h_b))
    k_ref[...] = jnp.where(p == -n, 0.0, taps).astype(BF16)


def _hy_filter(n, w1, b1, f1, w2, b2, f2, w3):
    rows = min(n, 256)
    hid = HY_FILTER_HIDDEN
    w1p = jnp.zeros((LANE, hid), F32).at[:HY_EMB].set(w1)
    bands = np.zeros((1, LANE), np.float32)
    base = np.linspace(1e-4, HY_BANDS - 1, HY_BANDS, dtype=np.float32)
    bands[0, 1:1 + HY_BANDS] = base
    bands[0, 1 + HY_BANDS:HY_EMB] = base
    log_target = math.log(HY_DECAY_TARGET)
    deltas = np.abs(np.linspace(log_target / HY_SLOW_DECAY_PCT, log_target / HY_FAST_DECAY_PCT, HY_WIDTH,
                                dtype=np.float32)).reshape(1, HY_WIDTH)
    full = lambda shape: pl.BlockSpec(shape, lambda i: (0,) * len(shape))
    return pl.pallas_call(
        functools.partial(_hy_filter_kernel, n=n, rows=rows),
        grid=(2 * n // rows,),
        in_specs=[full((LANE, hid)), full((1, hid)), full((1, hid)), full((hid, hid)), full((1, hid)),
                  full((1, hid)), full((hid, 2 * HY_WIDTH)), full((1, LANE)), full((1, HY_WIDTH))],
        out_specs=pl.BlockSpec((rows, HY_WIDTH), lambda i: (i, 0)),
        out_shape=jax.ShapeDtypeStruct((2 * n, HY_WIDTH), BF16),
        compiler_params=_cparams(("parallel",)),
        name="hy_filter",
    )(w1p, b1.reshape(1, hid), f1.reshape(1, hid), w2, b2.reshape(1, hid), f2.reshape(1, hid), w3,
      jnp.asarray(bands), jnp.asarray(deltas))


DFT_GROUP = 64


def _dft_table_kernel(c_ref, s_ref, cb_ref, sb_ref, *, m, ncols, col0, rows, transposed, blank_first):
    period = 4 * m
    scale = 2.0 * math.pi / period
    col = _col_iota((1, ncols)) + col0

    @pl.when(pl.program_id(0) == 0)
    def _():
        r2 = _row_iota((DFT_GROUP, ncols))
        c2 = _col_iota((DFT_GROUP, ncols)) + col0
        ph = (r2 * (2 * c2 + 1)) if transposed else ((2 * r2 + 1) * c2)
        ang = (ph & (period - 1)).astype(F32) * scale
        cb_ref[...] = jnp.cos(ang)
        sb_ref[...] = jnp.sin(ang)

    for g in range(rows // DFT_GROUP):
        r1 = pl.program_id(0) * (rows // DFT_GROUP) + g
        ph = (DFT_GROUP * r1) * (2 * col + 1) if transposed else (2 * DFT_GROUP * r1) * col
        ang = (ph & (period - 1)).astype(F32) * scale
        ca = jnp.cos(ang)
        sa = jnp.sin(ang)
        cb = cb_ref[...]
        sb = sb_ref[...]
        c_tile = ca * cb - sa * sb
        s_tile = sa * cb + ca * sb
        if blank_first:
            first = _col_iota((DFT_GROUP, ncols)) == 0
            c_tile = jnp.where(first, 0.0, c_tile)
            s_tile = jnp.where(first, 0.0, s_tile)
        c_ref[g * DFT_GROUP:(g + 1) * DFT_GROUP, :] = c_tile.astype(BF16)
        s_ref[g * DFT_GROUP:(g + 1) * DFT_GROUP, :] = s_tile.astype(BF16)


def _dft_tables(m, transposed=False, two_sided=False):
    rows = min(m, 256)
    ncols = 2 * m if two_sided else m
    spec = pl.BlockSpec((rows, ncols), lambda i: (i, 0))
    return pl.pallas_call(
        functools.partial(_dft_table_kernel, m=m, ncols=ncols, col0=3 * m if two_sided else 0, rows=rows,
                          transposed=transposed, blank_first=two_sided),
        grid=(m // rows,),
        in_specs=[],
        out_specs=[spec, spec],
        out_shape=[jax.ShapeDtypeStruct((m, ncols), BF16)] * 2,
        scratch_shapes=[pltpu.VMEM((DFT_GROUP, ncols), F32), pltpu.VMEM((DFT_GROUP, ncols), F32)],
        compiler_params=_cparams(("arbitrary",)),
        name="dft_tables",
    )()


def _hy_tables(m):
    return _dft_tables(m) + _dft_tables(m, transposed=True) + _dft_tables(m, two_sided=True)


def _hy_ktrans_kernel(cw_ref, sw_ref, lo_ref, hi_ref, kc_ref, ks_ref, *, m):
    dot = functools.partial(jnp.dot, preferred_element_type=F32)
    lo = lo_ref[...]
    hi = hi_ref[...]
    kc_ref[0] = dot(cw_ref[:, :m], lo) + dot(cw_ref[:, m:], hi)
    ks_ref[0] = dot(sw_ref[:, :m], lo) + dot(sw_ref[:, m:], hi)


def _hy_ktrans(cw, sw, taps, m):
    nd = taps.shape[0] // m - 1
    w = taps.shape[1]
    tn = 512
    tab = pl.BlockSpec((m, 2 * m), lambda e, j: (0, 0))
    ospec = pl.BlockSpec((1, m, tn), lambda e, j: (e, 0, j))
    return pl.pallas_call(
        functools.partial(_hy_ktrans_kernel, m=m),
        grid=(nd, w // tn),
        in_specs=[tab, tab, pl.BlockSpec((m, tn), lambda e, j: (e, j)), pl.BlockSpec((m, tn), lambda e, j: (e + 1, j))],
        out_specs=[ospec, ospec],
        out_shape=[jax.ShapeDtypeStruct((nd, m, w), F32)] * 2,
        compiler_params=_cparams(("parallel", "parallel")),
        name="hy_ktrans",
    )(cw, sw, taps, taps)


def _hy_fwd_kernel(c_ref, s_ref, zz_ref, kc_ref, ks_ref, a_ref, b_ref, *, m, nb):
    dot = functools.partial(jnp.dot, preferred_element_type=F32)
    c = c_ref[...]
    s = s_ref[...]
    uc = [dot(c, zz_ref[0, j * m:(j + 1) * m, :]) for j in range(nb)]
    us = [dot(s, zz_ref[0, j * m:(j + 1) * m, :]) for j in range(nb)]
    for i in range(nb):
        acc_a = acc_b = None
        for j in range(nb):
            kc = kc_ref[i - j + nb - 1]
            ks = ks_ref[i - j + nb - 1]
            ta = uc[j] * kc - us[j] * ks
            tb = uc[j] * ks + us[j] * kc
            acc_a = ta if acc_a is None else acc_a + ta
            acc_b = tb if acc_b is None else acc_b + tb
        a_ref[0, i] = acc_a.astype(BF16)
        b_ref[0, i] = acc_b.astype(BF16)


def _hy_forward(ctab, stab, zz, kc, ks, m):
    bsz, n, w = zz.shape
    nb = n // m
    nd = 2 * nb - 1
    tf = min(m, 256)
    tn = 512
    tab = pl.BlockSpec((tf, m), lambda b, j, i: (i, 0))
    kspec = pl.BlockSpec((nd, tf, tn), lambda b, j, i: (0, i, j))
    ospec = pl.BlockSpec((1, nb, tf, tn), lambda b, j, i: (b, 0, i, j))
    return pl.pallas_call(
        functools.partial(_hy_fwd_kernel, m=m, nb=nb),
        grid=(bsz, w // tn, m // tf),
        in_specs=[tab, tab, pl.BlockSpec((1, n, tn), lambda b, j, i: (b, 0, j)), kspec, kspec],
        out_specs=[ospec, ospec],
        out_shape=[jax.ShapeDtypeStruct((bsz, nb, m, w), BF16)] * 2,
        compiler_params=_cparams(("parallel", "parallel", "parallel")),
        name="hy_forward",
    )(ctab, stab, zz, kc, ks)


def _hy_inv_kernel(ct_ref, st_ref, a_ref, b_ref, x0_ref, zz_ref, bias_ref, o_ref, *, m):
    y = (jnp.dot(ct_ref[...], a_ref[0, 0], preferred_element_type=F32)
         + jnp.dot(st_ref[...], b_ref[0, 0], preferred_element_type=F32)) * (1.0 / m)
    zz = zz_ref[0].astype(F32)
    o_ref[0] = (x0_ref[0].astype(F32) * (y + zz * bias_ref[...])).astype(BF16)


def _hy_inverse(cttab, sttab, a, bq, x0, zz, bias, m):
    bsz, n, w = zz.shape
    nb = n // m
    tn = 512
    tab = pl.BlockSpec((m, m), lambda b, i, j: (0, 0))
    spec4 = pl.BlockSpec((1, 1, m, tn), lambda b, i, j: (b, i, 0, j))
    tile = pl.BlockSpec((1, m, tn), lambda b, i, j: (b, i, j))
    return pl.pallas_call(
        functools.partial(_hy_inv_kernel, m=m),
        grid=(bsz, nb, w // tn),
        in_specs=[tab, tab, spec4, spec4, tile, tile, pl.BlockSpec((1, tn), lambda b, i, j: (0, j))],
        out_specs=tile,
        out_shape=jax.ShapeDtypeStruct((bsz, n, w), BF16),
        compiler_params=_cparams(("parallel", "parallel", "parallel")),
        name="hy_inverse",
    )(cttab, sttab, a, bq, x0, zz, bias.reshape(1, w))


def _merge_kernel(odn_ref, ohy_ref, olru_ref, gdn_ref, ghy_ref, glru_ref, wdn_ref, why_ref, wlru_ref,
                  wout_ref, x_ref, gate_ref, o_ref):
    m = _sigmoid(gdn_ref[0].astype(F32)) * jnp.dot(odn_ref[0], wdn_ref[...], preferred_element_type=F32)
    m = m + _sigmoid(ghy_ref[0].astype(F32)) * jnp.dot(ohy_ref[0], why_ref[...], preferred_element_type=F32)
    m = m + _sigmoid(glru_ref[0].astype(F32)) * jnp.dot(olru_ref[0], wlru_ref[...], preferred_element_type=F32)
    y = jnp.dot(m.astype(BF16), wout_ref[...], preferred_element_type=F32)
    o_ref[0] = x_ref[0] + gate_ref[0] * y


def _merge(o_dn, o_hy, o_lru, proj, w_dn, w_hy, w_lru, w_out, x, gate):
    bsz, n, d = x.shape
    tm = min(n, 256)
    nblk = d // 1024
    act = pl.BlockSpec((1, tm, d), lambda b, i: (b, i, 0))
    gspec = lambda k: pl.BlockSpec((1, tm, d), lambda b, i: (b, i, GATE_BLK * LANE // d + k * nblk))
    wspec = pl.BlockSpec((d, d), lambda b, i: (0, 0))
    return pl.pallas_call(
        _merge_kernel,
        grid=(bsz, n // tm),
        in_specs=[act, act, act, gspec(0), gspec(1), gspec(2), wspec, wspec, wspec, wspec, act,
                  pl.BlockSpec((1, 1, d), lambda b, i: (b, 0, 0))],
        out_specs=act,
        out_shape=jax.ShapeDtypeStruct((bsz, n, d), F32),
        compiler_params=_cparams(("parallel", "parallel")),
        name="merge",
    )(o_dn, o_hy, o_lru, proj, proj, proj, w_dn, w_hy, w_lru, w_out, x, gate)


def _grid_conv(u, w_ref, rows, cols):
    n = u.shape[0]
    t = _row_iota(u.shape)
    c = _mod_pow2(t, cols)
    r = _div_pow2(t, cols)
    left = jnp.where(c >= 1, pltpu.roll(u, 1, 0), 0.0)
    right = jnp.where(c <= cols - 2, pltpu.roll(u, n - 1, 0), 0.0)
    acc = None
    for di in (-1, 0, 1):
        if rows == 1 and di != 0:
            continue
        k = 3 * (di + 1)
        inner = left * w_ref[k:k + 1, :] + u * w_ref[k + 1:k + 2, :] + right * w_ref[k + 2:k + 3, :]
        if di != 0:
            inner = jnp.where((r + di >= 0) & (r + di < rows), pltpu.roll(inner, (-di * cols) % n, 0), 0.0)
        acc = inner if acc is None else acc + inner
    return acc


def _ffn_act_kernel(ug_ref, uv_ref, wg_ref, wv_ref, o_ref, *, rows, cols):
    g = _grid_conv(ug_ref[0].astype(F32), wg_ref, rows, cols)
    v = _grid_conv(uv_ref[0].astype(F32), wv_ref, rows, cols)
    o_ref[0] = (_silu(g) * v).astype(BF16)


def _ffn_act(u, conv_w, rows, cols):
    bsz, n, _ = u.shape
    nblk = FFN_HIDDEN // LANE
    return pl.pallas_call(
        functools.partial(_ffn_act_kernel, rows=rows, cols=cols),
        grid=(bsz, nblk),
        in_specs=[pl.BlockSpec((1, n, LANE), lambda b, j: (b, 0, j)),
                  pl.BlockSpec((1, n, LANE), lambda b, j: (b, 0, nblk + j)),
                  pl.BlockSpec((9, LANE), lambda b, j: (0, j)),
                  pl.BlockSpec((9, LANE), lambda b, j: (0, nblk + j))],
        out_specs=pl.BlockSpec((1, n, LANE), lambda b, j: (b, 0, j)),
        out_shape=jax.ShapeDtypeStruct((bsz, n, FFN_HIDDEN), BF16),
        compiler_params=_cparams(("parallel", "parallel")),
        name="ffn_act",
    )(u, u, conv_w, conv_w)


def _final_norm_kernel(x_ref, g_ref, o_ref):
    x = x_ref[0]
    o_ref[0] = x * lax.rsqrt(jnp.mean(x * x, axis=-1, keepdims=True) + NORM_EPS) * g_ref[...]


def _final_norm(x, gain):
    bsz, n, d = x.shape
    tm = min(n, 512)
    spec = pl.BlockSpec((1, tm, d), lambda b, i: (b, i, 0))
    return pl.pallas_call(
        _final_norm_kernel,
        grid=(bsz, n // tm),
        in_specs=[spec, pl.BlockSpec((1, d), lambda b, i: (0, 0))],
        out_specs=spec,
        out_shape=jax.ShapeDtypeStruct((bsz, n, d), F32),
        compiler_params=_cparams(("parallel", "parallel")),
        name="final_norm",
    )(x, gain.reshape(1, d))


def _gate_column_map():
    src = np.full((LANE,), -1, np.int32)
    isdec = np.zeros((1, LANE), np.float32)
    dirs = np.zeros((LANE,), np.int32)
    for h in range(DN_HEADS):
        for slot, (d, kind) in enumerate(((0, 0), (0, 1), (1, 0), (1, 1), (0, 0), (1, 0))):
            src[h * SUBLANE + slot] = d * 2 * DN_HEADS + kind * DN_HEADS + h
            isdec[0, h * SUBLANE + slot] = 1.0 if kind == 0 else 0.0
            dirs[h * SUBLANE + slot] = d
    return src, isdec, dirs


def _split_in_proj(w_in):
    o = np.cumsum((3 * DN_WIDTH, DN_WIDTH, 4 * DN_HEADS, 3 * HY_WIDTH, LRU_WIDTH, LRU_WIDTH)).tolist()
    w_main = jnp.concatenate([w_in[:, :o[1]], w_in[:, o[2]:]], axis=1).astype(BF16)
    src, _, _ = _gate_column_map()
    w_ab = w_in[:, o[1]:o[2]]
    w_gate = jnp.where(jnp.asarray(src >= 0)[None, :], w_ab[:, np.maximum(src, 0)], 0.0).astype(BF16)
    return w_main, w_gate


def _gate_params(a_log, dt_bias):
    _, isdec, dirs = _gate_column_map()
    head = (np.arange(LANE) // SUBLANE).astype(np.int32)
    alog_c = a_log[dirs, head].reshape(1, LANE)
    dtb_c = dt_bias[dirs, head].reshape(1, LANE)
    return alog_c, dtb_c, jnp.asarray(isdec)


def _token_mixer(x, mods, lp, states, tables, with_output):
    bsz, n, _ = x.shape
    shift1, scale1, gate1 = mods[0], mods[1], mods[2]
    proj = _norm_mod_matmul(x, lp["norm1_g"], shift1, scale1, lp["w_main"], BF16, 2048, 1024)
    ab = _norm_mod_matmul(x, lp["norm1_g"], shift1, scale1, lp["w_gate"], F32, 2048, LANE)

    qkv_rm, qkv_tr = _dn_prep(proj, lp["dn_conv_w"])
    gates = _dn_gates(ab, *lp["gate_params"])
    o_dn, s_f, s_b = _delta_net(qkv_rm, qkv_tr, gates, proj, lp["dn_norm_g"], states[0], states[1], with_output)
    o_lru, h_last = _rglru(proj, lp["lru_conv_w"], lp["lru_conv_b"], lp["lru_w_a"], lp["lru_b_a"],
                           lp["lru_w_x"], lp["lru_b_x"], lp["lru_lambda"], states[2], with_output)
    new_states = (s_f, s_b, h_last)
    if not with_output:
        return None, new_states

    m = min(n, HY_BLOCK)
    ctab, stab, cttab, sttab, cwtab, swtab = tables
    taps = _hy_filter(n, lp["hy_w1"], lp["hy_b1"], lp["hy_f1"], lp["hy_w2"], lp["hy_b2"], lp["hy_f2"], lp["hy_w3"])
    kc, ks = _hy_ktrans(cwtab, swtab, taps, m)
    x0, zz = _hy_prep(proj, lp["hy_conv_w"], lp["hy_conv_b"])
    a, bq = _hy_forward(ctab, stab, zz, kc, ks, m)
    o_hy = _hy_inverse(cttab, sttab, a, bq, x0, zz, lp["hy_bias"], m)

    x = _merge(o_dn, o_hy, o_lru, proj, lp["w_proj_dn"], lp["w_proj_hy"], lp["w_proj_lru"], lp["w_out"], x, gate1)
    return x, new_states


def _conv_ffn(x, mods, lp, rows, cols):
    u = _norm_mod_matmul(x, lp["norm2_g"], mods[3], mods[4], lp["ffn_up"], BF16, 1024, FFN_HIDDEN)
    act = _ffn_act(u, lp["ffn_conv_w"], rows, cols)
    return _matmul_residual(act, lp["ffn_down"], x, mods[5])


def kernel(x, c, ctx, c_ctx, w_mod, b_mod, norm1_g, norm2_g, w_in, dn_conv_w, dn_a_log, dn_dt_bias, dn_norm_g,
           hy_conv_w, hy_conv_b, hy_w1, hy_b1, hy_f1, hy_w2, hy_b2, hy_f2, hy_w3, hy_bias,
           lru_conv_w, lru_conv_b, lru_w_a, lru_b_a, lru_w_x, lru_b_x, lru_lambda,
           w_proj_dn, w_proj_hy, w_proj_lru, w_out, ffn_up, ffn_conv_w, ffn_down, final_norm_g):
    bsz, n_lat, d = x.shape
    n_ctx = ctx.shape[1]
    depth = w_in.shape[0]
    rows = n_lat // GRID_W

    cvec = jnp.zeros((SUBLANE, d), F32).at[:bsz].set(c).at[bsz].set(c_ctx)
    lat_tables = _hy_tables(min(n_lat, HY_BLOCK))
    ctx_tables = _hy_tables(min(n_ctx, HY_BLOCK))
    zero_states = (jnp.zeros((bsz, DN_HEADS, LANE, LANE), F32), jnp.zeros((bsz, DN_HEADS, LANE, LANE), F32),
                   jnp.zeros((bsz, 2, LRU_WIDTH), F32))

    xc = ctx
    for l in range(depth):
        ctx_needed = l < depth - 1
        w_main, w_gate = _split_in_proj(w_in[l])
        lp = dict(
            norm1_g=norm1_g[l], norm2_g=norm2_g[l], w_main=w_main, w_gate=w_gate,
            dn_conv_w=dn_conv_w[l], gate_params=_gate_params(dn_a_log[l], dn_dt_bias[l]), dn_norm_g=dn_norm_g[l],
            hy_conv_w=hy_conv_w[l], hy_conv_b=hy_conv_b[l], hy_w1=hy_w1[l], hy_b1=hy_b1[l], hy_f1=hy_f1[l],
            hy_w2=hy_w2[l], hy_b2=hy_b2[l], hy_f2=hy_f2[l], hy_w3=hy_w3[l], hy_bias=hy_bias[l],
            lru_conv_w=lru_conv_w[l], lru_conv_b=lru_conv_b[l], lru_w_a=lru_w_a[l].astype(BF16),
            lru_b_a=lru_b_a[l], lru_w_x=lru_w_x[l].astype(BF16), lru_b_x=lru_b_x[l], lru_lambda=lru_lambda[l],
            w_proj_dn=w_proj_dn[l].astype(BF16), w_proj_hy=w_proj_hy[l].astype(BF16),
            w_proj_lru=w_proj_lru[l].astype(BF16), w_out=w_out[l].astype(BF16),
            ffn_up=ffn_up[l].astype(BF16), ffn_conv_w=ffn_conv_w[l].reshape(9, 2 * FFN_HIDDEN),
            ffn_down=ffn_down[l].astype(BF16))
        mod = _modulation(cvec, w_mod[l].astype(BF16), b_mod[l])
        lat_mod = [mod[:bsz, k * d:(k + 1) * d].reshape(bsz, 1, d) for k in range(N_MOD)]
        ctx_mod = [jnp.broadcast_to(mod[bsz:bsz + 1, k * d:(k + 1) * d].reshape(1, 1, d), (bsz, 1, d))
                   for k in range(N_MOD)]

        xc_new, ctx_states = _token_mixer(xc, ctx_mod, lp, zero_states, ctx_tables, ctx_needed)
        x, _ = _token_mixer(x, lat_mod, lp, ctx_states, lat_tables, True)
        x = _conv_ffn(x, lat_mod, lp, rows, GRID_W)
        if ctx_needed:
            xc = _conv_ffn(xc_new, ctx_mod, lp, 1, n_ctx)
    return _final_norm(x, final_norm_g)
```

```python
import functools
import math

import numpy as np
import jax
import jax.numpy as jnp
from jax import lax
from jax.experimental import pallas as pl
from jax.experimental.pallas import tpu as pltpu

F32 = jnp.float32
BF16 = jnp.bfloat16
HIGHEST = lax.Precision.HIGHEST

D_MODEL = 1024
DEPTH = 2
GRID_W = 64
NORM_EPS = 1e-6
N_MOD = 6

DN_HEADS = 8
DN_HEAD_DIM = 128
DN_WIDTH = DN_HEADS * DN_HEAD_DIM
HY_WIDTH = 1024
HY_EMB = 33
HY_BANDS = (HY_EMB - 1) // 2
HY_FILTER_HIDDEN = 64
HY_FAST_DECAY_PCT = 0.3
HY_SLOW_DECAY_PCT = 1.5
HY_DECAY_TARGET = 1e-2
LRU_WIDTH = 1024
LRU_BLOCKS = 8
LRU_BLOCK = LRU_WIDTH // LRU_BLOCKS
LRU_C = 8.0
FFN_HIDDEN = 2816

LANE = 128
SUBLANE = 8
TILE = 256
CHUNK = 128
DN_HEADS_PER_STEP = 4
DN_PACK = 64
HY_BLOCK = 1024
LRU_ROWS = 256
LRU_GROUP = 2
MIB = 1024 * 1024

QKV_BLK = 0
Z_BLK = 24
HY_BLK = 32
LX_BLK = 56
LY_BLK = 64
GATE_BLK = 72
N_MAIN = 96 * LANE


def _cparams(sem, vmem_mib=48):
    return pltpu.CompilerParams(dimension_semantics=sem, vmem_limit_bytes=vmem_mib * MIB)


def _sigmoid(x):
    return 1.0 / (1.0 + jnp.exp(-x))


def _silu(x):
    return x * _sigmoid(x)


def _softplus(x):
    return jnp.maximum(x, 0.0) + jnp.log(1.0 + jnp.exp(-jnp.abs(x)))


def _row_iota(shape):
    return lax.broadcasted_iota(jnp.int32, shape, 0)


def _col_iota(shape):
    return lax.broadcasted_iota(jnp.int32, shape, 1)


def _div_pow2(x, k):
    assert k & (k - 1) == 0
    return x >> (k.bit_length() - 1)


def _mod_pow2(x, k):
    assert k & (k - 1) == 0
    return x & (k - 1)


def _bdot(a, b):
    return jnp.dot(a.astype(BF16), b.astype(BF16), preferred_element_type=F32)


def _mod_kernel(c_ref, w_ref, b_ref, o_ref):
    o_ref[...] = _bdot(_silu(c_ref[...]), w_ref[...]) + b_ref[...]


def _modulation(cvec, w_mod, b_mod):
    n = w_mod.shape[1]
    tn = 1024
    return pl.pallas_call(
        _mod_kernel,
        grid=(n // tn,),
        in_specs=[pl.BlockSpec((SUBLANE, D_MODEL), lambda j: (0, 0)),
                  pl.BlockSpec((D_MODEL, tn), lambda j: (0, j)),
                  pl.BlockSpec((1, tn), lambda j: (0, j))],
        out_specs=pl.BlockSpec((SUBLANE, tn), lambda j: (0, j)),
        out_shape=jax.ShapeDtypeStruct((SUBLANE, n), F32),
        compiler_params=_cparams(("parallel",)),
        name="modulation",
    )(cvec, w_mod, b_mod.reshape(1, n))


def _nmm_kernel(x_ref, g_ref, sh_ref, sc_ref, w_ref, o_ref, h_ref):
    @pl.when(pl.program_id(2) == 0)
    def _():
        x = x_ref[0]
        y = x * lax.rsqrt(jnp.mean(x * x, axis=-1, keepdims=True) + NORM_EPS) * g_ref[...]
        h_ref[...] = (y * (1.0 + sc_ref[0]) + sh_ref[0]).astype(BF16)

    o_ref[0] = jnp.dot(h_ref[...], w_ref[...], preferred_element_type=F32).astype(o_ref.dtype)


def _norm_mod_matmul(x, gain, shift, scale, w, out_dtype, tm, tn):
    bsz, n, d = x.shape
    nout = w.shape[1]
    tm = min(n, tm)
    return pl.pallas_call(
        _nmm_kernel,
        grid=(bsz, n // tm, nout // tn),
        in_specs=[pl.BlockSpec((1, tm, d), lambda b, i, j: (b, i, 0)),
                  pl.BlockSpec((1, d), lambda b, i, j: (0, 0)),
                  pl.BlockSpec((1, 1, d), lambda b, i, j: (b, 0, 0)),
                  pl.BlockSpec((1, 1, d), lambda b, i, j: (b, 0, 0)),
                  pl.BlockSpec((d, tn), lambda b, i, j: (0, j))],
        out_specs=pl.BlockSpec((1, tm, tn), lambda b, i, j: (b, i, j)),
        out_shape=jax.ShapeDtypeStruct((bsz, n, nout), out_dtype),
        scratch_shapes=[pltpu.VMEM((tm, d), BF16)],
        compiler_params=_cparams(("parallel", "parallel", "arbitrary")),
        name="norm_mod_matmul",
    )(x, gain.reshape(1, d), shift, scale, w)


def _mm_kernel(a_ref, b_ref, o_ref):
    o_ref[...] = jnp.dot(a_ref[...], b_ref[...], preferred_element_type=F32).astype(o_ref.dtype)


def _matmul(a, b, out_dtype, tm, tn):
    m, k = a.shape
    n = b.shape[1]
    return pl.pallas_call(
        _mm_kernel,
        grid=(m // tm, n // tn),
        in_specs=[pl.BlockSpec((tm, k), lambda i, j: (i, 0)),
                  pl.BlockSpec((k, tn), lambda i, j: (0, j))],
        out_specs=pl.BlockSpec((tm, tn), lambda i, j: (i, j)),
        out_shape=jax.ShapeDtypeStruct((m, n), out_dtype),
        compiler_params=_cparams(("parallel", "parallel")),
        name="matmul",
    )(a, b)


def _mm_res_kernel(a_ref, b_ref, x_ref, g_ref, o_ref):
    y = jnp.dot(a_ref[0], b_ref[...], preferred_element_type=F32)
    o_ref[0] = x_ref[0] + g_ref[0] * y


def _matmul_residual(a, w, x, gate):
    bsz, n, k = a.shape
    d = w.shape[1]
    tm = min(n, 512)
    return pl.pallas_call(
        _mm_res_kernel,
        grid=(bsz, n // tm),
        in_specs=[pl.BlockSpec((1, tm, k), lambda b, i: (b, i, 0)),
                  pl.BlockSpec((k, d), lambda b, i: (0, 0)),
                  pl.BlockSpec((1, tm, d), lambda b, i: (b, i, 0)),
                  pl.BlockSpec((1, 1, d), lambda b, i: (b, 0, 0))],
        out_specs=pl.BlockSpec((1, tm, d), lambda b, i: (b, i, 0)),
        out_shape=jax.ShapeDtypeStruct((bsz, n, d), F32),
        compiler_params=_cparams(("parallel", "parallel")),
        name="matmul_residual",
    )(a, w, x, gate)


def _shift_rows(x, off):
    n = x.shape[0]
    if off == 0:
        return x
    rolled = pltpu.roll(x, (-off) % n, 0)
    t = _row_iota(x.shape)
    valid = (t < n - off) if off > 0 else (t >= -off)
    return jnp.where(valid, rolled, 0.0)


def _dwconv_rows(x, w_ref, k):
    left = (k - 1) // 2
    acc = None
    for j in range(k):
        term = _shift_rows(x, j - left) * w_ref[j:j + 1, :]
        acc = term if acc is None else acc + term
    return acc


def _dn_prep_kernel(p_ref, w_ref, rm_ref, tr_ref, *, n_tiles):
    c = pl.program_id(1)
    x = p_ref[0].astype(F32)
    y = _silu(_dwconv_rows(x, w_ref, 4))
    nrm = y * lax.rsqrt(jnp.sum(y * y, axis=-1, keepdims=True) + 1e-6)
    nrm = nrm * jnp.where(c < DN_HEADS, DN_HEAD_DIM ** -0.5, 1.0)
    y = jnp.where(c < 2 * DN_HEADS, nrm, y)
    rm_ref[0] = y.astype(BF16)
    for t in range(n_tiles):
        tr_ref[0, 0, t] = y[t * TILE:(t + 1) * TILE, :].T.astype(BF16)


def _dn_prep(proj, conv_w):
    bsz, n, _ = proj.shape
    nt = n // TILE
    nc = 3 * DN_HEADS
    return pl.pallas_call(
        functools.partial(_dn_prep_kernel, n_tiles=nt),
        grid=(bsz, nc),
        in_specs=[pl.BlockSpec((1, n, LANE), lambda b, c: (b, 0, QKV_BLK + c)),
                  pl.BlockSpec((4, LANE), lambda b, c: (0, c))],
        out_specs=[pl.BlockSpec((1, n, LANE), lambda b, c: (b, 0, c)),
                   pl.BlockSpec((1, 1, nt, LANE, TILE), lambda b, c: (b, c, 0, 0, 0))],
        out_shape=[jax.ShapeDtypeStruct((bsz, n, nc * LANE), BF16),
                   jax.ShapeDtypeStruct((bsz, nc, nt, LANE, TILE), BF16)],
        compiler_params=_cparams(("parallel", "parallel")),
        name="dn_prep",
    )(proj, conv_w)


def _dn_gate_kernel(ab_ref, alog_ref, dtb_ref, isdec_ref, o_ref):
    x = ab_ref[0]
    dec = -jnp.exp(alog_ref[...]) * _softplus(x + dtb_ref[...])
    e = jnp.where(isdec_ref[...] > 0.5, dec, _sigmoid(x))
    et = e.T
    s = _row_iota((TILE, TILE))
    t = _col_iota((TILE, TILE))
    same = _div_pow2(s, CHUNK) == _div_pow2(t, CHUNK)
    prefix = jnp.where(same & (s <= t), 1.0, 0.0)
    suffix = jnp.where(same & (s >= t), 1.0, 0.0)
    total = jnp.where(same, 1.0, 0.0)
    pre = jnp.dot(et, prefix, precision=HIGHEST, preferred_element_type=F32)
    suf = jnp.dot(et, suffix, precision=HIGHEST, preferred_element_type=F32)
    tot = jnp.dot(et, total, precision=HIGHEST, preferred_element_type=F32)
    slot = _mod_pow2(_row_iota((LANE, TILE)), SUBLANE)
    o_ref[0, 0] = jnp.where(slot == 0, pre, jnp.where(slot == 2, suf, jnp.where(slot >= 4, tot, et)))


def _dn_gates(ab, alog_c, dtb_c, isdec_c):
    bsz, n, _ = ab.shape
    nt = n // TILE
    vec = pl.BlockSpec((1, LANE), lambda b, i: (0, 0))
    return pl.pallas_call(
        _dn_gate_kernel,
        grid=(bsz, nt),
        in_specs=[pl.BlockSpec((1, TILE, LANE), lambda b, i: (b, i, 0)), vec, vec, vec],
        out_specs=pl.BlockSpec((1, 1, LANE, TILE), lambda b, i: (b, i, 0, 0)),
        out_shape=jax.ShapeDtypeStruct((bsz, nt, LANE, TILE), F32),
        compiler_params=_cparams(("parallel", "parallel")),
        name="dn_gates",
    )(ab, alog_c, dtb_c, isdec_c)


def _dn_tiles(chains):
    a = _row_iota((TILE, TILE))
    b = _col_iota((TILE, TILE))
    same = _div_pow2(a, CHUNK) == _div_pow2(b, CHUNK)
    apart = a ^ b
    n_chunks = TILE // CHUNK
    dot = functools.partial(jnp.dot, preferred_element_type=F32)

    def stack(v, reps):
        return jnp.concatenate([v] * reps, axis=0)

    kk = [dot(c["k_rm"], c["kt"]) for c in chains]
    kq = [dot(c["k_rm"], c["qt"]) for c in chains]
    xs, pw, attn = [], [], []
    for c, kk_c, kq_c in zip(chains, kk, kq):
        incl = same & ((a >= b) if c["backward"] else (a <= b))
        gcb = jnp.broadcast_to(c["gc"], (TILE, TILE))
        diff = gcb - gcb.T
        decay = jnp.where(incl, jnp.exp(jnp.where(incl, diff, 0.0)), 0.0)
        attn.append((kq_c * decay).astype(BF16))
        x = jnp.where(a == b, 0.0, -(kk_c * decay * c["beta"]))
        xs.append(x)
        base = jnp.where(a == b, 1.0, jnp.where(apart == 1, x, 0.0))
        acc = base[0:DN_PACK, :]
        for r in range(1, TILE // DN_PACK):
            acc = acc + base[r * DN_PACK:(r + 1) * DN_PACK, :]
        pw.append(acc)
    pack = DN_PACK
    s = 2
    while s < CHUNK:
        if s == pack:
            keep = _div_pow2(_row_iota((2 * pack, TILE)), pack) == (_div_pow2(_col_iota((2 * pack, TILE)), pack) & 1)
            pw = [jnp.where(keep, stack(p, 2), 0.0) for p in pw]
            pack *= 2
        couple = (apart >> (s.bit_length() - 1)) == 1
        blocks = _div_pow2(a, pack) == _div_pow2(b, pack)
        pb = [p.astype(BF16) for p in pw]
        px = [dot(pb_c, jnp.where(couple, x, 0.0).astype(BF16)) for pb_c, x in zip(pb, xs)]
        p_bd = [jnp.where(blocks, stack(pb_c, TILE // pack), 0.0).astype(BF16) for pb_c in pb]
        pw = [p + dot(px_c.astype(BF16), bd_c) for p, px_c, bd_c in zip(pw, px, p_bd)]
        s *= 2
    blocks = _div_pow2(a, pack) == _div_pow2(b, pack)
    t_inv = [jnp.where(blocks, stack(p.astype(BF16), TILE // pack), 0.0).astype(BF16) for p in pw]
    egc = [jnp.exp(c["gc"]) for c in chains]
    u_t = [dot((c["vt"].astype(F32) * c["beta"]).astype(BF16), t) for c, t in zip(chains, t_inv)]
    w_t = [dot((c["kt"].astype(F32) * (c["beta"] * e)).astype(BF16), t).astype(BF16)
           for c, e, t in zip(chains, egc, t_inv)]
    qd_t = [(c["qt"].astype(F32) * e).astype(BF16) for c, e in zip(chains, egc)]
    kdec = [jnp.exp(c["tot"] - c["gc"]) for c in chains]
    outs = [[None] * n_chunks for _ in chains]
    for step in range(n_chunks):
        cis = [(n_chunks - 1 - step) if c["backward"] else step for c in chains]
        sl = [slice(ci * CHUNK, (ci + 1) * CHUNK) for ci in cis]
        st = [c["st_ref"][...] for c in chains]
        stb = [s_c.astype(BF16) for s_c in st]
        swq = [dot(stb_c, jnp.concatenate([w_c[:, r], q_c[:, r]], axis=1))
               for stb_c, w_c, q_c, r in zip(stb, w_t, qd_t, sl)]
        sq = [v[:, CHUNK:] for v in swq]
        vn = [u_c[:, r] - v[:, :CHUNK] for u_c, v, r in zip(u_t, swq, sl)]
        av = [dot(vn_c.astype(BF16), at_c[r, r]) for vn_c, at_c, r in zip(vn, attn, sl)]
        upd = [dot((vn_c * kd_c[:, r]).astype(BF16), c["k_rm"][r, :])
               for vn_c, kd_c, c, r in zip(vn, kdec, chains, sl)]
        for idx, c in enumerate(chains):
            c["st_ref"][...] = st[idx] * jnp.exp(c["tot"][:, sl[idx]]) + upd[idx]
            outs[idx][cis[idx]] = sq[idx] + av[idx]
    return [jnp.concatenate(o, axis=1) for o in outs]


def _dn_kernel(k_ref, qt_ref, kt_ref, vt_ref, g_ref, z_ref, ng_ref, s0f_ref, s0b_ref,
               o_ref, sf_ref, sb_ref, ot_ref, st_ref, *, n_tiles, with_output):
    hb = DN_HEADS_PER_STEP
    for hh in range(hb):
        st_ref[2 * hh] = s0f_ref[0, hh]
        st_ref[2 * hh + 1] = s0b_ref[0, hh]

    def body(i, carry):
        nf = i
        nb = n_tiles - 1 - i
        chains = []
        for hh in range(hb):
            for backward, n in ((False, nf), (True, nb)):
                g = g_ref[0, n, hh * SUBLANE:(hh + 1) * SUBLANE, :]
                base = 2 if backward else 0
                chains.append(dict(
                    k_rm=k_ref[0, pl.ds(pl.multiple_of(n * TILE, TILE), TILE), hh * LANE:(hh + 1) * LANE],
                    qt=qt_ref[0, hh, n], kt=kt_ref[0, hh, n], vt=vt_ref[0, hh, n],
                    gc=g[base:base + 1, :], beta=g[base + 1:base + 2, :], tot=g[4 + base // 2:5 + base // 2, :],
                    st_ref=st_ref.at[2 * hh + (1 if backward else 0)], backward=backward, hh=hh, n=n))
        o_t = _dn_tiles(chains)
        if with_output:
            for c, o_c in zip(chains, o_t):
                ot_ref[c["hh"], c["n"]] = ot_ref[c["hh"], c["n"]] + o_c
        return carry

    if with_output:
        ot_ref[...] = jnp.zeros_like(ot_ref)
    lax.fori_loop(0, n_tiles, body, 0)
    for hh in range(hb):
        sf_ref[0, hh] = st_ref[2 * hh]
        sb_ref[0, hh] = st_ref[2 * hh + 1]
    if with_output:
        def finish(t, carry):
            rows = pl.ds(pl.multiple_of(t * TILE, TILE), TILE)
            for hh in range(hb):
                o = ot_ref[hh, t].T
                y = o * lax.rsqrt(jnp.mean(o * o, axis=-1, keepdims=True) + NORM_EPS) * ng_ref[...]
                z = z_ref[0, rows, hh * LANE:(hh + 1) * LANE].astype(F32)
                o_ref[0, rows, hh * LANE:(hh + 1) * LANE] = (y * _silu(z)).astype(BF16)
            return carry

        lax.fori_loop(0, n_tiles, finish, 0)
    else:
        o_ref[...] = jnp.zeros_like(o_ref)


def _delta_net(qkv_rm, qkv_tr, gates, proj, norm_g, s0f, s0b, with_output):
    bsz, n, _ = qkv_rm.shape
    nt = n // TILE
    h = DN_HEADS
    hb = DN_HEADS_PER_STEP
    wide = hb * LANE
    once = pl.Buffered(1)
    tr_spec = lambda off: pl.BlockSpec((1, hb, nt, LANE, TILE), lambda b, j: (b, off + j, 0, 0, 0), once)
    st_spec = pl.BlockSpec((1, hb, LANE, LANE), lambda b, j: (b, j, 0, 0))
    n_out = n if with_output else SUBLANE
    return pl.pallas_call(
        functools.partial(_dn_kernel, n_tiles=nt, with_output=with_output),
        grid=(bsz, h // hb),
        in_specs=[pl.BlockSpec((1, n, wide), lambda b, j: (b, 0, h // hb + j), once),
                  tr_spec(0), tr_spec(h // hb), tr_spec(2 * h // hb),
                  pl.BlockSpec((1, nt, hb * SUBLANE, TILE), lambda b, j: (b, 0, j, 0)),
                  pl.BlockSpec((1, n, wide), lambda b, j: (b, 0, Z_BLK // hb + j), once),
                  pl.BlockSpec((1, LANE), lambda b, j: (0, 0)),
                  st_spec, st_spec],
        out_specs=[pl.BlockSpec((1, n_out, wide), lambda b, j: (b, 0, j)), st_spec, st_spec],
        out_shape=[jax.ShapeDtypeStruct((bsz, n_out, DN_WIDTH), BF16),
                   jax.ShapeDtypeStruct((bsz, h, LANE, LANE), F32),
                   jax.ShapeDtypeStruct((bsz, h, LANE, LANE), F32)],
        scratch_shapes=[pltpu.VMEM((hb, nt, LANE, TILE), F32),
                        pltpu.VMEM((2 * hb, LANE, LANE), F32)],
        compiler_params=_cparams(("parallel", "parallel"), 58),
        name="delta_net",
    )(qkv_rm, qkv_tr, qkv_tr, qkv_tr, gates, proj, norm_g.reshape(1, LANE), s0f, s0b)


def _lru_scan_block(x, wa, ba, wx, bx, spl, h_in, backward):
    rows, width = x.shape
    xb = x.astype(BF16)

    def gate(ws, bias):
        parts = [jnp.dot(xb[:, k * LRU_BLOCK:(k + 1) * LRU_BLOCK], w, preferred_element_type=F32)
                 for k, w in enumerate(ws)]
        return _sigmoid(jnp.concatenate(parts, axis=1) + bias)

    r = gate(wa, ba)
    gi = gate(wx, bx)
    log_a = -LRU_C * r * spl
    a = jnp.exp(log_a)
    b = jnp.sqrt(1.0 - jnp.exp(2.0 * log_a)) * (gi * x)
    groups = rows // SUBLANE
    a = a.reshape(groups, SUBLANE, width)
    b = b.reshape(groups, SUBLANE, width)
    sub = lax.broadcasted_iota(jnp.int32, a.shape, 1)
    s = 1
    while s < SUBLANE:
        if backward:
            keep = sub < SUBLANE - s
            a_sh = jnp.where(keep, pltpu.roll(a, SUBLANE - s, 1), 1.0)
            b_sh = jnp.where(keep, pltpu.roll(b, SUBLANE - s, 1), 0.0)
        else:
            keep = sub >= s
            a_sh = jnp.where(keep, pltpu.roll(a, s, 1), 1.0)
            b_sh = jnp.where(keep, pltpu.roll(b, s, 1), 0.0)
        b = a * b_sh + b
        a = a * a_sh
        s *= 2
    a = a.reshape(rows, width)
    b = b.reshape(rows, width)
    pieces = [None] * groups
    carry = h_in
    order = range(groups - 1, -1, -1) if backward else range(groups)
    edge = 0 if backward else SUBLANE - 1
    for gidx in order:
        lo = gidx * SUBLANE
        hgrp = b[lo:lo + SUBLANE, :] + a[lo:lo + SUBLANE, :] * carry
        pieces[gidx] = hgrp
        carry = hgrp[edge:edge + 1, :]
    return jnp.concatenate(pieces, axis=0), carry


def _lru_kernel(px_ref, py_ref, cw_ref, cb_ref, wa_ref, ba_ref, wx_ref, bx_ref, lam_ref, h0_ref,
                o_ref, last_ref, xs_ref, hs_ref, *, n_blocks, with_output):
    x = px_ref[0].astype(F32)
    xs_ref[...] = _dwconv_rows(x, cw_ref, 4) + cb_ref[...]
    spl = _softplus(-lam_ref[...])
    wa = [[wa_ref[d, k] for k in range(LRU_GROUP)] for d in range(2)]
    wx = [[wx_ref[d, k] for k in range(LRU_GROUP)] for d in range(2)]

    def body(i, carry):
        hf, hb = carry
        rf = pl.multiple_of(i * LRU_ROWS, LRU_ROWS)
        rb = pl.multiple_of((n_blocks - 1 - i) * LRU_ROWS, LRU_ROWS)
        h_f, hf = _lru_scan_block(xs_ref[pl.ds(rf, LRU_ROWS), :], wa[0], ba_ref[0:1, :],
                                  wx[0], bx_ref[0:1, :], spl[0:1, :], hf, False)
        h_b, hb = _lru_scan_block(xs_ref[pl.ds(rb, LRU_ROWS), :], wa[1], ba_ref[1:2, :],
                                  wx[1], bx_ref[1:2, :], spl[1:2, :], hb, True)
        if with_output:
            hs_ref[pl.ds(rf, LRU_ROWS), :] = hs_ref[pl.ds(rf, LRU_ROWS), :] + h_f
            hs_ref[pl.ds(rb, LRU_ROWS), :] = hs_ref[pl.ds(rb, LRU_ROWS), :] + h_b
        return hf, hb

    if with_output:
        hs_ref[...] = jnp.zeros_like(hs_ref)
    h0 = h0_ref[0]
    hf, hb = lax.fori_loop(0, n_blocks, body, (h0[0:1, :], h0[1:2, :]))
    last_ref[0] = jnp.concatenate([hf, hb], axis=0)
    if with_output:
        y = py_ref[0].astype(F32)
        gelu = 0.5 * y * (1.0 + jnp.tanh(math.sqrt(2.0 / math.pi) * (y + 0.044715 * (y * y * y))))
        o_ref[0] = (hs_ref[...] * gelu).astype(BF16)
    else:
        o_ref[...] = jnp.zeros_like(o_ref)


def _rglru(proj, conv_w, conv_b, w_a, b_a, w_x, b_x, lam, h0, with_output):
    bsz, n, _ = proj.shape
    rows = min(n, LRU_ROWS)
    nb = n // rows
    n_out = n if with_output else SUBLANE
    width = LRU_GROUP * LRU_BLOCK
    lx = LX_BLK * LANE // width
    ly = LY_BLK * LANE // width
    vec2 = pl.BlockSpec((2, width), lambda b, j: (0, j))
    wspec = pl.BlockSpec((2, LRU_GROUP, LRU_BLOCK, LRU_BLOCK), lambda b, j: (0, j, 0, 0))
    return pl.pallas_call(
        functools.partial(_lru_kernel, n_blocks=nb, with_output=with_output),
        grid=(bsz, LRU_BLOCKS // LRU_GROUP),
        in_specs=[pl.BlockSpec((1, n, width), lambda b, j: (b, 0, lx + j)),
                  pl.BlockSpec((1, n, width), lambda b, j: (b, 0, ly + j)),
                  pl.BlockSpec((4, width), lambda b, j: (0, j)),
                  pl.BlockSpec((1, width), lambda b, j: (0, j)),
                  wspec, vec2, wspec, vec2, vec2,
                  pl.BlockSpec((1, 2, width), lambda b, j: (b, 0, j))],
        out_specs=[pl.BlockSpec((1, n_out, width), lambda b, j: (b, 0, j)),
                   pl.BlockSpec((1, 2, width), lambda b, j: (b, 0, j))],
        out_shape=[jax.ShapeDtypeStruct((bsz, n_out, LRU_WIDTH), BF16),
                   jax.ShapeDtypeStruct((bsz, 2, LRU_WIDTH), F32)],
        scratch_shapes=[pltpu.VMEM((n, width), F32), pltpu.VMEM((n, width), F32)],
        compiler_params=_cparams(("parallel", "parallel")),
        name="rglru",
    )(proj, proj, conv_w, conv_b.reshape(1, LRU_WIDTH), w_a, b_a, w_x, b_x, lam, h0)


def _hy_prep_kernel(p0_ref, p1_ref, pv_ref, w0_ref, w1_ref, wv_ref, b0_ref, b1_ref, bv_ref, x0_ref, zz_ref):
    x0 = _dwconv_rows(p0_ref[0].astype(F32), w0_ref, 3) + b0_ref[...]
    x1 = _dwconv_rows(p1_ref[0].astype(F32), w1_ref, 3) + b1_ref[...]
    v = _dwconv_rows(pv_ref[0].astype(F32), wv_ref, 3) + bv_ref[...]
    x0_ref[0] = x0.astype(BF16)
    zz_ref[0] = (x1 * v).astype(BF16)


def _hy_prep(proj, conv_w, conv_b):
    bsz, n, _ = proj.shape
    nblk = HY_WIDTH // LANE
    pspec = lambda off: pl.BlockSpec((1, n, LANE), lambda b, j: (b, 0, HY_BLK + off + j))
    wspec = lambda off: pl.BlockSpec((3, LANE), lambda b, j: (0, off + j))
    bspec = lambda off: pl.BlockSpec((1, LANE), lambda b, j: (0, off + j))
    ospec = pl.BlockSpec((1, n, LANE), lambda b, j: (b, 0, j))
    cb = conv_b.reshape(1, 3 * HY_WIDTH)
    return pl.pallas_call(
        _hy_prep_kernel,
        grid=(bsz, nblk),
        in_specs=[pspec(0), pspec(nblk), pspec(2 * nblk), wspec(0), wspec(nblk), wspec(2 * nblk),
                  bspec(0), bspec(nblk), bspec(2 * nblk)],
        out_specs=[ospec, ospec],
        out_shape=[jax.ShapeDtypeStruct((bsz, n, HY_WIDTH), BF16)] * 2,
        compiler_params=_cparams(("parallel", "parallel")),
        name="hy_prep",
    )(proj, proj, proj, conv_w, conv_w, conv_w, cb, cb, cb)


def _hy_filter_kernel(w1_ref, b1_ref, f1_ref, w2_ref, b2_ref, f2_ref, w3_ref, band_ref, delta_ref,
                      k_ref, *, n, rows):
    def lag(shape):
        p = _row_iota(shape) + (pl.program_id(0) * rows - n)
        return p, jnp.abs(p).astype(F32)

    _, i = lag((rows, LANE))
    lane = _col_iota((rows, LANE))
    t = i * (1.0 / (n - 1))
    ang = band_ref[...] * (i * (2.0 * math.pi / n))
    feat = jnp.where(lane == 0, t,
                     jnp.where(lane <= HY_BANDS, jnp.cos(ang), jnp.where(lane < HY_EMB, -jnp.sin(ang), 0.0)))
    hid = jnp.sin(f1_ref[...] * (jnp.dot(feat, w1_ref[...], precision=HIGHEST,
                                         preferred_element_type=F32) + b1_ref[...]))
    hid = jnp.sin(f2_ref[...] * (jnp.dot(hid, w2_ref[...], precision=HIGHEST,
                                         preferred_element_type=F32) + b2_ref[...]))
    filt = _bdot(hid, w3_ref[...])
    p, iw = lag((rows, HY_WIDTH))
    dec = jnp.exp(-(iw * (1.0 / (n - 1))) * delta_ref[...])
    h_f = filt[:, :HY_WIDTH] * dec
    h_b = filt[:, HY_WIDTH:] * dec
    taps = jnp.where(p > 0, h_f, jnp.where(p < 0, h_b, h_f + h_b))
    k_ref[...] = jnp.where(p == -n, 0.0, taps).astype(BF16)


def _hy_filter(n, w1, b1, f1, w2, b2, f2, w3):
    rows = min(n, 256)
    hid = HY_FILTER_HIDDEN
    w1p = jnp.zeros((LANE, hid), F32).at[:HY_EMB].set(w1)
    bands = np.zeros((1, LANE), np.float32)
    base = np.linspace(1e-4, HY_BANDS - 1, HY_BANDS, dtype=np.float32)
    bands[0, 1:1 + HY_BANDS] = base
    bands[0, 1 + HY_BANDS:HY_EMB] = base
    log_target = math.log(HY_DECAY_TARGET)
    deltas = np.abs(np.linspace(log_target / HY_SLOW_DECAY_PCT, log_target / HY_FAST_DECAY_PCT, HY_WIDTH,
                                dtype=np.float32)).reshape(1, HY_WIDTH)
    full = lambda shape: pl.BlockSpec(shape, lambda i: (0,) * len(shape))
    return pl.pallas_call(
        functools.partial(_hy_filter_kernel, n=n, rows=rows),
        grid=(2 * n // rows,),
        in_specs=[full((LANE, hid)), full((1, hid)), full((1, hid)), full((hid, hid)), full((1, hid)),
                  full((1, hid)), full((hid, 2 * HY_WIDTH)), full((1, LANE)), full((1, HY_WIDTH))],
        out_specs=pl.BlockSpec((rows, HY_WIDTH), lambda i: (i, 0)),
        out_shape=jax.ShapeDtypeStruct((2 * n, HY_WIDTH), BF16),
        compiler_params=_cparams(("parallel",)),
        name="hy_filter",
    )(w1p, b1.reshape(1, hid), f1.reshape(1, hid), w2, b2.reshape(1, hid), f2.reshape(1, hid), w3,
      jnp.asarray(bands), jnp.asarray(deltas))


DFT_GROUP = 64


def _dft_table_kernel(c_ref, s_ref, cb_ref, sb_ref, *, m, ncols, col0, rows, transposed, blank_first):
    period = 4 * m
    scale = 2.0 * math.pi / period
    col = _col_iota((1, ncols)) + col0

    @pl.when(pl.program_id(0) == 0)
    def _():
        r2 = _row_iota((DFT_GROUP, ncols))
        c2 = _col_iota((DFT_GROUP, ncols)) + col0
        ph = (r2 * (2 * c2 + 1)) if transposed else ((2 * r2 + 1) * c2)
        ang = (ph & (period - 1)).astype(F32) * scale
        cb_ref[...] = jnp.cos(ang)
        sb_ref[...] = jnp.sin(ang)

    for g in range(rows // DFT_GROUP):
        r1 = pl.program_id(0) * (rows // DFT_GROUP) + g
        ph = (DFT_GROUP * r1) * (2 * col + 1) if transposed else (2 * DFT_GROUP * r1) * col
        ang = (ph & (period - 1)).astype(F32) * scale
        ca = jnp.cos(ang)
        sa = jnp.sin(ang)
        cb = cb_ref[...]
        sb = sb_ref[...]
        c_tile = ca * cb - sa * sb
        s_tile = sa * cb + ca * sb
        if blank_first:
            first = _col_iota((DFT_GROUP, ncols)) == 0
            c_tile = jnp.where(first, 0.0, c_tile)
            s_tile = jnp.where(first, 0.0, s_tile)
        c_ref[g * DFT_GROUP:(g + 1) * DFT_GROUP, :] = c_tile.astype(BF16)
        s_ref[g * DFT_GROUP:(g + 1) * DFT_GROUP, :] = s_tile.astype(BF16)


def _dft_tables(m, transposed=False, two_sided=False):
    rows = min(m, 256)
    ncols = 2 * m if two_sided else m
    spec = pl.BlockSpec((rows, ncols), lambda i: (i, 0))
    return pl.pallas_call(
        functools.partial(_dft_table_kernel, m=m, ncols=ncols, col0=3 * m if two_sided else 0, rows=rows,
                          transposed=transposed, blank_first=two_sided),
        grid=(m // rows,),
        in_specs=[],
        out_specs=[spec, spec],
        out_shape=[jax.ShapeDtypeStruct((m, ncols), BF16)] * 2,
        scratch_shapes=[pltpu.VMEM((DFT_GROUP, ncols), F32), pltpu.VMEM((DFT_GROUP, ncols), F32)],
        compiler_params=_cparams(("arbitrary",)),
        name="dft_tables",
    )()


def _hy_tables(m):
    return _dft_tables(m) + _dft_tables(m, transposed=True) + _dft_tables(m, two_sided=True)


def _hy_ktrans_kernel(cw_ref, sw_ref, lo_ref, hi_ref, kc_ref, ks_ref, *, m):
    dot = functools.partial(jnp.dot, preferred_element_type=F32)
    lo = lo_ref[...]
    hi = hi_ref[...]
    kc_ref[0] = dot(cw_ref[:, :m], lo) + dot(cw_ref[:, m:], hi)
    ks_ref[0] = dot(sw_ref[:, :m], lo) + dot(sw_ref[:, m:], hi)


def _hy_ktrans(cw, sw, taps, m):
    nd = taps.shape[0] // m - 1
    w = taps.shape[1]
    tn = 512
    tab = pl.BlockSpec((m, 2 * m), lambda e, j: (0, 0))
    ospec = pl.BlockSpec((1, m, tn), lambda e, j: (e, 0, j))
    return pl.pallas_call(
        functools.partial(_hy_ktrans_kernel, m=m),
        grid=(nd, w // tn),
        in_specs=[tab, tab, pl.BlockSpec((m, tn), lambda e, j: (e, j)), pl.BlockSpec((m, tn), lambda e, j: (e + 1, j))],
        out_specs=[ospec, ospec],
        out_shape=[jax.ShapeDtypeStruct((nd, m, w), F32)] * 2,
        compiler_params=_cparams(("parallel", "parallel")),
        name="hy_ktrans",
    )(cw, sw, taps, taps)


def _hy_fwd_kernel(c_ref, s_ref, zz_ref, kc_ref, ks_ref, a_ref, b_ref, *, m, nb):
    dot = functools.partial(jnp.dot, preferred_element_type=F32)
    c = c_ref[...]
    s = s_ref[...]
    uc = [dot(c, zz_ref[0, j * m:(j + 1) * m, :]) for j in range(nb)]
    us = [dot(s, zz_ref[0, j * m:(j + 1) * m, :]) for j in range(nb)]
    for i in range(nb):
        acc_a = acc_b = None
        for j in range(nb):
            kc = kc_ref[i - j + nb - 1]
            ks = ks_ref[i - j + nb - 1]
            ta = uc[j] * kc - us[j] * ks
            tb = uc[j] * ks + us[j] * kc
            acc_a = ta if acc_a is None else acc_a + ta
            acc_b = tb if acc_b is None else acc_b + tb
        a_ref[0, i] = acc_a.astype(BF16)
        b_ref[0, i] = acc_b.astype(BF16)


def _hy_forward(ctab, stab, zz, kc, ks, m):
    bsz, n, w = zz.shape
    nb = n // m
    nd = 2 * nb - 1
    tf = min(m, 256)
    tn = 512
    tab = pl.BlockSpec((tf, m), lambda b, j, i: (i, 0))
    kspec = pl.BlockSpec((nd, tf, tn), lambda b, j, i: (0, i, j))
    ospec = pl.BlockSpec((1, nb, tf, tn), lambda b, j, i: (b, 0, i, j))
    return pl.pallas_call(
        functools.partial(_hy_fwd_kernel, m=m, nb=nb),
        grid=(bsz, w // tn, m // tf),
        in_specs=[tab, tab, pl.BlockSpec((1, n, tn), lambda b, j, i: (b, 0, j)), kspec, kspec],
        out_specs=[ospec, ospec],
        out_shape=[jax.ShapeDtypeStruct((bsz, nb, m, w), BF16)] * 2,
        compiler_params=_cparams(("parallel", "parallel", "parallel")),
        name="hy_forward",
    )(ctab, stab, zz, kc, ks)


def _hy_inv_kernel(ct_ref, st_ref, a_ref, b_ref, x0_ref, zz_ref, bias_ref, o_ref, *, m):
    y = (jnp.dot(ct_ref[...], a_ref[0, 0], preferred_element_type=F32)
         + jnp.dot(st_ref[...], b_ref[0, 0], preferred_element_type=F32)) * (1.0 / m)
    zz = zz_ref[0].astype(F32)
    o_ref[0] = (x0_ref[0].astype(F32) * (y + zz * bias_ref[...])).astype(BF16)


def _hy_inverse(cttab, sttab, a, bq, x0, zz, bias, m):
    bsz, n, w = zz.shape
    nb = n // m
    tn = 512
    tab = pl.BlockSpec((m, m), lambda b, i, j: (0, 0))
    spec4 = pl.BlockSpec((1, 1, m, tn), lambda b, i, j: (b, i, 0, j))
    tile = pl.BlockSpec((1, m, tn), lambda b, i, j: (b, i, j))
    return pl.pallas_call(
        functools.partial(_hy_inv_kernel, m=m),
        grid=(bsz, nb, w // tn),
        in_specs=[tab, tab, spec4, spec4, tile, tile, pl.BlockSpec((1, tn), lambda b, i, j: (0, j))],
        out_specs=tile,
        out_shape=jax.ShapeDtypeStruct((bsz, n, w), BF16),
        compiler_params=_cparams(("parallel", "parallel", "parallel")),
        name="hy_inverse",
    )(cttab, sttab, a, bq, x0, zz, bias.reshape(1, w))


def _merge_kernel(odn_ref, ohy_ref, olru_ref, gdn_ref, ghy_ref, glru_ref, wdn_ref, why_ref, wlru_ref,
                  wout_ref, x_ref, gate_ref, o_ref):
    m = _sigmoid(gdn_ref[0].astype(F32)) * jnp.dot(odn_ref[0], wdn_ref[...], preferred_element_type=F32)
    m = m + _sigmoid(ghy_ref[0].astype(F32)) * jnp.dot(ohy_ref[0], why_ref[...], preferred_element_type=F32)
    m = m + _sigmoid(glru_ref[0].astype(F32)) * jnp.dot(olru_ref[0], wlru_ref[...], preferred_element_type=F32)
    y = jnp.dot(m.astype(BF16), wout_ref[...], preferred_element_type=F32)
    o_ref[0] = x_ref[0] + gate_ref[0] * y


def _merge(o_dn, o_hy, o_lru, proj, w_dn, w_hy, w_lru, w_out, x, gate):
    bsz, n, d = x.shape
    tm = min(n, 256)
    nblk = d // 1024
    act = pl.BlockSpec((1, tm, d), lambda b, i: (b, i, 0))
    gspec = lambda k: pl.BlockSpec((1, tm, d), lambda b, i: (b, i, GATE_BLK * LANE // d + k * nblk))
    wspec = pl.BlockSpec((d, d), lambda b, i: (0, 0))
    return pl.pallas_call(
        _merge_kernel,
        grid=(bsz, n // tm),
        in_specs=[act, act, act, gspec(0), gspec(1), gspec(2), wspec, wspec, wspec, wspec, act,
                  pl.BlockSpec((1, 1, d), lambda b, i: (b, 0, 0))],
        out_specs=act,
        out_shape=jax.ShapeDtypeStruct((bsz, n, d), F32),
        compiler_params=_cparams(("parallel", "parallel")),
        name="merge",
    )(o_dn, o_hy, o_lru, proj, proj, proj, w_dn, w_hy, w_lru, w_out, x, gate)


def _grid_conv(u, w_ref, rows, cols):
    n = u.shape[0]
    t = _row_iota(u.shape)
    c = _mod_pow2(t, cols)
    left = jnp.where(c >= 1, pltpu.roll(u, 1, 0), 0.0)
    right = jnp.where(c <= cols - 2, pltpu.roll(u, n - 1, 0), 0.0)
    acc = None
    for di in (-1, 0, 1):
        if rows == 1 and di != 0:
            continue
        k = 3 * (di + 1)
        inner = left * w_ref[k:k + 1, :] + u * w_ref[k + 1:k + 2, :] + right * w_ref[k + 2:k + 3, :]
        if di != 0:
            inside = (t < n - cols) if di > 0 else (t >= cols)
            inner = jnp.where(inside, pltpu.roll(inner, (-di * cols) % n, 0), 0.0)
        acc = inner if acc is None else acc + inner
    return acc


def _ffn_act_kernel(ug_ref, uv_ref, wg_ref, wv_ref, o_ref, *, rows, cols):
    g = _grid_conv(ug_ref[0].astype(F32), wg_ref, rows, cols)
    v = _grid_conv(uv_ref[0].astype(F32), wv_ref, rows, cols)
    o_ref[0] = (_silu(g) * v).astype(BF16)


def _ffn_act(u, conv_w, rows, cols):
    bsz, n, _ = u.shape
    nblk = FFN_HIDDEN // LANE
    return pl.pallas_call(
        functools.partial(_ffn_act_kernel, rows=rows, cols=cols),
        grid=(bsz, nblk),
        in_specs=[pl.BlockSpec((1, n, LANE), lambda b, j: (b, 0, j)),
                  pl.BlockSpec((1, n, LANE), lambda b, j: (b, 0, nblk + j)),
                  pl.BlockSpec((9, LANE), lambda b, j: (0, j)),
                  pl.BlockSpec((9, LANE), lambda b, j: (0, nblk + j))],
        out_specs=pl.BlockSpec((1, n, LANE), lambda b, j: (b, 0, j)),
        out_shape=jax.ShapeDtypeStruct((bsz, n, FFN_HIDDEN), BF16),
        compiler_params=_cparams(("parallel", "parallel")),
        name="ffn_act",
    )(u, u, conv_w, conv_w)


def _final_norm_kernel(x_ref, g_ref, o_ref):
    x = x_ref[0]
    o_ref[0] = x * lax.rsqrt(jnp.mean(x * x, axis=-1, keepdims=True) + NORM_EPS) * g_ref[...]


def _final_norm(x, gain):
    bsz, n, d = x.shape
    tm = min(n, 512)
    spec = pl.BlockSpec((1, tm, d), lambda b, i: (b, i, 0))
    return pl.pallas_call(
        _final_norm_kernel,
        grid=(bsz, n // tm),
        in_specs=[spec, pl.BlockSpec((1, d), lambda b, i: (0, 0))],
        out_specs=spec,
        out_shape=jax.ShapeDtypeStruct((bsz, n, d), F32),
        compiler_params=_cparams(("parallel", "parallel")),
        name="final_norm",
    )(x, gain.reshape(1, d))


def _gate_column_map():
    src = np.full((LANE,), -1, np.int32)
    isdec = np.zeros((1, LANE), np.float32)
    dirs = np.zeros((LANE,), np.int32)
    for h in range(DN_HEADS):
        for slot, (d, kind) in enumerate(((0, 0), (0, 1), (1, 0), (1, 1), (0, 0), (1, 0))):
            src[h * SUBLANE + slot] = d * 2 * DN_HEADS + kind * DN_HEADS + h
            isdec[0, h * SUBLANE + slot] = 1.0 if kind == 0 else 0.0
            dirs[h * SUBLANE + slot] = d
    return src, isdec, dirs


def _split_in_proj(w_in):
    o = np.cumsum((3 * DN_WIDTH, DN_WIDTH, 4 * DN_HEADS, 3 * HY_WIDTH, LRU_WIDTH, LRU_WIDTH)).tolist()
    w_main = jnp.concatenate([w_in[:, :o[1]], w_in[:, o[2]:]], axis=1).astype(BF16)
    src, _, _ = _gate_column_map()
    w_ab = w_in[:, o[1]:o[2]]
    w_gate = jnp.where(jnp.asarray(src >= 0)[None, :], w_ab[:, np.maximum(src, 0)], 0.0).astype(BF16)
    return w_main, w_gate


def _gate_params(a_log, dt_bias):
    _, isdec, dirs = _gate_column_map()
    head = (np.arange(LANE) // SUBLANE).astype(np.int32)
    alog_c = a_log[dirs, head].reshape(1, LANE)
    dtb_c = dt_bias[dirs, head].reshape(1, LANE)
    return alog_c, dtb_c, jnp.asarray(isdec)


def _token_mixer(x, mods, lp, states, tables, with_output):
    bsz, n, _ = x.shape
    shift1, scale1, gate1 = mods[0], mods[1], mods[2]
    proj = _norm_mod_matmul(x, lp["norm1_g"], shift1, scale1, lp["w_main"], BF16, 2048, 1024)
    ab = _norm_mod_matmul(x, lp["norm1_g"], shift1, scale1, lp["w_gate"], F32, 2048, LANE)

    qkv_rm, qkv_tr = _dn_prep(proj, lp["dn_conv_w"])
    gates = _dn_gates(ab, *lp["gate_params"])
    o_dn, s_f, s_b = _delta_net(qkv_rm, qkv_tr, gates, proj, lp["dn_norm_g"], states[0], states[1], with_output)
    o_lru, h_last = _rglru(proj, lp["lru_conv_w"], lp["lru_conv_b"], lp["lru_w_a"], lp["lru_b_a"],
                           lp["lru_w_x"], lp["lru_b_x"], lp["lru_lambda"], states[2], with_output)
    new_states = (s_f, s_b, h_last)
    if not with_output:
        return None, new_states

    m = min(n, HY_BLOCK)
    ctab, stab, cttab, sttab, cwtab, swtab = tables
    taps = _hy_filter(n, lp["hy_w1"], lp["hy_b1"], lp["hy_f1"], lp["hy_w2"], lp["hy_b2"], lp["hy_f2"], lp["hy_w3"])
    kc, ks = _hy_ktrans(cwtab, swtab, taps, m)
    x0, zz = _hy_prep(proj, lp["hy_conv_w"], lp["hy_conv_b"])
    a, bq = _hy_forward(ctab, stab, zz, kc, ks, m)
    o_hy = _hy_inverse(cttab, sttab, a, bq, x0, zz, lp["hy_bias"], m)

    x = _merge(o_dn, o_hy, o_lru, proj, lp["w_proj_dn"], lp["w_proj_hy"], lp["w_proj_lru"], lp["w_out"], x, gate1)
    return x, new_states


def _conv_ffn(x, mods, lp, rows, cols):
    u = _norm_mod_matmul(x, lp["norm2_g"], mods[3], mods[4], lp["ffn_up"], BF16, 1024, FFN_HIDDEN)
    act = _ffn_act(u, lp["ffn_conv_w"], rows, cols)
    return _matmul_residual(act, lp["ffn_down"], x, mods[5])


def kernel(x, c, ctx, c_ctx, w_mod, b_mod, norm1_g, norm2_g, w_in, dn_conv_w, dn_a_log, dn_dt_bias, dn_norm_g,
           hy_conv_w, hy_conv_b, hy_w1, hy_b1, hy_f1, hy_w2, hy_b2, hy_f2, hy_w3, hy_bias,
           lru_conv_w, lru_conv_b, lru_w_a, lru_b_a, lru_w_x, lru_b_x, lru_lambda,
           w_proj_dn, w_proj_hy, w_proj_lru, w_out, ffn_up, ffn_conv_w, ffn_down, final_norm_g):
    bsz, n_lat, d = x.shape
    n_ctx = ctx.shape[1]
    depth = w_in.shape[0]
    rows = n_lat // GRID_W

    cvec = jnp.zeros((SUBLANE, d), F32).at[:bsz].set(c).at[bsz].set(c_ctx)
    lat_tables = _hy_tables(min(n_lat, HY_BLOCK))
    ctx_tables = _hy_tables(min(n_ctx, HY_BLOCK))
    zero_states = (jnp.zeros((bsz, DN_HEADS, LANE, LANE), F32), jnp.zeros((bsz, DN_HEADS, LANE, LANE), F32),
                   jnp.zeros((bsz, 2, LRU_WIDTH), F32))

    xc = ctx
    for l in range(depth):
        ctx_needed = l < depth - 1
        w_main, w_gate = _split_in_proj(w_in[l])
        lp = dict(
            norm1_g=norm1_g[l], norm2_g=norm2_g[l], w_main=w_main, w_gate=w_gate,
            dn_conv_w=dn_conv_w[l], gate_params=_gate_params(dn_a_log[l], dn_dt_bias[l]), dn_norm_g=dn_norm_g[l],
            hy_conv_w=hy_conv_w[l], hy_conv_b=hy_conv_b[l], hy_w1=hy_w1[l], hy_b1=hy_b1[l], hy_f1=hy_f1[l],
            hy_w2=hy_w2[l], hy_b2=hy_b2[l], hy_f2=hy_f2[l], hy_w3=hy_w3[l], hy_bias=hy_bias[l],
            lru_conv_w=lru_conv_w[l], lru_conv_b=lru_conv_b[l], lru_w_a=lru_w_a[l].astype(BF16),
            lru_b_a=lru_b_a[l], lru_w_x=lru_w_x[l].astype(BF16), lru_b_x=lru_b_x[l], lru_lambda=lru_lambda[l],
            w_proj_dn=w_proj_dn[l].astype(BF16), w_proj_hy=w_proj_hy[l].astype(BF16),
            w_proj_lru=w_proj_lru[l].astype(BF16), w_out=w_out[l].astype(BF16),
            ffn_up=ffn_up[l].astype(BF16), ffn_conv_w=ffn_conv_w[l].reshape(9, 2 * FFN_HIDDEN),
            ffn_down=ffn_down[l].astype(BF16))
        mod = _modulation(cvec, w_mod[l].astype(BF16), b_mod[l])
        lat_mod = [mod[:bsz, k * d:(k + 1) * d].reshape(bsz, 1, d) for k in range(N_MOD)]
        ctx_mod = [jnp.broadcast_to(mod[bsz:bsz + 1, k * d:(k + 1) * d].reshape(1, 1, d), (bsz, 1, d))
                   for k in range(N_MOD)]

        xc_new, ctx_states = _token_mixer(xc, ctx_mod, lp, zero_states, ctx_tables, ctx_needed)
        x, _ = _token_mixer(x, lat_mod, lp, ctx_states, lat_tables, True)
        x = _conv_ffn(x, lat_mod, lp, rows, GRID_W)
        if ctx_needed:
            xc = _conv_ffn(xc_new, ctx_mod, lp, 1, n_ctx)
    return _final_norm(x, final_norm_g)
```

```python
import functools
import math

import numpy as np
import jax
import jax.numpy as jnp
from jax import lax
from jax.experimental import pallas as pl
from jax.experimental.pallas import tpu as pltpu

F32 = jnp.float32
BF16 = jnp.bfloat16
HIGHEST = lax.Precision.HIGHEST

D_MODEL = 1024
DEPTH = 2
GRID_W = 64
NORM_EPS = 1e-6
N_MOD = 6

DN_HEADS = 8
DN_HEAD_DIM = 128
DN_WIDTH = DN_HEADS * DN_HEAD_DIM
HY_WIDTH = 1024
HY_EMB = 33
HY_BANDS = (HY_EMB - 1) // 2
HY_FILTER_HIDDEN = 64
HY_FAST_DECAY_PCT = 0.3
HY_SLOW_DECAY_PCT = 1.5
HY_DECAY_TARGET = 1e-2
LRU_WIDTH = 1024
LRU_BLOCKS = 8
LRU_BLOCK = LRU_WIDTH // LRU_BLOCKS
LRU_C = 8.0
FFN_HIDDEN = 2816

LANE = 128
SUBLANE = 8
TILE = 256
CHUNK = 128
DN_HEADS_PER_STEP = 4
DN_PACK = 64
HY_BLOCK = 1024
LRU_ROWS = 256
LRU_GROUP = 2
FFN_TILE = 256
FFN_WIDE = 256
MIB = 1024 * 1024

QKV_BLK = 0
Z_BLK = 24
HY_BLK = 32
LX_BLK = 56
LY_BLK = 64
GATE_BLK = 72
N_MAIN = 96 * LANE


def _cparams(sem, vmem_mib=48):
    return pltpu.CompilerParams(dimension_semantics=sem, vmem_limit_bytes=vmem_mib * MIB)


def _sigmoid(x):
    return 1.0 / (1.0 + jnp.exp(-x))


def _silu(x):
    return x * _sigmoid(x)


def _softplus(x):
    return jnp.maximum(x, 0.0) + jnp.log(1.0 + jnp.exp(-jnp.abs(x)))


def _row_iota(shape):
    return lax.broadcasted_iota(jnp.int32, shape, 0)


def _col_iota(shape):
    return lax.broadcasted_iota(jnp.int32, shape, 1)


def _div_pow2(x, k):
    assert k & (k - 1) == 0
    return x >> (k.bit_length() - 1)


def _mod_pow2(x, k):
    assert k & (k - 1) == 0
    return x & (k - 1)


def _bdot(a, b):
    return jnp.dot(a.astype(BF16), b.astype(BF16), preferred_element_type=F32)


def _mod_kernel(c_ref, w_ref, b_ref, o_ref):
    o_ref[...] = _bdot(_silu(c_ref[...]), w_ref[...]) + b_ref[...]


def _modulation(cvec, w_mod, b_mod):
    n = w_mod.shape[1]
    tn = 1024
    return pl.pallas_call(
        _mod_kernel,
        grid=(n // tn,),
        in_specs=[pl.BlockSpec((SUBLANE, D_MODEL), lambda j: (0, 0)),
                  pl.BlockSpec((D_MODEL, tn), lambda j: (0, j)),
                  pl.BlockSpec((1, tn), lambda j: (0, j))],
        out_specs=pl.BlockSpec((SUBLANE, tn), lambda j: (0, j)),
        out_shape=jax.ShapeDtypeStruct((SUBLANE, n), F32),
        compiler_params=_cparams(("parallel",)),
        name="modulation",
    )(cvec, w_mod, b_mod.reshape(1, n))


def _nmm_kernel(x_ref, g_ref, sh_ref, sc_ref, w_ref, *rest, with_side):
    o_ref = rest[1] if with_side else rest[0]
    h_ref = rest[-1]

    @pl.when(pl.program_id(2) == 0)
    def _():
        x = x_ref[0]
        y = x * lax.rsqrt(jnp.mean(x * x, axis=-1, keepdims=True) + NORM_EPS) * g_ref[...]
        h_ref[...] = (y * (1.0 + sc_ref[0]) + sh_ref[0]).astype(BF16)
        if with_side:
            rest[2][0] = jnp.dot(h_ref[...], rest[0][...], preferred_element_type=F32)

    o_ref[0] = jnp.dot(h_ref[...], w_ref[...], preferred_element_type=F32).astype(o_ref.dtype)


def _norm_mod_matmul(x, gain, shift, scale, w, out_dtype, tm, tn, w_side=None):
    bsz, n, d = x.shape
    nout = w.shape[1]
    tm = min(n, tm)
    with_side = w_side is not None
    in_specs = [pl.BlockSpec((1, tm, d), lambda b, i, j: (b, i, 0)),
                pl.BlockSpec((1, d), lambda b, i, j: (0, 0)),
                pl.BlockSpec((1, 1, d), lambda b, i, j: (b, 0, 0)),
                pl.BlockSpec((1, 1, d), lambda b, i, j: (b, 0, 0)),
                pl.BlockSpec((d, tn), lambda b, i, j: (0, j))]
    out_specs = [pl.BlockSpec((1, tm, tn), lambda b, i, j: (b, i, j))]
    out_shape = [jax.ShapeDtypeStruct((bsz, n, nout), out_dtype)]
    args = [x, gain.reshape(1, d), shift, scale, w]
    if with_side:
        ns = w_side.shape[1]
        in_specs.append(pl.BlockSpec((d, ns), lambda b, i, j: (0, 0)))
        out_specs.append(pl.BlockSpec((1, tm, ns), lambda b, i, j: (b, i, 0)))
        out_shape.append(jax.ShapeDtypeStruct((bsz, n, ns), F32))
        args.append(w_side)
    out = pl.pallas_call(
        functools.partial(_nmm_kernel, with_side=with_side),
        grid=(bsz, n // tm, nout // tn),
        in_specs=in_specs,
        out_specs=out_specs,
        out_shape=out_shape,
        scratch_shapes=[pltpu.VMEM((tm, d), BF16)],
        compiler_params=_cparams(("parallel", "parallel", "arbitrary")),
        name="norm_mod_matmul",
    )(*args)
    return out if with_side else out[0]


def _mm_kernel(a_ref, b_ref, o_ref):
    o_ref[...] = jnp.dot(a_ref[...], b_ref[...], preferred_element_type=F32).astype(o_ref.dtype)


def _matmul(a, b, out_dtype, tm, tn):
    m, k = a.shape
    n = b.shape[1]
    return pl.pallas_call(
        _mm_kernel,
        grid=(m // tm, n // tn),
        in_specs=[pl.BlockSpec((tm, k), lambda i, j: (i, 0)),
                  pl.BlockSpec((k, tn), lambda i, j: (0, j))],
        out_specs=pl.BlockSpec((tm, tn), lambda i, j: (i, j)),
        out_shape=jax.ShapeDtypeStruct((m, n), out_dtype),
        compiler_params=_cparams(("parallel", "parallel")),
        name="matmul",
    )(a, b)


def _mm_res_kernel(a_ref, b_ref, x_ref, g_ref, o_ref):
    y = jnp.dot(a_ref[0], b_ref[...], preferred_element_type=F32)
    o_ref[0] = x_ref[0] + g_ref[0] * y


def _matmul_residual(a, w, x, gate):
    bsz, n, k = a.shape
    d = w.shape[1]
    tm = min(n, 512)
    return pl.pallas_call(
        _mm_res_kernel,
        grid=(bsz, n // tm),
        in_specs=[pl.BlockSpec((1, tm, k), lambda b, i: (b, i, 0)),
                  pl.BlockSpec((k, d), lambda b, i: (0, 0)),
                  pl.BlockSpec((1, tm, d), lambda b, i: (b, i, 0)),
                  pl.BlockSpec((1, 1, d), lambda b, i: (b, 0, 0))],
        out_specs=pl.BlockSpec((1, tm, d), lambda b, i: (b, i, 0)),
        out_shape=jax.ShapeDtypeStruct((bsz, n, d), F32),
        compiler_params=_cparams(("parallel", "parallel")),
        name="matmul_residual",
    )(a, w, x, gate)


def _shift_rows(x, off):
    n = x.shape[0]
    if off == 0:
        return x
    rolled = pltpu.roll(x, (-off) % n, 0)
    t = _row_iota(x.shape)
    valid = (t < n - off) if off > 0 else (t >= -off)
    return jnp.where(valid, rolled, 0.0)


def _dwconv_rows(x, w_ref, k):
    left = (k - 1) // 2
    acc = None
    for j in range(k):
        term = _shift_rows(x, j - left) * w_ref[j:j + 1, :]
        acc = term if acc is None else acc + term
    return acc


def _dn_prep_kernel(p_ref, w_ref, rm_ref, tr_ref, *, n_tiles):
    c = pl.program_id(1)
    x = p_ref[0].astype(F32)
    y = _silu(_dwconv_rows(x, w_ref, 4))
    nrm = y * lax.rsqrt(jnp.sum(y * y, axis=-1, keepdims=True) + 1e-6)
    nrm = nrm * jnp.where(c < DN_HEADS, DN_HEAD_DIM ** -0.5, 1.0)
    y = jnp.where(c < 2 * DN_HEADS, nrm, y)
    rm_ref[0] = y.astype(BF16)
    for t in range(n_tiles):
        tr_ref[0, 0, t] = y[t * TILE:(t + 1) * TILE, :].T.astype(BF16)


def _dn_prep(proj, conv_w):
    bsz, n, _ = proj.shape
    nt = n // TILE
    nc = 3 * DN_HEADS
    return pl.pallas_call(
        functools.partial(_dn_prep_kernel, n_tiles=nt),
        grid=(bsz, nc),
        in_specs=[pl.BlockSpec((1, n, LANE), lambda b, c: (b, 0, QKV_BLK + c)),
                  pl.BlockSpec((4, LANE), lambda b, c: (0, c))],
        out_specs=[pl.BlockSpec((1, n, LANE), lambda b, c: (b, 0, c)),
                   pl.BlockSpec((1, 1, nt, LANE, TILE), lambda b, c: (b, c, 0, 0, 0))],
        out_shape=[jax.ShapeDtypeStruct((bsz, n, nc * LANE), BF16),
                   jax.ShapeDtypeStruct((bsz, nc, nt, LANE, TILE), BF16)],
        compiler_params=_cparams(("parallel", "parallel")),
        name="dn_prep",
    )(proj, conv_w)


def _dn_gate_kernel(ab_ref, alog_ref, dtb_ref, isdec_ref, o_ref):
    x = ab_ref[0]
    dec = -jnp.exp(alog_ref[...]) * _softplus(x + dtb_ref[...])
    e = jnp.where(isdec_ref[...] > 0.5, dec, _sigmoid(x))
    et = e.T
    s = _row_iota((TILE, TILE))
    t = _col_iota((TILE, TILE))
    same = _div_pow2(s, CHUNK) == _div_pow2(t, CHUNK)
    prefix = jnp.where(same & (s <= t), 1.0, 0.0)
    suffix = jnp.where(same & (s >= t), 1.0, 0.0)
    total = jnp.where(same, 1.0, 0.0)
    pre = jnp.dot(et, prefix, precision=HIGHEST, preferred_element_type=F32)
    suf = jnp.dot(et, suffix, precision=HIGHEST, preferred_element_type=F32)
    tot = jnp.dot(et, total, precision=HIGHEST, preferred_element_type=F32)
    slot = _mod_pow2(_row_iota((LANE, TILE)), SUBLANE)
    o_ref[0, 0] = jnp.where(slot == 0, pre, jnp.where(slot == 2, suf, jnp.where(slot >= 4, tot, et)))


def _dn_gates(ab, alog_c, dtb_c, isdec_c):
    bsz, n, _ = ab.shape
    nt = n // TILE
    vec = pl.BlockSpec((1, LANE), lambda b, i: (0, 0))
    return pl.pallas_call(
        _dn_gate_kernel,
        grid=(bsz, nt),
        in_specs=[pl.BlockSpec((1, TILE, LANE), lambda b, i: (b, i, 0)), vec, vec, vec],
        out_specs=pl.BlockSpec((1, 1, LANE, TILE), lambda b, i: (b, i, 0, 0)),
        out_shape=jax.ShapeDtypeStruct((bsz, nt, LANE, TILE), F32),
        compiler_params=_cparams(("parallel", "parallel")),
        name="dn_gates",
    )(ab, alog_c, dtb_c, isdec_c)


def _dn_tiles(chains):
    a = _row_iota((TILE, TILE))
    b = _col_iota((TILE, TILE))
    same = _div_pow2(a, CHUNK) == _div_pow2(b, CHUNK)
    apart = a ^ b
    n_chunks = TILE // CHUNK
    dot = functools.partial(jnp.dot, preferred_element_type=F32)

    def stack(v, reps):
        return jnp.concatenate([v] * reps, axis=0)

    kk = [dot(c["k_rm"], c["kt"]) for c in chains]
    kq = [dot(c["k_rm"], c["qt"]) for c in chains]
    xs, pw, attn = [], [], []
    for c, kk_c, kq_c in zip(chains, kk, kq):
        incl = same & ((a >= b) if c["backward"] else (a <= b))
        gcb = jnp.broadcast_to(c["gc"], (TILE, TILE))
        diff = gcb - gcb.T
        decay = jnp.where(incl, jnp.exp(jnp.where(incl, diff, 0.0)), 0.0)
        attn.append((kq_c * decay).astype(BF16))
        x = jnp.where(a == b, 0.0, -(kk_c * decay * c["beta"]))
        xs.append(x)
        base = jnp.where(a == b, 1.0, jnp.where(apart == 1, x, 0.0))
        acc = base[0:DN_PACK, :]
        for r in range(1, TILE // DN_PACK):
            acc = acc + base[r * DN_PACK:(r + 1) * DN_PACK, :]
        pw.append(acc)
    pack = DN_PACK
    s = 2
    while s < CHUNK:
        if s == pack:
            keep = _div_pow2(_row_iota((2 * pack, TILE)), pack) == (_div_pow2(_col_iota((2 * pack, TILE)), pack) & 1)
            pw = [jnp.where(keep, stack(p, 2), 0.0) for p in pw]
            pack *= 2
        couple = (apart >> (s.bit_length() - 1)) == 1
        blocks = _div_pow2(a, pack) == _div_pow2(b, pack)
        pb = [p.astype(BF16) for p in pw]
        px = [dot(pb_c, jnp.where(couple, x, 0.0).astype(BF16)) for pb_c, x in zip(pb, xs)]
        p_bd = [jnp.where(blocks, stack(pb_c, TILE // pack), 0.0).astype(BF16) for pb_c in pb]
        pw = [p + dot(px_c.astype(BF16), bd_c) for p, px_c, bd_c in zip(pw, px, p_bd)]
        s *= 2
    blocks = _div_pow2(a, pack) == _div_pow2(b, pack)
    t_inv = [jnp.where(blocks, stack(p.astype(BF16), TILE // pack), 0.0).astype(BF16) for p in pw]
    egc = [jnp.exp(c["gc"]) for c in chains]
    u_t = [dot((c["vt"].astype(F32) * c["beta"]).astype(BF16), t) for c, t in zip(chains, t_inv)]
    w_t = [dot((c["kt"].astype(F32) * (c["beta"] * e)).astype(BF16), t).astype(BF16)
           for c, e, t in zip(chains, egc, t_inv)]
    qd_t = [(c["qt"].astype(F32) * e).astype(BF16) for c, e in zip(chains, egc)]
    kdec = [jnp.exp(c["tot"] - c["gc"]) for c in chains]
    outs = [[None] * n_chunks for _ in chains]
    for step in range(n_chunks):
        cis = [(n_chunks - 1 - step) if c["backward"] else step for c in chains]
        sl = [slice(ci * CHUNK, (ci + 1) * CHUNK) for ci in cis]
        st = [c["st_ref"][...] for c in chains]
        stb = [s_c.astype(BF16) for s_c in st]
        swq = [dot(stb_c, jnp.concatenate([w_c[:, r], q_c[:, r]], axis=1))
               for stb_c, w_c, q_c, r in zip(stb, w_t, qd_t, sl)]
        sq = [v[:, CHUNK:] for v in swq]
        vn = [u_c[:, r] - v[:, :CHUNK] for u_c, v, r in zip(u_t, swq, sl)]
        av = [dot(vn_c.astype(BF16), at_c[r, r]) for vn_c, at_c, r in zip(vn, attn, sl)]
        upd = [dot((vn_c * kd_c[:, r]).astype(BF16), c["k_rm"][r, :])
               for vn_c, kd_c, c, r in zip(vn, kdec, chains, sl)]
        for idx, c in enumerate(chains):
            c["st_ref"][...] = st[idx] * jnp.exp(c["tot"][:, sl[idx]]) + upd[idx]
            outs[idx][cis[idx]] = sq[idx] + av[idx]
    return [jnp.concatenate(o, axis=1) for o in outs]


def _dn_kernel(k_ref, qt_ref, kt_ref, vt_ref, g_ref, z_ref, ng_ref, s0f_ref, s0b_ref,
               o_ref, sf_ref, sb_ref, ot_ref, st_ref, *, n_tiles, with_output):
    hb = DN_HEADS_PER_STEP
    for hh in range(hb):
        st_ref[2 * hh] = s0f_ref[0, hh]
        st_ref[2 * hh + 1] = s0b_ref[0, hh]

    def body(i, carry):
        nf = i
        nb = n_tiles - 1 - i
        chains = []
        for hh in range(hb):
            for backward, n in ((False, nf), (True, nb)):
                g = g_ref[0, n, hh * SUBLANE:(hh + 1) * SUBLANE, :]
                base = 2 if backward else 0
                chains.append(dict(
                    k_rm=k_ref[0, pl.ds(pl.multiple_of(n * TILE, TILE), TILE), hh * LANE:(hh + 1) * LANE],
                    qt=qt_ref[0, hh, n], kt=kt_ref[0, hh, n], vt=vt_ref[0, hh, n],
                    gc=g[base:base + 1, :], beta=g[base + 1:base + 2, :], tot=g[4 + base // 2:5 + base // 2, :],
                    st_ref=st_ref.at[2 * hh + (1 if backward else 0)], backward=backward, hh=hh, n=n))
        o_t = _dn_tiles(chains)
        if with_output:
            for c, o_c in zip(chains, o_t):
                ot_ref[c["hh"], c["n"]] = ot_ref[c["hh"], c["n"]] + o_c
        return carry

    if with_output:
        ot_ref[...] = jnp.zeros_like(ot_ref)
    lax.fori_loop(0, n_tiles, body, 0)
    for hh in range(hb):
        sf_ref[0, hh] = st_ref[2 * hh]
        sb_ref[0, hh] = st_ref[2 * hh + 1]
    if with_output:
        def finish(t, carry):
            rows = pl.ds(pl.multiple_of(t * TILE, TILE), TILE)
            for hh in range(hb):
                o = ot_ref[hh, t].T
                y = o * lax.rsqrt(jnp.mean(o * o, axis=-1, keepdims=True) + NORM_EPS) * ng_ref[...]
                z = z_ref[0, rows, hh * LANE:(hh + 1) * LANE].astype(F32)
                o_ref[0, rows, hh * LANE:(hh + 1) * LANE] = (y * _silu(z)).astype(BF16)
            return carry

        lax.fori_loop(0, n_tiles, finish, 0)
    else:
        o_ref[...] = jnp.zeros_like(o_ref)


def _delta_net(qkv_rm, qkv_tr, gates, proj, norm_g, s0f, s0b, with_output):
    bsz, n, _ = qkv_rm.shape
    nt = n // TILE
    h = DN_HEADS
    hb = DN_HEADS_PER_STEP
    wide = hb * LANE
    once = pl.Buffered(1)
    tr_spec = lambda off: pl.BlockSpec((1, hb, nt, LANE, TILE), lambda b, j: (b, off + j, 0, 0, 0), once)
    st_spec = pl.BlockSpec((1, hb, LANE, LANE), lambda b, j: (b, j, 0, 0))
    n_out = n if with_output else SUBLANE
    return pl.pallas_call(
        functools.partial(_dn_kernel, n_tiles=nt, with_output=with_output),
        grid=(bsz, h // hb),
        in_specs=[pl.BlockSpec((1, n, wide), lambda b, j: (b, 0, h // hb + j), once),
                  tr_spec(0), tr_spec(h // hb), tr_spec(2 * h // hb),
                  pl.BlockSpec((1, nt, hb * SUBLANE, TILE), lambda b, j: (b, 0, j, 0)),
                  pl.BlockSpec((1, n, wide), lambda b, j: (b, 0, Z_BLK // hb + j), once),
                  pl.BlockSpec((1, LANE), lambda b, j: (0, 0)),
                  st_spec, st_spec],
        out_specs=[pl.BlockSpec((1, n_out, wide), lambda b, j: (b, 0, j)), st_spec, st_spec],
        out_shape=[jax.ShapeDtypeStruct((bsz, n_out, DN_WIDTH), BF16),
                   jax.ShapeDtypeStruct((bsz, h, LANE, LANE), F32),
                   jax.ShapeDtypeStruct((bsz, h, LANE, LANE), F32)],
        scratch_shapes=[pltpu.VMEM((hb, nt, LANE, TILE), F32),
                        pltpu.VMEM((2 * hb, LANE, LANE), F32)],
        compiler_params=_cparams(("parallel", "parallel"), 58),
        name="delta_net",
    )(qkv_rm, qkv_tr, qkv_tr, qkv_tr, gates, proj, norm_g.reshape(1, LANE), s0f, s0b)


def _lru_scan_block(x, wa, ba, wx, bx, spl, h_in, backward):
    rows, width = x.shape
    xb = x.astype(BF16)

    def gate(ws, bias):
        parts = [jnp.dot(xb[:, k * LRU_BLOCK:(k + 1) * LRU_BLOCK], w, preferred_element_type=F32)
                 for k, w in enumerate(ws)]
        return _sigmoid(jnp.concatenate(parts, axis=1) + bias)

    r = gate(wa, ba)
    gi = gate(wx, bx)
    log_a = -LRU_C * r * spl
    a = jnp.exp(log_a)
    b = jnp.sqrt(1.0 - jnp.exp(2.0 * log_a)) * (gi * x)
    groups = rows // SUBLANE
    a = a.reshape(groups, SUBLANE, width)
    b = b.reshape(groups, SUBLANE, width)
    sub = lax.broadcasted_iota(jnp.int32, a.shape, 1)
    s = 1
    while s < SUBLANE:
        if backward:
            keep = sub < SUBLANE - s
            a_sh = jnp.where(keep, pltpu.roll(a, SUBLANE - s, 1), 1.0)
            b_sh = jnp.where(keep, pltpu.roll(b, SUBLANE - s, 1), 0.0)
        else:
            keep = sub >= s
            a_sh = jnp.where(keep, pltpu.roll(a, s, 1), 1.0)
            b_sh = jnp.where(keep, pltpu.roll(b, s, 1), 0.0)
        b = a * b_sh + b
        a = a * a_sh
        s *= 2
    a = a.reshape(rows, width)
    b = b.reshape(rows, width)
    pieces = [None] * groups
    carry = h_in
    order = range(groups - 1, -1, -1) if backward else range(groups)
    edge = 0 if backward else SUBLANE - 1
    for gidx in order:
        lo = gidx * SUBLANE
        hgrp = b[lo:lo + SUBLANE, :] + a[lo:lo + SUBLANE, :] * carry
        pieces[gidx] = hgrp
        carry = hgrp[edge:edge + 1, :]
    return jnp.concatenate(pieces, axis=0), carry


def _lru_kernel(px_ref, py_ref, cw_ref, cb_ref, wa_ref, ba_ref, wx_ref, bx_ref, lam_ref, h0_ref,
                o_ref, last_ref, xs_ref, hs_ref, *, n_blocks, with_output):
    x = px_ref[0].astype(F32)
    xs_ref[...] = _dwconv_rows(x, cw_ref, 4) + cb_ref[...]
    spl = _softplus(-lam_ref[...])
    wa = [[wa_ref[d, k] for k in range(LRU_GROUP)] for d in range(2)]
    wx = [[wx_ref[d, k] for k in range(LRU_GROUP)] for d in range(2)]

    def body(i, carry):
        hf, hb = carry
        rf = pl.multiple_of(i * LRU_ROWS, LRU_ROWS)
        rb = pl.multiple_of((n_blocks - 1 - i) * LRU_ROWS, LRU_ROWS)
        h_f, hf = _lru_scan_block(xs_ref[pl.ds(rf, LRU_ROWS), :], wa[0], ba_ref[0:1, :],
                                  wx[0], bx_ref[0:1, :], spl[0:1, :], hf, False)
        h_b, hb = _lru_scan_block(xs_ref[pl.ds(rb, LRU_ROWS), :], wa[1], ba_ref[1:2, :],
                                  wx[1], bx_ref[1:2, :], spl[1:2, :], hb, True)
        if with_output:
            hs_ref[pl.ds(rf, LRU_ROWS), :] = hs_ref[pl.ds(rf, LRU_ROWS), :] + h_f
            hs_ref[pl.ds(rb, LRU_ROWS), :] = hs_ref[pl.ds(rb, LRU_ROWS), :] + h_b
        return hf, hb

    if with_output:
        hs_ref[...] = jnp.zeros_like(hs_ref)
    h0 = h0_ref[0]
    hf, hb = lax.fori_loop(0, n_blocks, body, (h0[0:1, :], h0[1:2, :]))
    last_ref[0] = jnp.concatenate([hf, hb], axis=0)
    if with_output:
        y = py_ref[0].astype(F32)
        gelu = 0.5 * y * (1.0 + jnp.tanh(math.sqrt(2.0 / math.pi) * (y + 0.044715 * (y * y * y))))
        o_ref[0] = (hs_ref[...] * gelu).astype(BF16)
    else:
        o_ref[...] = jnp.zeros_like(o_ref)


def _rglru(proj, conv_w, conv_b, w_a, b_a, w_x, b_x, lam, h0, with_output):
    bsz, n, _ = proj.shape
    rows = min(n, LRU_ROWS)
    nb = n // rows
    n_out = n if with_output else SUBLANE
    width = LRU_GROUP * LRU_BLOCK
    lx = LX_BLK * LANE // width
    ly = LY_BLK * LANE // width
    vec2 = pl.BlockSpec((2, width), lambda b, j: (0, j))
    wspec = pl.BlockSpec((2, LRU_GROUP, LRU_BLOCK, LRU_BLOCK), lambda b, j: (0, j, 0, 0))
    return pl.pallas_call(
        functools.partial(_lru_kernel, n_blocks=nb, with_output=with_output),
        grid=(bsz, LRU_BLOCKS // LRU_GROUP),
        in_specs=[pl.BlockSpec((1, n, width), lambda b, j: (b, 0, lx + j)),
                  pl.BlockSpec((1, n, width), lambda b, j: (b, 0, ly + j)),
                  pl.BlockSpec((4, width), lambda b, j: (0, j)),
                  pl.BlockSpec((1, width), lambda b, j: (0, j)),
                  wspec, vec2, wspec, vec2, vec2,
                  pl.BlockSpec((1, 2, width), lambda b, j: (b, 0, j))],
        out_specs=[pl.BlockSpec((1, n_out, width), lambda b, j: (b, 0, j)),
                   pl.BlockSpec((1, 2, width), lambda b, j: (b, 0, j))],
        out_shape=[jax.ShapeDtypeStruct((bsz, n_out, LRU_WIDTH), BF16),
                   jax.ShapeDtypeStruct((bsz, 2, LRU_WIDTH), F32)],
        scratch_shapes=[pltpu.VMEM((n, width), F32), pltpu.VMEM((n, width), F32)],
        compiler_params=_cparams(("parallel", "parallel")),
        name="rglru",
    )(proj, proj, conv_w, conv_b.reshape(1, LRU_WIDTH), w_a, b_a, w_x, b_x, lam, h0)


def _hy_prep_kernel(p0_ref, p1_ref, pv_ref, w0_ref, w1_ref, wv_ref, b0_ref, b1_ref, bv_ref, x0_ref, zz_ref):
    x0 = _dwconv_rows(p0_ref[0].astype(F32), w0_ref, 3) + b0_ref[...]
    x1 = _dwconv_rows(p1_ref[0].astype(F32), w1_ref, 3) + b1_ref[...]
    v = _dwconv_rows(pv_ref[0].astype(F32), wv_ref, 3) + bv_ref[...]
    x0_ref[0] = x0.astype(BF16)
    zz_ref[0] = (x1 * v).astype(BF16)


def _hy_prep(proj, conv_w, conv_b):
    bsz, n, _ = proj.shape
    nblk = HY_WIDTH // LANE
    pspec = lambda off: pl.BlockSpec((1, n, LANE), lambda b, j: (b, 0, HY_BLK + off + j))
    wspec = lambda off: pl.BlockSpec((3, LANE), lambda b, j: (0, off + j))
    bspec = lambda off: pl.BlockSpec((1, LANE), lambda b, j: (0, off + j))
    ospec = pl.BlockSpec((1, n, LANE), lambda b, j: (b, 0, j))
    cb = conv_b.reshape(1, 3 * HY_WIDTH)
    return pl.pallas_call(
        _hy_prep_kernel,
        grid=(bsz, nblk),
        in_specs=[pspec(0), pspec(nblk), pspec(2 * nblk), wspec(0), wspec(nblk), wspec(2 * nblk),
                  bspec(0), bspec(nblk), bspec(2 * nblk)],
        out_specs=[ospec, ospec],
        out_shape=[jax.ShapeDtypeStruct((bsz, n, HY_WIDTH), BF16)] * 2,
        compiler_params=_cparams(("parallel", "parallel")),
        name="hy_prep",
    )(proj, proj, proj, conv_w, conv_w, conv_w, cb, cb, cb)


def _hy_filter_kernel(w1_ref, b1_ref, f1_ref, w2_ref, b2_ref, f2_ref, w3_ref, band_ref, delta_ref,
                      k_ref, *, n, rows):
    def lag(shape):
        p = _row_iota(shape) + (pl.program_id(0) * rows - n)
        return p, jnp.abs(p).astype(F32)

    _, i = lag((rows, LANE))
    lane = _col_iota((rows, LANE))
    t = i * (1.0 / (n - 1))
    ang = band_ref[...] * (i * (2.0 * math.pi / n))
    feat = jnp.where(lane == 0, t,
                     jnp.where(lane <= HY_BANDS, jnp.cos(ang), jnp.where(lane < HY_EMB, -jnp.sin(ang), 0.0)))
    hid = jnp.sin(f1_ref[...] * (jnp.dot(feat, w1_ref[...], precision=HIGHEST,
                                         preferred_element_type=F32) + b1_ref[...]))
    hid = jnp.sin(f2_ref[...] * (jnp.dot(hid, w2_ref[...], precision=HIGHEST,
                                         preferred_element_type=F32) + b2_ref[...]))
    filt = _bdot(hid, w3_ref[...])
    p, iw = lag((rows, HY_WIDTH))
    dec = jnp.exp(-(iw * (1.0 / (n - 1))) * delta_ref[...])
    h_f = filt[:, :HY_WIDTH] * dec
    h_b = filt[:, HY_WIDTH:] * dec
    taps = jnp.where(p > 0, h_f, jnp.where(p < 0, h_b, h_f + h_b))
    k_ref[...] = jnp.where(p == -n, 0.0, taps).astype(BF16)


def _hy_filter(n, w1, b1, f1, w2, b2, f2, w3):
    rows = min(n, 256)
    hid = HY_FILTER_HIDDEN
    w1p = jnp.zeros((LANE, hid), F32).at[:HY_EMB].set(w1)
    bands = np.zeros((1, LANE), np.float32)
    base = np.linspace(1e-4, HY_BANDS - 1, HY_BANDS, dtype=np.float32)
    bands[0, 1:1 + HY_BANDS] = base
    bands[0, 1 + HY_BANDS:HY_EMB] = base
    log_target = math.log(HY_DECAY_TARGET)
    deltas = np.abs(np.linspace(log_target / HY_SLOW_DECAY_PCT, log_target / HY_FAST_DECAY_PCT, HY_WIDTH,
                                dtype=np.float32)).reshape(1, HY_WIDTH)
    full = lambda shape: pl.BlockSpec(shape, lambda i: (0,) * len(shape))
    return pl.pallas_call(
        functools.partial(_hy_filter_kernel, n=n, rows=rows),
        grid=(2 * n // rows,),
        in_specs=[full((LANE, hid)), full((1, hid)), full((1, hid)), full((hid, hid)), full((1, hid)),
                  full((1, hid)), full((hid, 2 * HY_WIDTH)), full((1, LANE)), full((1, HY_WIDTH))],
        out_specs=pl.BlockSpec((rows, HY_WIDTH), lambda i: (i, 0)),
        out_shape=jax.ShapeDtypeStruct((2 * n, HY_WIDTH), BF16),
        compiler_params=_cparams(("parallel",)),
        name="hy_filter",
    )(w1p, b1.reshape(1, hid), f1.reshape(1, hid), w2, b2.reshape(1, hid), f2.reshape(1, hid), w3,
      jnp.asarray(bands), jnp.asarray(deltas))


DFT_GROUP = 64


def _dft_table_kernel(c_ref, s_ref, cb_ref, sb_ref, *, m, ncols, col0, rows, transposed, blank_first):
    period = 4 * m
    scale = 2.0 * math.pi / period
    col = _col_iota((1, ncols)) + col0

    @pl.when(pl.program_id(0) == 0)
    def _():
        r2 = _row_iota((DFT_GROUP, ncols))
        c2 = _col_iota((DFT_GROUP, ncols)) + col0
        ph = (r2 * (2 * c2 + 1)) if transposed else ((2 * r2 + 1) * c2)
        ang = (ph & (period - 1)).astype(F32) * scale
        cb_ref[...] = jnp.cos(ang)
        sb_ref[...] = jnp.sin(ang)

    for g in range(rows // DFT_GROUP):
        r1 = pl.program_id(0) * (rows // DFT_GROUP) + g
        ph = (DFT_GROUP * r1) * (2 * col + 1) if transposed else (2 * DFT_GROUP * r1) * col
        ang = (ph & (period - 1)).astype(F32) * scale
        ca = jnp.cos(ang)
        sa = jnp.sin(ang)
        cb = cb_ref[...]
        sb = sb_ref[...]
        c_tile = ca * cb - sa * sb
        s_tile = sa * cb + ca * sb
        if blank_first:
            first = _col_iota((DFT_GROUP, ncols)) == 0
            c_tile = jnp.where(first, 0.0, c_tile)
            s_tile = jnp.where(first, 0.0, s_tile)
        c_ref[g * DFT_GROUP:(g + 1) * DFT_GROUP, :] = c_tile.astype(BF16)
        s_ref[g * DFT_GROUP:(g + 1) * DFT_GROUP, :] = s_tile.astype(BF16)


def _dft_tables(m, transposed=False, two_sided=False):
    rows = min(m, 256)
    ncols = 2 * m if two_sided else m
    spec = pl.BlockSpec((rows, ncols), lambda i: (i, 0))
    return pl.pallas_call(
        functools.partial(_dft_table_kernel, m=m, ncols=ncols, col0=3 * m if two_sided else 0, rows=rows,
                          transposed=transposed, blank_first=two_sided),
        grid=(m // rows,),
        in_specs=[],
        out_specs=[spec, spec],
        out_shape=[jax.ShapeDtypeStruct((m, ncols), BF16)] * 2,
        scratch_shapes=[pltpu.VMEM((DFT_GROUP, ncols), F32), pltpu.VMEM((DFT_GROUP, ncols), F32)],
        compiler_params=_cparams(("arbitrary",)),
        name="dft_tables",
    )()


def _hy_tables(m):
    return _dft_tables(m) + _dft_tables(m, transposed=True) + _dft_tables(m, two_sided=True)


def _hy_ktrans_kernel(cw_ref, sw_ref, lo_ref, hi_ref, kc_ref, ks_ref, *, m):
    dot = functools.partial(jnp.dot, preferred_element_type=F32)
    lo = lo_ref[...]
    hi = hi_ref[...]
    kc_ref[0] = dot(cw_ref[:, :m], lo) + dot(cw_ref[:, m:], hi)
    ks_ref[0] = dot(sw_ref[:, :m], lo) + dot(sw_ref[:, m:], hi)


def _hy_ktrans(cw, sw, taps, m):
    nd = taps.shape[0] // m - 1
    w = taps.shape[1]
    tn = 512
    tab = pl.BlockSpec((m, 2 * m), lambda e, j: (0, 0))
    ospec = pl.BlockSpec((1, m, tn), lambda e, j: (e, 0, j))
    return pl.pallas_call(
        functools.partial(_hy_ktrans_kernel, m=m),
        grid=(nd, w // tn),
        in_specs=[tab, tab, pl.BlockSpec((m, tn), lambda e, j: (e, j)), pl.BlockSpec((m, tn), lambda e, j: (e + 1, j))],
        out_specs=[ospec, ospec],
        out_shape=[jax.ShapeDtypeStruct((nd, m, w), F32)] * 2,
        compiler_params=_cparams(("parallel", "parallel")),
        name="hy_ktrans",
    )(cw, sw, taps, taps)


def _hy_fwd_kernel(c_ref, s_ref, zz_ref, kc_ref, ks_ref, a_ref, b_ref, *, m, nb):
    dot = functools.partial(jnp.dot, preferred_element_type=F32)
    c = c_ref[...]
    s = s_ref[...]
    uc = [dot(c, zz_ref[0, j * m:(j + 1) * m, :]) for j in range(nb)]
    us = [dot(s, zz_ref[0, j * m:(j + 1) * m, :]) for j in range(nb)]
    for i in range(nb):
        acc_a = acc_b = None
        for j in range(nb):
            kc = kc_ref[i - j + nb - 1]
            ks = ks_ref[i - j + nb - 1]
            ta = uc[j] * kc - us[j] * ks
            tb = uc[j] * ks + us[j] * kc
            acc_a = ta if acc_a is None else acc_a + ta
            acc_b = tb if acc_b is None else acc_b + tb
        a_ref[0, i] = acc_a.astype(BF16)
        b_ref[0, i] = acc_b.astype(BF16)


def _hy_forward(ctab, stab, zz, kc, ks, m):
    bsz, n, w = zz.shape
    nb = n // m
    nd = 2 * nb - 1
    tf = min(m, 256)
    tn = 512
    tab = pl.BlockSpec((tf, m), lambda b, j, i: (i, 0))
    kspec = pl.BlockSpec((nd, tf, tn), lambda b, j, i: (0, i, j))
    ospec = pl.BlockSpec((1, nb, tf, tn), lambda b, j, i: (b, 0, i, j))
    return pl.pallas_call(
        functools.partial(_hy_fwd_kernel, m=m, nb=nb),
        grid=(bsz, w // tn, m // tf),
        in_specs=[tab, tab, pl.BlockSpec((1, n, tn), lambda b, j, i: (b, 0, j)), kspec, kspec],
        out_specs=[ospec, ospec],
        out_shape=[jax.ShapeDtypeStruct((bsz, nb, m, w), BF16)] * 2,
        compiler_params=_cparams(("parallel", "parallel", "parallel")),
        name="hy_forward",
    )(ctab, stab, zz, kc, ks)


def _hy_inv_kernel(ct_ref, st_ref, a_ref, b_ref, x0_ref, zz_ref, bias_ref, o_ref, *, m):
    y = (jnp.dot(ct_ref[...], a_ref[0, 0], preferred_element_type=F32)
         + jnp.dot(st_ref[...], b_ref[0, 0], preferred_element_type=F32)) * (1.0 / m)
    zz = zz_ref[0].astype(F32)
    o_ref[0] = (x0_ref[0].astype(F32) * (y + zz * bias_ref[...])).astype(BF16)


def _hy_inverse(cttab, sttab, a, bq, x0, zz, bias, m):
    bsz, n, w = zz.shape
    nb = n // m
    tn = 512
    tab = pl.BlockSpec((m, m), lambda b, i, j: (0, 0))
    spec4 = pl.BlockSpec((1, 1, m, tn), lambda b, i, j: (b, i, 0, j))
    tile = pl.BlockSpec((1, m, tn), lambda b, i, j: (b, i, j))
    return pl.pallas_call(
        functools.partial(_hy_inv_kernel, m=m),
        grid=(bsz, nb, w // tn),
        in_specs=[tab, tab, spec4, spec4, tile, tile, pl.BlockSpec((1, tn), lambda b, i, j: (0, j))],
        out_specs=tile,
        out_shape=jax.ShapeDtypeStruct((bsz, n, w), BF16),
        compiler_params=_cparams(("parallel", "parallel", "parallel")),
        name="hy_inverse",
    )(cttab, sttab, a, bq, x0, zz, bias.reshape(1, w))


def _merge_kernel(odn_ref, ohy_ref, olru_ref, gdn_ref, ghy_ref, glru_ref, wdn_ref, why_ref, wlru_ref,
                  wout_ref, x_ref, gate_ref, o_ref):
    m = _sigmoid(gdn_ref[0].astype(F32)) * jnp.dot(odn_ref[0], wdn_ref[...], preferred_element_type=F32)
    m = m + _sigmoid(ghy_ref[0].astype(F32)) * jnp.dot(ohy_ref[0], why_ref[...], preferred_element_type=F32)
    m = m + _sigmoid(glru_ref[0].astype(F32)) * jnp.dot(olru_ref[0], wlru_ref[...], preferred_element_type=F32)
    y = jnp.dot(m.astype(BF16), wout_ref[...], preferred_element_type=F32)
    o_ref[0] = x_ref[0] + gate_ref[0] * y


def _merge(o_dn, o_hy, o_lru, proj, w_dn, w_hy, w_lru, w_out, x, gate):
    bsz, n, d = x.shape
    tm = min(n, 256)
    nblk = d // 1024
    act = pl.BlockSpec((1, tm, d), lambda b, i: (b, i, 0))
    gspec = lambda k: pl.BlockSpec((1, tm, d), lambda b, i: (b, i, GATE_BLK * LANE // d + k * nblk))
    wspec = pl.BlockSpec((d, d), lambda b, i: (0, 0))
    return pl.pallas_call(
        _merge_kernel,
        grid=(bsz, n // tm),
        in_specs=[act, act, act, gspec(0), gspec(1), gspec(2), wspec, wspec, wspec, wspec, act,
                  pl.BlockSpec((1, 1, d), lambda b, i: (b, 0, 0))],
        out_specs=act,
        out_shape=jax.ShapeDtypeStruct((bsz, n, d), F32),
        compiler_params=_cparams(("parallel", "parallel")),
        name="merge",
    )(o_dn, o_hy, o_lru, proj, proj, proj, w_dn, w_hy, w_lru, w_out, x, gate)


def _ffn_act_kernel(ug_ref, uv_ref, wg_ref, wv_ref, o_ref, up_ref, mid_ref, dn_ref, *, rows, cols, n):
    tile = FFN_TILE
    a = _row_iota((tile, tile))
    b = _col_iota((tile, tile))
    c = _mod_pow2(a, cols)
    lmat = jnp.where((b == a - 1) & (c >= 1), 1.0, 0.0).astype(BF16)
    rmat = jnp.where((b == a + 1) & (c <= cols - 2), 1.0, 0.0).astype(BF16)
    taps = (-1, 0, 1) if rows > 1 else (0,)
    dst = {-1: up_ref, 0: mid_ref, 1: dn_ref}
    if rows > 1:
        pad = jnp.zeros((2, cols, FFN_WIDE), BF16)
        up_ref[:, 0:cols, :] = pad
        dn_ref[:, n + cols:n + 2 * cols, :] = pad

    def taps_of(i, carry):
        r0 = pl.multiple_of(i * tile, tile)
        for idx, (u_ref, w_ref) in enumerate(((ug_ref, wg_ref), (uv_ref, wv_ref))):
            u = u_ref[0, pl.ds(r0, tile), :]
            left = jnp.dot(lmat, u, preferred_element_type=F32).astype(BF16)
            right = jnp.dot(rmat, u, preferred_element_type=F32).astype(BF16)
            for di in taps:
                k = 3 * (di + 1)
                w = [w_ref[k + j:k + j + 1, :].astype(BF16) for j in range(3)]
                off = pl.multiple_of(r0 + (cols if di != 0 else 0), cols)
                dst[di][idx, pl.ds(off, tile), :] = left * w[0] + u * w[1] + right * w[2]
        return carry

    def combine(i, carry):
        r0 = pl.multiple_of(i * tile, tile)
        vals = []
        for idx in range(2):
            acc = mid_ref[idx, pl.ds(r0, tile), :]
            if rows > 1:
                acc = (acc + up_ref[idx, pl.ds(r0, tile), :]
                       + dn_ref[idx, pl.ds(pl.multiple_of(r0 + 2 * cols, cols), tile), :])
            vals.append(acc.astype(F32))
        o_ref[0, pl.ds(r0, tile), :] = (_silu(vals[0]) * vals[1]).astype(BF16)
        return carry

    lax.fori_loop(0, n // tile, taps_of, 0, unroll=min(2, n // tile))
    lax.fori_loop(0, n // tile, combine, 0)


def _ffn_act(u, conv_w, rows, cols):
    bsz, n, _ = u.shape
    nblk = FFN_HIDDEN // FFN_WIDE
    padded = n + 2 * cols if rows > 1 else SUBLANE * 2
    return pl.pallas_call(
        functools.partial(_ffn_act_kernel, rows=rows, cols=cols, n=n),
        grid=(bsz, nblk),
        in_specs=[pl.BlockSpec((1, n, FFN_WIDE), lambda b, j: (b, 0, j)),
                  pl.BlockSpec((1, n, FFN_WIDE), lambda b, j: (b, 0, nblk + j)),
                  pl.BlockSpec((9, FFN_WIDE), lambda b, j: (0, j)),
                  pl.BlockSpec((9, FFN_WIDE), lambda b, j: (0, nblk + j))],
        out_specs=pl.BlockSpec((1, n, FFN_WIDE), lambda b, j: (b, 0, j)),
        out_shape=jax.ShapeDtypeStruct((bsz, n, FFN_HIDDEN), BF16),
        scratch_shapes=[pltpu.VMEM((2, padded, FFN_WIDE), BF16), pltpu.VMEM((2, n, FFN_WIDE), BF16),
                        pltpu.VMEM((2, padded, FFN_WIDE), BF16)],
        compiler_params=_cparams(("parallel", "parallel")),
        name="ffn_act",
    )(u, u, conv_w, conv_w)


def _final_norm_kernel(x_ref, g_ref, o_ref):
    x = x_ref[0]
    o_ref[0] = x * lax.rsqrt(jnp.mean(x * x, axis=-1, keepdims=True) + NORM_EPS) * g_ref[...]


def _final_norm(x, gain):
    bsz, n, d = x.shape
    tm = min(n, 512)
    spec = pl.BlockSpec((1, tm, d), lambda b, i: (b, i, 0))
    return pl.pallas_call(
        _final_norm_kernel,
        grid=(bsz, n // tm),
        in_specs=[spec, pl.BlockSpec((1, d), lambda b, i: (0, 0))],
        out_specs=spec,
        out_shape=jax.ShapeDtypeStruct((bsz, n, d), F32),
        compiler_params=_cparams(("parallel", "parallel")),
        name="final_norm",
    )(x, gain.reshape(1, d))


def _gate_column_map():
    src = np.full((LANE,), -1, np.int32)
    isdec = np.zeros((1, LANE), np.float32)
    dirs = np.zeros((LANE,), np.int32)
    for h in range(DN_HEADS):
        for slot, (d, kind) in enumerate(((0, 0), (0, 1), (1, 0), (1, 1), (0, 0), (1, 0))):
            src[h * SUBLANE + slot] = d * 2 * DN_HEADS + kind * DN_HEADS + h
            isdec[0, h * SUBLANE + slot] = 1.0 if kind == 0 else 0.0
            dirs[h * SUBLANE + slot] = d
    return src, isdec, dirs


def _split_in_proj(w_in):
    o = np.cumsum((3 * DN_WIDTH, DN_WIDTH, 4 * DN_HEADS, 3 * HY_WIDTH, LRU_WIDTH, LRU_WIDTH)).tolist()
    w_main = jnp.concatenate([w_in[:, :o[1]], w_in[:, o[2]:]], axis=1).astype(BF16)
    src, _, _ = _gate_column_map()
    w_ab = w_in[:, o[1]:o[2]]
    w_gate = jnp.where(jnp.asarray(src >= 0)[None, :], w_ab[:, np.maximum(src, 0)], 0.0).astype(BF16)
    return w_main, w_gate


def _gate_params(a_log, dt_bias):
    _, isdec, dirs = _gate_column_map()
    head = (np.arange(LANE) // SUBLANE).astype(np.int32)
    alog_c = a_log[dirs, head].reshape(1, LANE)
    dtb_c = dt_bias[dirs, head].reshape(1, LANE)
    return alog_c, dtb_c, jnp.asarray(isdec)


def _token_mixer(x, mods, lp, states, tables, with_output):
    bsz, n, _ = x.shape
    shift1, scale1, gate1 = mods[0], mods[1], mods[2]
    proj, ab = _norm_mod_matmul(x, lp["norm1_g"], shift1, scale1, lp["w_main"], BF16, 2048, 1024, lp["w_gate"])

    qkv_rm, qkv_tr = _dn_prep(proj, lp["dn_conv_w"])
    gates = _dn_gates(ab, *lp["gate_params"])
    o_dn, s_f, s_b = _delta_net(qkv_rm, qkv_tr, gates, proj, lp["dn_norm_g"], states[0], states[1], with_output)
    o_lru, h_last = _rglru(proj, lp["lru_conv_w"], lp["lru_conv_b"], lp["lru_w_a"], lp["lru_b_a"],
                           lp["lru_w_x"], lp["lru_b_x"], lp["lru_lambda"], states[2], with_output)
    new_states = (s_f, s_b, h_last)
    if not with_output:
        return None, new_states

    m = min(n, HY_BLOCK)
    ctab, stab, cttab, sttab, cwtab, swtab = tables
    taps = _hy_filter(n, lp["hy_w1"], lp["hy_b1"], lp["hy_f1"], lp["hy_w2"], lp["hy_b2"], lp["hy_f2"], lp["hy_w3"])
    kc, ks = _hy_ktrans(cwtab, swtab, taps, m)
    x0, zz = _hy_prep(proj, lp["hy_conv_w"], lp["hy_conv_b"])
    a, bq = _hy_forward(ctab, stab, zz, kc, ks, m)
    o_hy = _hy_inverse(cttab, sttab, a, bq, x0, zz, lp["hy_bias"], m)

    x = _merge(o_dn, o_hy, o_lru, proj, lp["w_proj_dn"], lp["w_proj_hy"], lp["w_proj_lru"], lp["w_out"], x, gate1)
    return x, new_states


def _conv_ffn(x, mods, lp, rows, cols):
    u = _norm_mod_matmul(x, lp["norm2_g"], mods[3], mods[4], lp["ffn_up"], BF16, 1024, FFN_HIDDEN)
    act = _ffn_act(u, lp["ffn_conv_w"], rows, cols)
    return _matmul_residual(act, lp["ffn_down"], x, mods[5])


def kernel(x, c, ctx, c_ctx, w_mod, b_mod, norm1_g, norm2_g, w_in, dn_conv_w, dn_a_log, dn_dt_bias, dn_norm_g,
           hy_conv_w, hy_conv_b, hy_w1, hy_b1, hy_f1, hy_w2, hy_b2, hy_f2, hy_w3, hy_bias,
           lru_conv_w, lru_conv_b, lru_w_a, lru_b_a, lru_w_x, lru_b_x, lru_lambda,
           w_proj_dn, w_proj_hy, w_proj_lru, w_out, ffn_up, ffn_conv_w, ffn_down, final_norm_g):
    bsz, n_lat, d = x.shape
    n_ctx = ctx.shape[1]
    depth = w_in.shape[0]
    rows = n_lat // GRID_W

    cvec = jnp.zeros((SUBLANE, d), F32).at[:bsz].set(c).at[bsz].set(c_ctx)
    lat_tables = _hy_tables(min(n_lat, HY_BLOCK))
    ctx_tables = _hy_tables(min(n_ctx, HY_BLOCK))
    zero_states = (jnp.zeros((bsz, DN_HEADS, LANE, LANE), F32), jnp.zeros((bsz, DN_HEADS, LANE, LANE), F32),
                   jnp.zeros((bsz, 2, LRU_WIDTH), F32))

    xc = ctx
    for l in range(depth):
        ctx_needed = l < depth - 1
        w_main, w_gate = _split_in_proj(w_in[l])
        lp = dict(
            norm1_g=norm1_g[l], norm2_g=norm2_g[l], w_main=w_main, w_gate=w_gate,
            dn_conv_w=dn_conv_w[l], gate_params=_gate_params(dn_a_log[l], dn_dt_bias[l]), dn_norm_g=dn_norm_g[l],
            hy_conv_w=hy_conv_w[l], hy_conv_b=hy_conv_b[l], hy_w1=hy_w1[l], hy_b1=hy_b1[l], hy_f1=hy_f1[l],
            hy_w2=hy_w2[l], hy_b2=hy_b2[l], hy_f2=hy_f2[l], hy_w3=hy_w3[l], hy_bias=hy_bias[l],
            lru_conv_w=lru_conv_w[l], lru_conv_b=lru_conv_b[l], lru_w_a=lru_w_a[l].astype(BF16),
            lru_b_a=lru_b_a[l], lru_w_x=lru_w_x[l].astype(BF16), lru_b_x=lru_b_x[l], lru_lambda=lru_lambda[l],
            w_proj_dn=w_proj_dn[l].astype(BF16), w_proj_hy=w_proj_hy[l].astype(BF16),
            w_proj_lru=w_proj_lru[l].astype(BF16), w_out=w_out[l].astype(BF16),
            ffn_up=ffn_up[l].astype(BF16), ffn_conv_w=ffn_conv_w[l].reshape(9, 2 * FFN_HIDDEN),
            ffn_down=ffn_down[l].astype(BF16))
        mod = _modulation(cvec, w_mod[l].astype(BF16), b_mod[l])
        lat_mod = [mod[:bsz, k * d:(k + 1) * d].reshape(bsz, 1, d) for k in range(N_MOD)]
        ctx_mod = [jnp.broadcast_to(mod[bsz:bsz + 1, k * d:(k + 1) * d].reshape(1, 1, d), (bsz, 1, d))
                   for k in range(N_MOD)]

        xc_new, ctx_states = _token_mixer(xc, ctx_mod, lp, zero_states, ctx_tables, ctx_needed)
        x, _ = _token_mixer(x, lat_mod, lp, ctx_states, lat_tables, True)
        x = _conv_ffn(x, lat_mod, lp, rows, GRID_W)
        if ctx_needed:
            xc = _conv_ffn(xc_new, ctx_mod, lp, 1, n_ctx)
    return _final_norm(x, final_norm_g)
```

```python
import functools
import math

import numpy as np
import jax
import jax.numpy as jnp
from jax import lax
from jax.experimental import pallas as pl
from jax.experimental.pallas import tpu as pltpu

F32 = jnp.float32
BF16 = jnp.bfloat16
HIGHEST = lax.Precision.HIGHEST

D_MODEL = 1024
DEPTH = 2
GRID_W = 64
NORM_EPS = 1e-6
N_MOD = 6

DN_HEADS = 8
DN_HEAD_DIM = 128
DN_WIDTH = DN_HEADS * DN_HEAD_DIM
HY_WIDTH = 1024
HY_EMB = 33
HY_BANDS = (HY_EMB - 1) // 2
HY_FILTER_HIDDEN = 64
HY_FAST_DECAY_PCT = 0.3
HY_SLOW_DECAY_PCT = 1.5
HY_DECAY_TARGET = 1e-2
LRU_WIDTH = 1024
LRU_BLOCKS = 8
LRU_BLOCK = LRU_WIDTH // LRU_BLOCKS
LRU_C = 8.0
FFN_HIDDEN = 2816

LANE = 128
SUBLANE = 8
TILE = 256
CHUNK = 128
DN_HEADS_PER_STEP = 4
DN_PACK = 64
HY_BLOCK = 1024
LRU_ROWS = 256
LRU_GROUP = 2
FFN_TILE = 256
FFN_WIDE = 256
MIB = 1024 * 1024

QKV_BLK = 0
Z_BLK = 24
HY_BLK = 32
LX_BLK = 56
LY_BLK = 64
GATE_BLK = 72
N_MAIN = 96 * LANE


def _cparams(sem, vmem_mib=48):
    return pltpu.CompilerParams(dimension_semantics=sem, vmem_limit_bytes=vmem_mib * MIB)


def _sigmoid(x):
    return 1.0 / (1.0 + jnp.exp(-x))


def _silu(x):
    return x * _sigmoid(x)


def _softplus(x):
    return jnp.maximum(x, 0.0) + jnp.log(1.0 + jnp.exp(-jnp.abs(x)))


def _row_iota(shape):
    return lax.broadcasted_iota(jnp.int32, shape, 0)


def _col_iota(shape):
    return lax.broadcasted_iota(jnp.int32, shape, 1)


def _div_pow2(x, k):
    assert k & (k - 1) == 0
    return x >> (k.bit_length() - 1)


def _mod_pow2(x, k):
    assert k & (k - 1) == 0
    return x & (k - 1)


def _bdot(a, b):
    return jnp.dot(a.astype(BF16), b.astype(BF16), preferred_element_type=F32)


def _mod_kernel(c_ref, w_ref, b_ref, o_ref):
    o_ref[...] = _bdot(_silu(c_ref[...]), w_ref[...]) + b_ref[...]


def _modulation(cvec, w_mod, b_mod):
    n = w_mod.shape[1]
    tn = 1024
    return pl.pallas_call(
        _mod_kernel,
        grid=(n // tn,),
        in_specs=[pl.BlockSpec((SUBLANE, D_MODEL), lambda j: (0, 0)),
                  pl.BlockSpec((D_MODEL, tn), lambda j: (0, j)),
                  pl.BlockSpec((1, tn), lambda j: (0, j))],
        out_specs=pl.BlockSpec((SUBLANE, tn), lambda j: (0, j)),
        out_shape=jax.ShapeDtypeStruct((SUBLANE, n), F32),
        compiler_params=_cparams(("parallel",)),
        name="modulation",
    )(cvec, w_mod, b_mod.reshape(1, n))


def _nmm_kernel(x_ref, g_ref, sh_ref, sc_ref, w_ref, *rest, with_side):
    o_ref = rest[1] if with_side else rest[0]
    h_ref = rest[-1]

    @pl.when(pl.program_id(2) == 0)
    def _():
        x = x_ref[0]
        y = x * lax.rsqrt(jnp.mean(x * x, axis=-1, keepdims=True) + NORM_EPS) * g_ref[...]
        h_ref[...] = (y * (1.0 + sc_ref[0]) + sh_ref[0]).astype(BF16)
        if with_side:
            rest[2][0] = jnp.dot(h_ref[...], rest[0][...], preferred_element_type=F32)

    o_ref[0] = jnp.dot(h_ref[...], w_ref[...], preferred_element_type=F32).astype(o_ref.dtype)


def _norm_mod_matmul(x, gain, shift, scale, w, out_dtype, tm, tn, w_side=None):
    bsz, n, d = x.shape
    nout = w.shape[1]
    tm = min(n, tm)
    with_side = w_side is not None
    in_specs = [pl.BlockSpec((1, tm, d), lambda b, i, j: (b, i, 0)),
                pl.BlockSpec((1, d), lambda b, i, j: (0, 0)),
                pl.BlockSpec((1, 1, d), lambda b, i, j: (b, 0, 0)),
                pl.BlockSpec((1, 1, d), lambda b, i, j: (b, 0, 0)),
                pl.BlockSpec((d, tn), lambda b, i, j: (0, j))]
    out_specs = [pl.BlockSpec((1, tm, tn), lambda b, i, j: (b, i, j))]
    out_shape = [jax.ShapeDtypeStruct((bsz, n, nout), out_dtype)]
    args = [x, gain.reshape(1, d), shift, scale, w]
    if with_side:
        ns = w_side.shape[1]
        in_specs.append(pl.BlockSpec((d, ns), lambda b, i, j: (0, 0)))
        out_specs.append(pl.BlockSpec((1, tm, ns), lambda b, i, j: (b, i, 0)))
        out_shape.append(jax.ShapeDtypeStruct((bsz, n, ns), F32))
        args.append(w_side)
    out = pl.pallas_call(
        functools.partial(_nmm_kernel, with_side=with_side),
        grid=(bsz, n // tm, nout // tn),
        in_specs=in_specs,
        out_specs=out_specs,
        out_shape=out_shape,
        scratch_shapes=[pltpu.VMEM((tm, d), BF16)],
        compiler_params=_cparams(("parallel", "parallel", "arbitrary")),
        name="norm_mod_matmul",
    )(*args)
    return out if with_side else out[0]


def _mm_kernel(a_ref, b_ref, o_ref):
    o_ref[...] = jnp.dot(a_ref[...], b_ref[...], preferred_element_type=F32).astype(o_ref.dtype)


def _matmul(a, b, out_dtype, tm, tn):
    m, k = a.shape
    n = b.shape[1]
    return pl.pallas_call(
        _mm_kernel,
        grid=(m // tm, n // tn),
        in_specs=[pl.BlockSpec((tm, k), lambda i, j: (i, 0)),
                  pl.BlockSpec((k, tn), lambda i, j: (0, j))],
        out_specs=pl.BlockSpec((tm, tn), lambda i, j: (i, j)),
        out_shape=jax.ShapeDtypeStruct((m, n), out_dtype),
        compiler_params=_cparams(("parallel", "parallel")),
        name="matmul",
    )(a, b)


def _mm_res_kernel(a_ref, b_ref, x_ref, g_ref, *rest, final_norm):
    o_ref = rest[-1]
    out = x_ref[0] + g_ref[0] * jnp.dot(a_ref[0], b_ref[...], preferred_element_type=F32)
    if final_norm:
        out = out * lax.rsqrt(jnp.mean(out * out, axis=-1, keepdims=True) + NORM_EPS) * rest[0][...]
    o_ref[0] = out


def _matmul_residual(a, w, x, gate, final_gain=None):
    bsz, n, k = a.shape
    d = w.shape[1]
    tm = min(n, 512)
    in_specs = [pl.BlockSpec((1, tm, k), lambda b, i: (b, i, 0)),
                pl.BlockSpec((k, d), lambda b, i: (0, 0)),
                pl.BlockSpec((1, tm, d), lambda b, i: (b, i, 0)),
                pl.BlockSpec((1, 1, d), lambda b, i: (b, 0, 0))]
    args = [a, w, x, gate]
    if final_gain is not None:
        in_specs.append(pl.BlockSpec((1, d), lambda b, i: (0, 0)))
        args.append(final_gain.reshape(1, d))
    return pl.pallas_call(
        functools.partial(_mm_res_kernel, final_norm=final_gain is not None),
        grid=(bsz, n // tm),
        in_specs=in_specs,
        out_specs=pl.BlockSpec((1, tm, d), lambda b, i: (b, i, 0)),
        out_shape=jax.ShapeDtypeStruct((bsz, n, d), F32),
        compiler_params=_cparams(("parallel", "parallel")),
        name="matmul_residual",
    )(*args)


def _dwconv_rows(x, w_ref, k):
    n = x.shape[0]
    left = (k - 1) // 2
    t = _row_iota((SUBLANE, x.shape[1]))
    acc = x * w_ref[left:left + 1, :]
    head_fix = jnp.zeros((SUBLANE, x.shape[1]), F32)
    tail_fix = jnp.zeros((SUBLANE, x.shape[1]), F32)
    for j in range(k):
        off = j - left
        if off == 0:
            continue
        term = pltpu.roll(x, (-off) % n, 0) * w_ref[j:j + 1, :]
        acc = acc + term
        if off < 0:
            head_fix = head_fix + jnp.where(t < -off, term[:SUBLANE, :], 0.0)
        else:
            tail_fix = tail_fix + jnp.where(t >= SUBLANE - off, term[n - SUBLANE:, :], 0.0)
    return jnp.concatenate([acc[:SUBLANE, :] - head_fix, acc[SUBLANE:n - SUBLANE, :],
                            acc[n - SUBLANE:, :] - tail_fix], axis=0)


def _dn_prep_kernel(p_ref, w_ref, rm_ref, tr_ref, *, n_tiles):
    c = pl.program_id(1)
    x = p_ref[0].astype(F32)
    y = _silu(_dwconv_rows(x, w_ref, 4))
    nrm = y * lax.rsqrt(jnp.sum(y * y, axis=-1, keepdims=True) + 1e-6)
    nrm = nrm * jnp.where(c < DN_HEADS, DN_HEAD_DIM ** -0.5, 1.0)
    y = jnp.where(c < 2 * DN_HEADS, nrm, y)
    rm_ref[0] = y.astype(BF16)
    for t in range(n_tiles):
        tr_ref[0, 0, t] = y[t * TILE:(t + 1) * TILE, :].T.astype(BF16)


def _dn_prep(proj, conv_w):
    bsz, n, _ = proj.shape
    nt = n // TILE
    nc = 3 * DN_HEADS
    return pl.pallas_call(
        functools.partial(_dn_prep_kernel, n_tiles=nt),
        grid=(bsz, nc),
        in_specs=[pl.BlockSpec((1, n, LANE), lambda b, c: (b, 0, QKV_BLK + c)),
                  pl.BlockSpec((4, LANE), lambda b, c: (0, c))],
        out_specs=[pl.BlockSpec((1, n, LANE), lambda b, c: (b, 0, c)),
                   pl.BlockSpec((1, 1, nt, LANE, TILE), lambda b, c: (b, c, 0, 0, 0))],
        out_shape=[jax.ShapeDtypeStruct((bsz, n, nc * LANE), BF16),
                   jax.ShapeDtypeStruct((bsz, nc, nt, LANE, TILE), BF16)],
        compiler_params=_cparams(("parallel", "parallel")),
        name="dn_prep",
    )(proj, conv_w)


def _dn_gate_kernel(ab_ref, alog_ref, dtb_ref, isdec_ref, o_ref):
    x = ab_ref[0]
    dec = -jnp.exp(alog_ref[...]) * _softplus(x + dtb_ref[...])
    e = jnp.where(isdec_ref[...] > 0.5, dec, _sigmoid(x))
    et = e.T
    s = _row_iota((TILE, TILE))
    t = _col_iota((TILE, TILE))
    same = _div_pow2(s, CHUNK) == _div_pow2(t, CHUNK)
    prefix = jnp.where(same & (s <= t), 1.0, 0.0)
    suffix = jnp.where(same & (s >= t), 1.0, 0.0)
    total = jnp.where(same, 1.0, 0.0)
    pre = jnp.dot(et, prefix, precision=HIGHEST, preferred_element_type=F32)
    suf = jnp.dot(et, suffix, precision=HIGHEST, preferred_element_type=F32)
    tot = jnp.dot(et, total, precision=HIGHEST, preferred_element_type=F32)
    slot = _mod_pow2(_row_iota((LANE, TILE)), SUBLANE)
    o_ref[0, 0] = jnp.where(slot == 0, pre, jnp.where(slot == 2, suf, jnp.where(slot >= 4, tot, et)))


def _dn_gates(ab, alog_c, dtb_c, isdec_c):
    bsz, n, _ = ab.shape
    nt = n // TILE
    vec = pl.BlockSpec((1, LANE), lambda b, i: (0, 0))
    return pl.pallas_call(
        _dn_gate_kernel,
        grid=(bsz, nt),
        in_specs=[pl.BlockSpec((1, TILE, LANE), lambda b, i: (b, i, 0)), vec, vec, vec],
        out_specs=pl.BlockSpec((1, 1, LANE, TILE), lambda b, i: (b, i, 0, 0)),
        out_shape=jax.ShapeDtypeStruct((bsz, nt, LANE, TILE), F32),
        compiler_params=_cparams(("parallel", "parallel")),
        name="dn_gates",
    )(ab, alog_c, dtb_c, isdec_c)


def _dn_masks(mask_ref):
    a = _row_iota((TILE, TILE))
    b = _col_iota((TILE, TILE))
    apart = a ^ b
    n_levels = CHUNK.bit_length() - 2
    for lg in range(1, n_levels + 1):
        mask_ref[lg - 1] = jnp.where((apart >> lg) == 1, 1.0, 0.0).astype(BF16)
    for k, pack in enumerate((DN_PACK, 2 * DN_PACK)):
        mask_ref[n_levels + k] = jnp.where(_div_pow2(a, pack) == _div_pow2(b, pack), 1.0, 0.0).astype(BF16)


def _dn_tiles(chains, mask_ref):
    a = _row_iota((TILE, TILE))
    b = _col_iota((TILE, TILE))
    same = _div_pow2(a, CHUNK) == _div_pow2(b, CHUNK)
    apart = a ^ b
    n_chunks = TILE // CHUNK
    dot = functools.partial(jnp.dot, preferred_element_type=F32)

    def stack(v, reps):
        return jnp.concatenate([v] * reps, axis=0)

    kk = [dot(c["k_rm"], c["kt"]) for c in chains]
    kq = [dot(c["k_rm"], c["qt"]) for c in chains]
    xs, pw, attn = [], [], []
    for c, kk_c, kq_c in zip(chains, kk, kq):
        incl = same & ((a >= b) if c["backward"] else (a <= b))
        gcb = jnp.broadcast_to(c["gc"], (TILE, TILE))
        diff = gcb - gcb.T
        decay = jnp.where(incl, jnp.exp(jnp.where(incl, diff, 0.0)), 0.0)
        attn.append((kq_c * decay).astype(BF16))
        x = kk_c * decay * (-c["beta"])
        xs.append(x.astype(BF16))
        base = jnp.where(a == b, 1.0, jnp.where(apart == 1, x, 0.0))
        acc = base[0:DN_PACK, :]
        for r in range(1, TILE // DN_PACK):
            acc = acc + base[r * DN_PACK:(r + 1) * DN_PACK, :]
        pw.append(acc)
    pack = DN_PACK
    n_levels = CHUNK.bit_length() - 2
    block_mask = {DN_PACK: n_levels, 2 * DN_PACK: n_levels + 1}
    s = 2
    while s < CHUNK:
        if s == pack:
            keep = _div_pow2(_row_iota((2 * pack, TILE)), pack) == (_div_pow2(_col_iota((2 * pack, TILE)), pack) & 1)
            pw = [jnp.where(keep, stack(p, 2), 0.0) for p in pw]
            pack *= 2
        couple = mask_ref[s.bit_length() - 2]
        blocks = mask_ref[block_mask[pack]]
        pb = [p.astype(BF16) for p in pw]
        px = [dot(pb_c, x * couple) for pb_c, x in zip(pb, xs)]
        p_bd = [stack(pb_c, TILE // pack) * blocks for pb_c in pb]
        pw = [p + dot(px_c.astype(BF16), bd_c) for p, px_c, bd_c in zip(pw, px, p_bd)]
        s *= 2
    blocks = mask_ref[block_mask[pack]]
    t_inv = [stack(p.astype(BF16), TILE // pack) * blocks for p in pw]
    egc = [jnp.exp(c["gc"]) for c in chains]
    u_t = [dot((c["vt"].astype(F32) * c["beta"]).astype(BF16), t) for c, t in zip(chains, t_inv)]
    w_t = [dot((c["kt"].astype(F32) * (c["beta"] * e)).astype(BF16), t).astype(BF16)
           for c, e, t in zip(chains, egc, t_inv)]
    qd_t = [(c["qt"].astype(F32) * e).astype(BF16) for c, e in zip(chains, egc)]
    kdec = [jnp.exp(c["tot"] - c["gc"]) for c in chains]
    outs = [[None] * n_chunks for _ in chains]
    for step in range(n_chunks):
        cis = [(n_chunks - 1 - step) if c["backward"] else step for c in chains]
        sl = [slice(ci * CHUNK, (ci + 1) * CHUNK) for ci in cis]
        st = [c["st_ref"][...] for c in chains]
        stb = [s_c.astype(BF16) for s_c in st]
        swq = [dot(stb_c, jnp.concatenate([w_c[:, r], q_c[:, r]], axis=1))
               for stb_c, w_c, q_c, r in zip(stb, w_t, qd_t, sl)]
        sq = [v[:, CHUNK:] for v in swq]
        vn = [u_c[:, r] - v[:, :CHUNK] for u_c, v, r in zip(u_t, swq, sl)]
        av = [dot(vn_c.astype(BF16), at_c[r, r]) for vn_c, at_c, r in zip(vn, attn, sl)]
        upd = [dot((vn_c * kd_c[:, r]).astype(BF16), c["k_rm"][r, :])
               for vn_c, kd_c, c, r in zip(vn, kdec, chains, sl)]
        for idx, c in enumerate(chains):
            c["st_ref"][...] = st[idx] * jnp.exp(c["tot"][:, sl[idx]]) + upd[idx]
            outs[idx][cis[idx]] = sq[idx] + av[idx]
    return [jnp.concatenate(o, axis=1) for o in outs]


def _dn_kernel(k_ref, qt_ref, kt_ref, vt_ref, g_ref, z_ref, ng_ref, s0f_ref, s0b_ref,
               o_ref, sf_ref, sb_ref, ot_ref, st_ref, mask_ref, *, n_tiles, with_output):
    hb = DN_HEADS_PER_STEP
    _dn_masks(mask_ref)
    for hh in range(hb):
        st_ref[2 * hh] = s0f_ref[0, hh]
        st_ref[2 * hh + 1] = s0b_ref[0, hh]

    def body(i, carry):
        nf = i
        nb = n_tiles - 1 - i
        chains = []
        for hh in range(hb):
            for backward, n in ((False, nf), (True, nb)):
                g = g_ref[0, n, hh * SUBLANE:(hh + 1) * SUBLANE, :]
                base = 2 if backward else 0
                chains.append(dict(
                    k_rm=k_ref[0, pl.ds(pl.multiple_of(n * TILE, TILE), TILE), hh * LANE:(hh + 1) * LANE],
                    qt=qt_ref[0, hh, n], kt=kt_ref[0, hh, n], vt=vt_ref[0, hh, n],
                    gc=g[base:base + 1, :], beta=g[base + 1:base + 2, :], tot=g[4 + base // 2:5 + base // 2, :],
                    st_ref=st_ref.at[2 * hh + (1 if backward else 0)], backward=backward, hh=hh, n=n))
        o_t = _dn_tiles(chains, mask_ref)
        if with_output:
            for c, o_c in zip(chains, o_t):
                ot_ref[c["hh"], c["n"]] = ot_ref[c["hh"], c["n"]] + o_c
        return carry

    if with_output:
        ot_ref[...] = jnp.zeros_like(ot_ref)
    lax.fori_loop(0, n_tiles, body, 0)
    for hh in range(hb):
        sf_ref[0, hh] = st_ref[2 * hh]
        sb_ref[0, hh] = st_ref[2 * hh + 1]
    if with_output:
        def finish(t, carry):
            rows = pl.ds(pl.multiple_of(t * TILE, TILE), TILE)
            for hh in range(hb):
                o = ot_ref[hh, t].T
                y = o * lax.rsqrt(jnp.mean(o * o, axis=-1, keepdims=True) + NORM_EPS) * ng_ref[...]
                z = z_ref[0, rows, hh * LANE:(hh + 1) * LANE].astype(F32)
                o_ref[0, rows, hh * LANE:(hh + 1) * LANE] = (y * _silu(z)).astype(BF16)
            return carry

        lax.fori_loop(0, n_tiles, finish, 0)
    else:
        o_ref[...] = jnp.zeros_like(o_ref)


def _delta_net(qkv_rm, qkv_tr, gates, proj, norm_g, s0f, s0b, with_output):
    bsz, n, _ = qkv_rm.shape
    nt = n // TILE
    h = DN_HEADS
    hb = DN_HEADS_PER_STEP
    wide = hb * LANE
    once = pl.Buffered(1)
    tr_spec = lambda off: pl.BlockSpec((1, hb, nt, LANE, TILE), lambda b, j: (b, off + j, 0, 0, 0), once)
    st_spec = pl.BlockSpec((1, hb, LANE, LANE), lambda b, j: (b, j, 0, 0))
    n_out = n if with_output else SUBLANE
    return pl.pallas_call(
        functools.partial(_dn_kernel, n_tiles=nt, with_output=with_output),
        grid=(bsz, h // hb),
        in_specs=[pl.BlockSpec((1, n, wide), lambda b, j: (b, 0, h // hb + j), once),
                  tr_spec(0), tr_spec(h // hb), tr_spec(2 * h // hb),
                  pl.BlockSpec((1, nt, hb * SUBLANE, TILE), lambda b, j: (b, 0, j, 0)),
                  pl.BlockSpec((1, n, wide), lambda b, j: (b, 0, Z_BLK // hb + j), once),
                  pl.BlockSpec((1, LANE), lambda b, j: (0, 0)),
                  st_spec, st_spec],
        out_specs=[pl.BlockSpec((1, n_out, wide), lambda b, j: (b, 0, j)), st_spec, st_spec],
        out_shape=[jax.ShapeDtypeStruct((bsz, n_out, DN_WIDTH), BF16),
                   jax.ShapeDtypeStruct((bsz, h, LANE, LANE), F32),
                   jax.ShapeDtypeStruct((bsz, h, LANE, LANE), F32)],
        scratch_shapes=[pltpu.VMEM((hb, nt, LANE, TILE), F32),
                        pltpu.VMEM((2 * hb, LANE, LANE), F32),
                        pltpu.VMEM((CHUNK.bit_length(), TILE, TILE), BF16)],
        compiler_params=_cparams(("parallel", "parallel"), 58),
        name="delta_net",
    )(qkv_rm, qkv_tr, qkv_tr, qkv_tr, gates, proj, norm_g.reshape(1, LANE), s0f, s0b)


def _lru_scan_block(x, wa, ba, wx, bx, spl, h_in, backward):
    rows, width = x.shape
    xb = x.astype(BF16)

    def gate(ws, bias):
        parts = [jnp.dot(xb[:, k * LRU_BLOCK:(k + 1) * LRU_BLOCK], w, preferred_element_type=F32)
                 for k, w in enumerate(ws)]
        return _sigmoid(jnp.concatenate(parts, axis=1) + bias)

    r = gate(wa, ba)
    gi = gate(wx, bx)
    log_a = -LRU_C * r * spl
    a = jnp.exp(log_a)
    b = jnp.sqrt(1.0 - jnp.exp(2.0 * log_a)) * (gi * x)
    groups = rows // SUBLANE
    a = a.reshape(groups, SUBLANE, width)
    b = b.reshape(groups, SUBLANE, width)
    sub = lax.broadcasted_iota(jnp.int32, a.shape, 1)
    s = 1
    while s < SUBLANE:
        if backward:
            keep = sub < SUBLANE - s
            a_sh = jnp.where(keep, pltpu.roll(a, SUBLANE - s, 1), 1.0)
            b_sh = jnp.where(keep, pltpu.roll(b, SUBLANE - s, 1), 0.0)
        else:
            keep = sub >= s
            a_sh = jnp.where(keep, pltpu.roll(a, s, 1), 1.0)
            b_sh = jnp.where(keep, pltpu.roll(b, s, 1), 0.0)
        b = a * b_sh + b
        a = a * a_sh
        s *= 2
    a = a.reshape(rows, width)
    b = b.reshape(rows, width)
    pieces = [None] * groups
    carry = h_in
    order = range(groups - 1, -1, -1) if backward else range(groups)
    edge = 0 if backward else SUBLANE - 1
    for gidx in order:
        lo = gidx * SUBLANE
        hgrp = b[lo:lo + SUBLANE, :] + a[lo:lo + SUBLANE, :] * carry
        pieces[gidx] = hgrp
        carry = hgrp[edge:edge + 1, :]
    return jnp.concatenate(pieces, axis=0), carry


def _lru_kernel(px_ref, py_ref, cw_ref, cb_ref, wa_ref, ba_ref, wx_ref, bx_ref, lam_ref, h0_ref,
                o_ref, last_ref, xs_ref, hs_ref, *, n_blocks, with_output):
    x = px_ref[0].astype(F32)
    xs_ref[...] = _dwconv_rows(x, cw_ref, 4) + cb_ref[...]
    spl = _softplus(-lam_ref[...])
    wa = [[wa_ref[d, k] for k in range(LRU_GROUP)] for d in range(2)]
    wx = [[wx_ref[d, k] for k in range(LRU_GROUP)] for d in range(2)]

    def body(i, carry):
        hf, hb = carry
        rf = pl.multiple_of(i * LRU_ROWS, LRU_ROWS)
        rb = pl.multiple_of((n_blocks - 1 - i) * LRU_ROWS, LRU_ROWS)
        h_f, hf = _lru_scan_block(xs_ref[pl.ds(rf, LRU_ROWS), :], wa[0], ba_ref[0:1, :],
                                  wx[0], bx_ref[0:1, :], spl[0:1, :], hf, False)
        h_b, hb = _lru_scan_block(xs_ref[pl.ds(rb, LRU_ROWS), :], wa[1], ba_ref[1:2, :],
                                  wx[1], bx_ref[1:2, :], spl[1:2, :], hb, True)
        if with_output:
            hs_ref[pl.ds(rf, LRU_ROWS), :] = hs_ref[pl.ds(rf, LRU_ROWS), :] + h_f
            hs_ref[pl.ds(rb, LRU_ROWS), :] = hs_ref[pl.ds(rb, LRU_ROWS), :] + h_b
        return hf, hb

    if with_output:
        hs_ref[...] = jnp.zeros_like(hs_ref)
    h0 = h0_ref[0]
    hf, hb = lax.fori_loop(0, n_blocks, body, (h0[0:1, :], h0[1:2, :]))
    last_ref[0] = jnp.concatenate([hf, hb], axis=0)
    if with_output:
        y = py_ref[0].astype(F32)
        gelu = 0.5 * y * (1.0 + jnp.tanh(math.sqrt(2.0 / math.pi) * (y + 0.044715 * (y * y * y))))
        o_ref[0] = (hs_ref[...] * gelu).astype(BF16)
    else:
        o_ref[...] = jnp.zeros_like(o_ref)


def _rglru(proj, conv_w, conv_b, w_a, b_a, w_x, b_x, lam, h0, with_output):
    bsz, n, _ = proj.shape
    rows = min(n, LRU_ROWS)
    nb = n // rows
    n_out = n if with_output else SUBLANE
    width = LRU_GROUP * LRU_BLOCK
    lx = LX_BLK * LANE // width
    ly = LY_BLK * LANE // width
    vec2 = pl.BlockSpec((2, width), lambda b, j: (0, j))
    wspec = pl.BlockSpec((2, LRU_GROUP, LRU_BLOCK, LRU_BLOCK), lambda b, j: (0, j, 0, 0))
    return pl.pallas_call(
        functools.partial(_lru_kernel, n_blocks=nb, with_output=with_output),
        grid=(bsz, LRU_BLOCKS // LRU_GROUP),
        in_specs=[pl.BlockSpec((1, n, width), lambda b, j: (b, 0, lx + j)),
                  pl.BlockSpec((1, n, width), lambda b, j: (b, 0, ly + j)),
                  pl.BlockSpec((4, width), lambda b, j: (0, j)),
                  pl.BlockSpec((1, width), lambda b, j: (0, j)),
                  wspec, vec2, wspec, vec2, vec2,
                  pl.BlockSpec((1, 2, width), lambda b, j: (b, 0, j))],
        out_specs=[pl.BlockSpec((1, n_out, width), lambda b, j: (b, 0, j)),
                   pl.BlockSpec((1, 2, width), lambda b, j: (b, 0, j))],
        out_shape=[jax.ShapeDtypeStruct((bsz, n_out, LRU_WIDTH), BF16),
                   jax.ShapeDtypeStruct((bsz, 2, LRU_WIDTH), F32)],
        scratch_shapes=[pltpu.VMEM((n, width), F32), pltpu.VMEM((n, width), F32)],
        compiler_params=_cparams(("parallel", "parallel")),
        name="rglru",
    )(proj, proj, conv_w, conv_b.reshape(1, LRU_WIDTH), w_a, b_a, w_x, b_x, lam, h0)


def _hy_prep_kernel(p0_ref, p1_ref, pv_ref, w0_ref, w1_ref, wv_ref, b0_ref, b1_ref, bv_ref, x0_ref, zz_ref):
    x0 = _dwconv_rows(p0_ref[0].astype(F32), w0_ref, 3) + b0_ref[...]
    x1 = _dwconv_rows(p1_ref[0].astype(F32), w1_ref, 3) + b1_ref[...]
    v = _dwconv_rows(pv_ref[0].astype(F32), wv_ref, 3) + bv_ref[...]
    x0_ref[0] = x0.astype(BF16)
    zz_ref[0] = (x1 * v).astype(BF16)


def _hy_prep(proj, conv_w, conv_b):
    bsz, n, _ = proj.shape
    nblk = HY_WIDTH // LANE
    pspec = lambda off: pl.BlockSpec((1, n, LANE), lambda b, j: (b, 0, HY_BLK + off + j))
    wspec = lambda off: pl.BlockSpec((3, LANE), lambda b, j: (0, off + j))
    bspec = lambda off: pl.BlockSpec((1, LANE), lambda b, j: (0, off + j))
    ospec = pl.BlockSpec((1, n, LANE), lambda b, j: (b, 0, j))
    cb = conv_b.reshape(1, 3 * HY_WIDTH)
    return pl.pallas_call(
        _hy_prep_kernel,
        grid=(bsz, nblk),
        in_specs=[pspec(0), pspec(nblk), pspec(2 * nblk), wspec(0), wspec(nblk), wspec(2 * nblk),
                  bspec(0), bspec(nblk), bspec(2 * nblk)],
        out_specs=[ospec, ospec],
        out_shape=[jax.ShapeDtypeStruct((bsz, n, HY_WIDTH), BF16)] * 2,
        compiler_params=_cparams(("parallel", "parallel")),
        name="hy_prep",
    )(proj, proj, proj, conv_w, conv_w, conv_w, cb, cb, cb)


def _hy_filter_kernel(w1_ref, b1_ref, f1_ref, w2_ref, b2_ref, f2_ref, w3_ref, band_ref, delta_ref,
                      k_ref, *, n, rows):
    def lag(shape):
        p = _row_iota(shape) + (pl.program_id(0) * rows - n)
        return p, jnp.abs(p).astype(F32)

    _, i = lag((rows, LANE))
    lane = _col_iota((rows, LANE))
    t = i * (1.0 / (n - 1))
    ang = band_ref[...] * (i * (2.0 * math.pi / n))
    feat = jnp.where(lane == 0, t,
                     jnp.where(lane <= HY_BANDS, jnp.cos(ang), jnp.where(lane < HY_EMB, -jnp.sin(ang), 0.0)))
    hid = jnp.sin(f1_ref[...] * (jnp.dot(feat, w1_ref[...], precision=HIGHEST,
                                         preferred_element_type=F32) + b1_ref[...]))
    hid = jnp.sin(f2_ref[...] * (jnp.dot(hid, w2_ref[...], precision=HIGHEST,
                                         preferred_element_type=F32) + b2_ref[...]))
    filt = _bdot(hid, w3_ref[...])
    p, iw = lag((rows, HY_WIDTH))
    dec = jnp.exp(-(iw * (1.0 / (n - 1))) * delta_ref[...])
    h_f = filt[:, :HY_WIDTH] * dec
    h_b = filt[:, HY_WIDTH:] * dec
    taps = jnp.where(p > 0, h_f, jnp.where(p < 0, h_b, h_f + h_b))
    k_ref[...] = jnp.where(p == -n, 0.0, taps).astype(BF16)


def _hy_filter(n, w1, b1, f1, w2, b2, f2, w3):
    rows = min(n, 256)
    hid = HY_FILTER_HIDDEN
    w1p = jnp.zeros((LANE, hid), F32).at[:HY_EMB].set(w1)
    bands = np.zeros((1, LANE), np.float32)
    base = np.linspace(1e-4, HY_BANDS - 1, HY_BANDS, dtype=np.float32)
    bands[0, 1:1 + HY_BANDS] = base
    bands[0, 1 + HY_BANDS:HY_EMB] = base
    log_target = math.log(HY_DECAY_TARGET)
    deltas = np.abs(np.linspace(log_target / HY_SLOW_DECAY_PCT, log_target / HY_FAST_DECAY_PCT, HY_WIDTH,
                                dtype=np.float32)).reshape(1, HY_WIDTH)
    full = lambda shape: pl.BlockSpec(shape, lambda i: (0,) * len(shape))
    return pl.pallas_call(
        functools.partial(_hy_filter_kernel, n=n, rows=rows),
        grid=(2 * n // rows,),
        in_specs=[full((LANE, hid)), full((1, hid)), full((1, hid)), full((hid, hid)), full((1, hid)),
                  full((1, hid)), full((hid, 2 * HY_WIDTH)), full((1, LANE)), full((1, HY_WIDTH))],
        out_specs=pl.BlockSpec((rows, HY_WIDTH), lambda i: (i, 0)),
        out_shape=jax.ShapeDtypeStruct((2 * n, HY_WIDTH), BF16),
        compiler_params=_cparams(("parallel",)),
        name="hy_filter",
    )(w1p, b1.reshape(1, hid), f1.reshape(1, hid), w2, b2.reshape(1, hid), f2.reshape(1, hid), w3,
      jnp.asarray(bands), jnp.asarray(deltas))


DFT_GROUP = 64


def _dft_table_kernel(c_ref, s_ref, cb_ref, sb_ref, *, m, ncols, col0, rows, transposed, blank_first):
    period = 4 * m
    scale = 2.0 * math.pi / period
    col = _col_iota((1, ncols)) + col0

    @pl.when(pl.program_id(0) == 0)
    def _():
        r2 = _row_iota((DFT_GROUP, ncols))
        c2 = _col_iota((DFT_GROUP, ncols)) + col0
        ph = (r2 * (2 * c2 + 1)) if transposed else ((2 * r2 + 1) * c2)
        ang = (ph & (period - 1)).astype(F32) * scale
        cb_ref[...] = jnp.cos(ang)
        sb_ref[...] = jnp.sin(ang)

    for g in range(rows // DFT_GROUP):
        r1 = pl.program_id(0) * (rows // DFT_GROUP) + g
        ph = (DFT_GROUP * r1) * (2 * col + 1) if transposed else (2 * DFT_GROUP * r1) * col
        ang = (ph & (period - 1)).astype(F32) * scale
        ca = jnp.cos(ang)
        sa = jnp.sin(ang)
        cb = cb_ref[...]
        sb = sb_ref[...]
        c_tile = ca * cb - sa * sb
        s_tile = sa * cb + ca * sb
        if blank_first:
            first = _col_iota((DFT_GROUP, ncols)) == 0
            c_tile = jnp.where(first, 0.0, c_tile)
            s_tile = jnp.where(first, 0.0, s_tile)
        c_ref[g * DFT_GROUP:(g + 1) * DFT_GROUP, :] = c_tile.astype(BF16)
        s_ref[g * DFT_GROUP:(g + 1) * DFT_GROUP, :] = s_tile.astype(BF16)


def _dft_tables(m, transposed=False, two_sided=False):
    rows = min(m, 256)
    ncols = 2 * m if two_sided else m
    spec = pl.BlockSpec((rows, ncols), lambda i: (i, 0))
    return pl.pallas_call(
        functools.partial(_dft_table_kernel, m=m, ncols=ncols, col0=3 * m if two_sided else 0, rows=rows,
                          transposed=transposed, blank_first=two_sided),
        grid=(m // rows,),
        in_specs=[],
        out_specs=[spec, spec],
        out_shape=[jax.ShapeDtypeStruct((m, ncols), BF16)] * 2,
        scratch_shapes=[pltpu.VMEM((DFT_GROUP, ncols), F32), pltpu.VMEM((DFT_GROUP, ncols), F32)],
        compiler_params=_cparams(("arbitrary",)),
        name="dft_tables",
    )()


def _hy_tables(m):
    return _dft_tables(m) + _dft_tables(m, transposed=True) + _dft_tables(m, two_sided=True)


def _hy_ktrans_kernel(cw_ref, sw_ref, lo_ref, hi_ref, kc_ref, ks_ref, *, m):
    dot = functools.partial(jnp.dot, preferred_element_type=F32)
    lo = lo_ref[...]
    hi = hi_ref[...]
    kc_ref[0] = dot(cw_ref[:, :m], lo) + dot(cw_ref[:, m:], hi)
    ks_ref[0] = dot(sw_ref[:, :m], lo) + dot(sw_ref[:, m:], hi)


def _hy_ktrans(cw, sw, taps, m):
    nd = taps.shape[0] // m - 1
    w = taps.shape[1]
    tn = 512
    tab = pl.BlockSpec((m, 2 * m), lambda e, j: (0, 0))
    ospec = pl.BlockSpec((1, m, tn), lambda e, j: (e, 0, j))
    return pl.pallas_call(
        functools.partial(_hy_ktrans_kernel, m=m),
        grid=(nd, w // tn),
        in_specs=[tab, tab, pl.BlockSpec((m, tn), lambda e, j: (e, j)), pl.BlockSpec((m, tn), lambda e, j: (e + 1, j))],
        out_specs=[ospec, ospec],
        out_shape=[jax.ShapeDtypeStruct((nd, m, w), F32)] * 2,
        compiler_params=_cparams(("parallel", "parallel")),
        name="hy_ktrans",
    )(cw, sw, taps, taps)


def _hy_fwd_kernel(c_ref, s_ref, zz_ref, kc_ref, ks_ref, a_ref, b_ref, *, m, nb):
    dot = functools.partial(jnp.dot, preferred_element_type=F32)
    c = c_ref[...]
    s = s_ref[...]
    uc = [dot(c, zz_ref[0, j * m:(j + 1) * m, :]) for j in range(nb)]
    us = [dot(s, zz_ref[0, j * m:(j + 1) * m, :]) for j in range(nb)]
    for i in range(nb):
        acc_a = acc_b = None
        for j in range(nb):
            kc = kc_ref[i - j + nb - 1]
            ks = ks_ref[i - j + nb - 1]
            ta = uc[j] * kc - us[j] * ks
            tb = uc[j] * ks + us[j] * kc
            acc_a = ta if acc_a is None else acc_a + ta
            acc_b = tb if acc_b is None else acc_b + tb
        a_ref[0, i] = acc_a.astype(BF16)
        b_ref[0, i] = acc_b.astype(BF16)


def _hy_forward(ctab, stab, zz, kc, ks, m):
    bsz, n, w = zz.shape
    nb = n // m
    nd = 2 * nb - 1
    tf = min(m, 256)
    tn = 512
    tab = pl.BlockSpec((tf, m), lambda b, j, i: (i, 0))
    kspec = pl.BlockSpec((nd, tf, tn), lambda b, j, i: (0, i, j))
    ospec = pl.BlockSpec((1, nb, tf, tn), lambda b, j, i: (b, 0, i, j))
    return pl.pallas_call(
        functools.partial(_hy_fwd_kernel, m=m, nb=nb),
        grid=(bsz, w // tn, m // tf),
        in_specs=[tab, tab, pl.BlockSpec((1, n, tn), lambda b, j, i: (b, 0, j)), kspec, kspec],
        out_specs=[ospec, ospec],
        out_shape=[jax.ShapeDtypeStruct((bsz, nb, m, w), BF16)] * 2,
        compiler_params=_cparams(("parallel", "parallel", "parallel")),
        name="hy_forward",
    )(ctab, stab, zz, kc, ks)


def _hy_inv_kernel(ct_ref, st_ref, a_ref, b_ref, x0_ref, zz_ref, bias_ref, o_ref, *, m):
    y = (jnp.dot(ct_ref[...], a_ref[0, 0], preferred_element_type=F32)
         + jnp.dot(st_ref[...], b_ref[0, 0], preferred_element_type=F32)) * (1.0 / m)
    zz = zz_ref[0].astype(F32)
    o_ref[0] = (x0_ref[0].astype(F32) * (y + zz * bias_ref[...])).astype(BF16)


def _hy_inverse(cttab, sttab, a, bq, x0, zz, bias, m):
    bsz, n, w = zz.shape
    nb = n // m
    tn = 512
    tab = pl.BlockSpec((m, m), lambda b, i, j: (0, 0))
    spec4 = pl.BlockSpec((1, 1, m, tn), lambda b, i, j: (b, i, 0, j))
    tile = pl.BlockSpec((1, m, tn), lambda b, i, j: (b, i, j))
    return pl.pallas_call(
        functools.partial(_hy_inv_kernel, m=m),
        grid=(bsz, nb, w // tn),
        in_specs=[tab, tab, spec4, spec4, tile, tile, pl.BlockSpec((1, tn), lambda b, i, j: (0, j))],
        out_specs=tile,
        out_shape=jax.ShapeDtypeStruct((bsz, n, w), BF16),
        compiler_params=_cparams(("parallel", "parallel", "parallel")),
        name="hy_inverse",
    )(cttab, sttab, a, bq, x0, zz, bias.reshape(1, w))


def _merge_kernel(odn_ref, ohy_ref, olru_ref, gdn_ref, ghy_ref, glru_ref, wdn_ref, why_ref, wlru_ref,
                  wout_ref, x_ref, gate_ref, o_ref):
    m = _sigmoid(gdn_ref[0].astype(F32)) * jnp.dot(odn_ref[0], wdn_ref[...], preferred_element_type=F32)
    m = m + _sigmoid(ghy_ref[0].astype(F32)) * jnp.dot(ohy_ref[0], why_ref[...], preferred_element_type=F32)
    m = m + _sigmoid(glru_ref[0].astype(F32)) * jnp.dot(olru_ref[0], wlru_ref[...], preferred_element_type=F32)
    y = jnp.dot(m.astype(BF16), wout_ref[...], preferred_element_type=F32)
    o_ref[0] = x_ref[0] + gate_ref[0] * y


def _merge(o_dn, o_hy, o_lru, proj, w_dn, w_hy, w_lru, w_out, x, gate):
    bsz, n, d = x.shape
    tm = min(n, 256)
    nblk = d // 1024
    act = pl.BlockSpec((1, tm, d), lambda b, i: (b, i, 0))
    gspec = lambda k: pl.BlockSpec((1, tm, d), lambda b, i: (b, i, GATE_BLK * LANE // d + k * nblk))
    wspec = pl.BlockSpec((d, d), lambda b, i: (0, 0))
    return pl.pallas_call(
        _merge_kernel,
        grid=(bsz, n // tm),
        in_specs=[act, act, act, gspec(0), gspec(1), gspec(2), wspec, wspec, wspec, wspec, act,
                  pl.BlockSpec((1, 1, d), lambda b, i: (b, 0, 0))],
        out_specs=act,
        out_shape=jax.ShapeDtypeStruct((bsz, n, d), F32),
        compiler_params=_cparams(("parallel", "parallel")),
        name="merge",
    )(o_dn, o_hy, o_lru, proj, proj, proj, w_dn, w_hy, w_lru, w_out, x, gate)


def _ffn_act_kernel(ug_ref, uv_ref, wg_ref, wv_ref, o_ref, up_ref, mid_ref, dn_ref, *, rows, cols, n):
    tile = FFN_TILE
    a = _row_iota((tile, tile))
    b = _col_iota((tile, tile))
    c = _mod_pow2(a, cols)
    lmat = jnp.where((b == a - 1) & (c >= 1), 1.0, 0.0).astype(BF16)
    rmat = jnp.where((b == a + 1) & (c <= cols - 2), 1.0, 0.0).astype(BF16)
    taps = (-1, 0, 1) if rows > 1 else (0,)
    dst = {-1: up_ref, 0: mid_ref, 1: dn_ref}
    if rows > 1:
        pad = jnp.zeros((2, cols, FFN_WIDE), BF16)
        up_ref[:, 0:cols, :] = pad
        dn_ref[:, n + cols:n + 2 * cols, :] = pad

    def taps_of(i, carry):
        r0 = pl.multiple_of(i * tile, tile)
        for idx, (u_ref, w_ref) in enumerate(((ug_ref, wg_ref), (uv_ref, wv_ref))):
            u = u_ref[0, pl.ds(r0, tile), :]
            left = jnp.dot(lmat, u, preferred_element_type=F32).astype(BF16)
            right = jnp.dot(rmat, u, preferred_element_type=F32).astype(BF16)
            for di in taps:
                k = 3 * (di + 1)
                w = [w_ref[k + j:k + j + 1, :].astype(BF16) for j in range(3)]
                off = pl.multiple_of(r0 + (cols if di != 0 else 0), cols)
                dst[di][idx, pl.ds(off, tile), :] = left * w[0] + u * w[1] + right * w[2]
        return carry

    def combine(i, carry):
        r0 = pl.multiple_of(i * tile, tile)
        vals = []
        for idx in range(2):
            acc = mid_ref[idx, pl.ds(r0, tile), :]
            if rows > 1:
                acc = (acc + up_ref[idx, pl.ds(r0, tile), :]
                       + dn_ref[idx, pl.ds(pl.multiple_of(r0 + 2 * cols, cols), tile), :])
            vals.append(acc.astype(F32))
        o_ref[0, pl.ds(r0, tile), :] = (_silu(vals[0]) * vals[1]).astype(BF16)
        return carry

    lax.fori_loop(0, n // tile, taps_of, 0, unroll=min(2, n // tile))
    lax.fori_loop(0, n // tile, combine, 0)


def _ffn_act(u, conv_w, rows, cols):
    bsz, n, _ = u.shape
    nblk = FFN_HIDDEN // FFN_WIDE
    padded = n + 2 * cols if rows > 1 else SUBLANE * 2
    return pl.pallas_call(
        functools.partial(_ffn_act_kernel, rows=rows, cols=cols, n=n),
        grid=(bsz, nblk),
        in_specs=[pl.BlockSpec((1, n, FFN_WIDE), lambda b, j: (b, 0, j)),
                  pl.BlockSpec((1, n, FFN_WIDE), lambda b, j: (b, 0, nblk + j)),
                  pl.BlockSpec((9, FFN_WIDE), lambda b, j: (0, j)),
                  pl.BlockSpec((9, FFN_WIDE), lambda b, j: (0, nblk + j))],
        out_specs=pl.BlockSpec((1, n, FFN_WIDE), lambda b, j: (b, 0, j)),
        out_shape=jax.ShapeDtypeStruct((bsz, n, FFN_HIDDEN), BF16),
        scratch_shapes=[pltpu.VMEM((2, padded, FFN_WIDE), BF16), pltpu.VMEM((2, n, FFN_WIDE), BF16),
                        pltpu.VMEM((2, padded, FFN_WIDE), BF16)],
        compiler_params=_cparams(("parallel", "parallel")),
        name="ffn_act",
    )(u, u, conv_w, conv_w)


def _gate_column_map():
    src = np.full((LANE,), -1, np.int32)
    isdec = np.zeros((1, LANE), np.float32)
    dirs = np.zeros((LANE,), np.int32)
    for h in range(DN_HEADS):
        for slot, (d, kind) in enumerate(((0, 0), (0, 1), (1, 0), (1, 1), (0, 0), (1, 0))):
            src[h * SUBLANE + slot] = d * 2 * DN_HEADS + kind * DN_HEADS + h
            isdec[0, h * SUBLANE + slot] = 1.0 if kind == 0 else 0.0
            dirs[h * SUBLANE + slot] = d
    return src, isdec, dirs


def _split_in_proj(w_in):
    o = np.cumsum((3 * DN_WIDTH, DN_WIDTH, 4 * DN_HEADS, 3 * HY_WIDTH, LRU_WIDTH, LRU_WIDTH)).tolist()
    w_main = jnp.concatenate([w_in[:, :o[1]], w_in[:, o[2]:]], axis=1).astype(BF16)
    src, _, _ = _gate_column_map()
    w_ab = w_in[:, o[1]:o[2]]
    w_gate = jnp.where(jnp.asarray(src >= 0)[None, :], w_ab[:, np.maximum(src, 0)], 0.0).astype(BF16)
    return w_main, w_gate


def _gate_params(a_log, dt_bias):
    _, isdec, dirs = _gate_column_map()
    head = (np.arange(LANE) // SUBLANE).astype(np.int32)
    alog_c = a_log[dirs, head].reshape(1, LANE)
    dtb_c = dt_bias[dirs, head].reshape(1, LANE)
    return alog_c, dtb_c, jnp.asarray(isdec)


def _token_views(bsz, n, mods, shared_mod):
    if not shared_mod:
        same = lambda t: t
        return same, same, mods
    flat = lambda t: t.reshape(1, bsz * n, t.shape[-1])
    unflat = lambda t: t.reshape(bsz, n, t.shape[-1])
    return flat, unflat, [m[:1] for m in mods]


def _token_mixer(x, mods, lp, states, tables, with_output, shared_mod):
    bsz, n, _ = x.shape
    flat, unflat, tmods = _token_views(bsz, n, mods, shared_mod)
    proj, ab = _norm_mod_matmul(flat(x), lp["norm1_g"], tmods[0], tmods[1], lp["w_main"], BF16, 2048, 1024,
                                lp["w_gate"])
    proj, ab = unflat(proj), unflat(ab)

    qkv_rm, qkv_tr = _dn_prep(proj, lp["dn_conv_w"])
    gates = _dn_gates(ab, *lp["gate_params"])
    o_dn, s_f, s_b = _delta_net(qkv_rm, qkv_tr, gates, proj, lp["dn_norm_g"], states[0], states[1], with_output)
    o_lru, h_last = _rglru(proj, lp["lru_conv_w"], lp["lru_conv_b"], lp["lru_w_a"], lp["lru_b_a"],
                           lp["lru_w_x"], lp["lru_b_x"], lp["lru_lambda"], states[2], with_output)
    new_states = (s_f, s_b, h_last)
    if not with_output:
        return None, new_states

    m = min(n, HY_BLOCK)
    ctab, stab, cttab, sttab, cwtab, swtab = tables
    taps = _hy_filter(n, lp["hy_w1"], lp["hy_b1"], lp["hy_f1"], lp["hy_w2"], lp["hy_b2"], lp["hy_f2"], lp["hy_w3"])
    kc, ks = _hy_ktrans(cwtab, swtab, taps, m)
    x0, zz = _hy_prep(proj, lp["hy_conv_w"], lp["hy_conv_b"])
    a, bq = _hy_forward(ctab, stab, zz, kc, ks, m)
    o_hy = _hy_inverse(cttab, sttab, a, bq, x0, zz, lp["hy_bias"], m)

    x = _merge(flat(o_dn), flat(o_hy), flat(o_lru), flat(proj), lp["w_proj_dn"], lp["w_proj_hy"], lp["w_proj_lru"],
               lp["w_out"], flat(x), tmods[2])
    return unflat(x), new_states


def _conv_ffn(x, mods, lp, rows, cols, shared_mod, final_gain=None):
    bsz, n, _ = x.shape
    flat, unflat, tmods = _token_views(bsz, n, mods, shared_mod)
    u = _norm_mod_matmul(flat(x), lp["norm2_g"], tmods[3], tmods[4], lp["ffn_up"], BF16, 1024, FFN_HIDDEN)
    act = _ffn_act(unflat(u), lp["ffn_conv_w"], rows, cols)
    return unflat(_matmul_residual(flat(act), lp["ffn_down"], flat(x), tmods[5], final_gain))


def kernel(x, c, ctx, c_ctx, w_mod, b_mod, norm1_g, norm2_g, w_in, dn_conv_w, dn_a_log, dn_dt_bias, dn_norm_g,
           hy_conv_w, hy_conv_b, hy_w1, hy_b1, hy_f1, hy_w2, hy_b2, hy_f2, hy_w3, hy_bias,
           lru_conv_w, lru_conv_b, lru_w_a, lru_b_a, lru_w_x, lru_b_x, lru_lambda,
           w_proj_dn, w_proj_hy, w_proj_lru, w_out, ffn_up, ffn_conv_w, ffn_down, final_norm_g):
    bsz, n_lat, d = x.shape
    n_ctx = ctx.shape[1]
    depth = w_in.shape[0]
    rows = n_lat // GRID_W

    cvec = jnp.zeros((SUBLANE, d), F32).at[:bsz].set(c).at[bsz].set(c_ctx)
    lat_tables = _hy_tables(min(n_lat, HY_BLOCK))
    ctx_tables = _hy_tables(min(n_ctx, HY_BLOCK))
    zero_states = (jnp.zeros((bsz, DN_HEADS, LANE, LANE), F32), jnp.zeros((bsz, DN_HEADS, LANE, LANE), F32),
                   jnp.zeros((bsz, 2, LRU_WIDTH), F32))

    xc = ctx
    for l in range(depth):
        ctx_needed = l < depth - 1
        w_main, w_gate = _split_in_proj(w_in[l])
        lp = dict(
            norm1_g=norm1_g[l], norm2_g=norm2_g[l], w_main=w_main, w_gate=w_gate,
            dn_conv_w=dn_conv_w[l], gate_params=_gate_params(dn_a_log[l], dn_dt_bias[l]), dn_norm_g=dn_norm_g[l],
            hy_conv_w=hy_conv_w[l], hy_conv_b=hy_conv_b[l], hy_w1=hy_w1[l], hy_b1=hy_b1[l], hy_f1=hy_f1[l],
            hy_w2=hy_w2[l], hy_b2=hy_b2[l], hy_f2=hy_f2[l], hy_w3=hy_w3[l], hy_bias=hy_bias[l],
            lru_conv_w=lru_conv_w[l], lru_conv_b=lru_conv_b[l], lru_w_a=lru_w_a[l].astype(BF16),
            lru_b_a=lru_b_a[l], lru_w_x=lru_w_x[l].astype(BF16), lru_b_x=lru_b_x[l], lru_lambda=lru_lambda[l],
            w_proj_dn=w_proj_dn[l].astype(BF16), w_proj_hy=w_proj_hy[l].astype(BF16),
            w_proj_lru=w_proj_lru[l].astype(BF16), w_out=w_out[l].astype(BF16),
            ffn_up=ffn_up[l].astype(BF16), ffn_conv_w=ffn_conv_w[l].reshape(9, 2 * FFN_HIDDEN),
            ffn_down=ffn_down[l].astype(BF16))
        mod = _modulation(cvec, w_mod[l].astype(BF16), b_mod[l])
        lat_mod = [mod[:bsz, k * d:(k + 1) * d].reshape(bsz, 1, d) for k in range(N_MOD)]
        ctx_mod = [jnp.broadcast_to(mod[bsz:bsz + 1, k * d:(k + 1) * d].reshape(1, 1, d), (bsz, 1, d))
                   for k in range(N_MOD)]

        xc_new, ctx_states = _token_mixer(xc, ctx_mod, lp, zero_states, ctx_tables, ctx_needed, True)
        x, _ = _token_mixer(x, lat_mod, lp, ctx_states, lat_tables, True, False)
        x = _conv_ffn(x, lat_mod, lp, rows, GRID_W, False, None if ctx_needed else final_norm_g)
        if ctx_needed:
            xc = _conv_ffn(xc_new, ctx_mod, lp, 1, n_ctx, True)
    return x
```

```python
import functools
import math

import numpy as np
import jax
import jax.numpy as jnp
from jax import lax
from jax.experimental import pallas as pl
from jax.experimental.pallas import tpu as pltpu

F32 = jnp.float32
BF16 = jnp.bfloat16
HIGHEST = lax.Precision.HIGHEST

D_MODEL = 1024
DEPTH = 2
GRID_W = 64
NORM_EPS = 1e-6
N_MOD = 6

DN_HEADS = 8
DN_HEAD_DIM = 128
DN_WIDTH = DN_HEADS * DN_HEAD_DIM
HY_WIDTH = 1024
HY_EMB = 33
HY_BANDS = (HY_EMB - 1) // 2
HY_FILTER_HIDDEN = 64
HY_FAST_DECAY_PCT = 0.3
HY_SLOW_DECAY_PCT = 1.5
HY_DECAY_TARGET = 1e-2
LRU_WIDTH = 1024
LRU_BLOCKS = 8
LRU_BLOCK = LRU_WIDTH // LRU_BLOCKS
LRU_C = 8.0
FFN_HIDDEN = 2816

LANE = 128
SUBLANE = 8
TILE = 256
CHUNK = 128
DN_HEADS_PER_STEP = 4
DN_PREP_WIDE_ROWS = 512
DN_PACK = 64
HY_BLOCK = 1024
LRU_ROWS = 256
LRU_GROUP = 2
FFN_TILE = 256
FFN_WIDE = 256
MIB = 1024 * 1024

QKV_BLK = 0
Z_BLK = 24
HY_BLK = 32
LX_BLK = 56
LY_BLK = 64
GATE_BLK = 72
N_MAIN = 96 * LANE


def _cparams(sem, vmem_mib=48):
    return pltpu.CompilerParams(dimension_semantics=sem, vmem_limit_bytes=vmem_mib * MIB)


def _sigmoid(x):
    return 1.0 / (1.0 + jnp.exp(-x))


def _silu(x):
    return x * _sigmoid(x)


def _softplus(x):
    return jnp.maximum(x, 0.0) + jnp.log(1.0 + jnp.exp(-jnp.abs(x)))


def _row_iota(shape):
    return lax.broadcasted_iota(jnp.int32, shape, 0)


def _col_iota(shape):
    return lax.broadcasted_iota(jnp.int32, shape, 1)


def _div_pow2(x, k):
    assert k & (k - 1) == 0
    return x >> (k.bit_length() - 1)


def _mod_pow2(x, k):
    assert k & (k - 1) == 0
    return x & (k - 1)


def _bdot(a, b):
    return jnp.dot(a.astype(BF16), b.astype(BF16), preferred_element_type=F32)


def _mod_kernel(c_ref, w_ref, b_ref, o_ref):
    o_ref[...] = _bdot(_silu(c_ref[...]), w_ref[...]) + b_ref[...]


def _modulation(cvec, w_mod, b_mod):
    n = w_mod.shape[1]
    tn = 1024
    return pl.pallas_call(
        _mod_kernel,
        grid=(n // tn,),
        in_specs=[pl.BlockSpec((SUBLANE, D_MODEL), lambda j: (0, 0)),
                  pl.BlockSpec((D_MODEL, tn), lambda j: (0, j)),
                  pl.BlockSpec((1, tn), lambda j: (0, j))],
        out_specs=pl.BlockSpec((SUBLANE, tn), lambda j: (0, j)),
        out_shape=jax.ShapeDtypeStruct((SUBLANE, n), F32),
        compiler_params=_cparams(("parallel",)),
        name="modulation",
    )(cvec, w_mod, b_mod.reshape(1, n))


def _nmm_kernel(x_ref, g_ref, sh_ref, sc_ref, w_ref, *rest, with_side):
    o_ref = rest[1] if with_side else rest[0]
    h_ref = rest[-1]

    @pl.when(pl.program_id(2) == 0)
    def _():
        x = x_ref[0]
        y = x * lax.rsqrt(jnp.mean(x * x, axis=-1, keepdims=True) + NORM_EPS) * g_ref[...]
        h_ref[...] = (y * (1.0 + sc_ref[0]) + sh_ref[0]).astype(BF16)
        if with_side:
            rest[2][0] = jnp.dot(h_ref[...], rest[0][...], preferred_element_type=F32)

    o_ref[0] = jnp.dot(h_ref[...], w_ref[...], preferred_element_type=F32).astype(o_ref.dtype)


def _norm_mod_matmul(x, gain, shift, scale, w, out_dtype, tm, tn, w_side=None):
    bsz, n, d = x.shape
    nout = w.shape[1]
    tm = min(n, tm)
    with_side = w_side is not None
    in_specs = [pl.BlockSpec((1, tm, d), lambda b, i, j: (b, i, 0)),
                pl.BlockSpec((1, d), lambda b, i, j: (0, 0)),
                pl.BlockSpec((1, 1, d), lambda b, i, j: (b, 0, 0)),
                pl.BlockSpec((1, 1, d), lambda b, i, j: (b, 0, 0)),
                pl.BlockSpec((d, tn), lambda b, i, j: (0, j))]
    out_specs = [pl.BlockSpec((1, tm, tn), lambda b, i, j: (b, i, j))]
    out_shape = [jax.ShapeDtypeStruct((bsz, n, nout), out_dtype)]
    args = [x, gain.reshape(1, d), shift, scale, w]
    if with_side:
        ns = w_side.shape[1]
        in_specs.append(pl.BlockSpec((d, ns), lambda b, i, j: (0, 0)))
        out_specs.append(pl.BlockSpec((1, tm, ns), lambda b, i, j: (b, i, 0)))
        out_shape.append(jax.ShapeDtypeStruct((bsz, n, ns), F32))
        args.append(w_side)
    out = pl.pallas_call(
        functools.partial(_nmm_kernel, with_side=with_side),
        grid=(bsz, n // tm, nout // tn),
        in_specs=in_specs,
        out_specs=out_specs,
        out_shape=out_shape,
        scratch_shapes=[pltpu.VMEM((tm, d), BF16)],
        compiler_params=_cparams(("parallel", "parallel", "arbitrary")),
        name="norm_mod_matmul",
    )(*args)
    return out if with_side else out[0]


def _mm_kernel(a_ref, b_ref, o_ref):
    o_ref[...] = jnp.dot(a_ref[...], b_ref[...], preferred_element_type=F32).astype(o_ref.dtype)


def _matmul(a, b, out_dtype, tm, tn):
    m, k = a.shape
    n = b.shape[1]
    return pl.pallas_call(
        _mm_kernel,
        grid=(m // tm, n // tn),
        in_specs=[pl.BlockSpec((tm, k), lambda i, j: (i, 0)),
                  pl.BlockSpec((k, tn), lambda i, j: (0, j))],
        out_specs=pl.BlockSpec((tm, tn), lambda i, j: (i, j)),
        out_shape=jax.ShapeDtypeStruct((m, n), out_dtype),
        compiler_params=_cparams(("parallel", "parallel")),
        name="matmul",
    )(a, b)


def _mm_res_kernel(a_ref, b_ref, x_ref, g_ref, *rest, final_norm):
    o_ref = rest[-1]
    out = x_ref[0] + g_ref[0] * jnp.dot(a_ref[0], b_ref[...], preferred_element_type=F32)
    if final_norm:
        out = out * lax.rsqrt(jnp.mean(out * out, axis=-1, keepdims=True) + NORM_EPS) * rest[0][...]
    o_ref[0] = out


def _matmul_residual(a, w, x, gate, final_gain=None):
    bsz, n, k = a.shape
    d = w.shape[1]
    tm = min(n, 512)
    in_specs = [pl.BlockSpec((1, tm, k), lambda b, i: (b, i, 0)),
                pl.BlockSpec((k, d), lambda b, i: (0, 0)),
                pl.BlockSpec((1, tm, d), lambda b, i: (b, i, 0)),
                pl.BlockSpec((1, 1, d), lambda b, i: (b, 0, 0))]
    args = [a, w, x, gate]
    if final_gain is not None:
        in_specs.append(pl.BlockSpec((1, d), lambda b, i: (0, 0)))
        args.append(final_gain.reshape(1, d))
    return pl.pallas_call(
        functools.partial(_mm_res_kernel, final_norm=final_gain is not None),
        grid=(bsz, n // tm),
        in_specs=in_specs,
        out_specs=pl.BlockSpec((1, tm, d), lambda b, i: (b, i, 0)),
        out_shape=jax.ShapeDtypeStruct((bsz, n, d), F32),
        compiler_params=_cparams(("parallel", "parallel")),
        name="matmul_residual",
    )(*args)


def _dwconv_rows(x, w_ref, k):
    n = x.shape[0]
    left = (k - 1) // 2
    t = _row_iota((SUBLANE, x.shape[1]))
    acc = x * w_ref[left:left + 1, :]
    head_fix = jnp.zeros((SUBLANE, x.shape[1]), F32)
    tail_fix = jnp.zeros((SUBLANE, x.shape[1]), F32)
    for j in range(k):
        off = j - left
        if off == 0:
            continue
        term = pltpu.roll(x, (-off) % n, 0) * w_ref[j:j + 1, :]
        acc = acc + term
        if off < 0:
            head_fix = head_fix + jnp.where(t < -off, term[:SUBLANE, :], 0.0)
        else:
            tail_fix = tail_fix + jnp.where(t >= SUBLANE - off, term[n - SUBLANE:, :], 0.0)
    return jnp.concatenate([acc[:SUBLANE, :] - head_fix, acc[SUBLANE:n - SUBLANE, :],
                            acc[n - SUBLANE:, :] - tail_fix], axis=0)


def _dn_prep_kernel(p_ref, w_ref, rm_ref, tr_ref, *, n_tiles, heads):
    first = pl.program_id(1) * heads
    y_all = _silu(_dwconv_rows(p_ref[0].astype(F32), w_ref, 4))
    q_scale = jnp.where(first < DN_HEADS, DN_HEAD_DIM ** -0.5, 1.0)
    for h in range(heads):
        y = y_all[:, h * LANE:(h + 1) * LANE]
        unit = lax.rsqrt(jnp.sum(y * y, axis=-1, keepdims=True) + 1e-6) * q_scale
        y = y * jnp.where(first < 2 * DN_HEADS, unit, 1.0)
        rm_ref[0, :, h * LANE:(h + 1) * LANE] = y.astype(BF16)
        for t in range(n_tiles):
            tr_ref[0, h, t] = y[t * TILE:(t + 1) * TILE, :].T.astype(BF16)


def _dn_prep(proj, conv_w):
    bsz, n, _ = proj.shape
    nt = n // TILE
    nc = 3 * DN_HEADS
    heads = DN_HEADS if n <= DN_PREP_WIDE_ROWS else 1
    wide = heads * LANE
    return pl.pallas_call(
        functools.partial(_dn_prep_kernel, n_tiles=nt, heads=heads),
        grid=(bsz, nc // heads),
        in_specs=[pl.BlockSpec((1, n, wide), lambda b, c: (b, 0, QKV_BLK // heads + c)),
                  pl.BlockSpec((4, wide), lambda b, c: (0, c))],
        out_specs=[pl.BlockSpec((1, n, wide), lambda b, c: (b, 0, c)),
                   pl.BlockSpec((1, heads, nt, LANE, TILE), lambda b, c: (b, c, 0, 0, 0))],
        out_shape=[jax.ShapeDtypeStruct((bsz, n, nc * LANE), BF16),
                   jax.ShapeDtypeStruct((bsz, nc, nt, LANE, TILE), BF16)],
        compiler_params=_cparams(("parallel", "parallel")),
        name="dn_prep",
    )(proj, conv_w)


def _dn_gate_kernel(ab_ref, alog_ref, dtb_ref, isdec_ref, o_ref):
    x = ab_ref[0]
    dec = -jnp.exp(alog_ref[...]) * _softplus(x + dtb_ref[...])
    e = jnp.where(isdec_ref[...] > 0.5, dec, _sigmoid(x))
    et = e.T
    s = _row_iota((TILE, TILE))
    t = _col_iota((TILE, TILE))
    same = _div_pow2(s, CHUNK) == _div_pow2(t, CHUNK)
    hi = et.astype(BF16)
    rest = et - hi.astype(F32)
    mid = rest.astype(BF16)
    lo = (rest - mid.astype(F32)).astype(BF16)

    def summed(ones):
        m = jnp.where(ones, 1.0, 0.0).astype(BF16)
        return _bdot(hi, m) + _bdot(mid, m) + _bdot(lo, m)

    pre = summed(same & (s <= t))
    suf = summed(same & (s >= t))
    tot = summed(same)
    slot = _mod_pow2(_row_iota((LANE, TILE)), SUBLANE)
    o_ref[0, 0] = jnp.where(slot == 0, pre, jnp.where(slot == 2, suf, jnp.where(slot >= 4, tot, et)))


def _dn_gates(ab, alog_c, dtb_c, isdec_c):
    bsz, n, _ = ab.shape
    nt = n // TILE
    vec = pl.BlockSpec((1, LANE), lambda b, i: (0, 0))
    return pl.pallas_call(
        _dn_gate_kernel,
        grid=(bsz, nt),
        in_specs=[pl.BlockSpec((1, TILE, LANE), lambda b, i: (b, i, 0)), vec, vec, vec],
        out_specs=pl.BlockSpec((1, 1, LANE, TILE), lambda b, i: (b, i, 0, 0)),
        out_shape=jax.ShapeDtypeStruct((bsz, nt, LANE, TILE), F32),
        compiler_params=_cparams(("parallel", "parallel")),
        name="dn_gates",
    )(ab, alog_c, dtb_c, isdec_c)


def _dn_masks(mask_ref):
    a = _row_iota((TILE, TILE))
    b = _col_iota((TILE, TILE))
    apart = a ^ b
    n_levels = CHUNK.bit_length() - 2
    for lg in range(1, n_levels + 1):
        mask_ref[lg - 1] = jnp.where((apart >> lg) == 1, 1.0, 0.0).astype(BF16)
    for k, pack in enumerate((DN_PACK, 2 * DN_PACK)):
        mask_ref[n_levels + k] = jnp.where(_div_pow2(a, pack) == _div_pow2(b, pack), 1.0, 0.0).astype(BF16)


def _dn_tiles(chains, mask_ref):
    a = _row_iota((TILE, TILE))
    b = _col_iota((TILE, TILE))
    same = _div_pow2(a, CHUNK) == _div_pow2(b, CHUNK)
    apart = a ^ b
    n_chunks = TILE // CHUNK
    dot = functools.partial(jnp.dot, preferred_element_type=F32)

    def stack(v, reps):
        return jnp.concatenate([v] * reps, axis=0)

    kk = [dot(c["k_rm"], c["kt"]) for c in chains]
    kq = [dot(c["k_rm"], c["qt"]) for c in chains]
    xs, pw, attn = [], [], []
    for c, kk_c, kq_c in zip(chains, kk, kq):
        incl = same & ((a >= b) if c["backward"] else (a <= b))
        gcb = jnp.broadcast_to(c["gc"], (TILE, TILE))
        diff = gcb - gcb.T
        decay = jnp.where(incl, jnp.exp(jnp.where(incl, diff, 0.0)), 0.0)
        attn.append((kq_c * decay).astype(BF16))
        x = kk_c * decay * (-c["beta"])
        xs.append(x.astype(BF16))
        base = jnp.where(a == b, 1.0, jnp.where(apart == 1, x, 0.0))
        acc = base[0:DN_PACK, :]
        for r in range(1, TILE // DN_PACK):
            acc = acc + base[r * DN_PACK:(r + 1) * DN_PACK, :]
        pw.append(acc)
    pack = DN_PACK
    n_levels = CHUNK.bit_length() - 2
    block_mask = {DN_PACK: n_levels, 2 * DN_PACK: n_levels + 1}
    s = 2
    while s < CHUNK:
        if s == pack:
            keep = _div_pow2(_row_iota((2 * pack, TILE)), pack) == (_div_pow2(_col_iota((2 * pack, TILE)), pack) & 1)
            pw = [jnp.where(keep, stack(p, 2), 0.0) for p in pw]
            pack *= 2
        couple = mask_ref[s.bit_length() - 2]
        blocks = mask_ref[block_mask[pack]]
        pb = [p.astype(BF16) for p in pw]
        px = [dot(pb_c, x * couple) for pb_c, x in zip(pb, xs)]
        p_bd = [stack(pb_c, TILE // pack) * blocks for pb_c in pb]
        pw = [p + dot(px_c.astype(BF16), bd_c) for p, px_c, bd_c in zip(pw, px, p_bd)]
        s *= 2
    blocks = mask_ref[block_mask[pack]]
    t_inv = [stack(p.astype(BF16), TILE // pack) * blocks for p in pw]
    egc = [jnp.exp(c["gc"]) for c in chains]
    u_t = [dot((c["vt"].astype(F32) * c["beta"]).astype(BF16), t) for c, t in zip(chains, t_inv)]
    w_t = [dot((c["kt"].astype(F32) * (c["beta"] * e)).astype(BF16), t).astype(BF16)
           for c, e, t in zip(chains, egc, t_inv)]
    qd_t = [(c["qt"].astype(F32) * e).astype(BF16) for c, e in zip(chains, egc)]
    kdec = [jnp.exp(c["tot"] - c["gc"]) for c in chains]
    outs = [[None] * n_chunks for _ in chains]
    for step in range(n_chunks):
        cis = [(n_chunks - 1 - step) if c["backward"] else step for c in chains]
        sl = [slice(ci * CHUNK, (ci + 1) * CHUNK) for ci in cis]
        st = [c["st_ref"][...] for c in chains]
        stb = [s_c.astype(BF16) for s_c in st]
        swq = [dot(stb_c, jnp.concatenate([w_c[:, r], q_c[:, r]], axis=1))
               for stb_c, w_c, q_c, r in zip(stb, w_t, qd_t, sl)]
        sq = [v[:, CHUNK:] for v in swq]
        vn = [u_c[:, r] - v[:, :CHUNK] for u_c, v, r in zip(u_t, swq, sl)]
        av = [dot(vn_c.astype(BF16), at_c[r, r]) for vn_c, at_c, r in zip(vn, attn, sl)]
        upd = [dot((vn_c * kd_c[:, r]).astype(BF16), c["k_rm"][r, :])
               for vn_c, kd_c, c, r in zip(vn, kdec, chains, sl)]
        for idx, c in enumerate(chains):
            c["st_ref"][...] = st[idx] * jnp.exp(c["tot"][:, sl[idx]]) + upd[idx]
            outs[idx][cis[idx]] = sq[idx] + av[idx]
    return [jnp.concatenate(o, axis=1) for o in outs]


def _dn_kernel(k_ref, qt_ref, kt_ref, vt_ref, g_ref, z_ref, ng_ref, s0f_ref, s0b_ref,
               o_ref, sf_ref, sb_ref, ot_ref, st_ref, mask_ref, *, n_tiles, with_output):
    hb = DN_HEADS_PER_STEP
    _dn_masks(mask_ref)
    for hh in range(hb):
        st_ref[2 * hh] = s0f_ref[0, hh]
        st_ref[2 * hh + 1] = s0b_ref[0, hh]

    def body(i, carry):
        nf = i
        nb = n_tiles - 1 - i
        chains = []
        for hh in range(hb):
            for backward, n in ((False, nf), (True, nb)):
                g = g_ref[0, n, hh * SUBLANE:(hh + 1) * SUBLANE, :]
                base = 2 if backward else 0
                chains.append(dict(
                    k_rm=k_ref[0, pl.ds(pl.multiple_of(n * TILE, TILE), TILE), hh * LANE:(hh + 1) * LANE],
                    qt=qt_ref[0, hh, n], kt=kt_ref[0, hh, n], vt=vt_ref[0, hh, n],
                    gc=g[base:base + 1, :], beta=g[base + 1:base + 2, :], tot=g[4 + base // 2:5 + base // 2, :],
                    st_ref=st_ref.at[2 * hh + (1 if backward else 0)], backward=backward, hh=hh, n=n))
        o_t = _dn_tiles(chains, mask_ref)
        if with_output:
            for c, o_c in zip(chains, o_t):
                ot_ref[c["hh"], c["n"]] = ot_ref[c["hh"], c["n"]] + o_c
        return carry

    if with_output:
        ot_ref[...] = jnp.zeros_like(ot_ref)
    lax.fori_loop(0, n_tiles, body, 0)
    for hh in range(hb):
        sf_ref[0, hh] = st_ref[2 * hh]
        sb_ref[0, hh] = st_ref[2 * hh + 1]
    if with_output:
        def finish(t, carry):
            rows = pl.ds(pl.multiple_of(t * TILE, TILE), TILE)
            for hh in range(hb):
                o = ot_ref[hh, t].T
                y = o * lax.rsqrt(jnp.mean(o * o, axis=-1, keepdims=True) + NORM_EPS) * ng_ref[...]
                z = z_ref[0, rows, hh * LANE:(hh + 1) * LANE].astype(F32)
                o_ref[0, rows, hh * LANE:(hh + 1) * LANE] = (y * _silu(z)).astype(BF16)
            return carry

        lax.fori_loop(0, n_tiles, finish, 0)
    else:
        o_ref[...] = jnp.zeros_like(o_ref)


def _delta_net(qkv_rm, qkv_tr, gates, proj, norm_g, s0f, s0b, with_output):
    bsz, n, _ = qkv_rm.shape
    nt = n // TILE
    h = DN_HEADS
    hb = DN_HEADS_PER_STEP
    wide = hb * LANE
    once = pl.Buffered(1)
    tr_spec = lambda off: pl.BlockSpec((1, hb, nt, LANE, TILE), lambda b, j: (b, off + j, 0, 0, 0), once)
    st_spec = pl.BlockSpec((1, hb, LANE, LANE), lambda b, j: (b, j, 0, 0))
    n_out = n if with_output else SUBLANE
    return pl.pallas_call(
        functools.partial(_dn_kernel, n_tiles=nt, with_output=with_output),
        grid=(bsz, h // hb),
        in_specs=[pl.BlockSpec((1, n, wide), lambda b, j: (b, 0, h // hb + j), once),
                  tr_spec(0), tr_spec(h // hb), tr_spec(2 * h // hb),
                  pl.BlockSpec((1, nt, hb * SUBLANE, TILE), lambda b, j: (b, 0, j, 0)),
                  pl.BlockSpec((1, n, wide), lambda b, j: (b, 0, Z_BLK // hb + j), once),
                  pl.BlockSpec((1, LANE), lambda b, j: (0, 0)),
                  st_spec, st_spec],
        out_specs=[pl.BlockSpec((1, n_out, wide), lambda b, j: (b, 0, j)), st_spec, st_spec],
        out_shape=[jax.ShapeDtypeStruct((bsz, n_out, DN_WIDTH), BF16),
                   jax.ShapeDtypeStruct((bsz, h, LANE, LANE), F32),
                   jax.ShapeDtypeStruct((bsz, h, LANE, LANE), F32)],
        scratch_shapes=[pltpu.VMEM((hb, nt, LANE, TILE), F32),
                        pltpu.VMEM((2 * hb, LANE, LANE), F32),
                        pltpu.VMEM((CHUNK.bit_length(), TILE, TILE), BF16)],
        compiler_params=_cparams(("parallel", "parallel"), 58),
        name="delta_net",
    )(qkv_rm, qkv_tr, qkv_tr, qkv_tr, gates, proj, norm_g.reshape(1, LANE), s0f, s0b)


def _lru_scan_block(x, wa, ba, wx, bx, spl, h_in, backward):
    rows, width = x.shape
    xb = x.astype(BF16)

    def gate(ws, bias):
        parts = [jnp.dot(xb[:, k * LRU_BLOCK:(k + 1) * LRU_BLOCK], w, preferred_element_type=F32)
                 for k, w in enumerate(ws)]
        return _sigmoid(jnp.concatenate(parts, axis=1) + bias)

    r = gate(wa, ba)
    gi = gate(wx, bx)
    log_a = -LRU_C * r * spl
    a = jnp.exp(log_a)
    b = jnp.sqrt(1.0 - jnp.exp(2.0 * log_a)) * (gi * x)
    groups = rows // SUBLANE
    a = a.reshape(groups, SUBLANE, width)
    b = b.reshape(groups, SUBLANE, width)
    sub = lax.broadcasted_iota(jnp.int32, a.shape, 1)
    s = 1
    while s < SUBLANE:
        if backward:
            keep = sub < SUBLANE - s
            a_sh = jnp.where(keep, pltpu.roll(a, SUBLANE - s, 1), 1.0)
            b_sh = jnp.where(keep, pltpu.roll(b, SUBLANE - s, 1), 0.0)
        else:
            keep = sub >= s
            a_sh = jnp.where(keep, pltpu.roll(a, s, 1), 1.0)
            b_sh = jnp.where(keep, pltpu.roll(b, s, 1), 0.0)
        b = a * b_sh + b
        a = a * a_sh
        s *= 2
    a = a.reshape(rows, width)
    b = b.reshape(rows, width)
    pieces = [None] * groups
    carry = h_in
    order = range(groups - 1, -1, -1) if backward else range(groups)
    edge = 0 if backward else SUBLANE - 1
    for gidx in order:
        lo = gidx * SUBLANE
        hgrp = b[lo:lo + SUBLANE, :] + a[lo:lo + SUBLANE, :] * carry
        pieces[gidx] = hgrp
        carry = hgrp[edge:edge + 1, :]
    return jnp.concatenate(pieces, axis=0), carry


def _lru_kernel(px_ref, py_ref, cw_ref, cb_ref, wa_ref, ba_ref, wx_ref, bx_ref, lam_ref, h0_ref,
                o_ref, last_ref, xs_ref, hs_ref, *, n_blocks, with_output):
    x = px_ref[0].astype(F32)
    xs_ref[...] = _dwconv_rows(x, cw_ref, 4) + cb_ref[...]
    spl = _softplus(-lam_ref[...])
    wa = [[wa_ref[d, k] for k in range(LRU_GROUP)] for d in range(2)]
    wx = [[wx_ref[d, k] for k in range(LRU_GROUP)] for d in range(2)]

    def body(i, carry):
        hf, hb = carry
        rf = pl.multiple_of(i * LRU_ROWS, LRU_ROWS)
        rb = pl.multiple_of((n_blocks - 1 - i) * LRU_ROWS, LRU_ROWS)
        h_f, hf = _lru_scan_block(xs_ref[pl.ds(rf, LRU_ROWS), :], wa[0], ba_ref[0:1, :],
                                  wx[0], bx_ref[0:1, :], spl[0:1, :], hf, False)
        h_b, hb = _lru_scan_block(xs_ref[pl.ds(rb, LRU_ROWS), :], wa[1], ba_ref[1:2, :],
                                  wx[1], bx_ref[1:2, :], spl[1:2, :], hb, True)
        if with_output:
            hs_ref[pl.ds(rf, LRU_ROWS), :] = hs_ref[pl.ds(rf, LRU_ROWS), :] + h_f
            hs_ref[pl.ds(rb, LRU_ROWS), :] = hs_ref[pl.ds(rb, LRU_ROWS), :] + h_b
        return hf, hb

    if with_output:
        hs_ref[...] = jnp.zeros_like(hs_ref)
    h0 = h0_ref[0]
    hf, hb = lax.fori_loop(0, n_blocks, body, (h0[0:1, :], h0[1:2, :]))
    last_ref[0] = jnp.concatenate([hf, hb], axis=0)
    if with_output:
        y = py_ref[0].astype(F32)
        gelu = 0.5 * y * (1.0 + jnp.tanh(math.sqrt(2.0 / math.pi) * (y + 0.044715 * (y * y * y))))
        o_ref[0] = (hs_ref[...] * gelu).astype(BF16)
    else:
        o_ref[...] = jnp.zeros_like(o_ref)


def _rglru(proj, conv_w, conv_b, w_a, b_a, w_x, b_x, lam, h0, with_output):
    bsz, n, _ = proj.shape
    rows = min(n, LRU_ROWS)
    nb = n // rows
    n_out = n if with_output else SUBLANE
    width = LRU_GROUP * LRU_BLOCK
    lx = LX_BLK * LANE // width
    ly = LY_BLK * LANE // width
    vec2 = pl.BlockSpec((2, width), lambda b, j: (0, j))
    wspec = pl.BlockSpec((2, LRU_GROUP, LRU_BLOCK, LRU_BLOCK), lambda b, j: (0, j, 0, 0))
    return pl.pallas_call(
        functools.partial(_lru_kernel, n_blocks=nb, with_output=with_output),
        grid=(bsz, LRU_BLOCKS // LRU_GROUP),
        in_specs=[pl.BlockSpec((1, n, width), lambda b, j: (b, 0, lx + j)),
                  pl.BlockSpec((1, n, width), lambda b, j: (b, 0, ly + j)),
                  pl.BlockSpec((4, width), lambda b, j: (0, j)),
                  pl.BlockSpec((1, width), lambda b, j: (0, j)),
                  wspec, vec2, wspec, vec2, vec2,
                  pl.BlockSpec((1, 2, width), lambda b, j: (b, 0, j))],
        out_specs=[pl.BlockSpec((1, n_out, width), lambda b, j: (b, 0, j)),
                   pl.BlockSpec((1, 2, width), lambda b, j: (b, 0, j))],
        out_shape=[jax.ShapeDtypeStruct((bsz, n_out, LRU_WIDTH), BF16),
                   jax.ShapeDtypeStruct((bsz, 2, LRU_WIDTH), F32)],
        scratch_shapes=[pltpu.VMEM((n, width), F32), pltpu.VMEM((n, width), F32)],
        compiler_params=_cparams(("parallel", "parallel")),
        name="rglru",
    )(proj, proj, conv_w, conv_b.reshape(1, LRU_WIDTH), w_a, b_a, w_x, b_x, lam, h0)


def _hy_prep_kernel(p0_ref, p1_ref, pv_ref, w0_ref, w1_ref, wv_ref, b0_ref, b1_ref, bv_ref, x0_ref, zz_ref):
    x0 = _dwconv_rows(p0_ref[0].astype(F32), w0_ref, 3) + b0_ref[...]
    x1 = _dwconv_rows(p1_ref[0].astype(F32), w1_ref, 3) + b1_ref[...]
    v = _dwconv_rows(pv_ref[0].astype(F32), wv_ref, 3) + bv_ref[...]
    x0_ref[0] = x0.astype(BF16)
    zz_ref[0] = (x1 * v).astype(BF16)


def _hy_prep(proj, conv_w, conv_b):
    bsz, n, _ = proj.shape
    nblk = HY_WIDTH // LANE
    pspec = lambda off: pl.BlockSpec((1, n, LANE), lambda b, j: (b, 0, HY_BLK + off + j))
    wspec = lambda off: pl.BlockSpec((3, LANE), lambda b, j: (0, off + j))
    bspec = lambda off: pl.BlockSpec((1, LANE), lambda b, j: (0, off + j))
    ospec = pl.BlockSpec((1, n, LANE), lambda b, j: (b, 0, j))
    cb = conv_b.reshape(1, 3 * HY_WIDTH)
    return pl.pallas_call(
        _hy_prep_kernel,
        grid=(bsz, nblk),
        in_specs=[pspec(0), pspec(nblk), pspec(2 * nblk), wspec(0), wspec(nblk), wspec(2 * nblk),
                  bspec(0), bspec(nblk), bspec(2 * nblk)],
        out_specs=[ospec, ospec],
        out_shape=[jax.ShapeDtypeStruct((bsz, n, HY_WIDTH), BF16)] * 2,
        compiler_params=_cparams(("parallel", "parallel")),
        name="hy_prep",
    )(proj, proj, proj, conv_w, conv_w, conv_w, cb, cb, cb)


def _hy_filter_kernel(w1_ref, b1_ref, f1_ref, w2_ref, b2_ref, f2_ref, w3_ref, band_ref, delta_ref,
                      k_ref, *, n, rows):
    def lag(shape):
        p = _row_iota(shape) + (pl.program_id(0) * rows - n)
        return p, jnp.abs(p).astype(F32)

    _, i = lag((rows, LANE))
    lane = _col_iota((rows, LANE))
    t = i * (1.0 / (n - 1))
    ang = band_ref[...] * (i * (2.0 * math.pi / n))
    feat = jnp.where(lane == 0, t,
                     jnp.where(lane <= HY_BANDS, jnp.cos(ang), jnp.where(lane < HY_EMB, -jnp.sin(ang), 0.0)))
    hid = jnp.sin(f1_ref[...] * (jnp.dot(feat, w1_ref[...], precision=HIGHEST,
                                         preferred_element_type=F32) + b1_ref[...]))
    hid = jnp.sin(f2_ref[...] * (jnp.dot(hid, w2_ref[...], precision=HIGHEST,
                                         preferred_element_type=F32) + b2_ref[...]))
    filt = _bdot(hid, w3_ref[...])
    p, iw = lag((rows, HY_WIDTH))
    dec = jnp.exp(-(iw * (1.0 / (n - 1))) * delta_ref[...])
    h_f = filt[:, :HY_WIDTH] * dec
    h_b = filt[:, HY_WIDTH:] * dec
    taps = jnp.where(p > 0, h_f, jnp.where(p < 0, h_b, h_f + h_b))
    k_ref[...] = jnp.where(p == -n, 0.0, taps).astype(BF16)


def _hy_filter(n, w1, b1, f1, w2, b2, f2, w3):
    rows = min(n, 256)
    hid = HY_FILTER_HIDDEN
    w1p = jnp.zeros((LANE, hid), F32).at[:HY_EMB].set(w1)
    bands = np.zeros((1, LANE), np.float32)
    base = np.linspace(1e-4, HY_BANDS - 1, HY_BANDS, dtype=np.float32)
    bands[0, 1:1 + HY_BANDS] = base
    bands[0, 1 + HY_BANDS:HY_EMB] = base
    log_target = math.log(HY_DECAY_TARGET)
    deltas = np.abs(np.linspace(log_target / HY_SLOW_DECAY_PCT, log_target / HY_FAST_DECAY_PCT, HY_WIDTH,
                                dtype=np.float32)).reshape(1, HY_WIDTH)
    full = lambda shape: pl.BlockSpec(shape, lambda i: (0,) * len(shape))
    return pl.pallas_call(
        functools.partial(_hy_filter_kernel, n=n, rows=rows),
        grid=(2 * n // rows,),
        in_specs=[full((LANE, hid)), full((1, hid)), full((1, hid)), full((hid, hid)), full((1, hid)),
                  full((1, hid)), full((hid, 2 * HY_WIDTH)), full((1, LANE)), full((1, HY_WIDTH))],
        out_specs=pl.BlockSpec((rows, HY_WIDTH), lambda i: (i, 0)),
        out_shape=jax.ShapeDtypeStruct((2 * n, HY_WIDTH), BF16),
        compiler_params=_cparams(("parallel",)),
        name="hy_filter",
    )(w1p, b1.reshape(1, hid), f1.reshape(1, hid), w2, b2.reshape(1, hid), f2.reshape(1, hid), w3,
      jnp.asarray(bands), jnp.asarray(deltas))


DFT_GROUP = 64


def _dft_table_kernel(c_ref, s_ref, cb_ref, sb_ref, *, m, ncols, col0, rows, transposed, blank_first):
    period = 4 * m
    scale = 2.0 * math.pi / period
    col = _col_iota((1, ncols)) + col0

    @pl.when(pl.program_id(0) == 0)
    def _():
        r2 = _row_iota((DFT_GROUP, ncols))
        c2 = _col_iota((DFT_GROUP, ncols)) + col0
        ph = (r2 * (2 * c2 + 1)) if transposed else ((2 * r2 + 1) * c2)
        ang = (ph & (period - 1)).astype(F32) * scale
        cb_ref[...] = jnp.cos(ang)
        sb_ref[...] = jnp.sin(ang)

    for g in range(rows // DFT_GROUP):
        r1 = pl.program_id(0) * (rows // DFT_GROUP) + g
        ph = (DFT_GROUP * r1) * (2 * col + 1) if transposed else (2 * DFT_GROUP * r1) * col
        ang = (ph & (period - 1)).astype(F32) * scale
        ca = jnp.cos(ang)
        sa = jnp.sin(ang)
        cb = cb_ref[...]
        sb = sb_ref[...]
        c_tile = ca * cb - sa * sb
        s_tile = sa * cb + ca * sb
        if blank_first:
            first = _col_iota((DFT_GROUP, ncols)) == 0
            c_tile = jnp.where(first, 0.0, c_tile)
            s_tile = jnp.where(first, 0.0, s_tile)
        c_ref[g * DFT_GROUP:(g + 1) * DFT_GROUP, :] = c_tile.astype(BF16)
        s_ref[g * DFT_GROUP:(g + 1) * DFT_GROUP, :] = s_tile.astype(BF16)


def _dft_tables(m, transposed=False, two_sided=False):
    rows = min(m, 256)
    ncols = 2 * m if two_sided else m
    spec = pl.BlockSpec((rows, ncols), lambda i: (i, 0))
    return pl.pallas_call(
        functools.partial(_dft_table_kernel, m=m, ncols=ncols, col0=3 * m if two_sided else 0, rows=rows,
                          transposed=transposed, blank_first=two_sided),
        grid=(m // rows,),
        in_specs=[],
        out_specs=[spec, spec],
        out_shape=[jax.ShapeDtypeStruct((m, ncols), BF16)] * 2,
        scratch_shapes=[pltpu.VMEM((DFT_GROUP, ncols), F32), pltpu.VMEM((DFT_GROUP, ncols), F32)],
        compiler_params=_cparams(("arbitrary",)),
        name="dft_tables",
    )()


def _hy_tables(m):
    return _dft_tables(m) + _dft_tables(m, transposed=True) + _dft_tables(m, two_sided=True)


def _hy_ktrans_kernel(cw_ref, sw_ref, lo_ref, hi_ref, kc_ref, ks_ref, *, m):
    dot = functools.partial(jnp.dot, preferred_element_type=F32)
    lo = lo_ref[...]
    hi = hi_ref[...]
    kc_ref[0] = dot(cw_ref[:, :m], lo) + dot(cw_ref[:, m:], hi)
    ks_ref[0] = dot(sw_ref[:, :m], lo) + dot(sw_ref[:, m:], hi)


def _hy_ktrans(cw, sw, taps, m):
    nd = taps.shape[0] // m - 1
    w = taps.shape[1]
    tn = 512
    tab = pl.BlockSpec((m, 2 * m), lambda e, j: (0, 0))
    ospec = pl.BlockSpec((1, m, tn), lambda e, j: (e, 0, j))
    return pl.pallas_call(
        functools.partial(_hy_ktrans_kernel, m=m),
        grid=(nd, w // tn),
        in_specs=[tab, tab, pl.BlockSpec((m, tn), lambda e, j: (e, j)), pl.BlockSpec((m, tn), lambda e, j: (e + 1, j))],
        out_specs=[ospec, ospec],
        out_shape=[jax.ShapeDtypeStruct((nd, m, w), F32)] * 2,
        compiler_params=_cparams(("parallel", "parallel")),
        name="hy_ktrans",
    )(cw, sw, taps, taps)


def _hy_fwd_kernel(c_ref, s_ref, zz_ref, kc_ref, ks_ref, a_ref, b_ref, *, m, nb):
    dot = functools.partial(jnp.dot, preferred_element_type=F32)
    c = c_ref[...]
    s = s_ref[...]
    acc_a = [None] * nb
    acc_b = [None] * nb
    for j in range(nb):
        zz = zz_ref[0, j * m:(j + 1) * m, :]
        uc = dot(c, zz)
        us = dot(s, zz)
        for i in range(nb):
            kc = kc_ref[i - j + nb - 1]
            ks = ks_ref[i - j + nb - 1]
            ta = uc * kc - us * ks
            tb = uc * ks + us * kc
            acc_a[i] = ta if acc_a[i] is None else acc_a[i] + ta
            acc_b[i] = tb if acc_b[i] is None else acc_b[i] + tb
    for i in range(nb):
        a_ref[0, i] = acc_a[i].astype(BF16)
        b_ref[0, i] = acc_b[i].astype(BF16)


def _hy_forward(ctab, stab, zz, kc, ks, m):
    bsz, n, w = zz.shape
    nb = n // m
    nd = 2 * nb - 1
    tf = min(m, 256)
    tn = 512
    tab = pl.BlockSpec((tf, m), lambda b, j, i: (i, 0))
    kspec = pl.BlockSpec((nd, tf, tn), lambda b, j, i: (0, i, j))
    ospec = pl.BlockSpec((1, nb, tf, tn), lambda b, j, i: (b, 0, i, j))
    return pl.pallas_call(
        functools.partial(_hy_fwd_kernel, m=m, nb=nb),
        grid=(bsz, w // tn, m // tf),
        in_specs=[tab, tab, pl.BlockSpec((1, n, tn), lambda b, j, i: (b, 0, j)), kspec, kspec],
        out_specs=[ospec, ospec],
        out_shape=[jax.ShapeDtypeStruct((bsz, nb, m, w), BF16)] * 2,
        compiler_params=_cparams(("parallel", "parallel", "parallel")),
        name="hy_forward",
    )(ctab, stab, zz, kc, ks)


def _hy_inv_kernel(ct_ref, st_ref, a_ref, b_ref, x0_ref, zz_ref, bias_ref, o_ref, *, m):
    y = (jnp.dot(ct_ref[...], a_ref[0, 0], preferred_element_type=F32)
         + jnp.dot(st_ref[...], b_ref[0, 0], preferred_element_type=F32)) * (1.0 / m)
    zz = zz_ref[0].astype(F32)
    o_ref[0] = (x0_ref[0].astype(F32) * (y + zz * bias_ref[...])).astype(BF16)


def _hy_inverse(cttab, sttab, a, bq, x0, zz, bias, m):
    bsz, n, w = zz.shape
    nb = n // m
    tn = 512
    tab = pl.BlockSpec((m, m), lambda b, i, j: (0, 0))
    spec4 = pl.BlockSpec((1, 1, m, tn), lambda b, i, j: (b, i, 0, j))
    tile = pl.BlockSpec((1, m, tn), lambda b, i, j: (b, i, j))
    return pl.pallas_call(
        functools.partial(_hy_inv_kernel, m=m),
        grid=(bsz, nb, w // tn),
        in_specs=[tab, tab, spec4, spec4, tile, tile, pl.BlockSpec((1, tn), lambda b, i, j: (0, j))],
        out_specs=tile,
        out_shape=jax.ShapeDtypeStruct((bsz, n, w), BF16),
        compiler_params=_cparams(("parallel", "parallel", "parallel")),
        name="hy_inverse",
    )(cttab, sttab, a, bq, x0, zz, bias.reshape(1, w))


def _merge_kernel(odn_ref, ohy_ref, olru_ref, gdn_ref, ghy_ref, glru_ref, wdn_ref, why_ref, wlru_ref,
                  wout_ref, x_ref, gate_ref, o_ref):
    m = _sigmoid(gdn_ref[0].astype(F32)) * jnp.dot(odn_ref[0], wdn_ref[...], preferred_element_type=F32)
    m = m + _sigmoid(ghy_ref[0].astype(F32)) * jnp.dot(ohy_ref[0], why_ref[...], preferred_element_type=F32)
    m = m + _sigmoid(glru_ref[0].astype(F32)) * jnp.dot(olru_ref[0], wlru_ref[...], preferred_element_type=F32)
    y = jnp.dot(m.astype(BF16), wout_ref[...], preferred_element_type=F32)
    o_ref[0] = x_ref[0] + gate_ref[0] * y


def _merge(o_dn, o_hy, o_lru, proj, w_dn, w_hy, w_lru, w_out, x, gate):
    bsz, n, d = x.shape
    tm = min(n, 256)
    nblk = d // 1024
    act = pl.BlockSpec((1, tm, d), lambda b, i: (b, i, 0))
    gspec = lambda k: pl.BlockSpec((1, tm, d), lambda b, i: (b, i, GATE_BLK * LANE // d + k * nblk))
    wspec = pl.BlockSpec((d, d), lambda b, i: (0, 0))
    return pl.pallas_call(
        _merge_kernel,
        grid=(bsz, n // tm),
        in_specs=[act, act, act, gspec(0), gspec(1), gspec(2), wspec, wspec, wspec, wspec, act,
                  pl.BlockSpec((1, 1, d), lambda b, i: (b, 0, 0))],
        out_specs=act,
        out_shape=jax.ShapeDtypeStruct((bsz, n, d), F32),
        compiler_params=_cparams(("parallel", "parallel")),
        name="merge",
    )(o_dn, o_hy, o_lru, proj, proj, proj, w_dn, w_hy, w_lru, w_out, x, gate)


def _ffn_act_kernel(ug_ref, uv_ref, wg_ref, wv_ref, o_ref, up_ref, mid_ref, dn_ref, *, rows, cols, n):
    tile = FFN_TILE
    a = _row_iota((tile, tile))
    b = _col_iota((tile, tile))
    c = _mod_pow2(a, cols)
    lmat = jnp.where((b == a - 1) & (c >= 1), 1.0, 0.0).astype(BF16)
    rmat = jnp.where((b == a + 1) & (c <= cols - 2), 1.0, 0.0).astype(BF16)
    taps = (-1, 0, 1) if rows > 1 else (0,)
    dst = {-1: up_ref, 0: mid_ref, 1: dn_ref}
    if rows > 1:
        pad = jnp.zeros((2, cols, FFN_WIDE), BF16)
        up_ref[:, 0:cols, :] = pad
        dn_ref[:, n + cols:n + 2 * cols, :] = pad

    def taps_of(i, carry):
        r0 = pl.multiple_of(i * tile, tile)
        for idx, (u_ref, w_ref) in enumerate(((ug_ref, wg_ref), (uv_ref, wv_ref))):
            u = u_ref[0, pl.ds(r0, tile), :]
            left = jnp.dot(lmat, u, preferred_element_type=F32).astype(BF16)
            right = jnp.dot(rmat, u, preferred_element_type=F32).astype(BF16)
            for di in taps:
                k = 3 * (di + 1)
                w = [w_ref[k + j:k + j + 1, :].astype(BF16) for j in range(3)]
                off = pl.multiple_of(r0 + (cols if di != 0 else 0), cols)
                dst[di][idx, pl.ds(off, tile), :] = left * w[0] + u * w[1] + right * w[2]
        return carry

    def combine(i, carry):
        r0 = pl.multiple_of(i * tile, tile)
        vals = []
        for idx in range(2):
            acc = mid_ref[idx, pl.ds(r0, tile), :]
            if rows > 1:
                acc = (acc + up_ref[idx, pl.ds(r0, tile), :]
                       + dn_ref[idx, pl.ds(pl.multiple_of(r0 + 2 * cols, cols), tile), :])
            vals.append(acc.astype(F32))
        o_ref[0, pl.ds(r0, tile), :] = (_silu(vals[0]) * vals[1]).astype(BF16)
        return carry

    lax.fori_loop(0, n // tile, taps_of, 0, unroll=min(2, n // tile))
    lax.fori_loop(0, n // tile, combine, 0)


def _ffn_act(u, conv_w, rows, cols):
    bsz, n, _ = u.shape
    nblk = FFN_HIDDEN // FFN_WIDE
    padded = n + 2 * cols if rows > 1 else SUBLANE * 2
    return pl.pallas_call(
        functools.partial(_ffn_act_kernel, rows=rows, cols=cols, n=n),
        grid=(bsz, nblk),
        in_specs=[pl.BlockSpec((1, n, FFN_WIDE), lambda b, j: (b, 0, j)),
                  pl.BlockSpec((1, n, FFN_WIDE), lambda b, j: (b, 0, nblk + j)),
                  pl.BlockSpec((9, FFN_WIDE), lambda b, j: (0, j)),
                  pl.BlockSpec((9, FFN_WIDE), lambda b, j: (0, nblk + j))],
        out_specs=pl.BlockSpec((1, n, FFN_WIDE), lambda b, j: (b, 0, j)),
        out_shape=jax.ShapeDtypeStruct((bsz, n, FFN_HIDDEN), BF16),
        scratch_shapes=[pltpu.VMEM((2, padded, FFN_WIDE), BF16), pltpu.VMEM((2, n, FFN_WIDE), BF16),
                        pltpu.VMEM((2, padded, FFN_WIDE), BF16)],
        compiler_params=_cparams(("parallel", "parallel")),
        name="ffn_act",
    )(u, u, conv_w, conv_w)


def _gate_column_map():
    src = np.full((LANE,), -1, np.int32)
    isdec = np.zeros((1, LANE), np.float32)
    dirs = np.zeros((LANE,), np.int32)
    for h in range(DN_HEADS):
        for slot, (d, kind) in enumerate(((0, 0), (0, 1), (1, 0), (1, 1), (0, 0), (1, 0))):
            src[h * SUBLANE + slot] = d * 2 * DN_HEADS + kind * DN_HEADS + h
            isdec[0, h * SUBLANE + slot] = 1.0 if kind == 0 else 0.0
            dirs[h * SUBLANE + slot] = d
    return src, isdec, dirs


def _split_in_proj(w_in):
    o = np.cumsum((3 * DN_WIDTH, DN_WIDTH, 4 * DN_HEADS, 3 * HY_WIDTH, LRU_WIDTH, LRU_WIDTH)).tolist()
    w_main = jnp.concatenate([w_in[:, :o[1]], w_in[:, o[2]:]], axis=1).astype(BF16)
    src, _, _ = _gate_column_map()
    w_ab = w_in[:, o[1]:o[2]]
    w_gate = jnp.where(jnp.asarray(src >= 0)[None, :], w_ab[:, np.maximum(src, 0)], 0.0).astype(BF16)
    return w_main, w_gate


def _gate_params(a_log, dt_bias):
    _, isdec, dirs = _gate_column_map()
    head = (np.arange(LANE) // SUBLANE).astype(np.int32)
    alog_c = a_log[dirs, head].reshape(1, LANE)
    dtb_c = dt_bias[dirs, head].reshape(1, LANE)
    return alog_c, dtb_c, jnp.asarray(isdec)


def _token_views(bsz, n, mods, shared_mod):
    if not shared_mod:
        same = lambda t: t
        return same, same, mods
    flat = lambda t: t.reshape(1, bsz * n, t.shape[-1])
    unflat = lambda t: t.reshape(bsz, n, t.shape[-1])
    return flat, unflat, [m[:1] for m in mods]


def _token_mixer(x, mods, lp, states, tables, with_output, shared_mod):
    bsz, n, _ = x.shape
    flat, unflat, tmods = _token_views(bsz, n, mods, shared_mod)
    proj, ab = _norm_mod_matmul(flat(x), lp["norm1_g"], tmods[0], tmods[1], lp["w_main"], BF16, 2048, 1024,
                                lp["w_gate"])
    proj, ab = unflat(proj), unflat(ab)

    qkv_rm, qkv_tr = _dn_prep(proj, lp["dn_conv_w"])
    gates = _dn_gates(ab, *lp["gate_params"])
    o_dn, s_f, s_b = _delta_net(qkv_rm, qkv_tr, gates, proj, lp["dn_norm_g"], states[0], states[1], with_output)
    o_lru, h_last = _rglru(proj, lp["lru_conv_w"], lp["lru_conv_b"], lp["lru_w_a"], lp["lru_b_a"],
                           lp["lru_w_x"], lp["lru_b_x"], lp["lru_lambda"], states[2], with_output)
    new_states = (s_f, s_b, h_last)
    if not with_output:
        return None, new_states

    m = min(n, HY_BLOCK)
    ctab, stab, cttab, sttab, cwtab, swtab = tables
    taps = _hy_filter(n, lp["hy_w1"], lp["hy_b1"], lp["hy_f1"], lp["hy_w2"], lp["hy_b2"], lp["hy_f2"], lp["hy_w3"])
    kc, ks = _hy_ktrans(cwtab, swtab, taps, m)
    x0, zz = _hy_prep(proj, lp["hy_conv_w"], lp["hy_conv_b"])
    a, bq = _hy_forward(ctab, stab, zz, kc, ks, m)
    o_hy = _hy_inverse(cttab, sttab, a, bq, x0, zz, lp["hy_bias"], m)

    x = _merge(flat(o_dn), flat(o_hy), flat(o_lru), flat(proj), lp["w_proj_dn"], lp["w_proj_hy"], lp["w_proj_lru"],
               lp["w_out"], flat(x), tmods[2])
    return unflat(x), new_states


def _conv_ffn(x, mods, lp, rows, cols, shared_mod, final_gain=None):
    bsz, n, _ = x.shape
    flat, unflat, tmods = _token_views(bsz, n, mods, shared_mod)
    u = _norm_mod_matmul(flat(x), lp["norm2_g"], tmods[3], tmods[4], lp["ffn_up"], BF16, 1024, FFN_HIDDEN)
    act = _ffn_act(unflat(u), lp["ffn_conv_w"], rows, cols)
    return unflat(_matmul_residual(flat(act), lp["ffn_down"], flat(x), tmods[5], final_gain))


def kernel(x, c, ctx, c_ctx, w_mod, b_mod, norm1_g, norm2_g, w_in, dn_conv_w, dn_a_log, dn_dt_bias, dn_norm_g,
           hy_conv_w, hy_conv_b, hy_w1, hy_b1, hy_f1, hy_w2, hy_b2, hy_f2, hy_w3, hy_bias,
           lru_conv_w, lru_conv_b, lru_w_a, lru_b_a, lru_w_x, lru_b_x, lru_lambda,
           w_proj_dn, w_proj_hy, w_proj_lru, w_out, ffn_up, ffn_conv_w, ffn_down, final_norm_g):
    bsz, n_lat, d = x.shape
    n_ctx = ctx.shape[1]
    depth = w_in.shape[0]
    rows = n_lat // GRID_W

    cvec = jnp.zeros((SUBLANE, d), F32).at[:bsz].set(c).at[bsz].set(c_ctx)
    lat_tables = _hy_tables(min(n_lat, HY_BLOCK))
    ctx_tables = _hy_tables(min(n_ctx, HY_BLOCK))
    zero_states = (jnp.zeros((bsz, DN_HEADS, LANE, LANE), F32), jnp.zeros((bsz, DN_HEADS, LANE, LANE), F32),
                   jnp.zeros((bsz, 2, LRU_WIDTH), F32))

    xc = ctx
    for l in range(depth):
        ctx_needed = l < depth - 1
        w_main, w_gate = _split_in_proj(w_in[l])
        lp = dict(
            norm1_g=norm1_g[l], norm2_g=norm2_g[l], w_main=w_main, w_gate=w_gate,
            dn_conv_w=dn_conv_w[l], gate_params=_gate_params(dn_a_log[l], dn_dt_bias[l]), dn_norm_g=dn_norm_g[l],
            hy_conv_w=hy_conv_w[l], hy_conv_b=hy_conv_b[l], hy_w1=hy_w1[l], hy_b1=hy_b1[l], hy_f1=hy_f1[l],
            hy_w2=hy_w2[l], hy_b2=hy_b2[l], hy_f2=hy_f2[l], hy_w3=hy_w3[l], hy_bias=hy_bias[l],
            lru_conv_w=lru_conv_w[l], lru_conv_b=lru_conv_b[l], lru_w_a=lru_w_a[l].astype(BF16),
            lru_b_a=lru_b_a[l], lru_w_x=lru_w_x[l].astype(BF16), lru_b_x=lru_b_x[l], lru_lambda=lru_lambda[l],
            w_proj_dn=w_proj_dn[l].astype(BF16), w_proj_hy=w_proj_hy[l].astype(BF16),
            w_proj_lru=w_proj_lru[l].astype(BF16), w_out=w_out[l].astype(BF16),
            ffn_up=ffn_up[l].astype(BF16), ffn_conv_w=ffn_conv_w[l].reshape(9, 2 * FFN_HIDDEN),
            ffn_down=ffn_down[l].astype(BF16))
        mod = _modulation(cvec, w_mod[l].astype(BF16), b_mod[l])
        lat_mod = [mod[:bsz, k * d:(k + 1) * d].reshape(bsz, 1, d) for k in range(N_MOD)]
        ctx_mod = [jnp.broadcast_to(mod[bsz:bsz + 1, k * d:(k + 1) * d].reshape(1, 1, d), (bsz, 1, d))
                   for k in range(N_MOD)]

        xc_new, ctx_states = _token_mixer(xc, ctx_mod, lp, zero_states, ctx_tables, ctx_needed, True)
        x, _ = _token_mixer(x, lat_mod, lp, ctx_states, lat_tables, True, False)
        x = _conv_ffn(x, lat_mod, lp, rows, GRID_W, False, None if ctx_needed else final_norm_g)
        if ctx_needed:
            xc = _conv_ffn(xc_new, ctx_mod, lp, 1, n_ctx, True)
    return x
```

```python
import functools
import math

import numpy as np
import jax
import jax.numpy as jnp
from jax import lax
from jax.experimental import pallas as pl
from jax.experimental.pallas import tpu as pltpu

F32 = jnp.float32
BF16 = jnp.bfloat16
HIGHEST = lax.Precision.HIGHEST

D_MODEL = 1024
DEPTH = 2
GRID_W = 64
NORM_EPS = 1e-6
N_MOD = 6

DN_HEADS = 8
DN_HEAD_DIM = 128
DN_WIDTH = DN_HEADS * DN_HEAD_DIM
HY_WIDTH = 1024
HY_EMB = 33
HY_BANDS = (HY_EMB - 1) // 2
HY_FILTER_HIDDEN = 64
HY_FAST_DECAY_PCT = 0.3
HY_SLOW_DECAY_PCT = 1.5
HY_DECAY_TARGET = 1e-2
LRU_WIDTH = 1024
LRU_BLOCKS = 8
LRU_BLOCK = LRU_WIDTH // LRU_BLOCKS
LRU_C = 8.0
FFN_HIDDEN = 2816

LANE = 128
SUBLANE = 8
TILE = 256
CHUNK = 128
DN_HEADS_PER_STEP = 4
DN_PREP_WIDE_ROWS = 512
DN_PACK = 64
HY_BLOCK = 1024
LRU_ROWS = 256
LRU_GROUP = 4
FFN_TILE = 256
FFN_WIDE = 256
MIB = 1024 * 1024

QKV_BLK = 0
Z_BLK = 24
HY_BLK = 32
LX_BLK = 56
LY_BLK = 64
GATE_BLK = 72
N_MAIN = 96 * LANE


def _cparams(sem, vmem_mib=48):
    return pltpu.CompilerParams(dimension_semantics=sem, vmem_limit_bytes=vmem_mib * MIB)


def _sigmoid(x):
    return 1.0 / (1.0 + jnp.exp(-x))


def _silu(x):
    return x * _sigmoid(x)


def _softplus(x):
    return jnp.maximum(x, 0.0) + jnp.log(1.0 + jnp.exp(-jnp.abs(x)))


def _row_iota(shape):
    return lax.broadcasted_iota(jnp.int32, shape, 0)


def _col_iota(shape):
    return lax.broadcasted_iota(jnp.int32, shape, 1)


def _div_pow2(x, k):
    assert k & (k - 1) == 0
    return x >> (k.bit_length() - 1)


def _mod_pow2(x, k):
    assert k & (k - 1) == 0
    return x & (k - 1)


def _bdot(a, b):
    return jnp.dot(a.astype(BF16), b.astype(BF16), preferred_element_type=F32)


def _mod_kernel(c_ref, w_ref, b_ref, o_ref):
    o_ref[...] = _bdot(_silu(c_ref[...]), w_ref[...]) + b_ref[...]


def _modulation(cvec, w_mod, b_mod):
    n = w_mod.shape[1]
    tn = 1024
    return pl.pallas_call(
        _mod_kernel,
        grid=(n // tn,),
        in_specs=[pl.BlockSpec((SUBLANE, D_MODEL), lambda j: (0, 0)),
                  pl.BlockSpec((D_MODEL, tn), lambda j: (0, j)),
                  pl.BlockSpec((1, tn), lambda j: (0, j))],
        out_specs=pl.BlockSpec((SUBLANE, tn), lambda j: (0, j)),
        out_shape=jax.ShapeDtypeStruct((SUBLANE, n), F32),
        compiler_params=_cparams(("parallel",)),
        name="modulation",
    )(cvec, w_mod, b_mod.reshape(1, n))


def _nmm_kernel(x_ref, g_ref, sh_ref, sc_ref, w_ref, *rest, with_side):
    o_ref = rest[1] if with_side else rest[0]
    h_ref = rest[-1]

    @pl.when(pl.program_id(2) == 0)
    def _():
        x = x_ref[0]
        y = x * lax.rsqrt(jnp.mean(x * x, axis=-1, keepdims=True) + NORM_EPS) * g_ref[...]
        h_ref[...] = (y * (1.0 + sc_ref[0]) + sh_ref[0]).astype(BF16)
        if with_side:
            rest[2][0] = jnp.dot(h_ref[...], rest[0][...], preferred_element_type=F32)

    o_ref[0] = jnp.dot(h_ref[...], w_ref[...], preferred_element_type=F32).astype(o_ref.dtype)


def _norm_mod_matmul(x, gain, shift, scale, w, out_dtype, tm, tn, w_side=None):
    bsz, n, d = x.shape
    nout = w.shape[1]
    tm = min(n, tm)
    with_side = w_side is not None
    in_specs = [pl.BlockSpec((1, tm, d), lambda b, i, j: (b, i, 0)),
                pl.BlockSpec((1, d), lambda b, i, j: (0, 0)),
                pl.BlockSpec((1, 1, d), lambda b, i, j: (b, 0, 0)),
                pl.BlockSpec((1, 1, d), lambda b, i, j: (b, 0, 0)),
                pl.BlockSpec((d, tn), lambda b, i, j: (0, j))]
    out_specs = [pl.BlockSpec((1, tm, tn), lambda b, i, j: (b, i, j))]
    out_shape = [jax.ShapeDtypeStruct((bsz, n, nout), out_dtype)]
    args = [x, gain.reshape(1, d), shift, scale, w]
    if with_side:
        ns = w_side.shape[1]
        in_specs.append(pl.BlockSpec((d, ns), lambda b, i, j: (0, 0)))
        out_specs.append(pl.BlockSpec((1, tm, ns), lambda b, i, j: (b, i, 0)))
        out_shape.append(jax.ShapeDtypeStruct((bsz, n, ns), F32))
        args.append(w_side)
    out = pl.pallas_call(
        functools.partial(_nmm_kernel, with_side=with_side),
        grid=(bsz, n // tm, nout // tn),
        in_specs=in_specs,
        out_specs=out_specs,
        out_shape=out_shape,
        scratch_shapes=[pltpu.VMEM((tm, d), BF16)],
        compiler_params=_cparams(("parallel", "parallel", "arbitrary")),
        name="norm_mod_matmul",
    )(*args)
    return out if with_side else out[0]


def _mm_kernel(a_ref, b_ref, o_ref):
    o_ref[...] = jnp.dot(a_ref[...], b_ref[...], preferred_element_type=F32).astype(o_ref.dtype)


def _matmul(a, b, out_dtype, tm, tn):
    m, k = a.shape
    n = b.shape[1]
    return pl.pallas_call(
        _mm_kernel,
        grid=(m // tm, n // tn),
        in_specs=[pl.BlockSpec((tm, k), lambda i, j: (i, 0)),
                  pl.BlockSpec((k, tn), lambda i, j: (0, j))],
        out_specs=pl.BlockSpec((tm, tn), lambda i, j: (i, j)),
        out_shape=jax.ShapeDtypeStruct((m, n), out_dtype),
        compiler_params=_cparams(("parallel", "parallel")),
        name="matmul",
    )(a, b)


def _mm_res_kernel(a_ref, b_ref, x_ref, g_ref, *rest, final_norm):
    o_ref = rest[-1]
    out = x_ref[0] + g_ref[0] * jnp.dot(a_ref[0], b_ref[...], preferred_element_type=F32)
    if final_norm:
        out = out * lax.rsqrt(jnp.mean(out * out, axis=-1, keepdims=True) + NORM_EPS) * rest[0][...]
    o_ref[0] = out


def _matmul_residual(a, w, x, gate, final_gain=None):
    bsz, n, k = a.shape
    d = w.shape[1]
    tm = min(n, 512)
    in_specs = [pl.BlockSpec((1, tm, k), lambda b, i: (b, i, 0)),
                pl.BlockSpec((k, d), lambda b, i: (0, 0)),
                pl.BlockSpec((1, tm, d), lambda b, i: (b, i, 0)),
                pl.BlockSpec((1, 1, d), lambda b, i: (b, 0, 0))]
    args = [a, w, x, gate]
    if final_gain is not None:
        in_specs.append(pl.BlockSpec((1, d), lambda b, i: (0, 0)))
        args.append(final_gain.reshape(1, d))
    return pl.pallas_call(
        functools.partial(_mm_res_kernel, final_norm=final_gain is not None),
        grid=(bsz, n // tm),
        in_specs=in_specs,
        out_specs=pl.BlockSpec((1, tm, d), lambda b, i: (b, i, 0)),
        out_shape=jax.ShapeDtypeStruct((bsz, n, d), F32),
        compiler_params=_cparams(("parallel", "parallel")),
        name="matmul_residual",
    )(*args)


def _dwconv_rows(x, w_ref, k):
    n = x.shape[0]
    left = (k - 1) // 2
    t = _row_iota((SUBLANE, x.shape[1]))
    acc = x * w_ref[left:left + 1, :]
    head_fix = jnp.zeros((SUBLANE, x.shape[1]), F32)
    tail_fix = jnp.zeros((SUBLANE, x.shape[1]), F32)
    for j in range(k):
        off = j - left
        if off == 0:
            continue
        term = pltpu.roll(x, (-off) % n, 0) * w_ref[j:j + 1, :]
        acc = acc + term
        if off < 0:
            head_fix = head_fix + jnp.where(t < -off, term[:SUBLANE, :], 0.0)
        else:
            tail_fix = tail_fix + jnp.where(t >= SUBLANE - off, term[n - SUBLANE:, :], 0.0)
    return jnp.concatenate([acc[:SUBLANE, :] - head_fix, acc[SUBLANE:n - SUBLANE, :],
                            acc[n - SUBLANE:, :] - tail_fix], axis=0)


def _dn_prep_kernel(p_ref, w_ref, rm_ref, tr_ref, *, n_tiles, heads):
    first = pl.program_id(1) * heads
    y_all = _silu(_dwconv_rows(p_ref[0].astype(F32), w_ref, 4))
    q_scale = jnp.where(first < DN_HEADS, DN_HEAD_DIM ** -0.5, 1.0)
    for h in range(heads):
        y = y_all[:, h * LANE:(h + 1) * LANE]
        unit = lax.rsqrt(jnp.sum(y * y, axis=-1, keepdims=True) + 1e-6) * q_scale
        y = y * jnp.where(first < 2 * DN_HEADS, unit, 1.0)
        rm_ref[0, :, h * LANE:(h + 1) * LANE] = y.astype(BF16)
        for t in range(n_tiles):
            tr_ref[0, h, t] = y[t * TILE:(t + 1) * TILE, :].T.astype(BF16)


def _dn_prep(proj, conv_w):
    bsz, n, _ = proj.shape
    nt = n // TILE
    nc = 3 * DN_HEADS
    heads = DN_HEADS if n <= DN_PREP_WIDE_ROWS else 1
    wide = heads * LANE
    return pl.pallas_call(
        functools.partial(_dn_prep_kernel, n_tiles=nt, heads=heads),
        grid=(bsz, nc // heads),
        in_specs=[pl.BlockSpec((1, n, wide), lambda b, c: (b, 0, QKV_BLK // heads + c)),
                  pl.BlockSpec((4, wide), lambda b, c: (0, c))],
        out_specs=[pl.BlockSpec((1, n, wide), lambda b, c: (b, 0, c)),
                   pl.BlockSpec((1, heads, nt, LANE, TILE), lambda b, c: (b, c, 0, 0, 0))],
        out_shape=[jax.ShapeDtypeStruct((bsz, n, nc * LANE), BF16),
                   jax.ShapeDtypeStruct((bsz, nc, nt, LANE, TILE), BF16)],
        compiler_params=_cparams(("parallel", "parallel")),
        name="dn_prep",
    )(proj, conv_w)


def _dn_gate_kernel(ab_ref, alog_ref, dtb_ref, isdec_ref, o_ref):
    x = ab_ref[0]
    dec = -jnp.exp(alog_ref[...]) * _softplus(x + dtb_ref[...])
    e = jnp.where(isdec_ref[...] > 0.5, dec, _sigmoid(x))
    et = e.T
    s = _row_iota((TILE, TILE))
    t = _col_iota((TILE, TILE))
    same = _div_pow2(s, CHUNK) == _div_pow2(t, CHUNK)
    hi = et.astype(BF16)
    rest = et - hi.astype(F32)
    mid = rest.astype(BF16)
    lo = (rest - mid.astype(F32)).astype(BF16)

    def summed(ones):
        m = jnp.where(ones, 1.0, 0.0).astype(BF16)
        return _bdot(hi, m) + _bdot(mid, m) + _bdot(lo, m)

    pre = summed(same & (s <= t))
    suf = summed(same & (s >= t))
    tot = summed(same)
    slot = _mod_pow2(_row_iota((LANE, TILE)), SUBLANE)
    o_ref[0, 0] = jnp.where(slot == 0, pre, jnp.where(slot == 2, suf, jnp.where(slot >= 4, tot, et)))


def _dn_gates(ab, alog_c, dtb_c, isdec_c):
    bsz, n, _ = ab.shape
    nt = n // TILE
    vec = pl.BlockSpec((1, LANE), lambda b, i: (0, 0))
    return pl.pallas_call(
        _dn_gate_kernel,
        grid=(bsz, nt),
        in_specs=[pl.BlockSpec((1, TILE, LANE), lambda b, i: (b, i, 0)), vec, vec, vec],
        out_specs=pl.BlockSpec((1, 1, LANE, TILE), lambda b, i: (b, i, 0, 0)),
        out_shape=jax.ShapeDtypeStruct((bsz, nt, LANE, TILE), F32),
        compiler_params=_cparams(("parallel", "parallel")),
        name="dn_gates",
    )(ab, alog_c, dtb_c, isdec_c)


def _dn_masks(mask_ref):
    a = _row_iota((TILE, TILE))
    b = _col_iota((TILE, TILE))
    apart = a ^ b
    n_levels = CHUNK.bit_length() - 2
    for lg in range(1, n_levels + 1):
        mask_ref[lg - 1] = jnp.where((apart >> lg) == 1, 1.0, 0.0).astype(BF16)
    for k, pack in enumerate((DN_PACK, 2 * DN_PACK)):
        mask_ref[n_levels + k] = jnp.where(_div_pow2(a, pack) == _div_pow2(b, pack), 1.0, 0.0).astype(BF16)


def _dn_tiles(chains, mask_ref):
    a = _row_iota((TILE, TILE))
    b = _col_iota((TILE, TILE))
    same = _div_pow2(a, CHUNK) == _div_pow2(b, CHUNK)
    apart = a ^ b
    n_chunks = TILE // CHUNK
    dot = functools.partial(jnp.dot, preferred_element_type=F32)

    def stack(v, reps):
        return jnp.concatenate([v] * reps, axis=0)

    kk = [dot(c["k_rm"], c["kt"]) for c in chains]
    kq = [dot(c["k_rm"], c["qt"]) for c in chains]
    xs, pw, attn = [], [], []
    for c, kk_c, kq_c in zip(chains, kk, kq):
        incl = same & ((a >= b) if c["backward"] else (a <= b))
        gcb = jnp.broadcast_to(c["gc"], (TILE, TILE))
        diff = gcb - gcb.T
        decay = jnp.where(incl, jnp.exp(jnp.where(incl, diff, 0.0)), 0.0)
        attn.append((kq_c * decay).astype(BF16))
        x = kk_c * decay * (-c["beta"])
        xs.append(x.astype(BF16))
        base = jnp.where(a == b, 1.0, jnp.where(apart == 1, x, 0.0))
        acc = base[0:DN_PACK, :]
        for r in range(1, TILE // DN_PACK):
            acc = acc + base[r * DN_PACK:(r + 1) * DN_PACK, :]
        pw.append(acc)
    pack = DN_PACK
    n_levels = CHUNK.bit_length() - 2
    block_mask = {DN_PACK: n_levels, 2 * DN_PACK: n_levels + 1}
    s = 2
    while s < CHUNK:
        if s == pack:
            keep = _div_pow2(_row_iota((2 * pack, TILE)), pack) == (_div_pow2(_col_iota((2 * pack, TILE)), pack) & 1)
            pw = [jnp.where(keep, stack(p, 2), 0.0) for p in pw]
            pack *= 2
        couple = mask_ref[s.bit_length() - 2]
        blocks = mask_ref[block_mask[pack]]
        pb = [p.astype(BF16) for p in pw]
        px = [dot(pb_c, x * couple) for pb_c, x in zip(pb, xs)]
        p_bd = [stack(pb_c, TILE // pack) * blocks for pb_c in pb]
        pw = [p + dot(px_c.astype(BF16), bd_c) for p, px_c, bd_c in zip(pw, px, p_bd)]
        s *= 2
    blocks = mask_ref[block_mask[pack]]
    t_inv = [stack(p.astype(BF16), TILE // pack) * blocks for p in pw]
    egc = [jnp.exp(c["gc"]) for c in chains]
    u_t = [dot((c["vt"].astype(F32) * c["beta"]).astype(BF16), t) for c, t in zip(chains, t_inv)]
    w_t = [dot((c["kt"].astype(F32) * (c["beta"] * e)).astype(BF16), t).astype(BF16)
           for c, e, t in zip(chains, egc, t_inv)]
    qd_t = [(c["qt"].astype(F32) * e).astype(BF16) for c, e in zip(chains, egc)]
    kdec = [jnp.exp(c["tot"] - c["gc"]) for c in chains]
    outs = [[None] * n_chunks for _ in chains]
    for step in range(n_chunks):
        cis = [(n_chunks - 1 - step) if c["backward"] else step for c in chains]
        sl = [slice(ci * CHUNK, (ci + 1) * CHUNK) for ci in cis]
        st = [c["st_ref"][...] for c in chains]
        stb = [s_c.astype(BF16) for s_c in st]
        swq = [dot(stb_c, jnp.concatenate([w_c[:, r], q_c[:, r]], axis=1))
               for stb_c, w_c, q_c, r in zip(stb, w_t, qd_t, sl)]
        sq = [v[:, CHUNK:] for v in swq]
        vn = [u_c[:, r] - v[:, :CHUNK] for u_c, v, r in zip(u_t, swq, sl)]
        av = [dot(vn_c.astype(BF16), at_c[r, r]) for vn_c, at_c, r in zip(vn, attn, sl)]
        upd = [dot((vn_c * kd_c[:, r]).astype(BF16), c["k_rm"][r, :])
               for vn_c, kd_c, c, r in zip(vn, kdec, chains, sl)]
        for idx, c in enumerate(chains):
            c["st_ref"][...] = st[idx] * jnp.exp(c["tot"][:, sl[idx]]) + upd[idx]
            outs[idx][cis[idx]] = sq[idx] + av[idx]
    return [jnp.concatenate(o, axis=1) for o in outs]


def _dn_kernel(k_ref, qt_ref, kt_ref, vt_ref, g_ref, s0f_ref, s0b_ref,
               o_ref, sf_ref, sb_ref, ot_ref, st_ref, mask_ref, *, n_tiles, with_output):
    hb = DN_HEADS_PER_STEP
    _dn_masks(mask_ref)
    for hh in range(hb):
        st_ref[2 * hh] = s0f_ref[0, hh]
        st_ref[2 * hh + 1] = s0b_ref[0, hh]

    def body(i, carry):
        nf = i
        nb = n_tiles - 1 - i
        chains = []
        for hh in range(hb):
            for backward, n in ((False, nf), (True, nb)):
                g = g_ref[0, n, hh * SUBLANE:(hh + 1) * SUBLANE, :]
                base = 2 if backward else 0
                chains.append(dict(
                    k_rm=k_ref[0, pl.ds(pl.multiple_of(n * TILE, TILE), TILE), hh * LANE:(hh + 1) * LANE],
                    qt=qt_ref[0, hh, n], kt=kt_ref[0, hh, n], vt=vt_ref[0, hh, n],
                    gc=g[base:base + 1, :], beta=g[base + 1:base + 2, :], tot=g[4 + base // 2:5 + base // 2, :],
                    st_ref=st_ref.at[2 * hh + (1 if backward else 0)], backward=backward, hh=hh, n=n))
        o_t = _dn_tiles(chains, mask_ref)
        if with_output:
            for c, o_c in zip(chains, o_t):
                ot_ref[c["hh"], c["n"]] = ot_ref[c["hh"], c["n"]] + o_c
        return carry

    if with_output:
        ot_ref[...] = jnp.zeros_like(ot_ref)
    lax.fori_loop(0, n_tiles, body, 0)
    for hh in range(hb):
        sf_ref[0, hh] = st_ref[2 * hh]
        sb_ref[0, hh] = st_ref[2 * hh + 1]
    if with_output:
        def finish(t, carry):
            rows = pl.ds(pl.multiple_of(t * TILE, TILE), TILE)
            for hh in range(hb):
                o_ref[0, rows, hh * LANE:(hh + 1) * LANE] = ot_ref[hh, t].T.astype(BF16)
            return carry

        lax.fori_loop(0, n_tiles, finish, 0)
    else:
        o_ref[...] = jnp.zeros_like(o_ref)


def _delta_net(qkv_rm, qkv_tr, gates, s0f, s0b, with_output):
    bsz, n, _ = qkv_rm.shape
    nt = n // TILE
    h = DN_HEADS
    hb = DN_HEADS_PER_STEP
    wide = hb * LANE
    once = pl.Buffered(1)
    tr_spec = lambda off: pl.BlockSpec((1, hb, nt, LANE, TILE), lambda b, j: (b, off + j, 0, 0, 0))
    st_spec = pl.BlockSpec((1, hb, LANE, LANE), lambda b, j: (b, j, 0, 0))
    n_out = n if with_output else SUBLANE
    return pl.pallas_call(
        functools.partial(_dn_kernel, n_tiles=nt, with_output=with_output),
        grid=(bsz, h // hb),
        in_specs=[pl.BlockSpec((1, n, wide), lambda b, j: (b, 0, h // hb + j), once),
                  tr_spec(0), tr_spec(h // hb), tr_spec(2 * h // hb),
                  pl.BlockSpec((1, nt, hb * SUBLANE, TILE), lambda b, j: (b, 0, j, 0)),
                  st_spec, st_spec],
        out_specs=[pl.BlockSpec((1, n_out, wide), lambda b, j: (b, 0, j)), st_spec, st_spec],
        out_shape=[jax.ShapeDtypeStruct((bsz, n_out, DN_WIDTH), BF16),
                   jax.ShapeDtypeStruct((bsz, h, LANE, LANE), F32),
                   jax.ShapeDtypeStruct((bsz, h, LANE, LANE), F32)],
        scratch_shapes=[pltpu.VMEM((hb, nt, LANE, TILE), F32),
                        pltpu.VMEM((2 * hb, LANE, LANE), F32),
                        pltpu.VMEM((CHUNK.bit_length(), TILE, TILE), BF16)],
        compiler_params=_cparams(("parallel", "parallel"), 58),
        name="delta_net",
    )(qkv_rm, qkv_tr, qkv_tr, qkv_tr, gates, s0f, s0b)


def _lru_scan_block(x, wa, ba, wx, bx, spl, h_in, backward):
    rows, width = x.shape
    xb = x.astype(BF16)

    def gate(ws, bias):
        parts = [jnp.dot(xb[:, k * LRU_BLOCK:(k + 1) * LRU_BLOCK], w, preferred_element_type=F32)
                 for k, w in enumerate(ws)]
        return _sigmoid(jnp.concatenate(parts, axis=1) + bias)

    r = gate(wa, ba)
    gi = gate(wx, bx)
    log_a = -LRU_C * r * spl
    a = jnp.exp(log_a)
    b = jnp.sqrt(1.0 - jnp.exp(2.0 * log_a)) * (gi * x)
    groups = rows // SUBLANE
    a = a.reshape(groups, SUBLANE, width)
    b = b.reshape(groups, SUBLANE, width)
    sub = lax.broadcasted_iota(jnp.int32, a.shape, 1)
    s = 1
    while s < SUBLANE:
        if backward:
            keep = sub < SUBLANE - s
            a_sh = jnp.where(keep, pltpu.roll(a, SUBLANE - s, 1), 1.0)
            b_sh = jnp.where(keep, pltpu.roll(b, SUBLANE - s, 1), 0.0)
        else:
            keep = sub >= s
            a_sh = jnp.where(keep, pltpu.roll(a, s, 1), 1.0)
            b_sh = jnp.where(keep, pltpu.roll(b, s, 1), 0.0)
        b = a * b_sh + b
        a = a * a_sh
        s *= 2
    a = a.reshape(rows, width)
    b = b.reshape(rows, width)
    pieces = [None] * groups
    carry = h_in
    order = range(groups - 1, -1, -1) if backward else range(groups)
    edge = 0 if backward else SUBLANE - 1
    for gidx in order:
        lo = gidx * SUBLANE
        hgrp = b[lo:lo + SUBLANE, :] + a[lo:lo + SUBLANE, :] * carry
        pieces[gidx] = hgrp
        carry = hgrp[edge:edge + 1, :]
    return jnp.concatenate(pieces, axis=0), carry


def _lru_kernel(px_ref, py_ref, cw_ref, cb_ref, wa_ref, ba_ref, wx_ref, bx_ref, lam_ref, h0_ref,
                o_ref, last_ref, xs_ref, hs_ref, *, n_blocks, with_output):
    x = px_ref[0].astype(F32)
    xs_ref[...] = _dwconv_rows(x, cw_ref, 4) + cb_ref[...]
    spl = _softplus(-lam_ref[...])
    wa = [[wa_ref[d, k] for k in range(LRU_GROUP)] for d in range(2)]
    wx = [[wx_ref[d, k] for k in range(LRU_GROUP)] for d in range(2)]

    def body(i, carry):
        hf, hb = carry
        rf = pl.multiple_of(i * LRU_ROWS, LRU_ROWS)
        rb = pl.multiple_of((n_blocks - 1 - i) * LRU_ROWS, LRU_ROWS)
        h_f, hf = _lru_scan_block(xs_ref[pl.ds(rf, LRU_ROWS), :], wa[0], ba_ref[0:1, :],
                                  wx[0], bx_ref[0:1, :], spl[0:1, :], hf, False)
        h_b, hb = _lru_scan_block(xs_ref[pl.ds(rb, LRU_ROWS), :], wa[1], ba_ref[1:2, :],
                                  wx[1], bx_ref[1:2, :], spl[1:2, :], hb, True)
        if with_output:
            hs_ref[pl.ds(rf, LRU_ROWS), :] = hs_ref[pl.ds(rf, LRU_ROWS), :] + h_f
            hs_ref[pl.ds(rb, LRU_ROWS), :] = hs_ref[pl.ds(rb, LRU_ROWS), :] + h_b
        return hf, hb

    if with_output:
        hs_ref[...] = jnp.zeros_like(hs_ref)
    h0 = h0_ref[0]
    hf, hb = lax.fori_loop(0, n_blocks, body, (h0[0:1, :], h0[1:2, :]))
    last_ref[0] = jnp.concatenate([hf, hb], axis=0)
    if with_output:
        y = py_ref[0].astype(F32)
        gelu = 0.5 * y * (1.0 + jnp.tanh(math.sqrt(2.0 / math.pi) * (y + 0.044715 * (y * y * y))))
        o_ref[0] = (hs_ref[...] * gelu).astype(BF16)
    else:
        o_ref[...] = jnp.zeros_like(o_ref)


def _rglru(proj, conv_w, conv_b, w_a, b_a, w_x, b_x, lam, h0, with_output):
    bsz, n, _ = proj.shape
    rows = min(n, LRU_ROWS)
    nb = n // rows
    n_out = n if with_output else SUBLANE
    width = LRU_GROUP * LRU_BLOCK
    lx = LX_BLK * LANE // width
    ly = LY_BLK * LANE // width
    vec2 = pl.BlockSpec((2, width), lambda b, j: (0, j))
    wspec = pl.BlockSpec((2, LRU_GROUP, LRU_BLOCK, LRU_BLOCK), lambda b, j: (0, j, 0, 0))
    return pl.pallas_call(
        functools.partial(_lru_kernel, n_blocks=nb, with_output=with_output),
        grid=(bsz, LRU_BLOCKS // LRU_GROUP),
        in_specs=[pl.BlockSpec((1, n, width), lambda b, j: (b, 0, lx + j)),
                  pl.BlockSpec((1, n, width), lambda b, j: (b, 0, ly + j)),
                  pl.BlockSpec((4, width), lambda b, j: (0, j)),
                  pl.BlockSpec((1, width), lambda b, j: (0, j)),
                  wspec, vec2, wspec, vec2, vec2,
                  pl.BlockSpec((1, 2, width), lambda b, j: (b, 0, j))],
        out_specs=[pl.BlockSpec((1, n_out, width), lambda b, j: (b, 0, j)),
                   pl.BlockSpec((1, 2, width), lambda b, j: (b, 0, j))],
        out_shape=[jax.ShapeDtypeStruct((bsz, n_out, LRU_WIDTH), BF16),
                   jax.ShapeDtypeStruct((bsz, 2, LRU_WIDTH), F32)],
        scratch_shapes=[pltpu.VMEM((n, width), F32), pltpu.VMEM((n, width), F32)],
        compiler_params=_cparams(("parallel", "parallel")),
        name="rglru",
    )(proj, proj, conv_w, conv_b.reshape(1, LRU_WIDTH), w_a, b_a, w_x, b_x, lam, h0)


def _hy_prep_kernel(p0_ref, p1_ref, pv_ref, w0_ref, w1_ref, wv_ref, b0_ref, b1_ref, bv_ref, x0_ref, zz_ref):
    x0 = _dwconv_rows(p0_ref[0].astype(F32), w0_ref, 3) + b0_ref[...]
    x1 = _dwconv_rows(p1_ref[0].astype(F32), w1_ref, 3) + b1_ref[...]
    v = _dwconv_rows(pv_ref[0].astype(F32), wv_ref, 3) + bv_ref[...]
    x0_ref[0] = x0.astype(BF16)
    zz_ref[0] = (x1 * v).astype(BF16)


def _hy_prep(proj, conv_w, conv_b):
    bsz, n, _ = proj.shape
    nblk = HY_WIDTH // LANE
    pspec = lambda off: pl.BlockSpec((1, n, LANE), lambda b, j: (b, 0, HY_BLK + off + j))
    wspec = lambda off: pl.BlockSpec((3, LANE), lambda b, j: (0, off + j))
    bspec = lambda off: pl.BlockSpec((1, LANE), lambda b, j: (0, off + j))
    ospec = pl.BlockSpec((1, n, LANE), lambda b, j: (b, 0, j))
    cb = conv_b.reshape(1, 3 * HY_WIDTH)
    return pl.pallas_call(
        _hy_prep_kernel,
        grid=(bsz, nblk),
        in_specs=[pspec(0), pspec(nblk), pspec(2 * nblk), wspec(0), wspec(nblk), wspec(2 * nblk),
                  bspec(0), bspec(nblk), bspec(2 * nblk)],
        out_specs=[ospec, ospec],
        out_shape=[jax.ShapeDtypeStruct((bsz, n, HY_WIDTH), BF16)] * 2,
        compiler_params=_cparams(("parallel", "parallel")),
        name="hy_prep",
    )(proj, proj, proj, conv_w, conv_w, conv_w, cb, cb, cb)


def _hy_filter_kernel(w1_ref, b1_ref, f1_ref, w2_ref, b2_ref, f2_ref, w3_ref, band_ref, delta_ref,
                      k_ref, *, n, rows):
    def lag(shape):
        p = _row_iota(shape) + (pl.program_id(0) * rows - n)
        return p, jnp.abs(p).astype(F32)

    _, i = lag((rows, LANE))
    lane = _col_iota((rows, LANE))
    t = i * (1.0 / (n - 1))
    ang = band_ref[...] * (i * (2.0 * math.pi / n))
    feat = jnp.where(lane == 0, t,
                     jnp.where(lane <= HY_BANDS, jnp.cos(ang), jnp.where(lane < HY_EMB, -jnp.sin(ang), 0.0)))
    hid = jnp.sin(f1_ref[...] * (jnp.dot(feat, w1_ref[...], precision=HIGHEST,
                                         preferred_element_type=F32) + b1_ref[...]))
    hid = jnp.sin(f2_ref[...] * (jnp.dot(hid, w2_ref[...], precision=HIGHEST,
                                         preferred_element_type=F32) + b2_ref[...]))
    filt = _bdot(hid, w3_ref[...])
    p, iw = lag((rows, HY_WIDTH))
    dec = jnp.exp(-(iw * (1.0 / (n - 1))) * delta_ref[...])
    h_f = filt[:, :HY_WIDTH] * dec
    h_b = filt[:, HY_WIDTH:] * dec
    taps = jnp.where(p > 0, h_f, jnp.where(p < 0, h_b, h_f + h_b))
    k_ref[...] = jnp.where(p == -n, 0.0, taps).astype(BF16)


def _hy_filter(n, w1, b1, f1, w2, b2, f2, w3):
    rows = min(n, 256)
    hid = HY_FILTER_HIDDEN
    w1p = jnp.zeros((LANE, hid), F32).at[:HY_EMB].set(w1)
    bands = np.zeros((1, LANE), np.float32)
    base = np.linspace(1e-4, HY_BANDS - 1, HY_BANDS, dtype=np.float32)
    bands[0, 1:1 + HY_BANDS] = base
    bands[0, 1 + HY_BANDS:HY_EMB] = base
    log_target = math.log(HY_DECAY_TARGET)
    deltas = np.abs(np.linspace(log_target / HY_SLOW_DECAY_PCT, log_target / HY_FAST_DECAY_PCT, HY_WIDTH,
                                dtype=np.float32)).reshape(1, HY_WIDTH)
    full = lambda shape: pl.BlockSpec(shape, lambda i: (0,) * len(shape))
    return pl.pallas_call(
        functools.partial(_hy_filter_kernel, n=n, rows=rows),
        grid=(2 * n // rows,),
        in_specs=[full((LANE, hid)), full((1, hid)), full((1, hid)), full((hid, hid)), full((1, hid)),
                  full((1, hid)), full((hid, 2 * HY_WIDTH)), full((1, LANE)), full((1, HY_WIDTH))],
        out_specs=pl.BlockSpec((rows, HY_WIDTH), lambda i: (i, 0)),
        out_shape=jax.ShapeDtypeStruct((2 * n, HY_WIDTH), BF16),
        compiler_params=_cparams(("parallel",)),
        name="hy_filter",
    )(w1p, b1.reshape(1, hid), f1.reshape(1, hid), w2, b2.reshape(1, hid), f2.reshape(1, hid), w3,
      jnp.asarray(bands), jnp.asarray(deltas))


DFT_GROUP = 64


def _dft_table_kernel(c_ref, s_ref, cb_ref, sb_ref, *, m, ncols, col0, rows, transposed, blank_first):
    period = 4 * m
    scale = 2.0 * math.pi / period
    col = _col_iota((1, ncols)) + col0

    @pl.when(pl.program_id(0) == 0)
    def _():
        r2 = _row_iota((DFT_GROUP, ncols))
        c2 = _col_iota((DFT_GROUP, ncols)) + col0
        ph = (r2 * (2 * c2 + 1)) if transposed else ((2 * r2 + 1) * c2)
        ang = (ph & (period - 1)).astype(F32) * scale
        cb_ref[...] = jnp.cos(ang)
        sb_ref[...] = jnp.sin(ang)

    for g in range(rows // DFT_GROUP):
        r1 = pl.program_id(0) * (rows // DFT_GROUP) + g
        ph = (DFT_GROUP * r1) * (2 * col + 1) if transposed else (2 * DFT_GROUP * r1) * col
        ang = (ph & (period - 1)).astype(F32) * scale
        ca = jnp.cos(ang)
        sa = jnp.sin(ang)
        cb = cb_ref[...]
        sb = sb_ref[...]
        c_tile = ca * cb - sa * sb
        s_tile = sa * cb + ca * sb
        if blank_first:
            first = _col_iota((DFT_GROUP, ncols)) == 0
            c_tile = jnp.where(first, 0.0, c_tile)
            s_tile = jnp.where(first, 0.0, s_tile)
        c_ref[g * DFT_GROUP:(g + 1) * DFT_GROUP, :] = c_tile.astype(BF16)
        s_ref[g * DFT_GROUP:(g + 1) * DFT_GROUP, :] = s_tile.astype(BF16)


def _dft_tables(m, transposed=False, two_sided=False):
    rows = min(m, 256)
    ncols = 2 * m if two_sided else m
    spec = pl.BlockSpec((rows, ncols), lambda i: (i, 0))
    return pl.pallas_call(
        functools.partial(_dft_table_kernel, m=m, ncols=ncols, col0=3 * m if two_sided else 0, rows=rows,
                          transposed=transposed, blank_first=two_sided),
        grid=(m // rows,),
        in_specs=[],
        out_specs=[spec, spec],
        out_shape=[jax.ShapeDtypeStruct((m, ncols), BF16)] * 2,
        scratch_shapes=[pltpu.VMEM((DFT_GROUP, ncols), F32), pltpu.VMEM((DFT_GROUP, ncols), F32)],
        compiler_params=_cparams(("arbitrary",)),
        name="dft_tables",
    )()


def _hy_tables(m):
    return _dft_tables(m) + _dft_tables(m, transposed=True) + _dft_tables(m, two_sided=True)


def _hy_ktrans_kernel(cw_ref, sw_ref, lo_ref, hi_ref, kc_ref, ks_ref, *, m):
    dot = functools.partial(jnp.dot, preferred_element_type=F32)
    lo = lo_ref[...]
    hi = hi_ref[...]
    kc_ref[0] = dot(cw_ref[:, :m], lo) + dot(cw_ref[:, m:], hi)
    ks_ref[0] = dot(sw_ref[:, :m], lo) + dot(sw_ref[:, m:], hi)


def _hy_ktrans(cw, sw, taps, m):
    nd = taps.shape[0] // m - 1
    w = taps.shape[1]
    tn = 512
    tab = pl.BlockSpec((m, 2 * m), lambda e, j: (0, 0))
    ospec = pl.BlockSpec((1, m, tn), lambda e, j: (e, 0, j))
    return pl.pallas_call(
        functools.partial(_hy_ktrans_kernel, m=m),
        grid=(nd, w // tn),
        in_specs=[tab, tab, pl.BlockSpec((m, tn), lambda e, j: (e, j)), pl.BlockSpec((m, tn), lambda e, j: (e + 1, j))],
        out_specs=[ospec, ospec],
        out_shape=[jax.ShapeDtypeStruct((nd, m, w), F32)] * 2,
        compiler_params=_cparams(("parallel", "parallel")),
        name="hy_ktrans",
    )(cw, sw, taps, taps)


def _hy_fwd_kernel(c_ref, s_ref, zz_ref, kc_ref, ks_ref, a_ref, b_ref, *, m, nb):
    dot = functools.partial(jnp.dot, preferred_element_type=F32)
    c = c_ref[...]
    s = s_ref[...]
    acc_a = [None] * nb
    acc_b = [None] * nb
    for j in range(nb):
        zz = zz_ref[0, j * m:(j + 1) * m, :]
        uc = dot(c, zz)
        us = dot(s, zz)
        for i in range(nb):
            kc = kc_ref[i - j + nb - 1]
            ks = ks_ref[i - j + nb - 1]
            ta = uc * kc - us * ks
            tb = uc * ks + us * kc
            acc_a[i] = ta if acc_a[i] is None else acc_a[i] + ta
            acc_b[i] = tb if acc_b[i] is None else acc_b[i] + tb
    for i in range(nb):
        a_ref[0, i] = acc_a[i].astype(BF16)
        b_ref[0, i] = acc_b[i].astype(BF16)


def _hy_forward(ctab, stab, zz, kc, ks, m):
    bsz, n, w = zz.shape
    nb = n // m
    nd = 2 * nb - 1
    tf = min(m, 256)
    tn = 512
    tab = pl.BlockSpec((tf, m), lambda b, j, i: (i, 0))
    kspec = pl.BlockSpec((nd, tf, tn), lambda b, j, i: (0, i, j))
    ospec = pl.BlockSpec((1, nb, tf, tn), lambda b, j, i: (b, 0, i, j))
    return pl.pallas_call(
        functools.partial(_hy_fwd_kernel, m=m, nb=nb),
        grid=(bsz, w // tn, m // tf),
        in_specs=[tab, tab, pl.BlockSpec((1, n, tn), lambda b, j, i: (b, 0, j)), kspec, kspec],
        out_specs=[ospec, ospec],
        out_shape=[jax.ShapeDtypeStruct((bsz, nb, m, w), BF16)] * 2,
        compiler_params=_cparams(("parallel", "parallel", "parallel")),
        name="hy_forward",
    )(ctab, stab, zz, kc, ks)


def _hy_inv_kernel(ct_ref, st_ref, a_ref, b_ref, x0_ref, zz_ref, bias_ref, o_ref, *, m):
    y = (jnp.dot(ct_ref[...], a_ref[0, 0], preferred_element_type=F32)
         + jnp.dot(st_ref[...], b_ref[0, 0], preferred_element_type=F32)) * (1.0 / m)
    zz = zz_ref[0].astype(F32)
    o_ref[0] = (x0_ref[0].astype(F32) * (y + zz * bias_ref[...])).astype(BF16)


def _hy_inverse(cttab, sttab, a, bq, x0, zz, bias, m):
    bsz, n, w = zz.shape
    nb = n // m
    tn = 512
    tab = pl.BlockSpec((m, m), lambda b, i, j: (0, 0))
    spec4 = pl.BlockSpec((1, 1, m, tn), lambda b, i, j: (b, i, 0, j))
    tile = pl.BlockSpec((1, m, tn), lambda b, i, j: (b, i, j))
    return pl.pallas_call(
        functools.partial(_hy_inv_kernel, m=m),
        grid=(bsz, nb, w // tn),
        in_specs=[tab, tab, spec4, spec4, tile, tile, pl.BlockSpec((1, tn), lambda b, i, j: (0, j))],
        out_specs=tile,
        out_shape=jax.ShapeDtypeStruct((bsz, n, w), BF16),
        compiler_params=_cparams(("parallel", "parallel", "parallel")),
        name="hy_inverse",
    )(cttab, sttab, a, bq, x0, zz, bias.reshape(1, w))


def _merge_kernel(odn_ref, z_ref, ng_ref, ohy_ref, olru_ref, gdn_ref, ghy_ref, glru_ref, wdn_ref, why_ref,
                  wlru_ref, wout_ref, x_ref, gate_ref, o_ref):
    heads = []
    for h in range(DN_HEADS):
        o = odn_ref[0, :, h * LANE:(h + 1) * LANE].astype(F32)
        y = o * lax.rsqrt(jnp.mean(o * o, axis=-1, keepdims=True) + NORM_EPS) * ng_ref[...]
        heads.append((y * _silu(z_ref[0, :, h * LANE:(h + 1) * LANE].astype(F32))).astype(BF16))
    odn = jnp.concatenate(heads, axis=1)
    m = _sigmoid(gdn_ref[0].astype(F32)) * jnp.dot(odn, wdn_ref[...], preferred_element_type=F32)
    m = m + _sigmoid(ghy_ref[0].astype(F32)) * jnp.dot(ohy_ref[0], why_ref[...], preferred_element_type=F32)
    m = m + _sigmoid(glru_ref[0].astype(F32)) * jnp.dot(olru_ref[0], wlru_ref[...], preferred_element_type=F32)
    y = jnp.dot(m.astype(BF16), wout_ref[...], preferred_element_type=F32)
    o_ref[0] = x_ref[0] + gate_ref[0] * y


def _merge(o_dn, dn_norm_g, o_hy, o_lru, proj, w_dn, w_hy, w_lru, w_out, x, gate):
    bsz, n, d = x.shape
    tm = min(n, 256)
    nblk = d // 1024
    act = pl.BlockSpec((1, tm, d), lambda b, i: (b, i, 0))
    gspec = lambda k: pl.BlockSpec((1, tm, d), lambda b, i: (b, i, GATE_BLK * LANE // d + k * nblk))
    wspec = pl.BlockSpec((d, d), lambda b, i: (0, 0))
    return pl.pallas_call(
        _merge_kernel,
        grid=(bsz, n // tm),
        in_specs=[act, pl.BlockSpec((1, tm, d), lambda b, i: (b, i, Z_BLK * LANE // d)),
                  pl.BlockSpec((1, LANE), lambda b, i: (0, 0)),
                  act, act, gspec(0), gspec(1), gspec(2), wspec, wspec, wspec, wspec, act,
                  pl.BlockSpec((1, 1, d), lambda b, i: (b, 0, 0))],
        out_specs=act,
        out_shape=jax.ShapeDtypeStruct((bsz, n, d), F32),
        compiler_params=_cparams(("parallel", "parallel")),
        name="merge",
    )(o_dn, proj, dn_norm_g.reshape(1, LANE), o_hy, o_lru, proj, proj, proj, w_dn, w_hy, w_lru, w_out, x, gate)


def _ffn_act_kernel(ug_ref, uv_ref, wg_ref, wv_ref, o_ref, up_ref, mid_ref, dn_ref, *, rows, cols, n):
    tile = FFN_TILE
    a = _row_iota((tile, tile))
    b = _col_iota((tile, tile))
    c = _mod_pow2(a, cols)
    lmat = jnp.where((b == a - 1) & (c >= 1), 1.0, 0.0).astype(BF16)
    rmat = jnp.where((b == a + 1) & (c <= cols - 2), 1.0, 0.0).astype(BF16)
    taps = (-1, 0, 1) if rows > 1 else (0,)
    dst = {-1: up_ref, 0: mid_ref, 1: dn_ref}
    if rows > 1:
        pad = jnp.zeros((2, cols, FFN_WIDE), BF16)
        up_ref[:, 0:cols, :] = pad
        dn_ref[:, n + cols:n + 2 * cols, :] = pad

    def taps_of(i, carry):
        r0 = pl.multiple_of(i * tile, tile)
        for idx, (u_ref, w_ref) in enumerate(((ug_ref, wg_ref), (uv_ref, wv_ref))):
            u = u_ref[0, pl.ds(r0, tile), :]
            left = jnp.dot(lmat, u, preferred_element_type=F32).astype(BF16)
            right = jnp.dot(rmat, u, preferred_element_type=F32).astype(BF16)
            for di in taps:
                k = 3 * (di + 1)
                w = [w_ref[k + j:k + j + 1, :].astype(BF16) for j in range(3)]
                off = pl.multiple_of(r0 + (cols if di != 0 else 0), cols)
                dst[di][idx, pl.ds(off, tile), :] = left * w[0] + u * w[1] + right * w[2]
        return carry

    def combine(i, carry):
        r0 = pl.multiple_of(i * tile, tile)
        vals = []
        for idx in range(2):
            acc = mid_ref[idx, pl.ds(r0, tile), :]
            if rows > 1:
                acc = (acc + up_ref[idx, pl.ds(r0, tile), :]
                       + dn_ref[idx, pl.ds(pl.multiple_of(r0 + 2 * cols, cols), tile), :])
            vals.append(acc.astype(F32))
        o_ref[0, pl.ds(r0, tile), :] = (_silu(vals[0]) * vals[1]).astype(BF16)
        return carry

    lax.fori_loop(0, n // tile, taps_of, 0, unroll=min(2, n // tile))
    lax.fori_loop(0, n // tile, combine, 0)


def _ffn_act(u, conv_w, rows, cols):
    bsz, n, _ = u.shape
    nblk = FFN_HIDDEN // FFN_WIDE
    padded = n + 2 * cols if rows > 1 else SUBLANE * 2
    return pl.pallas_call(
        functools.partial(_ffn_act_kernel, rows=rows, cols=cols, n=n),
        grid=(bsz, nblk),
        in_specs=[pl.BlockSpec((1, n, FFN_WIDE), lambda b, j: (b, 0, j)),
                  pl.BlockSpec((1, n, FFN_WIDE), lambda b, j: (b, 0, nblk + j)),
                  pl.BlockSpec((9, FFN_WIDE), lambda b, j: (0, j)),
                  pl.BlockSpec((9, FFN_WIDE), lambda b, j: (0, nblk + j))],
        out_specs=pl.BlockSpec((1, n, FFN_WIDE), lambda b, j: (b, 0, j)),
        out_shape=jax.ShapeDtypeStruct((bsz, n, FFN_HIDDEN), BF16),
        scratch_shapes=[pltpu.VMEM((2, padded, FFN_WIDE), BF16), pltpu.VMEM((2, n, FFN_WIDE), BF16),
                        pltpu.VMEM((2, padded, FFN_WIDE), BF16)],
        compiler_params=_cparams(("parallel", "parallel")),
        name="ffn_act",
    )(u, u, conv_w, conv_w)


def _gate_column_map():
    src = np.full((LANE,), -1, np.int32)
    isdec = np.zeros((1, LANE), np.float32)
    dirs = np.zeros((LANE,), np.int32)
    for h in range(DN_HEADS):
        for slot, (d, kind) in enumerate(((0, 0), (0, 1), (1, 0), (1, 1), (0, 0), (1, 0))):
            src[h * SUBLANE + slot] = d * 2 * DN_HEADS + kind * DN_HEADS + h
            isdec[0, h * SUBLANE + slot] = 1.0 if kind == 0 else 0.0
            dirs[h * SUBLANE + slot] = d
    return src, isdec, dirs


def _split_in_proj(w_in):
    o = np.cumsum((3 * DN_WIDTH, DN_WIDTH, 4 * DN_HEADS, 3 * HY_WIDTH, LRU_WIDTH, LRU_WIDTH)).tolist()
    w_main = jnp.concatenate([w_in[:, :o[1]], w_in[:, o[2]:]], axis=1).astype(BF16)
    src, _, _ = _gate_column_map()
    w_ab = w_in[:, o[1]:o[2]]
    w_gate = jnp.where(jnp.asarray(src >= 0)[None, :], w_ab[:, np.maximum(src, 0)], 0.0).astype(BF16)
    return w_main, w_gate


def _gate_params(a_log, dt_bias):
    _, isdec, dirs = _gate_column_map()
    head = (np.arange(LANE) // SUBLANE).astype(np.int32)
    alog_c = a_log[dirs, head].reshape(1, LANE)
    dtb_c = dt_bias[dirs, head].reshape(1, LANE)
    return alog_c, dtb_c, jnp.asarray(isdec)


def _token_views(bsz, n, mods, shared_mod):
    if not shared_mod:
        same = lambda t: t
        return same, same, mods
    flat = lambda t: t.reshape(1, bsz * n, t.shape[-1])
    unflat = lambda t: t.reshape(bsz, n, t.shape[-1])
    return flat, unflat, [m[:1] for m in mods]


def _token_mixer(x, mods, lp, states, tables, with_output, shared_mod):
    bsz, n, _ = x.shape
    flat, unflat, tmods = _token_views(bsz, n, mods, shared_mod)
    proj, ab = _norm_mod_matmul(flat(x), lp["norm1_g"], tmods[0], tmods[1], lp["w_main"], BF16, 2048, 1024,
                                lp["w_gate"])
    proj, ab = unflat(proj), unflat(ab)

    qkv_rm, qkv_tr = _dn_prep(proj, lp["dn_conv_w"])
    gates = _dn_gates(ab, *lp["gate_params"])
    o_dn, s_f, s_b = _delta_net(qkv_rm, qkv_tr, gates, states[0], states[1], with_output)
    o_lru, h_last = _rglru(proj, lp["lru_conv_w"], lp["lru_conv_b"], lp["lru_w_a"], lp["lru_b_a"],
                           lp["lru_w_x"], lp["lru_b_x"], lp["lru_lambda"], states[2], with_output)
    new_states = (s_f, s_b, h_last)
    if not with_output:
        return None, new_states

    m = min(n, HY_BLOCK)
    ctab, stab, cttab, sttab, cwtab, swtab = tables
    taps = _hy_filter(n, lp["hy_w1"], lp["hy_b1"], lp["hy_f1"], lp["hy_w2"], lp["hy_b2"], lp["hy_f2"], lp["hy_w3"])
    kc, ks = _hy_ktrans(cwtab, swtab, taps, m)
    x0, zz = _hy_prep(proj, lp["hy_conv_w"], lp["hy_conv_b"])
    a, bq = _hy_forward(ctab, stab, zz, kc, ks, m)
    o_hy = _hy_inverse(cttab, sttab, a, bq, x0, zz, lp["hy_bias"], m)

    x = _merge(flat(o_dn), lp["dn_norm_g"], flat(o_hy), flat(o_lru), flat(proj), lp["w_proj_dn"], lp["w_proj_hy"], lp["w_proj_lru"],
               lp["w_out"], flat(x), tmods[2])
    return unflat(x), new_states


def _conv_ffn(x, mods, lp, rows, cols, shared_mod, final_gain=None):
    bsz, n, _ = x.shape
    flat, unflat, tmods = _token_views(bsz, n, mods, shared_mod)
    u = _norm_mod_matmul(flat(x), lp["norm2_g"], tmods[3], tmods[4], lp["ffn_up"], BF16, 1024, FFN_HIDDEN)
    act = _ffn_act(unflat(u), lp["ffn_conv_w"], rows, cols)
    return unflat(_matmul_residual(flat(act), lp["ffn_down"], flat(x), tmods[5], final_gain))


def kernel(x, c, ctx, c_ctx, w_mod, b_mod, norm1_g, norm2_g, w_in, dn_conv_w, dn_a_log, dn_dt_bias, dn_norm_g,
           hy_conv_w, hy_conv_b, hy_w1, hy_b1, hy_f1, hy_w2, hy_b2, hy_f2, hy_w3, hy_bias,
           lru_conv_w, lru_conv_b, lru_w_a, lru_b_a, lru_w_x, lru_b_x, lru_lambda,
           w_proj_dn, w_proj_hy, w_proj_lru, w_out, ffn_up, ffn_conv_w, ffn_down, final_norm_g):
    bsz, n_lat, d = x.shape
    n_ctx = ctx.shape[1]
    depth = w_in.shape[0]
    rows = n_lat // GRID_W

    cvec = jnp.zeros((SUBLANE, d), F32).at[:bsz].set(c).at[bsz].set(c_ctx)
    lat_tables = _hy_tables(min(n_lat, HY_BLOCK))
    ctx_tables = _hy_tables(min(n_ctx, HY_BLOCK))
    zero_states = (jnp.zeros((bsz, DN_HEADS, LANE, LANE), F32), jnp.zeros((bsz, DN_HEADS, LANE, LANE), F32),
                   jnp.zeros((bsz, 2, LRU_WIDTH), F32))

    xc = ctx
    for l in range(depth):
        ctx_needed = l < depth - 1
        w_main, w_gate = _split_in_proj(w_in[l])
        lp = dict(
            norm1_g=norm1_g[l], norm2_g=norm2_g[l], w_main=w_main, w_gate=w_gate,
            dn_conv_w=dn_conv_w[l], gate_params=_gate_params(dn_a_log[l], dn_dt_bias[l]), dn_norm_g=dn_norm_g[l],
            hy_conv_w=hy_conv_w[l], hy_conv_b=hy_conv_b[l], hy_w1=hy_w1[l], hy_b1=hy_b1[l], hy_f1=hy_f1[l],
            hy_w2=hy_w2[l], hy_b2=hy_b2[l], hy_f2=hy_f2[l], hy_w3=hy_w3[l], hy_bias=hy_bias[l],
            lru_conv_w=lru_conv_w[l], lru_conv_b=lru_conv_b[l], lru_w_a=lru_w_a[l].astype(BF16),
            lru_b_a=lru_b_a[l], lru_w_x=lru_w_x[l].astype(BF16), lru_b_x=lru_b_x[l], lru_lambda=lru_lambda[l],
            w_proj_dn=w_proj_dn[l].astype(BF16), w_proj_hy=w_proj_hy[l].astype(BF16),
            w_proj_lru=w_proj_lru[l].astype(BF16), w_out=w_out[l].astype(BF16),
            ffn_up=ffn_up[l].astype(BF16), ffn_conv_w=ffn_conv_w[l].reshape(9, 2 * FFN_HIDDEN),
            ffn_down=ffn_down[l].astype(BF16))
        mod = _modulation(cvec, w_mod[l].astype(BF16), b_mod[l])
        lat_mod = [mod[:bsz, k * d:(k + 1) * d].reshape(bsz, 1, d) for k in range(N_MOD)]
        ctx_mod = [jnp.broadcast_to(mod[bsz:bsz + 1, k * d:(k + 1) * d].reshape(1, 1, d), (bsz, 1, d))
                   for k in range(N_MOD)]

        xc_new, ctx_states = _token_mixer(xc, ctx_mod, lp, zero_states, ctx_tables, ctx_needed, True)
        x, _ = _token_mixer(x, lat_mod, lp, ctx_states, lat_tables, True, False)
        x = _conv_ffn(x, lat_mod, lp, rows, GRID_W, False, None if ctx_needed else final_norm_g)
        if ctx_needed:
            xc = _conv_ffn(xc_new, ctx_mod, lp, 1, n_ctx, True)
    return x
```

```python
import functools
import math

import numpy as np
import jax
import jax.numpy as jnp
from jax import lax
from jax.experimental import pallas as pl
from jax.experimental.pallas import tpu as pltpu

F32 = jnp.float32
BF16 = jnp.bfloat16
HIGHEST = lax.Precision.HIGHEST

D_MODEL = 1024
DEPTH = 2
GRID_W = 64
NORM_EPS = 1e-6
N_MOD = 6

DN_HEADS = 8
DN_HEAD_DIM = 128
DN_WIDTH = DN_HEADS * DN_HEAD_DIM
HY_WIDTH = 1024
HY_EMB = 33
HY_BANDS = (HY_EMB - 1) // 2
HY_FILTER_HIDDEN = 64
HY_FAST_DECAY_PCT = 0.3
HY_SLOW_DECAY_PCT = 1.5
HY_DECAY_TARGET = 1e-2
LRU_WIDTH = 1024
LRU_BLOCKS = 8
LRU_BLOCK = LRU_WIDTH // LRU_BLOCKS
LRU_C = 8.0
FFN_HIDDEN = 2816

LANE = 128
SUBLANE = 8
TILE = 256
CHUNK = 128
DN_HEADS_PER_STEP = 4
DN_GATE_TILES = 4
DN_PREP_WIDE_ROWS = 512
DN_PACK = 64
HY_BLOCK = 1024
LRU_ROWS = 256
LRU_GROUP = 4
FFN_TILE = 256
FFN_WIDE = 256
MIB = 1024 * 1024

QKV_BLK = 0
Z_BLK = 24
HY_BLK = 32
LX_BLK = 56
LY_BLK = 64
GATE_BLK = 72
N_MAIN = 96 * LANE


def _cparams(sem, vmem_mib=48):
    return pltpu.CompilerParams(dimension_semantics=sem, vmem_limit_bytes=vmem_mib * MIB)


def _sigmoid(x):
    return 1.0 / (1.0 + jnp.exp(-x))


def _silu(x):
    return x * _sigmoid(x)


def _softplus(x):
    return jnp.maximum(x, 0.0) + jnp.log(1.0 + jnp.exp(-jnp.abs(x)))


def _row_iota(shape):
    return lax.broadcasted_iota(jnp.int32, shape, 0)


def _col_iota(shape):
    return lax.broadcasted_iota(jnp.int32, shape, 1)


def _div_pow2(x, k):
    assert k & (k - 1) == 0
    return x >> (k.bit_length() - 1)


def _mod_pow2(x, k):
    assert k & (k - 1) == 0
    return x & (k - 1)


def _bdot(a, b):
    return jnp.dot(a.astype(BF16), b.astype(BF16), preferred_element_type=F32)


def _mod_kernel(c_ref, w_ref, b_ref, o_ref):
    o_ref[...] = _bdot(_silu(c_ref[...]), w_ref[...]) + b_ref[...]


def _modulation(cvec, w_mod, b_mod):
    n = w_mod.shape[1]
    tn = 1024
    return pl.pallas_call(
        _mod_kernel,
        grid=(n // tn,),
        in_specs=[pl.BlockSpec((SUBLANE, D_MODEL), lambda j: (0, 0)),
                  pl.BlockSpec((D_MODEL, tn), lambda j: (0, j)),
                  pl.BlockSpec((1, tn), lambda j: (0, j))],
        out_specs=pl.BlockSpec((SUBLANE, tn), lambda j: (0, j)),
        out_shape=jax.ShapeDtypeStruct((SUBLANE, n), F32),
        compiler_params=_cparams(("parallel",)),
        name="modulation",
    )(cvec, w_mod, b_mod.reshape(1, n))


def _nmm_kernel(x_ref, g_ref, sh_ref, sc_ref, w_ref, *rest, with_side):
    o_ref = rest[1] if with_side else rest[0]
    h_ref = rest[-1]

    @pl.when(pl.program_id(2) == 0)
    def _():
        x = x_ref[0]
        y = x * lax.rsqrt(jnp.mean(x * x, axis=-1, keepdims=True) + NORM_EPS) * g_ref[...]
        h_ref[...] = (y * (1.0 + sc_ref[0]) + sh_ref[0]).astype(BF16)
        if with_side:
            rest[2][0] = jnp.dot(h_ref[...], rest[0][...], preferred_element_type=F32)

    o_ref[0] = jnp.dot(h_ref[...], w_ref[...], preferred_element_type=F32).astype(o_ref.dtype)


def _norm_mod_matmul(x, gain, shift, scale, w, out_dtype, tm, tn, w_side=None):
    bsz, n, d = x.shape
    nout = w.shape[1]
    tm = min(n, tm)
    with_side = w_side is not None
    in_specs = [pl.BlockSpec((1, tm, d), lambda b, i, j: (b, i, 0)),
                pl.BlockSpec((1, d), lambda b, i, j: (0, 0)),
                pl.BlockSpec((1, 1, d), lambda b, i, j: (b, 0, 0)),
                pl.BlockSpec((1, 1, d), lambda b, i, j: (b, 0, 0)),
                pl.BlockSpec((d, tn), lambda b, i, j: (0, j))]
    out_specs = [pl.BlockSpec((1, tm, tn), lambda b, i, j: (b, i, j))]
    out_shape = [jax.ShapeDtypeStruct((bsz, n, nout), out_dtype)]
    args = [x, gain.reshape(1, d), shift, scale, w]
    if with_side:
        ns = w_side.shape[1]
        in_specs.append(pl.BlockSpec((d, ns), lambda b, i, j: (0, 0)))
        out_specs.append(pl.BlockSpec((1, tm, ns), lambda b, i, j: (b, i, 0)))
        out_shape.append(jax.ShapeDtypeStruct((bsz, n, ns), F32))
        args.append(w_side)
    out = pl.pallas_call(
        functools.partial(_nmm_kernel, with_side=with_side),
        grid=(bsz, n // tm, nout // tn),
        in_specs=in_specs,
        out_specs=out_specs,
        out_shape=out_shape,
        scratch_shapes=[pltpu.VMEM((tm, d), BF16)],
        compiler_params=_cparams(("parallel", "parallel", "arbitrary")),
        name="norm_mod_matmul",
    )(*args)
    return out if with_side else out[0]


def _mm_kernel(a_ref, b_ref, o_ref):
    o_ref[...] = jnp.dot(a_ref[...], b_ref[...], preferred_element_type=F32).astype(o_ref.dtype)


def _matmul(a, b, out_dtype, tm, tn):
    m, k = a.shape
    n = b.shape[1]
    return pl.pallas_call(
        _mm_kernel,
        grid=(m // tm, n // tn),
        in_specs=[pl.BlockSpec((tm, k), lambda i, j: (i, 0)),
                  pl.BlockSpec((k, tn), lambda i, j: (0, j))],
        out_specs=pl.BlockSpec((tm, tn), lambda i, j: (i, j)),
        out_shape=jax.ShapeDtypeStruct((m, n), out_dtype),
        compiler_params=_cparams(("parallel", "parallel")),
        name="matmul",
    )(a, b)


def _mm_res_kernel(a_ref, b_ref, x_ref, g_ref, *rest, final_norm):
    o_ref = rest[-1]
    out = x_ref[0] + g_ref[0] * jnp.dot(a_ref[0], b_ref[...], preferred_element_type=F32)
    if final_norm:
        out = out * lax.rsqrt(jnp.mean(out * out, axis=-1, keepdims=True) + NORM_EPS) * rest[0][...]
    o_ref[0] = out


def _matmul_residual(a, w, x, gate, final_gain=None):
    bsz, n, k = a.shape
    d = w.shape[1]
    tm = min(n, 512)
    in_specs = [pl.BlockSpec((1, tm, k), lambda b, i: (b, i, 0)),
                pl.BlockSpec((k, d), lambda b, i: (0, 0)),
                pl.BlockSpec((1, tm, d), lambda b, i: (b, i, 0)),
                pl.BlockSpec((1, 1, d), lambda b, i: (b, 0, 0))]
    args = [a, w, x, gate]
    if final_gain is not None:
        in_specs.append(pl.BlockSpec((1, d), lambda b, i: (0, 0)))
        args.append(final_gain.reshape(1, d))
    return pl.pallas_call(
        functools.partial(_mm_res_kernel, final_norm=final_gain is not None),
        grid=(bsz, n // tm),
        in_specs=in_specs,
        out_specs=pl.BlockSpec((1, tm, d), lambda b, i: (b, i, 0)),
        out_shape=jax.ShapeDtypeStruct((bsz, n, d), F32),
        compiler_params=_cparams(("parallel", "parallel")),
        name="matmul_residual",
    )(*args)


def _dwconv_rows(x, w_ref, k):
    n = x.shape[0]
    left = (k - 1) // 2
    t = _row_iota((SUBLANE, x.shape[1]))
    acc = x * w_ref[left:left + 1, :]
    head_fix = jnp.zeros((SUBLANE, x.shape[1]), F32)
    tail_fix = jnp.zeros((SUBLANE, x.shape[1]), F32)
    for j in range(k):
        off = j - left
        if off == 0:
            continue
        term = pltpu.roll(x, (-off) % n, 0) * w_ref[j:j + 1, :]
        acc = acc + term
        if off < 0:
            head_fix = head_fix + jnp.where(t < -off, term[:SUBLANE, :], 0.0)
        else:
            tail_fix = tail_fix + jnp.where(t >= SUBLANE - off, term[n - SUBLANE:, :], 0.0)
    return jnp.concatenate([acc[:SUBLANE, :] - head_fix, acc[SUBLANE:n - SUBLANE, :],
                            acc[n - SUBLANE:, :] - tail_fix], axis=0)


def _dn_prep_kernel(p_ref, w_ref, rm_ref, tr_ref, *, n_tiles, heads):
    first = pl.program_id(1) * heads
    y_all = _silu(_dwconv_rows(p_ref[0].astype(F32), w_ref, 4))
    q_scale = jnp.where(first < DN_HEADS, DN_HEAD_DIM ** -0.5, 1.0)
    for h in range(heads):
        y = y_all[:, h * LANE:(h + 1) * LANE]
        unit = lax.rsqrt(jnp.sum(y * y, axis=-1, keepdims=True) + 1e-6) * q_scale
        y = y * jnp.where(first < 2 * DN_HEADS, unit, 1.0)
        rm_ref[0, :, h * LANE:(h + 1) * LANE] = y.astype(BF16)
        for t in range(n_tiles):
            tr_ref[0, h, t] = y[t * TILE:(t + 1) * TILE, :].T.astype(BF16)


def _dn_prep(proj, conv_w):
    bsz, n, _ = proj.shape
    nt = n // TILE
    nc = 3 * DN_HEADS
    heads = DN_HEADS if n <= DN_PREP_WIDE_ROWS else 2
    wide = heads * LANE
    return pl.pallas_call(
        functools.partial(_dn_prep_kernel, n_tiles=nt, heads=heads),
        grid=(bsz, nc // heads),
        in_specs=[pl.BlockSpec((1, n, wide), lambda b, c: (b, 0, QKV_BLK // heads + c)),
                  pl.BlockSpec((4, wide), lambda b, c: (0, c))],
        out_specs=[pl.BlockSpec((1, n, wide), lambda b, c: (b, 0, c)),
                   pl.BlockSpec((1, heads, nt, LANE, TILE), lambda b, c: (b, c, 0, 0, 0))],
        out_shape=[jax.ShapeDtypeStruct((bsz, n, nc * LANE), BF16),
                   jax.ShapeDtypeStruct((bsz, nc, nt, LANE, TILE), BF16)],
        compiler_params=_cparams(("parallel", "parallel")),
        name="dn_prep",
    )(proj, conv_w)


def _dn_gate_kernel(ab_ref, alog_ref, dtb_ref, isdec_ref, o_ref, *, tiles):
    s = _row_iota((TILE, TILE))
    t = _col_iota((TILE, TILE))
    same = _div_pow2(s, CHUNK) == _div_pow2(t, CHUNK)
    ones = [jnp.where(m, 1.0, 0.0).astype(BF16) for m in (same & (s <= t), same & (s >= t), same)]
    slot = _mod_pow2(_row_iota((LANE, TILE)), SUBLANE)
    for k in range(tiles):
        x = ab_ref[0, k * TILE:(k + 1) * TILE, :]
        dec = -jnp.exp(alog_ref[...]) * _softplus(x + dtb_ref[...])
        e = jnp.where(isdec_ref[...] > 0.5, dec, _sigmoid(x))
        et = e.T
        hi = et.astype(BF16)
        rest = et - hi.astype(F32)
        mid = rest.astype(BF16)
        lo = (rest - mid.astype(F32)).astype(BF16)
        pre, suf, tot = [_bdot(hi, m) + _bdot(mid, m) + _bdot(lo, m) for m in ones]
        o_ref[0, k] = jnp.where(slot == 0, pre, jnp.where(slot == 2, suf, jnp.where(slot >= 4, tot, et)))


def _dn_gates(ab, alog_c, dtb_c, isdec_c):
    bsz, n, _ = ab.shape
    nt = n // TILE
    tiles = min(nt, DN_GATE_TILES)
    vec = pl.BlockSpec((1, LANE), lambda b, i: (0, 0))
    return pl.pallas_call(
        functools.partial(_dn_gate_kernel, tiles=tiles),
        grid=(bsz, nt // tiles),
        in_specs=[pl.BlockSpec((1, tiles * TILE, LANE), lambda b, i: (b, i, 0)), vec, vec, vec],
        out_specs=pl.BlockSpec((1, tiles, LANE, TILE), lambda b, i: (b, i, 0, 0)),
        out_shape=jax.ShapeDtypeStruct((bsz, nt, LANE, TILE), F32),
        compiler_params=_cparams(("parallel", "parallel")),
        name="dn_gates",
    )(ab, alog_c, dtb_c, isdec_c)


def _dn_masks(mask_ref):
    a = _row_iota((TILE, TILE))
    b = _col_iota((TILE, TILE))
    apart = a ^ b
    n_levels = CHUNK.bit_length() - 2
    for lg in range(1, n_levels + 1):
        mask_ref[lg - 1] = jnp.where((apart >> lg) == 1, 1.0, 0.0).astype(BF16)
    for k, pack in enumerate((DN_PACK, 2 * DN_PACK)):
        mask_ref[n_levels + k] = jnp.where(_div_pow2(a, pack) == _div_pow2(b, pack), 1.0, 0.0).astype(BF16)


def _dn_tiles(chains, mask_ref):
    a = _row_iota((TILE, TILE))
    b = _col_iota((TILE, TILE))
    same = _div_pow2(a, CHUNK) == _div_pow2(b, CHUNK)
    apart = a ^ b
    n_chunks = TILE // CHUNK
    dot = functools.partial(jnp.dot, preferred_element_type=F32)

    def stack(v, reps):
        return jnp.concatenate([v] * reps, axis=0)

    kk = [dot(c["k_rm"], c["kt"]) for c in chains]
    kq = [dot(c["k_rm"], c["qt"]) for c in chains]
    xs, pw, attn = [], [], []
    for c, kk_c, kq_c in zip(chains, kk, kq):
        incl = same & ((a >= b) if c["backward"] else (a <= b))
        gcb = jnp.broadcast_to(c["gc"], (TILE, TILE))
        diff = gcb - gcb.T
        decay = jnp.where(incl, jnp.exp(jnp.where(incl, diff, 0.0)), 0.0)
        attn.append((kq_c * decay).astype(BF16))
        x = kk_c * decay * (-c["beta"])
        xs.append(x.astype(BF16))
        base = jnp.where(a == b, 1.0, jnp.where(apart == 1, x, 0.0))
        acc = base[0:DN_PACK, :]
        for r in range(1, TILE // DN_PACK):
            acc = acc + base[r * DN_PACK:(r + 1) * DN_PACK, :]
        pw.append(acc)
    pack = DN_PACK
    n_levels = CHUNK.bit_length() - 2
    block_mask = {DN_PACK: n_levels, 2 * DN_PACK: n_levels + 1}
    s = 2
    while s < CHUNK:
        if s == pack:
            keep = _div_pow2(_row_iota((2 * pack, TILE)), pack) == (_div_pow2(_col_iota((2 * pack, TILE)), pack) & 1)
            pw = [jnp.where(keep, stack(p, 2), 0.0) for p in pw]
            pack *= 2
        couple = mask_ref[s.bit_length() - 2]
        blocks = mask_ref[block_mask[pack]]
        pb = [p.astype(BF16) for p in pw]
        px = [dot(pb_c, x * couple) for pb_c, x in zip(pb, xs)]
        p_bd = [stack(pb_c, TILE // pack) * blocks for pb_c in pb]
        pw = [p + dot(px_c.astype(BF16), bd_c) for p, px_c, bd_c in zip(pw, px, p_bd)]
        s *= 2
    blocks = mask_ref[block_mask[pack]]
    t_inv = [stack(p.astype(BF16), TILE // pack) * blocks for p in pw]
    egc = [jnp.exp(c["gc"]) for c in chains]
    u_t = [dot((c["vt"].astype(F32) * c["beta"]).astype(BF16), t) for c, t in zip(chains, t_inv)]
    w_t = [dot((c["kt"].astype(F32) * (c["beta"] * e)).astype(BF16), t).astype(BF16)
           for c, e, t in zip(chains, egc, t_inv)]
    qd_t = [(c["qt"].astype(F32) * e).astype(BF16) for c, e in zip(chains, egc)]
    kdec = [jnp.exp(c["tot"] - c["gc"]) for c in chains]
    outs = [[None] * n_chunks for _ in chains]
    for step in range(n_chunks):
        cis = [(n_chunks - 1 - step) if c["backward"] else step for c in chains]
        sl = [slice(ci * CHUNK, (ci + 1) * CHUNK) for ci in cis]
        st = [c["st_ref"][...] for c in chains]
        stb = [s_c.astype(BF16) for s_c in st]
        swq = [dot(stb_c, jnp.concatenate([w_c[:, r], q_c[:, r]], axis=1))
               for stb_c, w_c, q_c, r in zip(stb, w_t, qd_t, sl)]
        sq = [v[:, CHUNK:] for v in swq]
        vn = [u_c[:, r] - v[:, :CHUNK] for u_c, v, r in zip(u_t, swq, sl)]
        av = [dot(vn_c.astype(BF16), at_c[r, r]) for vn_c, at_c, r in zip(vn, attn, sl)]
        upd = [dot((vn_c * kd_c[:, r]).astype(BF16), c["k_rm"][r, :])
               for vn_c, kd_c, c, r in zip(vn, kdec, chains, sl)]
        for idx, c in enumerate(chains):
            c["st_ref"][...] = st[idx] * jnp.exp(c["tot"][:, sl[idx]]) + upd[idx]
            outs[idx][cis[idx]] = sq[idx] + av[idx]
    return [jnp.concatenate(o, axis=1) for o in outs]


def _dn_kernel(k_ref, qt_ref, kt_ref, vt_ref, g_ref, s0f_ref, s0b_ref,
               o_ref, sf_ref, sb_ref, ot_ref, st_ref, mask_ref, *, n_tiles, with_output):
    hb = DN_HEADS_PER_STEP
    _dn_masks(mask_ref)
    for hh in range(hb):
        st_ref[2 * hh] = s0f_ref[0, hh]
        st_ref[2 * hh + 1] = s0b_ref[0, hh]

    def body(i, carry):
        nf = i
        nb = n_tiles - 1 - i
        chains = []
        for hh in range(hb):
            for backward, n in ((False, nf), (True, nb)):
                g = g_ref[0, n, hh * SUBLANE:(hh + 1) * SUBLANE, :]
                base = 2 if backward else 0
                chains.append(dict(
                    k_rm=k_ref[0, pl.ds(pl.multiple_of(n * TILE, TILE), TILE), hh * LANE:(hh + 1) * LANE],
                    qt=qt_ref[0, hh, n], kt=kt_ref[0, hh, n], vt=vt_ref[0, hh, n],
                    gc=g[base:base + 1, :], beta=g[base + 1:base + 2, :], tot=g[4 + base // 2:5 + base // 2, :],
                    st_ref=st_ref.at[2 * hh + (1 if backward else 0)], backward=backward, hh=hh, n=n))
        o_t = _dn_tiles(chains, mask_ref)
        if with_output:
            for c, o_c in zip(chains, o_t):
                ot_ref[c["hh"], c["n"]] = ot_ref[c["hh"], c["n"]] + o_c
        return carry

    if with_output:
        ot_ref[...] = jnp.zeros_like(ot_ref)
    lax.fori_loop(0, n_tiles, body, 0)
    for hh in range(hb):
        sf_ref[0, hh] = st_ref[2 * hh]
        sb_ref[0, hh] = st_ref[2 * hh + 1]
    if with_output:
        def finish(t, carry):
            rows = pl.ds(pl.multiple_of(t * TILE, TILE), TILE)
            for hh in range(hb):
                o_ref[0, rows, hh * LANE:(hh + 1) * LANE] = ot_ref[hh, t].T.astype(BF16)
            return carry

        lax.fori_loop(0, n_tiles, finish, 0)
    else:
        o_ref[...] = jnp.zeros_like(o_ref)


def _delta_net(qkv_rm, qkv_tr, gates, s0f, s0b, with_output):
    bsz, n, _ = qkv_rm.shape
    nt = n // TILE
    h = DN_HEADS
    hb = DN_HEADS_PER_STEP
    wide = hb * LANE
    once = pl.Buffered(1)
    tr_spec = lambda off: pl.BlockSpec((1, hb, nt, LANE, TILE), lambda b, j: (b, off + j, 0, 0, 0))
    st_spec = pl.BlockSpec((1, hb, LANE, LANE), lambda b, j: (b, j, 0, 0))
    n_out = n if with_output else SUBLANE
    return pl.pallas_call(
        functools.partial(_dn_kernel, n_tiles=nt, with_output=with_output),
        grid=(bsz, h // hb),
        in_specs=[pl.BlockSpec((1, n, wide), lambda b, j: (b, 0, h // hb + j), once),
                  tr_spec(0), tr_spec(h // hb), tr_spec(2 * h // hb),
                  pl.BlockSpec((1, nt, hb * SUBLANE, TILE), lambda b, j: (b, 0, j, 0)),
                  st_spec, st_spec],
        out_specs=[pl.BlockSpec((1, n_out, wide), lambda b, j: (b, 0, j)), st_spec, st_spec],
        out_shape=[jax.ShapeDtypeStruct((bsz, n_out, DN_WIDTH), BF16),
                   jax.ShapeDtypeStruct((bsz, h, LANE, LANE), F32),
                   jax.ShapeDtypeStruct((bsz, h, LANE, LANE), F32)],
        scratch_shapes=[pltpu.VMEM((hb, nt, LANE, TILE), F32),
                        pltpu.VMEM((2 * hb, LANE, LANE), F32),
                        pltpu.VMEM((CHUNK.bit_length(), TILE, TILE), BF16)],
        compiler_params=_cparams(("parallel", "parallel"), 58),
        name="delta_net",
    )(qkv_rm, qkv_tr, qkv_tr, qkv_tr, gates, s0f, s0b)


def _lru_scan_block(x, wa, ba, wx, bx, spl, h_in, backward):
    rows, width = x.shape
    xb = x.astype(BF16)

    def gate(ws, bias):
        parts = [jnp.dot(xb[:, k * LRU_BLOCK:(k + 1) * LRU_BLOCK], w, preferred_element_type=F32)
                 for k, w in enumerate(ws)]
        return _sigmoid(jnp.concatenate(parts, axis=1) + bias)

    r = gate(wa, ba)
    gi = gate(wx, bx)
    log_a = -LRU_C * r * spl
    a = jnp.exp(log_a)
    b = jnp.sqrt(1.0 - jnp.exp(2.0 * log_a)) * (gi * x)
    groups = rows // SUBLANE
    a = a.reshape(groups, SUBLANE, width)
    b = b.reshape(groups, SUBLANE, width)
    sub = lax.broadcasted_iota(jnp.int32, a.shape, 1)
    s = 1
    while s < SUBLANE:
        if backward:
            keep = sub < SUBLANE - s
            a_sh = jnp.where(keep, pltpu.roll(a, SUBLANE - s, 1), 1.0)
            b_sh = jnp.where(keep, pltpu.roll(b, SUBLANE - s, 1), 0.0)
        else:
            keep = sub >= s
            a_sh = jnp.where(keep, pltpu.roll(a, s, 1), 1.0)
            b_sh = jnp.where(keep, pltpu.roll(b, s, 1), 0.0)
        b = a * b_sh + b
        a = a * a_sh
        s *= 2
    a = a.reshape(rows, width)
    b = b.reshape(rows, width)
    pieces = [None] * groups
    carry = h_in
    order = range(groups - 1, -1, -1) if backward else range(groups)
    edge = 0 if backward else SUBLANE - 1
    for gidx in order:
        lo = gidx * SUBLANE
        hgrp = b[lo:lo + SUBLANE, :] + a[lo:lo + SUBLANE, :] * carry
        pieces[gidx] = hgrp
        carry = hgrp[edge:edge + 1, :]
    return jnp.concatenate(pieces, axis=0), carry


def _lru_kernel(px_ref, py_ref, cw_ref, cb_ref, wa_ref, ba_ref, wx_ref, bx_ref, lam_ref, h0_ref,
                o_ref, last_ref, xs_ref, hs_ref, *, n_blocks, with_output):
    x = px_ref[0].astype(F32)
    xs_ref[...] = _dwconv_rows(x, cw_ref, 4) + cb_ref[...]
    spl = _softplus(-lam_ref[...])
    wa = [[wa_ref[d, k] for k in range(LRU_GROUP)] for d in range(2)]
    wx = [[wx_ref[d, k] for k in range(LRU_GROUP)] for d in range(2)]

    def body(i, carry):
        hf, hb = carry
        rf = pl.multiple_of(i * LRU_ROWS, LRU_ROWS)
        rb = pl.multiple_of((n_blocks - 1 - i) * LRU_ROWS, LRU_ROWS)
        h_f, hf = _lru_scan_block(xs_ref[pl.ds(rf, LRU_ROWS), :], wa[0], ba_ref[0:1, :],
                                  wx[0], bx_ref[0:1, :], spl[0:1, :], hf, False)
        h_b, hb = _lru_scan_block(xs_ref[pl.ds(rb, LRU_ROWS), :], wa[1], ba_ref[1:2, :],
                                  wx[1], bx_ref[1:2, :], spl[1:2, :], hb, True)
        if with_output:
            hs_ref[pl.ds(rf, LRU_ROWS), :] = hs_ref[pl.ds(rf, LRU_ROWS), :] + h_f
            hs_ref[pl.ds(rb, LRU_ROWS), :] = hs_ref[pl.ds(rb, LRU_ROWS), :] + h_b
        return hf, hb

    if with_output:
        hs_ref[...] = jnp.zeros_like(hs_ref)
    h0 = h0_ref[0]
    hf, hb = lax.fori_loop(0, n_blocks, body, (h0[0:1, :], h0[1:2, :]))
    last_ref[0] = jnp.concatenate([hf, hb], axis=0)
    if with_output:
        y = py_ref[0].astype(F32)
        gelu = 0.5 * y * (1.0 + jnp.tanh(math.sqrt(2.0 / math.pi) * (y + 0.044715 * (y * y * y))))
        o_ref[0] = (hs_ref[...] * gelu).astype(BF16)
    else:
        o_ref[...] = jnp.zeros_like(o_ref)


def _rglru(proj, conv_w, conv_b, w_a, b_a, w_x, b_x, lam, h0, with_output):
    bsz, n, _ = proj.shape
    rows = min(n, LRU_ROWS)
    nb = n // rows
    n_out = n if with_output else SUBLANE
    width = LRU_GROUP * LRU_BLOCK
    lx = LX_BLK * LANE // width
    ly = LY_BLK * LANE // width
    vec2 = pl.BlockSpec((2, width), lambda b, j: (0, j))
    wspec = pl.BlockSpec((2, LRU_GROUP, LRU_BLOCK, LRU_BLOCK), lambda b, j: (0, j, 0, 0))
    return pl.pallas_call(
        functools.partial(_lru_kernel, n_blocks=nb, with_output=with_output),
        grid=(bsz, LRU_BLOCKS // LRU_GROUP),
        in_specs=[pl.BlockSpec((1, n, width), lambda b, j: (b, 0, lx + j)),
                  pl.BlockSpec((1, n, width), lambda b, j: (b, 0, ly + j)),
                  pl.BlockSpec((4, width), lambda b, j: (0, j)),
                  pl.BlockSpec((1, width), lambda b, j: (0, j)),
                  wspec, vec2, wspec, vec2, vec2,
                  pl.BlockSpec((1, 2, width), lambda b, j: (b, 0, j))],
        out_specs=[pl.BlockSpec((1, n_out, width), lambda b, j: (b, 0, j)),
                   pl.BlockSpec((1, 2, width), lambda b, j: (b, 0, j))],
        out_shape=[jax.ShapeDtypeStruct((bsz, n_out, LRU_WIDTH), BF16),
                   jax.ShapeDtypeStruct((bsz, 2, LRU_WIDTH), F32)],
        scratch_shapes=[pltpu.VMEM((n, width), F32), pltpu.VMEM((n, width), F32)],
        compiler_params=_cparams(("parallel", "parallel")),
        name="rglru",
    )(proj, proj, conv_w, conv_b.reshape(1, LRU_WIDTH), w_a, b_a, w_x, b_x, lam, h0)


def _hy_prep_kernel(p0_ref, p1_ref, pv_ref, w0_ref, w1_ref, wv_ref, b0_ref, b1_ref, bv_ref, x0_ref, zz_ref):
    x0 = _dwconv_rows(p0_ref[0].astype(F32), w0_ref, 3) + b0_ref[...]
    x1 = _dwconv_rows(p1_ref[0].astype(F32), w1_ref, 3) + b1_ref[...]
    v = _dwconv_rows(pv_ref[0].astype(F32), wv_ref, 3) + bv_ref[...]
    x0_ref[0] = x0.astype(BF16)
    zz_ref[0] = (x1 * v).astype(BF16)


def _hy_prep(proj, conv_w, conv_b):
    bsz, n, _ = proj.shape
    nblk = HY_WIDTH // LANE
    pspec = lambda off: pl.BlockSpec((1, n, LANE), lambda b, j: (b, 0, HY_BLK + off + j))
    wspec = lambda off: pl.BlockSpec((3, LANE), lambda b, j: (0, off + j))
    bspec = lambda off: pl.BlockSpec((1, LANE), lambda b, j: (0, off + j))
    ospec = pl.BlockSpec((1, n, LANE), lambda b, j: (b, 0, j))
    cb = conv_b.reshape(1, 3 * HY_WIDTH)
    return pl.pallas_call(
        _hy_prep_kernel,
        grid=(bsz, nblk),
        in_specs=[pspec(0), pspec(nblk), pspec(2 * nblk), wspec(0), wspec(nblk), wspec(2 * nblk),
                  bspec(0), bspec(nblk), bspec(2 * nblk)],
        out_specs=[ospec, ospec],
        out_shape=[jax.ShapeDtypeStruct((bsz, n, HY_WIDTH), BF16)] * 2,
        compiler_params=_cparams(("parallel", "parallel")),
        name="hy_prep",
    )(proj, proj, proj, conv_w, conv_w, conv_w, cb, cb, cb)


def _hy_filter_kernel(w1_ref, b1_ref, f1_ref, w2_ref, b2_ref, f2_ref, w3_ref, band_ref, delta_ref,
                      k_ref, *, n, rows):
    def lag(shape):
        p = _row_iota(shape) + (pl.program_id(0) * rows - n)
        return p, jnp.abs(p).astype(F32)

    _, i = lag((rows, LANE))
    lane = _col_iota((rows, LANE))
    t = i * (1.0 / (n - 1))
    ang = band_ref[...] * (i * (2.0 * math.pi / n))
    feat = jnp.where(lane == 0, t,
                     jnp.where(lane <= HY_BANDS, jnp.cos(ang), jnp.where(lane < HY_EMB, -jnp.sin(ang), 0.0)))
    hid = jnp.sin(f1_ref[...] * (jnp.dot(feat, w1_ref[...], precision=HIGHEST,
                                         preferred_element_type=F32) + b1_ref[...]))
    hid = jnp.sin(f2_ref[...] * (jnp.dot(hid, w2_ref[...], precision=HIGHEST,
                                         preferred_element_type=F32) + b2_ref[...]))
    filt = _bdot(hid, w3_ref[...])
    p, iw = lag((rows, HY_WIDTH))
    dec = jnp.exp(-(iw * (1.0 / (n - 1))) * delta_ref[...])
    h_f = filt[:, :HY_WIDTH] * dec
    h_b = filt[:, HY_WIDTH:] * dec
    taps = jnp.where(p > 0, h_f, jnp.where(p < 0, h_b, h_f + h_b))
    k_ref[...] = jnp.where(p == -n, 0.0, taps).astype(BF16)


def _hy_filter(n, w1, b1, f1, w2, b2, f2, w3):
    rows = min(n, 256)
    hid = HY_FILTER_HIDDEN
    w1p = jnp.zeros((LANE, hid), F32).at[:HY_EMB].set(w1)
    bands = np.zeros((1, LANE), np.float32)
    base = np.linspace(1e-4, HY_BANDS - 1, HY_BANDS, dtype=np.float32)
    bands[0, 1:1 + HY_BANDS] = base
    bands[0, 1 + HY_BANDS:HY_EMB] = base
    log_target = math.log(HY_DECAY_TARGET)
    deltas = np.abs(np.linspace(log_target / HY_SLOW_DECAY_PCT, log_target / HY_FAST_DECAY_PCT, HY_WIDTH,
                                dtype=np.float32)).reshape(1, HY_WIDTH)
    full = lambda shape: pl.BlockSpec(shape, lambda i: (0,) * len(shape))
    return pl.pallas_call(
        functools.partial(_hy_filter_kernel, n=n, rows=rows),
        grid=(2 * n // rows,),
        in_specs=[full((LANE, hid)), full((1, hid)), full((1, hid)), full((hid, hid)), full((1, hid)),
                  full((1, hid)), full((hid, 2 * HY_WIDTH)), full((1, LANE)), full((1, HY_WIDTH))],
        out_specs=pl.BlockSpec((rows, HY_WIDTH), lambda i: (i, 0)),
        out_shape=jax.ShapeDtypeStruct((2 * n, HY_WIDTH), BF16),
        compiler_params=_cparams(("parallel",)),
        name="hy_filter",
    )(w1p, b1.reshape(1, hid), f1.reshape(1, hid), w2, b2.reshape(1, hid), f2.reshape(1, hid), w3,
      jnp.asarray(bands), jnp.asarray(deltas))


DFT_GROUP = 64


def _dft_table_kernel(c_ref, s_ref, cb_ref, sb_ref, *, m, ncols, col0, rows, transposed, blank_first):
    period = 4 * m
    scale = 2.0 * math.pi / period
    col = _col_iota((1, ncols)) + col0

    @pl.when(pl.program_id(0) == 0)
    def _():
        r2 = _row_iota((DFT_GROUP, ncols))
        c2 = _col_iota((DFT_GROUP, ncols)) + col0
        ph = (r2 * (2 * c2 + 1)) if transposed else ((2 * r2 + 1) * c2)
        ang = (ph & (period - 1)).astype(F32) * scale
        cb_ref[...] = jnp.cos(ang)
        sb_ref[...] = jnp.sin(ang)

    for g in range(rows // DFT_GROUP):
        r1 = pl.program_id(0) * (rows // DFT_GROUP) + g
        ph = (DFT_GROUP * r1) * (2 * col + 1) if transposed else (2 * DFT_GROUP * r1) * col
        ang = (ph & (period - 1)).astype(F32) * scale
        ca = jnp.cos(ang)
        sa = jnp.sin(ang)
        cb = cb_ref[...]
        sb = sb_ref[...]
        c_tile = ca * cb - sa * sb
        s_tile = sa * cb + ca * sb
        if blank_first:
            first = _col_iota((DFT_GROUP, ncols)) == 0
            c_tile = jnp.where(first, 0.0, c_tile)
            s_tile = jnp.where(first, 0.0, s_tile)
        c_ref[g * DFT_GROUP:(g + 1) * DFT_GROUP, :] = c_tile.astype(BF16)
        s_ref[g * DFT_GROUP:(g + 1) * DFT_GROUP, :] = s_tile.astype(BF16)


def _dft_tables(m, transposed=False, two_sided=False):
    rows = min(m, 256)
    ncols = 2 * m if two_sided else m
    spec = pl.BlockSpec((rows, ncols), lambda i: (i, 0))
    return pl.pallas_call(
        functools.partial(_dft_table_kernel, m=m, ncols=ncols, col0=3 * m if two_sided else 0, rows=rows,
                          transposed=transposed, blank_first=two_sided),
        grid=(m // rows,),
        in_specs=[],
        out_specs=[spec, spec],
        out_shape=[jax.ShapeDtypeStruct((m, ncols), BF16)] * 2,
        scratch_shapes=[pltpu.VMEM((DFT_GROUP, ncols), F32), pltpu.VMEM((DFT_GROUP, ncols), F32)],
        compiler_params=_cparams(("arbitrary",)),
        name="dft_tables",
    )()


def _hy_tables(m):
    return _dft_tables(m) + _dft_tables(m, transposed=True) + _dft_tables(m, two_sided=True)


def _hy_ktrans_kernel(cw_ref, sw_ref, lo_ref, hi_ref, kc_ref, ks_ref, *, m):
    dot = functools.partial(jnp.dot, preferred_element_type=F32)
    lo = lo_ref[...]
    hi = hi_ref[...]
    kc_ref[0] = dot(cw_ref[:, :m], lo) + dot(cw_ref[:, m:], hi)
    ks_ref[0] = dot(sw_ref[:, :m], lo) + dot(sw_ref[:, m:], hi)


def _hy_ktrans(cw, sw, taps, m):
    nd = taps.shape[0] // m - 1
    w = taps.shape[1]
    tn = 512
    tab = pl.BlockSpec((m, 2 * m), lambda e, j: (0, 0))
    ospec = pl.BlockSpec((1, m, tn), lambda e, j: (e, 0, j))
    return pl.pallas_call(
        functools.partial(_hy_ktrans_kernel, m=m),
        grid=(nd, w // tn),
        in_specs=[tab, tab, pl.BlockSpec((m, tn), lambda e, j: (e, j)), pl.BlockSpec((m, tn), lambda e, j: (e + 1, j))],
        out_specs=[ospec, ospec],
        out_shape=[jax.ShapeDtypeStruct((nd, m, w), F32)] * 2,
        compiler_params=_cparams(("parallel", "parallel")),
        name="hy_ktrans",
    )(cw, sw, taps, taps)


def _hy_fwd_kernel(c_ref, s_ref, zz_ref, kc_ref, ks_ref, a_ref, b_ref, *, m, nb):
    dot = functools.partial(jnp.dot, preferred_element_type=F32)
    c = c_ref[...]
    s = s_ref[...]
    acc_a = [None] * nb
    acc_b = [None] * nb
    for j in range(nb):
        zz = zz_ref[0, j * m:(j + 1) * m, :]
        uc = dot(c, zz)
        us = dot(s, zz)
        for i in range(nb):
            kc = kc_ref[i - j + nb - 1]
            ks = ks_ref[i - j + nb - 1]
            ta = uc * kc - us * ks
            tb = uc * ks + us * kc
            acc_a[i] = ta if acc_a[i] is None else acc_a[i] + ta
            acc_b[i] = tb if acc_b[i] is None else acc_b[i] + tb
    for i in range(nb):
        a_ref[0, i] = acc_a[i].astype(BF16)
        b_ref[0, i] = acc_b[i].astype(BF16)


def _hy_forward(ctab, stab, zz, kc, ks, m):
    bsz, n, w = zz.shape
    nb = n // m
    nd = 2 * nb - 1
    tf = min(m, 256)
    tn = 512
    tab = pl.BlockSpec((tf, m), lambda b, j, i: (i, 0))
    kspec = pl.BlockSpec((nd, tf, tn), lambda b, j, i: (0, i, j))
    ospec = pl.BlockSpec((1, nb, tf, tn), lambda b, j, i: (b, 0, i, j))
    return pl.pallas_call(
        functools.partial(_hy_fwd_kernel, m=m, nb=nb),
        grid=(bsz, w // tn, m // tf),
        in_specs=[tab, tab, pl.BlockSpec((1, n, tn), lambda b, j, i: (b, 0, j)), kspec, kspec],
        out_specs=[ospec, ospec],
        out_shape=[jax.ShapeDtypeStruct((bsz, nb, m, w), BF16)] * 2,
        compiler_params=_cparams(("parallel", "parallel", "parallel")),
        name="hy_forward",
    )(ctab, stab, zz, kc, ks)


def _hy_inv_kernel(ct_ref, st_ref, a_ref, b_ref, x0_ref, zz_ref, bias_ref, o_ref, *, m):
    y = (jnp.dot(ct_ref[...], a_ref[0, 0], preferred_element_type=F32)
         + jnp.dot(st_ref[...], b_ref[0, 0], preferred_element_type=F32)) * (1.0 / m)
    zz = zz_ref[0].astype(F32)
    o_ref[0] = (x0_ref[0].astype(F32) * (y + zz * bias_ref[...])).astype(BF16)


def _hy_inverse(cttab, sttab, a, bq, x0, zz, bias, m):
    bsz, n, w = zz.shape
    nb = n // m
    tn = 512
    tab = pl.BlockSpec((m, m), lambda b, i, j: (0, 0))
    spec4 = pl.BlockSpec((1, 1, m, tn), lambda b, i, j: (b, i, 0, j))
    tile = pl.BlockSpec((1, m, tn), lambda b, i, j: (b, i, j))
    return pl.pallas_call(
        functools.partial(_hy_inv_kernel, m=m),
        grid=(bsz, nb, w // tn),
        in_specs=[tab, tab, spec4, spec4, tile, tile, pl.BlockSpec((1, tn), lambda b, i, j: (0, j))],
        out_specs=tile,
        out_shape=jax.ShapeDtypeStruct((bsz, n, w), BF16),
        compiler_params=_cparams(("parallel", "parallel", "parallel")),
        name="hy_inverse",
    )(cttab, sttab, a, bq, x0, zz, bias.reshape(1, w))


def _merge_kernel(odn_ref, z_ref, ng_ref, ohy_ref, olru_ref, gdn_ref, ghy_ref, glru_ref, wdn_ref, why_ref,
                  wlru_ref, wout_ref, x_ref, gate_ref, o_ref):
    heads = []
    for h in range(DN_HEADS):
        o = odn_ref[0, :, h * LANE:(h + 1) * LANE].astype(F32)
        y = o * lax.rsqrt(jnp.mean(o * o, axis=-1, keepdims=True) + NORM_EPS) * ng_ref[...]
        heads.append((y * _silu(z_ref[0, :, h * LANE:(h + 1) * LANE].astype(F32))).astype(BF16))
    odn = jnp.concatenate(heads, axis=1)
    m = _sigmoid(gdn_ref[0].astype(F32)) * jnp.dot(odn, wdn_ref[...], preferred_element_type=F32)
    m = m + _sigmoid(ghy_ref[0].astype(F32)) * jnp.dot(ohy_ref[0], why_ref[...], preferred_element_type=F32)
    m = m + _sigmoid(glru_ref[0].astype(F32)) * jnp.dot(olru_ref[0], wlru_ref[...], preferred_element_type=F32)
    y = jnp.dot(m.astype(BF16), wout_ref[...], preferred_element_type=F32)
    o_ref[0] = x_ref[0] + gate_ref[0] * y


def _merge(o_dn, dn_norm_g, o_hy, o_lru, proj, w_dn, w_hy, w_lru, w_out, x, gate):
    bsz, n, d = x.shape
    tm = min(n, 256)
    nblk = d // 1024
    act = pl.BlockSpec((1, tm, d), lambda b, i: (b, i, 0))
    gspec = lambda k: pl.BlockSpec((1, tm, d), lambda b, i: (b, i, GATE_BLK * LANE // d + k * nblk))
    wspec = pl.BlockSpec((d, d), lambda b, i: (0, 0))
    return pl.pallas_call(
        _merge_kernel,
        grid=(bsz, n // tm),
        in_specs=[act, pl.BlockSpec((1, tm, d), lambda b, i: (b, i, Z_BLK * LANE // d)),
                  pl.BlockSpec((1, LANE), lambda b, i: (0, 0)),
                  act, act, gspec(0), gspec(1), gspec(2), wspec, wspec, wspec, wspec, act,
                  pl.BlockSpec((1, 1, d), lambda b, i: (b, 0, 0))],
        out_specs=act,
        out_shape=jax.ShapeDtypeStruct((bsz, n, d), F32),
        compiler_params=_cparams(("parallel", "parallel")),
        name="merge",
    )(o_dn, proj, dn_norm_g.reshape(1, LANE), o_hy, o_lru, proj, proj, proj, w_dn, w_hy, w_lru, w_out, x, gate)


def _ffn_act_kernel(ug_ref, uv_ref, wg_ref, wv_ref, o_ref, up_ref, mid_ref, dn_ref, *, rows, cols, n):
    tile = FFN_TILE
    a = _row_iota((tile, tile))
    b = _col_iota((tile, tile))
    c = _mod_pow2(a, cols)
    lmat = jnp.where((b == a - 1) & (c >= 1), 1.0, 0.0).astype(BF16)
    rmat = jnp.where((b == a + 1) & (c <= cols - 2), 1.0, 0.0).astype(BF16)
    taps = (-1, 0, 1) if rows > 1 else (0,)
    dst = {-1: up_ref, 0: mid_ref, 1: dn_ref}
    if rows > 1:
        pad = jnp.zeros((2, cols, FFN_WIDE), BF16)
        up_ref[:, 0:cols, :] = pad
        dn_ref[:, n + cols:n + 2 * cols, :] = pad

    def taps_of(i, carry):
        r0 = pl.multiple_of(i * tile, tile)
        for idx, (u_ref, w_ref) in enumerate(((ug_ref, wg_ref), (uv_ref, wv_ref))):
            u = u_ref[0, pl.ds(r0, tile), :]
            left = jnp.dot(lmat, u, preferred_element_type=F32).astype(BF16)
            right = jnp.dot(rmat, u, preferred_element_type=F32).astype(BF16)
            for di in taps:
                k = 3 * (di + 1)
                w = [w_ref[k + j:k + j + 1, :].astype(BF16) for j in range(3)]
                off = pl.multiple_of(r0 + (cols if di != 0 else 0), cols)
                dst[di][idx, pl.ds(off, tile), :] = left * w[0] + u * w[1] + right * w[2]
        return carry

    def combine(i, carry):
        r0 = pl.multiple_of(i * tile, tile)
        vals = []
        for idx in range(2):
            acc = mid_ref[idx, pl.ds(r0, tile), :]
            if rows > 1:
                acc = (acc + up_ref[idx, pl.ds(r0, tile), :]
                       + dn_ref[idx, pl.ds(pl.multiple_of(r0 + 2 * cols, cols), tile), :])
            vals.append(acc.astype(F32))
        o_ref[0, pl.ds(r0, tile), :] = (_silu(vals[0]) * vals[1]).astype(BF16)
        return carry

    lax.fori_loop(0, n // tile, taps_of, 0, unroll=min(2, n // tile))
    lax.fori_loop(0, n // tile, combine, 0)


def _ffn_act(u, conv_w, rows, cols):
    bsz, n, _ = u.shape
    nblk = FFN_HIDDEN // FFN_WIDE
    padded = n + 2 * cols if rows > 1 else SUBLANE * 2
    return pl.pallas_call(
        functools.partial(_ffn_act_kernel, rows=rows, cols=cols, n=n),
        grid=(bsz, nblk),
        in_specs=[pl.BlockSpec((1, n, FFN_WIDE), lambda b, j: (b, 0, j)),
                  pl.BlockSpec((1, n, FFN_WIDE), lambda b, j: (b, 0, nblk + j)),
                  pl.BlockSpec((9, FFN_WIDE), lambda b, j: (0, j)),
                  pl.BlockSpec((9, FFN_WIDE), lambda b, j: (0, nblk + j))],
        out_specs=pl.BlockSpec((1, n, FFN_WIDE), lambda b, j: (b, 0, j)),
        out_shape=jax.ShapeDtypeStruct((bsz, n, FFN_HIDDEN), BF16),
        scratch_shapes=[pltpu.VMEM((2, padded, FFN_WIDE), BF16), pltpu.VMEM((2, n, FFN_WIDE), BF16),
                        pltpu.VMEM((2, padded, FFN_WIDE), BF16)],
        compiler_params=_cparams(("parallel", "parallel")),
        name="ffn_act",
    )(u, u, conv_w, conv_w)


def _gate_column_map():
    src = np.full((LANE,), -1, np.int32)
    isdec = np.zeros((1, LANE), np.float32)
    dirs = np.zeros((LANE,), np.int32)
    for h in range(DN_HEADS):
        for slot, (d, kind) in enumerate(((0, 0), (0, 1), (1, 0), (1, 1), (0, 0), (1, 0))):
            src[h * SUBLANE + slot] = d * 2 * DN_HEADS + kind * DN_HEADS + h
            isdec[0, h * SUBLANE + slot] = 1.0 if kind == 0 else 0.0
            dirs[h * SUBLANE + slot] = d
    return src, isdec, dirs


def _split_in_proj(w_in):
    o = np.cumsum((3 * DN_WIDTH, DN_WIDTH, 4 * DN_HEADS, 3 * HY_WIDTH, LRU_WIDTH, LRU_WIDTH)).tolist()
    w_main = jnp.concatenate([w_in[:, :o[1]], w_in[:, o[2]:]], axis=1).astype(BF16)
    src, _, _ = _gate_column_map()
    w_ab = w_in[:, o[1]:o[2]]
    w_gate = jnp.where(jnp.asarray(src >= 0)[None, :], w_ab[:, np.maximum(src, 0)], 0.0).astype(BF16)
    return w_main, w_gate


def _gate_params(a_log, dt_bias):
    _, isdec, dirs = _gate_column_map()
    head = (np.arange(LANE) // SUBLANE).astype(np.int32)
    alog_c = a_log[dirs, head].reshape(1, LANE)
    dtb_c = dt_bias[dirs, head].reshape(1, LANE)
    return alog_c, dtb_c, jnp.asarray(isdec)


def _token_views(bsz, n, mods, shared_mod):
    if not shared_mod:
        same = lambda t: t
        return same, same, mods
    flat = lambda t: t.reshape(1, bsz * n, t.shape[-1])
    unflat = lambda t: t.reshape(bsz, n, t.shape[-1])
    return flat, unflat, [m[:1] for m in mods]


def _token_mixer(x, mods, lp, states, tables, with_output, shared_mod):
    bsz, n, _ = x.shape
    flat, unflat, tmods = _token_views(bsz, n, mods, shared_mod)
    proj, ab = _norm_mod_matmul(flat(x), lp["norm1_g"], tmods[0], tmods[1], lp["w_main"], BF16, 2048, 1024,
                                lp["w_gate"])
    proj, ab = unflat(proj), unflat(ab)

    qkv_rm, qkv_tr = _dn_prep(proj, lp["dn_conv_w"])
    gates = _dn_gates(ab, *lp["gate_params"])
    o_dn, s_f, s_b = _delta_net(qkv_rm, qkv_tr, gates, states[0], states[1], with_output)
    o_lru, h_last = _rglru(proj, lp["lru_conv_w"], lp["lru_conv_b"], lp["lru_w_a"], lp["lru_b_a"],
                           lp["lru_w_x"], lp["lru_b_x"], lp["lru_lambda"], states[2], with_output)
    new_states = (s_f, s_b, h_last)
    if not with_output:
        return None, new_states

    m = min(n, HY_BLOCK)
    ctab, stab, cttab, sttab, cwtab, swtab = tables
    taps = _hy_filter(n, lp["hy_w1"], lp["hy_b1"], lp["hy_f1"], lp["hy_w2"], lp["hy_b2"], lp["hy_f2"], lp["hy_w3"])
    kc, ks = _hy_ktrans(cwtab, swtab, taps, m)
    x0, zz = _hy_prep(proj, lp["hy_conv_w"], lp["hy_conv_b"])
    a, bq = _hy_forward(ctab, stab, zz, kc, ks, m)
    o_hy = _hy_inverse(cttab, sttab, a, bq, x0, zz, lp["hy_bias"], m)

    x = _merge(flat(o_dn), lp["dn_norm_g"], flat(o_hy), flat(o_lru), flat(proj), lp["w_proj_dn"], lp["w_proj_hy"], lp["w_proj_lru"],
               lp["w_out"], flat(x), tmods[2])
    return unflat(x), new_states


def _conv_ffn(x, mods, lp, rows, cols, shared_mod, final_gain=None):
    bsz, n, _ = x.shape
    flat, unflat, tmods = _token_views(bsz, n, mods, shared_mod)
    u = _norm_mod_matmul(flat(x), lp["norm2_g"], tmods[3], tmods[4], lp["ffn_up"], BF16, 1024, FFN_HIDDEN)
    act = _ffn_act(unflat(u), lp["ffn_conv_w"], rows, cols)
    return unflat(_matmul_residual(flat(act), lp["ffn_down"], flat(x), tmods[5], final_gain))


def kernel(x, c, ctx, c_ctx, w_mod, b_mod, norm1_g, norm2_g, w_in, dn_conv_w, dn_a_log, dn_dt_bias, dn_norm_g,
           hy_conv_w, hy_conv_b, hy_w1, hy_b1, hy_f1, hy_w2, hy_b2, hy_f2, hy_w3, hy_bias,
           lru_conv_w, lru_conv_b, lru_w_a, lru_b_a, lru_w_x, lru_b_x, lru_lambda,
           w_proj_dn, w_proj_hy, w_proj_lru, w_out, ffn_up, ffn_conv_w, ffn_down, final_norm_g):
    bsz, n_lat, d = x.shape
    n_ctx = ctx.shape[1]
    depth = w_in.shape[0]
    rows = n_lat // GRID_W

    cvec = jnp.zeros((SUBLANE, d), F32).at[:bsz].set(c).at[bsz].set(c_ctx)
    lat_tables = _hy_tables(min(n_lat, HY_BLOCK))
    ctx_tables = _hy_tables(min(n_ctx, HY_BLOCK))
    zero_states = (jnp.zeros((bsz, DN_HEADS, LANE, LANE), F32), jnp.zeros((bsz, DN_HEADS, LANE, LANE), F32),
                   jnp.zeros((bsz, 2, LRU_WIDTH), F32))

    xc = ctx
    for l in range(depth):
        ctx_needed = l < depth - 1
        w_main, w_gate = _split_in_proj(w_in[l])
        lp = dict(
            norm1_g=norm1_g[l], norm2_g=norm2_g[l], w_main=w_main, w_gate=w_gate,
            dn_conv_w=dn_conv_w[l], gate_params=_gate_params(dn_a_log[l], dn_dt_bias[l]), dn_norm_g=dn_norm_g[l],
            hy_conv_w=hy_conv_w[l], hy_conv_b=hy_conv_b[l], hy_w1=hy_w1[l], hy_b1=hy_b1[l], hy_f1=hy_f1[l],
            hy_w2=hy_w2[l], hy_b2=hy_b2[l], hy_f2=hy_f2[l], hy_w3=hy_w3[l], hy_bias=hy_bias[l],
            lru_conv_w=lru_conv_w[l], lru_conv_b=lru_conv_b[l], lru_w_a=lru_w_a[l].astype(BF16),
            lru_b_a=lru_b_a[l], lru_w_x=lru_w_x[l].astype(BF16), lru_b_x=lru_b_x[l], lru_lambda=lru_lambda[l],
            w_proj_dn=w_proj_dn[l].astype(BF16), w_proj_hy=w_proj_hy[l].astype(BF16),
            w_proj_lru=w_proj_lru[l].astype(BF16), w_out=w_out[l].astype(BF16),
            ffn_up=ffn_up[l].astype(BF16), ffn_conv_w=ffn_conv_w[l].reshape(9, 2 * FFN_HIDDEN),
            ffn_down=ffn_down[l].astype(BF16))
        mod = _modulation(cvec, w_mod[l].astype(BF16), b_mod[l])
        lat_mod = [mod[:bsz, k * d:(k + 1) * d].reshape(bsz, 1, d) for k in range(N_MOD)]
        ctx_mod = [jnp.broadcast_to(mod[bsz:bsz + 1, k * d:(k + 1) * d].reshape(1, 1, d), (bsz, 1, d))
                   for k in range(N_MOD)]

        xc_new, ctx_states = _token_mixer(xc, ctx_mod, lp, zero_states, ctx_tables, ctx_needed, True)
        x, _ = _token_mixer(x, lat_mod, lp, ctx_states, lat_tables, True, False)
        x = _conv_ffn(x, lat_mod, lp, rows, GRID_W, False, None if ctx_needed else final_norm_g)
        if ctx_needed:
            xc = _conv_ffn(xc_new, ctx_mod, lp, 1, n_ctx, True)
    return x
```

```python
import functools
import math

import numpy as np
import jax
import jax.numpy as jnp
from jax import lax
from jax.experimental import pallas as pl
from jax.experimental.pallas import tpu as pltpu

F32 = jnp.float32
BF16 = jnp.bfloat16
HIGHEST = lax.Precision.HIGHEST

D_MODEL = 1024
DEPTH = 2
GRID_W = 64
NORM_EPS = 1e-6
N_MOD = 6

DN_HEADS = 8
DN_HEAD_DIM = 128
DN_WIDTH = DN_HEADS * DN_HEAD_DIM
HY_WIDTH = 1024
HY_EMB = 33
HY_BANDS = (HY_EMB - 1) // 2
HY_FILTER_HIDDEN = 64
HY_FAST_DECAY_PCT = 0.3
HY_SLOW_DECAY_PCT = 1.5
HY_DECAY_TARGET = 1e-2
LRU_WIDTH = 1024
LRU_BLOCKS = 8
LRU_BLOCK = LRU_WIDTH // LRU_BLOCKS
LRU_C = 8.0
FFN_HIDDEN = 2816

LANE = 128
SUBLANE = 8
TILE = 256
CHUNK = 128
DN_HEADS_PER_STEP = 4
DN_GATE_TILES = 4
DN_PREP_WIDE_ROWS = 512
DN_PACK = 64
HY_BLOCK = 1024
LRU_ROWS = 256
LRU_GROUP = 4
FFN_TILE = 256
FFN_WIDE = 256
MIB = 1024 * 1024
VMEM_LIMIT_MIB = 48
DN_VMEM_LIMIT_MIB = 58

QKV_BLK = 0
Z_BLK = 24
HY_BLK = 32
LX_BLK = 56
LY_BLK = 64
GATE_BLK = 72
N_MAIN = 96 * LANE


def _cparams(sem, vmem_mib=VMEM_LIMIT_MIB):
    return pltpu.CompilerParams(dimension_semantics=sem, vmem_limit_bytes=vmem_mib * MIB)


def _sigmoid(x):
    return 1.0 / (1.0 + jnp.exp(-x))


def _silu(x):
    return x * _sigmoid(x)


def _softplus(x):
    return jnp.maximum(x, 0.0) + jnp.log(1.0 + jnp.exp(-jnp.abs(x)))


def _row_iota(shape):
    return lax.broadcasted_iota(jnp.int32, shape, 0)


def _col_iota(shape):
    return lax.broadcasted_iota(jnp.int32, shape, 1)


def _div_pow2(x, k):
    assert k & (k - 1) == 0
    return x >> (k.bit_length() - 1)


def _mod_pow2(x, k):
    assert k & (k - 1) == 0
    return x & (k - 1)


def _bdot(a, b):
    return jnp.dot(a.astype(BF16), b.astype(BF16), preferred_element_type=F32)


def _mod_kernel(c_ref, w_ref, b_ref, o_ref):
    o_ref[...] = _bdot(_silu(c_ref[...]), w_ref[...]) + b_ref[...]


def _modulation(cvec, w_mod, b_mod):
    n = w_mod.shape[1]
    tn = 1024
    return pl.pallas_call(
        _mod_kernel,
        grid=(n // tn,),
        in_specs=[pl.BlockSpec((SUBLANE, D_MODEL), lambda j: (0, 0)),
                  pl.BlockSpec((D_MODEL, tn), lambda j: (0, j)),
                  pl.BlockSpec((1, tn), lambda j: (0, j))],
        out_specs=pl.BlockSpec((SUBLANE, tn), lambda j: (0, j)),
        out_shape=jax.ShapeDtypeStruct((SUBLANE, n), F32),
        compiler_params=_cparams(("parallel",)),
        name="modulation",
    )(cvec, w_mod, b_mod.reshape(1, n))


def _nmm_kernel(x_ref, g_ref, sh_ref, sc_ref, w_ref, *rest, with_side):
    o_ref = rest[1] if with_side else rest[0]
    h_ref = rest[-1]

    @pl.when(pl.program_id(2) == 0)
    def _():
        x = x_ref[0]
        y = x * lax.rsqrt(jnp.mean(x * x, axis=-1, keepdims=True) + NORM_EPS) * g_ref[...]
        h_ref[...] = (y * (1.0 + sc_ref[0]) + sh_ref[0]).astype(BF16)
        if with_side:
            rest[2][0] = jnp.dot(h_ref[...], rest[0][...], preferred_element_type=F32)

    o_ref[0] = jnp.dot(h_ref[...], w_ref[...], preferred_element_type=F32).astype(o_ref.dtype)


def _norm_mod_matmul(x, gain, shift, scale, w, out_dtype, tm, tn, w_side=None):
    bsz, n, d = x.shape
    nout = w.shape[1]
    tm = min(n, tm)
    with_side = w_side is not None
    in_specs = [pl.BlockSpec((1, tm, d), lambda b, i, j: (b, i, 0)),
                pl.BlockSpec((1, d), lambda b, i, j: (0, 0)),
                pl.BlockSpec((1, 1, d), lambda b, i, j: (b, 0, 0)),
                pl.BlockSpec((1, 1, d), lambda b, i, j: (b, 0, 0)),
                pl.BlockSpec((d, tn), lambda b, i, j: (0, j))]
    out_specs = [pl.BlockSpec((1, tm, tn), lambda b, i, j: (b, i, j))]
    out_shape = [jax.ShapeDtypeStruct((bsz, n, nout), out_dtype)]
    args = [x, gain.reshape(1, d), shift, scale, w]
    if with_side:
        ns = w_side.shape[1]
        in_specs.append(pl.BlockSpec((d, ns), lambda b, i, j: (0, 0)))
        out_specs.append(pl.BlockSpec((1, tm, ns), lambda b, i, j: (b, i, 0)))
        out_shape.append(jax.ShapeDtypeStruct((bsz, n, ns), F32))
        args.append(w_side)
    out = pl.pallas_call(
        functools.partial(_nmm_kernel, with_side=with_side),
        grid=(bsz, n // tm, nout // tn),
        in_specs=in_specs,
        out_specs=out_specs,
        out_shape=out_shape,
        scratch_shapes=[pltpu.VMEM((tm, d), BF16)],
        compiler_params=_cparams(("parallel", "parallel", "arbitrary")),
        name="norm_mod_matmul",
    )(*args)
    return out if with_side else out[0]


def _mm_kernel(a_ref, b_ref, o_ref):
    o_ref[...] = jnp.dot(a_ref[...], b_ref[...], preferred_element_type=F32).astype(o_ref.dtype)


def _matmul(a, b, out_dtype, tm, tn):
    m, k = a.shape
    n = b.shape[1]
    return pl.pallas_call(
        _mm_kernel,
        grid=(m // tm, n // tn),
        in_specs=[pl.BlockSpec((tm, k), lambda i, j: (i, 0)),
                  pl.BlockSpec((k, tn), lambda i, j: (0, j))],
        out_specs=pl.BlockSpec((tm, tn), lambda i, j: (i, j)),
        out_shape=jax.ShapeDtypeStruct((m, n), out_dtype),
        compiler_params=_cparams(("parallel", "parallel")),
        name="matmul",
    )(a, b)


def _mm_res_kernel(a_ref, b_ref, x_ref, g_ref, *rest, final_norm):
    o_ref = rest[-1]
    out = x_ref[0] + g_ref[0] * jnp.dot(a_ref[0], b_ref[...], preferred_element_type=F32)
    if final_norm:
        out = out * lax.rsqrt(jnp.mean(out * out, axis=-1, keepdims=True) + NORM_EPS) * rest[0][...]
    o_ref[0] = out


def _matmul_residual(a, w, x, gate, final_gain=None):
    bsz, n, k = a.shape
    d = w.shape[1]
    tm = min(n, 512)
    in_specs = [pl.BlockSpec((1, tm, k), lambda b, i: (b, i, 0)),
                pl.BlockSpec((k, d), lambda b, i: (0, 0)),
                pl.BlockSpec((1, tm, d), lambda b, i: (b, i, 0)),
                pl.BlockSpec((1, 1, d), lambda b, i: (b, 0, 0))]
    args = [a, w, x, gate]
    if final_gain is not None:
        in_specs.append(pl.BlockSpec((1, d), lambda b, i: (0, 0)))
        args.append(final_gain.reshape(1, d))
    return pl.pallas_call(
        functools.partial(_mm_res_kernel, final_norm=final_gain is not None),
        grid=(bsz, n // tm),
        in_specs=in_specs,
        out_specs=pl.BlockSpec((1, tm, d), lambda b, i: (b, i, 0)),
        out_shape=jax.ShapeDtypeStruct((bsz, n, d), F32),
        compiler_params=_cparams(("parallel", "parallel")),
        name="matmul_residual",
    )(*args)


def _dwconv_rows(x, w_ref, k):
    n = x.shape[0]
    left = (k - 1) // 2
    t = _row_iota((SUBLANE, x.shape[1]))
    acc = x * w_ref[left:left + 1, :]
    head_fix = jnp.zeros((SUBLANE, x.shape[1]), F32)
    tail_fix = jnp.zeros((SUBLANE, x.shape[1]), F32)
    for j in range(k):
        off = j - left
        if off == 0:
            continue
        term = pltpu.roll(x, (-off) % n, 0) * w_ref[j:j + 1, :]
        acc = acc + term
        if off < 0:
            head_fix = head_fix + jnp.where(t < -off, term[:SUBLANE, :], 0.0)
        else:
            tail_fix = tail_fix + jnp.where(t >= SUBLANE - off, term[n - SUBLANE:, :], 0.0)
    return jnp.concatenate([acc[:SUBLANE, :] - head_fix, acc[SUBLANE:n - SUBLANE, :],
                            acc[n - SUBLANE:, :] - tail_fix], axis=0)


def _dn_prep_kernel(p_ref, w_ref, rm_ref, tr_ref, *, n_tiles, heads):
    first = pl.program_id(1) * heads
    y_all = _silu(_dwconv_rows(p_ref[0].astype(F32), w_ref, 4))
    q_scale = jnp.where(first < DN_HEADS, DN_HEAD_DIM ** -0.5, 1.0)
    for h in range(heads):
        y = y_all[:, h * LANE:(h + 1) * LANE]
        unit = lax.rsqrt(jnp.sum(y * y, axis=-1, keepdims=True) + 1e-6) * q_scale
        y = y * jnp.where(first < 2 * DN_HEADS, unit, 1.0)
        rm_ref[0, :, h * LANE:(h + 1) * LANE] = y.astype(BF16)
        for t in range(n_tiles):
            tr_ref[0, h, t] = y[t * TILE:(t + 1) * TILE, :].T.astype(BF16)


def _dn_prep(proj, conv_w):
    bsz, n, _ = proj.shape
    nt = n // TILE
    nc = 3 * DN_HEADS
    heads = DN_HEADS if n <= DN_PREP_WIDE_ROWS else 2
    wide = heads * LANE
    return pl.pallas_call(
        functools.partial(_dn_prep_kernel, n_tiles=nt, heads=heads),
        grid=(bsz, nc // heads),
        in_specs=[pl.BlockSpec((1, n, wide), lambda b, c: (b, 0, QKV_BLK // heads + c)),
                  pl.BlockSpec((4, wide), lambda b, c: (0, c))],
        out_specs=[pl.BlockSpec((1, n, wide), lambda b, c: (b, 0, c)),
                   pl.BlockSpec((1, heads, nt, LANE, TILE), lambda b, c: (b, c, 0, 0, 0))],
        out_shape=[jax.ShapeDtypeStruct((bsz, n, nc * LANE), BF16),
                   jax.ShapeDtypeStruct((bsz, nc, nt, LANE, TILE), BF16)],
        compiler_params=_cparams(("parallel", "parallel")),
        name="dn_prep",
    )(proj, conv_w)


def _dn_gate_kernel(ab_ref, alog_ref, dtb_ref, isdec_ref, o_ref, *, tiles):
    s = _row_iota((TILE, TILE))
    t = _col_iota((TILE, TILE))
    same = _div_pow2(s, CHUNK) == _div_pow2(t, CHUNK)
    ones = [jnp.where(m, 1.0, 0.0).astype(BF16) for m in (same & (s <= t), same & (s >= t), same)]
    slot = _mod_pow2(_row_iota((LANE, TILE)), SUBLANE)
    for k in range(tiles):
        x = ab_ref[0, k * TILE:(k + 1) * TILE, :]
        dec = -jnp.exp(alog_ref[...]) * _softplus(x + dtb_ref[...])
        e = jnp.where(isdec_ref[...] > 0.5, dec, _sigmoid(x))
        et = e.T
        hi = et.astype(BF16)
        rest = et - hi.astype(F32)
        mid = rest.astype(BF16)
        lo = (rest - mid.astype(F32)).astype(BF16)
        pre, suf, tot = [_bdot(hi, m) + _bdot(mid, m) + _bdot(lo, m) for m in ones]
        o_ref[0, k] = jnp.where(slot == 0, pre, jnp.where(slot == 2, suf, jnp.where(slot >= 4, tot, et)))


def _dn_gates(ab, alog_c, dtb_c, isdec_c):
    bsz, n, _ = ab.shape
    nt = n // TILE
    tiles = min(nt, DN_GATE_TILES)
    vec = pl.BlockSpec((1, LANE), lambda b, i: (0, 0))
    return pl.pallas_call(
        functools.partial(_dn_gate_kernel, tiles=tiles),
        grid=(bsz, nt // tiles),
        in_specs=[pl.BlockSpec((1, tiles * TILE, LANE), lambda b, i: (b, i, 0)), vec, vec, vec],
        out_specs=pl.BlockSpec((1, tiles, LANE, TILE), lambda b, i: (b, i, 0, 0)),
        out_shape=jax.ShapeDtypeStruct((bsz, nt, LANE, TILE), F32),
        compiler_params=_cparams(("parallel", "parallel")),
        name="dn_gates",
    )(ab, alog_c, dtb_c, isdec_c)


def _dn_masks(mask_ref):
    a = _row_iota((TILE, TILE))
    b = _col_iota((TILE, TILE))
    apart = a ^ b
    n_levels = CHUNK.bit_length() - 2
    for lg in range(1, n_levels + 1):
        mask_ref[lg - 1] = jnp.where((apart >> lg) == 1, 1.0, 0.0).astype(BF16)
    for k, pack in enumerate((DN_PACK, 2 * DN_PACK)):
        mask_ref[n_levels + k] = jnp.where(_div_pow2(a, pack) == _div_pow2(b, pack), 1.0, 0.0).astype(BF16)


def _dn_tiles(chains, mask_ref):
    a = _row_iota((TILE, TILE))
    b = _col_iota((TILE, TILE))
    same = _div_pow2(a, CHUNK) == _div_pow2(b, CHUNK)
    apart = a ^ b
    n_chunks = TILE // CHUNK
    dot = functools.partial(jnp.dot, preferred_element_type=F32)

    def stack(v, reps):
        return jnp.concatenate([v] * reps, axis=0)

    kk = [dot(c["k_rm"], c["kt"]) for c in chains]
    kq = [dot(c["k_rm"], c["qt"]) for c in chains]
    xs, pw, attn = [], [], []
    for c, kk_c, kq_c in zip(chains, kk, kq):
        incl = same & ((a >= b) if c["backward"] else (a <= b))
        gcb = jnp.broadcast_to(c["gc"], (TILE, TILE))
        diff = gcb - gcb.T
        decay = jnp.where(incl, jnp.exp(jnp.where(incl, diff, 0.0)), 0.0)
        attn.append((kq_c * decay).astype(BF16))
        x = kk_c * decay * (-c["beta"])
        xs.append(x.astype(BF16))
        base = jnp.where(a == b, 1.0, jnp.where(apart == 1, x, 0.0))
        acc = base[0:DN_PACK, :]
        for r in range(1, TILE // DN_PACK):
            acc = acc + base[r * DN_PACK:(r + 1) * DN_PACK, :]
        pw.append(acc)
    pack = DN_PACK
    n_levels = CHUNK.bit_length() - 2
    block_mask = {DN_PACK: n_levels, 2 * DN_PACK: n_levels + 1}
    s = 2
    while s < CHUNK:
        if s == pack:
            keep = _div_pow2(_row_iota((2 * pack, TILE)), pack) == (_div_pow2(_col_iota((2 * pack, TILE)), pack) & 1)
            pw = [jnp.where(keep, stack(p, 2), 0.0) for p in pw]
            pack *= 2
        couple = mask_ref[s.bit_length() - 2]
        blocks = mask_ref[block_mask[pack]]
        pb = [p.astype(BF16) for p in pw]
        px = [dot(pb_c, x * couple) for pb_c, x in zip(pb, xs)]
        p_bd = [stack(pb_c, TILE // pack) * blocks for pb_c in pb]
        pw = [p + dot(px_c.astype(BF16), bd_c) for p, px_c, bd_c in zip(pw, px, p_bd)]
        s *= 2
    blocks = mask_ref[block_mask[pack]]
    t_inv = [stack(p.astype(BF16), TILE // pack) * blocks for p in pw]
    egc = [jnp.exp(c["gc"]) for c in chains]
    u_t = [dot((c["vt"].astype(F32) * c["beta"]).astype(BF16), t) for c, t in zip(chains, t_inv)]
    w_t = [dot((c["kt"].astype(F32) * (c["beta"] * e)).astype(BF16), t).astype(BF16)
           for c, e, t in zip(chains, egc, t_inv)]
    qd_t = [(c["qt"].astype(F32) * e).astype(BF16) for c, e in zip(chains, egc)]
    kdec = [jnp.exp(c["tot"] - c["gc"]) for c in chains]
    outs = [[None] * n_chunks for _ in chains]
    for step in range(n_chunks):
        cis = [(n_chunks - 1 - step) if c["backward"] else step for c in chains]
        sl = [slice(ci * CHUNK, (ci + 1) * CHUNK) for ci in cis]
        st = [c["st_ref"][...] for c in chains]
        stb = [s_c.astype(BF16) for s_c in st]
        swq = [dot(stb_c, jnp.concatenate([w_c[:, r], q_c[:, r]], axis=1))
               for stb_c, w_c, q_c, r in zip(stb, w_t, qd_t, sl)]
        sq = [v[:, CHUNK:] for v in swq]
        vn = [u_c[:, r] - v[:, :CHUNK] for u_c, v, r in zip(u_t, swq, sl)]
        av = [dot(vn_c.astype(BF16), at_c[r, r]) for vn_c, at_c, r in zip(vn, attn, sl)]
        upd = [dot((vn_c * kd_c[:, r]).astype(BF16), c["k_rm"][r, :])
               for vn_c, kd_c, c, r in zip(vn, kdec, chains, sl)]
        for idx, c in enumerate(chains):
            c["st_ref"][...] = st[idx] * jnp.exp(c["tot"][:, sl[idx]]) + upd[idx]
            outs[idx][cis[idx]] = sq[idx] + av[idx]
    return [jnp.concatenate(o, axis=1) for o in outs]


def _dn_kernel(k_ref, qt_ref, kt_ref, vt_ref, g_ref, s0f_ref, s0b_ref,
               o_ref, sf_ref, sb_ref, ot_ref, st_ref, mask_ref, *, n_tiles, with_output):
    hb = DN_HEADS_PER_STEP
    _dn_masks(mask_ref)
    for hh in range(hb):
        st_ref[2 * hh] = s0f_ref[0, hh]
        st_ref[2 * hh + 1] = s0b_ref[0, hh]

    def body(i, carry):
        nf = i
        nb = n_tiles - 1 - i
        chains = []
        for hh in range(hb):
            for backward, n in ((False, nf), (True, nb)):
                g = g_ref[0, n, hh * SUBLANE:(hh + 1) * SUBLANE, :]
                base = 2 if backward else 0
                chains.append(dict(
                    k_rm=k_ref[0, pl.ds(pl.multiple_of(n * TILE, TILE), TILE), hh * LANE:(hh + 1) * LANE],
                    qt=qt_ref[0, hh, n], kt=kt_ref[0, hh, n], vt=vt_ref[0, hh, n],
                    gc=g[base:base + 1, :], beta=g[base + 1:base + 2, :], tot=g[4 + base // 2:5 + base // 2, :],
                    st_ref=st_ref.at[2 * hh + (1 if backward else 0)], backward=backward, hh=hh, n=n))
        o_t = _dn_tiles(chains, mask_ref)
        if with_output:
            for c, o_c in zip(chains, o_t):
                ot_ref[c["hh"], c["n"]] = ot_ref[c["hh"], c["n"]] + o_c
        return carry

    if with_output:
        ot_ref[...] = jnp.zeros_like(ot_ref)
    lax.fori_loop(0, n_tiles, body, 0)
    for hh in range(hb):
        sf_ref[0, hh] = st_ref[2 * hh]
        sb_ref[0, hh] = st_ref[2 * hh + 1]
    if with_output:
        def finish(t, carry):
            rows = pl.ds(pl.multiple_of(t * TILE, TILE), TILE)
            for hh in range(hb):
                o_ref[0, rows, hh * LANE:(hh + 1) * LANE] = ot_ref[hh, t].T.astype(BF16)
            return carry

        lax.fori_loop(0, n_tiles, finish, 0)
    else:
        o_ref[...] = jnp.zeros_like(o_ref)


def _delta_net(qkv_rm, qkv_tr, gates, s0f, s0b, with_output):
    bsz, n, _ = qkv_rm.shape
    nt = n // TILE
    h = DN_HEADS
    hb = DN_HEADS_PER_STEP
    wide = hb * LANE
    once = pl.Buffered(1)
    tr_spec = lambda off: pl.BlockSpec((1, hb, nt, LANE, TILE), lambda b, j: (b, off + j, 0, 0, 0))
    st_spec = pl.BlockSpec((1, hb, LANE, LANE), lambda b, j: (b, j, 0, 0))
    n_out = n if with_output else SUBLANE
    return pl.pallas_call(
        functools.partial(_dn_kernel, n_tiles=nt, with_output=with_output),
        grid=(bsz, h // hb),
        in_specs=[pl.BlockSpec((1, n, wide), lambda b, j: (b, 0, h // hb + j), once),
                  tr_spec(0), tr_spec(h // hb), tr_spec(2 * h // hb),
                  pl.BlockSpec((1, nt, hb * SUBLANE, TILE), lambda b, j: (b, 0, j, 0)),
                  st_spec, st_spec],
        out_specs=[pl.BlockSpec((1, n_out, wide), lambda b, j: (b, 0, j)), st_spec, st_spec],
        out_shape=[jax.ShapeDtypeStruct((bsz, n_out, DN_WIDTH), BF16),
                   jax.ShapeDtypeStruct((bsz, h, LANE, LANE), F32),
                   jax.ShapeDtypeStruct((bsz, h, LANE, LANE), F32)],
        scratch_shapes=[pltpu.VMEM((hb, nt, LANE, TILE), F32),
                        pltpu.VMEM((2 * hb, LANE, LANE), F32),
                        pltpu.VMEM((CHUNK.bit_length(), TILE, TILE), BF16)],
        compiler_params=_cparams(("parallel", "parallel"), DN_VMEM_LIMIT_MIB),
        name="delta_net",
    )(qkv_rm, qkv_tr, qkv_tr, qkv_tr, gates, s0f, s0b)


def _lru_scan_block(x, wa, ba, wx, bx, spl, h_in, backward):
    rows, width = x.shape
    xb = x.astype(BF16)

    def gate(ws, bias):
        parts = [jnp.dot(xb[:, k * LRU_BLOCK:(k + 1) * LRU_BLOCK], w, preferred_element_type=F32)
                 for k, w in enumerate(ws)]
        return _sigmoid(jnp.concatenate(parts, axis=1) + bias)

    r = gate(wa, ba)
    gi = gate(wx, bx)
    log_a = -LRU_C * r * spl
    a = jnp.exp(log_a)
    b = jnp.sqrt(1.0 - jnp.exp(2.0 * log_a)) * (gi * x)
    groups = rows // SUBLANE
    a = a.reshape(groups, SUBLANE, width)
    b = b.reshape(groups, SUBLANE, width)
    sub = lax.broadcasted_iota(jnp.int32, a.shape, 1)
    s = 1
    while s < SUBLANE:
        if backward:
            keep = sub < SUBLANE - s
            a_sh = jnp.where(keep, pltpu.roll(a, SUBLANE - s, 1), 1.0)
            b_sh = jnp.where(keep, pltpu.roll(b, SUBLANE - s, 1), 0.0)
        else:
            keep = sub >= s
            a_sh = jnp.where(keep, pltpu.roll(a, s, 1), 1.0)
            b_sh = jnp.where(keep, pltpu.roll(b, s, 1), 0.0)
        b = a * b_sh + b
        a = a * a_sh
        s *= 2
    a = a.reshape(rows, width)
    b = b.reshape(rows, width)
    pieces = [None] * groups
    carry = h_in
    order = range(groups - 1, -1, -1) if backward else range(groups)
    edge = 0 if backward else SUBLANE - 1
    for gidx in order:
        lo = gidx * SUBLANE
        hgrp = b[lo:lo + SUBLANE, :] + a[lo:lo + SUBLANE, :] * carry
        pieces[gidx] = hgrp
        carry = hgrp[edge:edge + 1, :]
    return jnp.concatenate(pieces, axis=0), carry


def _lru_kernel(px_ref, py_ref, cw_ref, cb_ref, wa_ref, ba_ref, wx_ref, bx_ref, lam_ref, h0_ref,
                o_ref, last_ref, xs_ref, hs_ref, *, n_blocks, with_output):
    x = px_ref[0].astype(F32)
    xs_ref[...] = _dwconv_rows(x, cw_ref, 4) + cb_ref[...]
    spl = _softplus(-lam_ref[...])
    wa = [[wa_ref[d, k] for k in range(LRU_GROUP)] for d in range(2)]
    wx = [[wx_ref[d, k] for k in range(LRU_GROUP)] for d in range(2)]

    def body(i, carry):
        hf, hb = carry
        rf = pl.multiple_of(i * LRU_ROWS, LRU_ROWS)
        rb = pl.multiple_of((n_blocks - 1 - i) * LRU_ROWS, LRU_ROWS)
        h_f, hf = _lru_scan_block(xs_ref[pl.ds(rf, LRU_ROWS), :], wa[0], ba_ref[0:1, :],
                                  wx[0], bx_ref[0:1, :], spl[0:1, :], hf, False)
        h_b, hb = _lru_scan_block(xs_ref[pl.ds(rb, LRU_ROWS), :], wa[1], ba_ref[1:2, :],
                                  wx[1], bx_ref[1:2, :], spl[1:2, :], hb, True)
        if with_output:
            hs_ref[pl.ds(rf, LRU_ROWS), :] = hs_ref[pl.ds(rf, LRU_ROWS), :] + h_f
            hs_ref[pl.ds(rb, LRU_ROWS), :] = hs_ref[pl.ds(rb, LRU_ROWS), :] + h_b
        return hf, hb

    if with_output:
        hs_ref[...] = jnp.zeros_like(hs_ref)
    h0 = h0_ref[0]
    hf, hb = lax.fori_loop(0, n_blocks, body, (h0[0:1, :], h0[1:2, :]))
    last_ref[0] = jnp.concatenate([hf, hb], axis=0)
    if with_output:
        y = py_ref[0].astype(F32)
        gelu = 0.5 * y * (1.0 + jnp.tanh(math.sqrt(2.0 / math.pi) * (y + 0.044715 * (y * y * y))))
        o_ref[0] = (hs_ref[...] * gelu).astype(BF16)
    else:
        o_ref[...] = jnp.zeros_like(o_ref)


def _rglru(proj, conv_w, conv_b, w_a, b_a, w_x, b_x, lam, h0, with_output):
    bsz, n, _ = proj.shape
    rows = min(n, LRU_ROWS)
    nb = n // rows
    n_out = n if with_output else SUBLANE
    width = LRU_GROUP * LRU_BLOCK
    lx = LX_BLK * LANE // width
    ly = LY_BLK * LANE // width
    vec2 = pl.BlockSpec((2, width), lambda b, j: (0, j))
    wspec = pl.BlockSpec((2, LRU_GROUP, LRU_BLOCK, LRU_BLOCK), lambda b, j: (0, j, 0, 0))
    return pl.pallas_call(
        functools.partial(_lru_kernel, n_blocks=nb, with_output=with_output),
        grid=(bsz, LRU_BLOCKS // LRU_GROUP),
        in_specs=[pl.BlockSpec((1, n, width), lambda b, j: (b, 0, lx + j)),
                  pl.BlockSpec((1, n, width), lambda b, j: (b, 0, ly + j)),
                  pl.BlockSpec((4, width), lambda b, j: (0, j)),
                  pl.BlockSpec((1, width), lambda b, j: (0, j)),
                  wspec, vec2, wspec, vec2, vec2,
                  pl.BlockSpec((1, 2, width), lambda b, j: (b, 0, j))],
        out_specs=[pl.BlockSpec((1, n_out, width), lambda b, j: (b, 0, j)),
                   pl.BlockSpec((1, 2, width), lambda b, j: (b, 0, j))],
        out_shape=[jax.ShapeDtypeStruct((bsz, n_out, LRU_WIDTH), BF16),
                   jax.ShapeDtypeStruct((bsz, 2, LRU_WIDTH), F32)],
        scratch_shapes=[pltpu.VMEM((n, width), F32), pltpu.VMEM((n, width), F32)],
        compiler_params=_cparams(("parallel", "parallel")),
        name="rglru",
    )(proj, proj, conv_w, conv_b.reshape(1, LRU_WIDTH), w_a, b_a, w_x, b_x, lam, h0)


def _hy_prep_kernel(p0_ref, p1_ref, pv_ref, w0_ref, w1_ref, wv_ref, b0_ref, b1_ref, bv_ref, x0_ref, zz_ref):
    x0 = _dwconv_rows(p0_ref[0].astype(F32), w0_ref, 3) + b0_ref[...]
    x1 = _dwconv_rows(p1_ref[0].astype(F32), w1_ref, 3) + b1_ref[...]
    v = _dwconv_rows(pv_ref[0].astype(F32), wv_ref, 3) + bv_ref[...]
    x0_ref[0] = x0.astype(BF16)
    zz_ref[0] = (x1 * v).astype(BF16)


def _hy_prep(proj, conv_w, conv_b):
    bsz, n, _ = proj.shape
    nblk = HY_WIDTH // LANE
    pspec = lambda off: pl.BlockSpec((1, n, LANE), lambda b, j: (b, 0, HY_BLK + off + j))
    wspec = lambda off: pl.BlockSpec((3, LANE), lambda b, j: (0, off + j))
    bspec = lambda off: pl.BlockSpec((1, LANE), lambda b, j: (0, off + j))
    ospec = pl.BlockSpec((1, n, LANE), lambda b, j: (b, 0, j))
    cb = conv_b.reshape(1, 3 * HY_WIDTH)
    return pl.pallas_call(
        _hy_prep_kernel,
        grid=(bsz, nblk),
        in_specs=[pspec(0), pspec(nblk), pspec(2 * nblk), wspec(0), wspec(nblk), wspec(2 * nblk),
                  bspec(0), bspec(nblk), bspec(2 * nblk)],
        out_specs=[ospec, ospec],
        out_shape=[jax.ShapeDtypeStruct((bsz, n, HY_WIDTH), BF16)] * 2,
        compiler_params=_cparams(("parallel", "parallel")),
        name="hy_prep",
    )(proj, proj, proj, conv_w, conv_w, conv_w, cb, cb, cb)


def _hy_filter_kernel(w1_ref, b1_ref, f1_ref, w2_ref, b2_ref, f2_ref, w3_ref, band_ref, delta_ref,
                      k_ref, *, n, rows):
    def lag(shape):
        p = _row_iota(shape) + (pl.program_id(0) * rows - n)
        return p, jnp.abs(p).astype(F32)

    _, i = lag((rows, LANE))
    lane = _col_iota((rows, LANE))
    t = i * (1.0 / (n - 1))
    ang = band_ref[...] * (i * (2.0 * math.pi / n))
    feat = jnp.where(lane == 0, t,
                     jnp.where(lane <= HY_BANDS, jnp.cos(ang), jnp.where(lane < HY_EMB, -jnp.sin(ang), 0.0)))
    hid = jnp.sin(f1_ref[...] * (jnp.dot(feat, w1_ref[...], precision=HIGHEST,
                                         preferred_element_type=F32) + b1_ref[...]))
    hid = jnp.sin(f2_ref[...] * (jnp.dot(hid, w2_ref[...], precision=HIGHEST,
                                         preferred_element_type=F32) + b2_ref[...]))
    filt = _bdot(hid, w3_ref[...])
    p, iw = lag((rows, HY_WIDTH))
    dec = jnp.exp(-(iw * (1.0 / (n - 1))) * delta_ref[...])
    h_f = filt[:, :HY_WIDTH] * dec
    h_b = filt[:, HY_WIDTH:] * dec
    taps = jnp.where(p > 0, h_f, jnp.where(p < 0, h_b, h_f + h_b))
    k_ref[...] = jnp.where(p == -n, 0.0, taps).astype(BF16)


def _hy_filter(n, w1, b1, f1, w2, b2, f2, w3):
    rows = min(n, 256)
    hid = HY_FILTER_HIDDEN
    w1p = jnp.zeros((LANE, hid), F32).at[:HY_EMB].set(w1)
    bands = np.zeros((1, LANE), np.float32)
    base = np.linspace(1e-4, HY_BANDS - 1, HY_BANDS, dtype=np.float32)
    bands[0, 1:1 + HY_BANDS] = base
    bands[0, 1 + HY_BANDS:HY_EMB] = base
    log_target = math.log(HY_DECAY_TARGET)
    deltas = np.abs(np.linspace(log_target / HY_SLOW_DECAY_PCT, log_target / HY_FAST_DECAY_PCT, HY_WIDTH,
                                dtype=np.float32)).reshape(1, HY_WIDTH)
    full = lambda shape: pl.BlockSpec(shape, lambda i: (0,) * len(shape))
    return pl.pallas_call(
        functools.partial(_hy_filter_kernel, n=n, rows=rows),
        grid=(2 * n // rows,),
        in_specs=[full((LANE, hid)), full((1, hid)), full((1, hid)), full((hid, hid)), full((1, hid)),
                  full((1, hid)), full((hid, 2 * HY_WIDTH)), full((1, LANE)), full((1, HY_WIDTH))],
        out_specs=pl.BlockSpec((rows, HY_WIDTH), lambda i: (i, 0)),
        out_shape=jax.ShapeDtypeStruct((2 * n, HY_WIDTH), BF16),
        compiler_params=_cparams(("parallel",)),
        name="hy_filter",
    )(w1p, b1.reshape(1, hid), f1.reshape(1, hid), w2, b2.reshape(1, hid), f2.reshape(1, hid), w3,
      jnp.asarray(bands), jnp.asarray(deltas))


DFT_GROUP = 64


def _dft_table_kernel(c_ref, s_ref, cb_ref, sb_ref, *, m, ncols, col0, rows, transposed, blank_first):
    period = 4 * m
    scale = 2.0 * math.pi / period
    col = _col_iota((1, ncols)) + col0

    @pl.when(pl.program_id(0) == 0)
    def _():
        r2 = _row_iota((DFT_GROUP, ncols))
        c2 = _col_iota((DFT_GROUP, ncols)) + col0
        ph = (r2 * (2 * c2 + 1)) if transposed else ((2 * r2 + 1) * c2)
        ang = (ph & (period - 1)).astype(F32) * scale
        cb_ref[...] = jnp.cos(ang)
        sb_ref[...] = jnp.sin(ang)

    for g in range(rows // DFT_GROUP):
        r1 = pl.program_id(0) * (rows // DFT_GROUP) + g
        ph = (DFT_GROUP * r1) * (2 * col + 1) if transposed else (2 * DFT_GROUP * r1) * col
        ang = (ph & (period - 1)).astype(F32) * scale
        ca = jnp.cos(ang)
        sa = jnp.sin(ang)
        cb = cb_ref[...]
        sb = sb_ref[...]
        c_tile = ca * cb - sa * sb
        s_tile = sa * cb + ca * sb
        if blank_first:
            first = _col_iota((DFT_GROUP, ncols)) == 0
            c_tile = jnp.where(first, 0.0, c_tile)
            s_tile = jnp.where(first, 0.0, s_tile)
        c_ref[g * DFT_GROUP:(g + 1) * DFT_GROUP, :] = c_tile.astype(BF16)
        s_ref[g * DFT_GROUP:(g + 1) * DFT_GROUP, :] = s_tile.astype(BF16)


def _dft_tables(m, transposed=False, two_sided=False):
    rows = min(m, 256)
    ncols = 2 * m if two_sided else m
    spec = pl.BlockSpec((rows, ncols), lambda i: (i, 0))
    return pl.pallas_call(
        functools.partial(_dft_table_kernel, m=m, ncols=ncols, col0=3 * m if two_sided else 0, rows=rows,
                          transposed=transposed, blank_first=two_sided),
        grid=(m // rows,),
        in_specs=[],
        out_specs=[spec, spec],
        out_shape=[jax.ShapeDtypeStruct((m, ncols), BF16)] * 2,
        scratch_shapes=[pltpu.VMEM((DFT_GROUP, ncols), F32), pltpu.VMEM((DFT_GROUP, ncols), F32)],
        compiler_params=_cparams(("arbitrary",)),
        name="dft_tables",
    )()


def _hy_tables(m):
    return _dft_tables(m) + _dft_tables(m, transposed=True) + _dft_tables(m, two_sided=True)


def _hy_ktrans_kernel(cw_ref, sw_ref, lo_ref, hi_ref, kc_ref, ks_ref, *, m):
    dot = functools.partial(jnp.dot, preferred_element_type=F32)
    lo = lo_ref[...]
    hi = hi_ref[...]
    kc_ref[0] = dot(cw_ref[:, :m], lo) + dot(cw_ref[:, m:], hi)
    ks_ref[0] = dot(sw_ref[:, :m], lo) + dot(sw_ref[:, m:], hi)


def _hy_ktrans(cw, sw, taps, m):
    nd = taps.shape[0] // m - 1
    w = taps.shape[1]
    tn = 512
    tab = pl.BlockSpec((m, 2 * m), lambda e, j: (0, 0))
    ospec = pl.BlockSpec((1, m, tn), lambda e, j: (e, 0, j))
    return pl.pallas_call(
        functools.partial(_hy_ktrans_kernel, m=m),
        grid=(nd, w // tn),
        in_specs=[tab, tab, pl.BlockSpec((m, tn), lambda e, j: (e, j)), pl.BlockSpec((m, tn), lambda e, j: (e + 1, j))],
        out_specs=[ospec, ospec],
        out_shape=[jax.ShapeDtypeStruct((nd, m, w), F32)] * 2,
        compiler_params=_cparams(("parallel", "parallel")),
        name="hy_ktrans",
    )(cw, sw, taps, taps)


def _hy_fwd_kernel(c_ref, s_ref, zz_ref, kc_ref, ks_ref, a_ref, b_ref, *, m, nb):
    dot = functools.partial(jnp.dot, preferred_element_type=F32)
    c = c_ref[...]
    s = s_ref[...]
    acc_a = [None] * nb
    acc_b = [None] * nb
    for j in range(nb):
        zz = zz_ref[0, j * m:(j + 1) * m, :]
        uc = dot(c, zz)
        us = dot(s, zz)
        for i in range(nb):
            kc = kc_ref[i - j + nb - 1]
            ks = ks_ref[i - j + nb - 1]
            ta = uc * kc - us * ks
            tb = uc * ks + us * kc
            acc_a[i] = ta if acc_a[i] is None else acc_a[i] + ta
            acc_b[i] = tb if acc_b[i] is None else acc_b[i] + tb
    for i in range(nb):
        a_ref[0, i] = acc_a[i].astype(BF16)
        b_ref[0, i] = acc_b[i].astype(BF16)


def _hy_forward(ctab, stab, zz, kc, ks, m):
    bsz, n, w = zz.shape
    nb = n // m
    nd = 2 * nb - 1
    tf = min(m, 256)
    tn = 512
    tab = pl.BlockSpec((tf, m), lambda b, j, i: (i, 0))
    kspec = pl.BlockSpec((nd, tf, tn), lambda b, j, i: (0, i, j))
    ospec = pl.BlockSpec((1, nb, tf, tn), lambda b, j, i: (b, 0, i, j))
    return pl.pallas_call(
        functools.partial(_hy_fwd_kernel, m=m, nb=nb),
        grid=(bsz, w // tn, m // tf),
        in_specs=[tab, tab, pl.BlockSpec((1, n, tn), lambda b, j, i: (b, 0, j)), kspec, kspec],
        out_specs=[ospec, ospec],
        out_shape=[jax.ShapeDtypeStruct((bsz, nb, m, w), BF16)] * 2,
        compiler_params=_cparams(("parallel", "parallel", "parallel")),
        name="hy_forward",
    )(ctab, stab, zz, kc, ks)


def _hy_inv_kernel(ct_ref, st_ref, a_ref, b_ref, x0_ref, zz_ref, bias_ref, o_ref, *, m):
    y = (jnp.dot(ct_ref[...], a_ref[0, 0], preferred_element_type=F32)
         + jnp.dot(st_ref[...], b_ref[0, 0], preferred_element_type=F32)) * (1.0 / m)
    zz = zz_ref[0].astype(F32)
    o_ref[0] = (x0_ref[0].astype(F32) * (y + zz * bias_ref[...])).astype(BF16)


def _hy_inverse(cttab, sttab, a, bq, x0, zz, bias, m):
    bsz, n, w = zz.shape
    nb = n // m
    tn = 512
    tab = pl.BlockSpec((m, m), lambda b, i, j: (0, 0))
    spec4 = pl.BlockSpec((1, 1, m, tn), lambda b, i, j: (b, i, 0, j))
    tile = pl.BlockSpec((1, m, tn), lambda b, i, j: (b, i, j))
    return pl.pallas_call(
        functools.partial(_hy_inv_kernel, m=m),
        grid=(bsz, nb, w // tn),
        in_specs=[tab, tab, spec4, spec4, tile, tile, pl.BlockSpec((1, tn), lambda b, i, j: (0, j))],
        out_specs=tile,
        out_shape=jax.ShapeDtypeStruct((bsz, n, w), BF16),
        compiler_params=_cparams(("parallel", "parallel", "parallel")),
        name="hy_inverse",
    )(cttab, sttab, a, bq, x0, zz, bias.reshape(1, w))


def _merge_kernel(odn_ref, z_ref, ng_ref, ohy_ref, olru_ref, gdn_ref, ghy_ref, glru_ref, wdn_ref, why_ref,
                  wlru_ref, wout_ref, x_ref, gate_ref, o_ref):
    heads = []
    for h in range(DN_HEADS):
        o = odn_ref[0, :, h * LANE:(h + 1) * LANE].astype(F32)
        y = o * lax.rsqrt(jnp.mean(o * o, axis=-1, keepdims=True) + NORM_EPS) * ng_ref[...]
        heads.append((y * _silu(z_ref[0, :, h * LANE:(h + 1) * LANE].astype(F32))).astype(BF16))
    odn = jnp.concatenate(heads, axis=1)
    m = _sigmoid(gdn_ref[0].astype(F32)) * jnp.dot(odn, wdn_ref[...], preferred_element_type=F32)
    m = m + _sigmoid(ghy_ref[0].astype(F32)) * jnp.dot(ohy_ref[0], why_ref[...], preferred_element_type=F32)
    m = m + _sigmoid(glru_ref[0].astype(F32)) * jnp.dot(olru_ref[0], wlru_ref[...], preferred_element_type=F32)
    y = jnp.dot(m.astype(BF16), wout_ref[...], preferred_element_type=F32)
    o_ref[0] = x_ref[0] + gate_ref[0] * y


def _merge(o_dn, dn_norm_g, o_hy, o_lru, proj, w_dn, w_hy, w_lru, w_out, x, gate):
    bsz, n, d = x.shape
    tm = min(n, 256)
    nblk = d // 1024
    act = pl.BlockSpec((1, tm, d), lambda b, i: (b, i, 0))
    gspec = lambda k: pl.BlockSpec((1, tm, d), lambda b, i: (b, i, GATE_BLK * LANE // d + k * nblk))
    wspec = pl.BlockSpec((d, d), lambda b, i: (0, 0))
    return pl.pallas_call(
        _merge_kernel,
        grid=(bsz, n // tm),
        in_specs=[act, pl.BlockSpec((1, tm, d), lambda b, i: (b, i, Z_BLK * LANE // d)),
                  pl.BlockSpec((1, LANE), lambda b, i: (0, 0)),
                  act, act, gspec(0), gspec(1), gspec(2), wspec, wspec, wspec, wspec, act,
                  pl.BlockSpec((1, 1, d), lambda b, i: (b, 0, 0))],
        out_specs=act,
        out_shape=jax.ShapeDtypeStruct((bsz, n, d), F32),
        compiler_params=_cparams(("parallel", "parallel")),
        name="merge",
    )(o_dn, proj, dn_norm_g.reshape(1, LANE), o_hy, o_lru, proj, proj, proj, w_dn, w_hy, w_lru, w_out, x, gate)


def _ffn_act_kernel(ug_ref, uv_ref, wg_ref, wv_ref, o_ref, up_ref, mid_ref, dn_ref, *, rows, cols, n):
    tile = FFN_TILE
    a = _row_iota((tile, tile))
    b = _col_iota((tile, tile))
    c = _mod_pow2(a, cols)
    lmat = jnp.where((b == a - 1) & (c >= 1), 1.0, 0.0).astype(BF16)
    rmat = jnp.where((b == a + 1) & (c <= cols - 2), 1.0, 0.0).astype(BF16)
    taps = (-1, 0, 1) if rows > 1 else (0,)
    dst = {-1: up_ref, 0: mid_ref, 1: dn_ref}
    if rows > 1:
        pad = jnp.zeros((2, cols, FFN_WIDE), BF16)
        up_ref[:, 0:cols, :] = pad
        dn_ref[:, n + cols:n + 2 * cols, :] = pad

    def taps_of(i, carry):
        r0 = pl.multiple_of(i * tile, tile)
        for idx, (u_ref, w_ref) in enumerate(((ug_ref, wg_ref), (uv_ref, wv_ref))):
            u = u_ref[0, pl.ds(r0, tile), :]
            left = jnp.dot(lmat, u, preferred_element_type=F32).astype(BF16)
            right = jnp.dot(rmat, u, preferred_element_type=F32).astype(BF16)
            for di in taps:
                k = 3 * (di + 1)
                w = [w_ref[k + j:k + j + 1, :].astype(BF16) for j in range(3)]
                off = pl.multiple_of(r0 + (cols if di != 0 else 0), cols)
                dst[di][idx, pl.ds(off, tile), :] = left * w[0] + u * w[1] + right * w[2]
        return carry

    def combine(i, carry):
        r0 = pl.multiple_of(i * tile, tile)
        vals = []
        for idx in range(2):
            acc = mid_ref[idx, pl.ds(r0, tile), :]
            if rows > 1:
                acc = (acc + up_ref[idx, pl.ds(r0, tile), :]
                       + dn_ref[idx, pl.ds(pl.multiple_of(r0 + 2 * cols, cols), tile), :])
            vals.append(acc.astype(F32))
        o_ref[0, pl.ds(r0, tile), :] = (_silu(vals[0]) * vals[1]).astype(BF16)
        return carry

    lax.fori_loop(0, n // tile, taps_of, 0, unroll=min(2, n // tile))
    lax.fori_loop(0, n // tile, combine, 0)


def _ffn_act(u, conv_w, rows, cols):
    bsz, n, _ = u.shape
    nblk = FFN_HIDDEN // FFN_WIDE
    padded = n + 2 * cols if rows > 1 else SUBLANE * 2
    return pl.pallas_call(
        functools.partial(_ffn_act_kernel, rows=rows, cols=cols, n=n),
        grid=(bsz, nblk),
        in_specs=[pl.BlockSpec((1, n, FFN_WIDE), lambda b, j: (b, 0, j)),
                  pl.BlockSpec((1, n, FFN_WIDE), lambda b, j: (b, 0, nblk + j)),
                  pl.BlockSpec((9, FFN_WIDE), lambda b, j: (0, j)),
                  pl.BlockSpec((9, FFN_WIDE), lambda b, j: (0, nblk + j))],
        out_specs=pl.BlockSpec((1, n, FFN_WIDE), lambda b, j: (b, 0, j)),
        out_shape=jax.ShapeDtypeStruct((bsz, n, FFN_HIDDEN), BF16),
        scratch_shapes=[pltpu.VMEM((2, padded, FFN_WIDE), BF16), pltpu.VMEM((2, n, FFN_WIDE), BF16),
                        pltpu.VMEM((2, padded, FFN_WIDE), BF16)],
        compiler_params=_cparams(("parallel", "parallel")),
        name="ffn_act",
    )(u, u, conv_w, conv_w)


def _gate_column_map():
    src = np.full((LANE,), -1, np.int32)
    isdec = np.zeros((1, LANE), np.float32)
    dirs = np.zeros((LANE,), np.int32)
    for h in range(DN_HEADS):
        for slot, (d, kind) in enumerate(((0, 0), (0, 1), (1, 0), (1, 1), (0, 0), (1, 0))):
            src[h * SUBLANE + slot] = d * 2 * DN_HEADS + kind * DN_HEADS + h
            isdec[0, h * SUBLANE + slot] = 1.0 if kind == 0 else 0.0
            dirs[h * SUBLANE + slot] = d
    return src, isdec, dirs


def _split_in_proj(w_in):
    o = np.cumsum((3 * DN_WIDTH, DN_WIDTH, 4 * DN_HEADS, 3 * HY_WIDTH, LRU_WIDTH, LRU_WIDTH)).tolist()
    w_main = jnp.concatenate([w_in[:, :o[1]], w_in[:, o[2]:]], axis=1).astype(BF16)
    src, _, _ = _gate_column_map()
    w_ab = w_in[:, o[1]:o[2]]
    w_gate = jnp.where(jnp.asarray(src >= 0)[None, :], w_ab[:, np.maximum(src, 0)], 0.0).astype(BF16)
    return w_main, w_gate


def _gate_params(a_log, dt_bias):
    _, isdec, dirs = _gate_column_map()
    head = (np.arange(LANE) // SUBLANE).astype(np.int32)
    alog_c = a_log[dirs, head].reshape(1, LANE)
    dtb_c = dt_bias[dirs, head].reshape(1, LANE)
    return alog_c, dtb_c, jnp.asarray(isdec)


def _token_views(bsz, n, mods, shared_mod):
    if not shared_mod:
        same = lambda t: t
        return same, same, mods
    flat = lambda t: t.reshape(1, bsz * n, t.shape[-1])
    unflat = lambda t: t.reshape(bsz, n, t.shape[-1])
    return flat, unflat, [m[:1] for m in mods]


def _token_mixer(x, mods, lp, states, tables, with_output, shared_mod):
    bsz, n, _ = x.shape
    flat, unflat, tmods = _token_views(bsz, n, mods, shared_mod)
    proj, ab = _norm_mod_matmul(flat(x), lp["norm1_g"], tmods[0], tmods[1], lp["w_main"], BF16, 2048, 1024,
                                lp["w_gate"])
    proj, ab = unflat(proj), unflat(ab)

    qkv_rm, qkv_tr = _dn_prep(proj, lp["dn_conv_w"])
    gates = _dn_gates(ab, *lp["gate_params"])
    o_dn, s_f, s_b = _delta_net(qkv_rm, qkv_tr, gates, states[0], states[1], with_output)
    o_lru, h_last = _rglru(proj, lp["lru_conv_w"], lp["lru_conv_b"], lp["lru_w_a"], lp["lru_b_a"],
                           lp["lru_w_x"], lp["lru_b_x"], lp["lru_lambda"], states[2], with_output)
    new_states = (s_f, s_b, h_last)
    if not with_output:
        return None, new_states

    m = min(n, HY_BLOCK)
    ctab, stab, cttab, sttab, cwtab, swtab = tables
    taps = _hy_filter(n, lp["hy_w1"], lp["hy_b1"], lp["hy_f1"], lp["hy_w2"], lp["hy_b2"], lp["hy_f2"], lp["hy_w3"])
    kc, ks = _hy_ktrans(cwtab, swtab, taps, m)
    x0, zz = _hy_prep(proj, lp["hy_conv_w"], lp["hy_conv_b"])
    a, bq = _hy_forward(ctab, stab, zz, kc, ks, m)
    o_hy = _hy_inverse(cttab, sttab, a, bq, x0, zz, lp["hy_bias"], m)

    x = _merge(flat(o_dn), lp["dn_norm_g"], flat(o_hy), flat(o_lru), flat(proj), lp["w_proj_dn"], lp["w_proj_hy"], lp["w_proj_lru"],
               lp["w_out"], flat(x), tmods[2])
    return unflat(x), new_states


def _conv_ffn(x, mods, lp, rows, cols, shared_mod, final_gain=None):
    bsz, n, _ = x.shape
    flat, unflat, tmods = _token_views(bsz, n, mods, shared_mod)
    u = _norm_mod_matmul(flat(x), lp["norm2_g"], tmods[3], tmods[4], lp["ffn_up"], BF16, 1024, FFN_HIDDEN)
    act = _ffn_act(unflat(u), lp["ffn_conv_w"], rows, cols)
    return unflat(_matmul_residual(flat(act), lp["ffn_down"], flat(x), tmods[5], final_gain))


def kernel(x, c, ctx, c_ctx, w_mod, b_mod, norm1_g, norm2_g, w_in, dn_conv_w, dn_a_log, dn_dt_bias, dn_norm_g,
           hy_conv_w, hy_conv_b, hy_w1, hy_b1, hy_f1, hy_w2, hy_b2, hy_f2, hy_w3, hy_bias,
           lru_conv_w, lru_conv_b, lru_w_a, lru_b_a, lru_w_x, lru_b_x, lru_lambda,
           w_proj_dn, w_proj_hy, w_proj_lru, w_out, ffn_up, ffn_conv_w, ffn_down, final_norm_g):
    bsz, n_lat, d = x.shape
    n_ctx = ctx.shape[1]
    depth = w_in.shape[0]
    rows = n_lat // GRID_W

    cvec = jnp.zeros((SUBLANE, d), F32).at[:bsz].set(c).at[bsz].set(c_ctx)
    lat_tables = _hy_tables(min(n_lat, HY_BLOCK))
    ctx_tables = _hy_tables(min(n_ctx, HY_BLOCK))
    zero_states = (jnp.zeros((bsz, DN_HEADS, LANE, LANE), F32), jnp.zeros((bsz, DN_HEADS, LANE, LANE), F32),
                   jnp.zeros((bsz, 2, LRU_WIDTH), F32))

    xc = ctx
    for l in range(depth):
        ctx_needed = l < depth - 1
        w_main, w_gate = _split_in_proj(w_in[l])
        lp = dict(
            norm1_g=norm1_g[l], norm2_g=norm2_g[l], w_main=w_main, w_gate=w_gate,
            dn_conv_w=dn_conv_w[l], gate_params=_gate_params(dn_a_log[l], dn_dt_bias[l]), dn_norm_g=dn_norm_g[l],
            hy_conv_w=hy_conv_w[l], hy_conv_b=hy_conv_b[l], hy_w1=hy_w1[l], hy_b1=hy_b1[l], hy_f1=hy_f1[l],
            hy_w2=hy_w2[l], hy_b2=hy_b2[l], hy_f2=hy_f2[l], hy_w3=hy_w3[l], hy_bias=hy_bias[l],
            lru_conv_w=lru_conv_w[l], lru_conv_b=lru_conv_b[l], lru_w_a=lru_w_a[l].astype(BF16),
            lru_b_a=lru_b_a[l], lru_w_x=lru_w_x[l].astype(BF16), lru_b_x=lru_b_x[l], lru_lambda=lru_lambda[l],
            w_proj_dn=w_proj_dn[l].astype(BF16), w_proj_hy=w_proj_hy[l].astype(BF16),
            w_proj_lru=w_proj_lru[l].astype(BF16), w_out=w_out[l].astype(BF16),
            ffn_up=ffn_up[l].astype(BF16), ffn_conv_w=ffn_conv_w[l].reshape(9, 2 * FFN_HIDDEN),
            ffn_down=ffn_down[l].astype(BF16))
        mod = _modulation(cvec, w_mod[l].astype(BF16), b_mod[l])
        lat_mod = [mod[:bsz, k * d:(k + 1) * d].reshape(bsz, 1, d) for k in range(N_MOD)]
        ctx_mod = [jnp.broadcast_to(mod[bsz:bsz + 1, k * d:(k + 1) * d].reshape(1, 1, d), (bsz, 1, d))
                   for k in range(N_MOD)]

        xc_new, ctx_states = _token_mixer(xc, ctx_mod, lp, zero_states, ctx_tables, ctx_needed, True)
        x, _ = _token_mixer(x, lat_mod, lp, ctx_states, lat_tables, True, False)
        x = _conv_ffn(x, lat_mod, lp, rows, GRID_W, False, None if ctx_needed else final_norm_g)
        if ctx_needed:
            xc = _conv_ffn(xc_new, ctx_mod, lp, 1, n_ctx, True)
    return x
```

```python
import functools
import math

import numpy as np
import jax
import jax.numpy as jnp
from jax import lax
from jax.experimental import pallas as pl
from jax.experimental.pallas import tpu as pltpu

F32 = jnp.float32
BF16 = jnp.bfloat16
HIGHEST = lax.Precision.HIGHEST

D_MODEL = 1024
DEPTH = 2
GRID_W = 64
NORM_EPS = 1e-6
N_MOD = 6

DN_HEADS = 8
DN_HEAD_DIM = 128
DN_WIDTH = DN_HEADS * DN_HEAD_DIM
HY_WIDTH = 1024
HY_EMB = 33
HY_BANDS = (HY_EMB - 1) // 2
HY_FILTER_HIDDEN = 64
HY_FAST_DECAY_PCT = 0.3
HY_SLOW_DECAY_PCT = 1.5
HY_DECAY_TARGET = 1e-2
LRU_WIDTH = 1024
LRU_BLOCKS = 8
LRU_BLOCK = LRU_WIDTH // LRU_BLOCKS
LRU_C = 8.0
FFN_HIDDEN = 2816

LANE = 128
SUBLANE = 8
TILE = 256
CHUNK = 128
DN_HEADS_PER_STEP = 4
DN_GATE_TILES = 4
DN_PREP_WIDE_ROWS = 512
DN_PACK = 64
HY_BLOCK = 1024
LRU_ROWS = 256
LRU_GROUP = 4
FFN_TILE = 256
FFN_WIDE = 256
MIB = 1024 * 1024
VMEM_LIMIT_MIB = 48
DN_VMEM_LIMIT_MIB = 58

QKV_BLK = 0
Z_BLK = 24
HY_BLK = 32
LX_BLK = 56
LY_BLK = 64
GATE_BLK = 72
N_MAIN = 96 * LANE


def _cparams(sem, vmem_mib=VMEM_LIMIT_MIB):
    return pltpu.CompilerParams(dimension_semantics=sem, vmem_limit_bytes=vmem_mib * MIB)


def _sigmoid(x):
    return 1.0 / (1.0 + jnp.exp(-x))


def _silu(x):
    return x * _sigmoid(x)


def _softplus(x):
    return jnp.maximum(x, 0.0) + jnp.log(1.0 + jnp.exp(-jnp.abs(x)))


def _row_iota(shape):
    return lax.broadcasted_iota(jnp.int32, shape, 0)


def _col_iota(shape):
    return lax.broadcasted_iota(jnp.int32, shape, 1)


def _div_pow2(x, k):
    assert k & (k - 1) == 0
    return x >> (k.bit_length() - 1)


def _mod_pow2(x, k):
    assert k & (k - 1) == 0
    return x & (k - 1)


def _bdot(a, b):
    return jnp.dot(a.astype(BF16), b.astype(BF16), preferred_element_type=F32)


def _mod_kernel(c_ref, w_ref, b_ref, o_ref):
    o_ref[...] = _bdot(_silu(c_ref[...]), w_ref[...]) + b_ref[...]


def _modulation(cvec, w_mod, b_mod):
    n = w_mod.shape[1]
    tn = 1024
    return pl.pallas_call(
        _mod_kernel,
        grid=(n // tn,),
        in_specs=[pl.BlockSpec((SUBLANE, D_MODEL), lambda j: (0, 0)),
                  pl.BlockSpec((D_MODEL, tn), lambda j: (0, j)),
                  pl.BlockSpec((1, tn), lambda j: (0, j))],
        out_specs=pl.BlockSpec((SUBLANE, tn), lambda j: (0, j)),
        out_shape=jax.ShapeDtypeStruct((SUBLANE, n), F32),
        compiler_params=_cparams(("parallel",)),
        name="modulation",
    )(cvec, w_mod, b_mod.reshape(1, n))


def _nmm_kernel(x_ref, g_ref, sh_ref, sc_ref, w_ref, *rest, with_side):
    o_ref = rest[1] if with_side else rest[0]
    h_ref = rest[-1]

    @pl.when(pl.program_id(2) == 0)
    def _():
        x = x_ref[0]
        y = x * lax.rsqrt(jnp.mean(x * x, axis=-1, keepdims=True) + NORM_EPS) * g_ref[...]
        h_ref[...] = (y * (1.0 + sc_ref[0]) + sh_ref[0]).astype(BF16)
        if with_side:
            rest[2][0] = jnp.dot(h_ref[...], rest[0][...], preferred_element_type=F32)

    o_ref[0] = jnp.dot(h_ref[...], w_ref[...], preferred_element_type=F32).astype(o_ref.dtype)


def _norm_mod_matmul(x, gain, shift, scale, w, out_dtype, tm, tn, w_side=None):
    bsz, n, d = x.shape
    nout = w.shape[1]
    tm = min(n, tm)
    with_side = w_side is not None
    in_specs = [pl.BlockSpec((1, tm, d), lambda b, i, j: (b, i, 0)),
                pl.BlockSpec((1, d), lambda b, i, j: (0, 0)),
                pl.BlockSpec((1, 1, d), lambda b, i, j: (b, 0, 0)),
                pl.BlockSpec((1, 1, d), lambda b, i, j: (b, 0, 0)),
                pl.BlockSpec((d, tn), lambda b, i, j: (0, j))]
    out_specs = [pl.BlockSpec((1, tm, tn), lambda b, i, j: (b, i, j))]
    out_shape = [jax.ShapeDtypeStruct((bsz, n, nout), out_dtype)]
    args = [x, gain.reshape(1, d), shift, scale, w]
    if with_side:
        ns = w_side.shape[1]
        in_specs.append(pl.BlockSpec((d, ns), lambda b, i, j: (0, 0)))
        out_specs.append(pl.BlockSpec((1, tm, ns), lambda b, i, j: (b, i, 0)))
        out_shape.append(jax.ShapeDtypeStruct((bsz, n, ns), F32))
        args.append(w_side)
    out = pl.pallas_call(
        functools.partial(_nmm_kernel, with_side=with_side),
        grid=(bsz, n // tm, nout // tn),
        in_specs=in_specs,
        out_specs=out_specs,
        out_shape=out_shape,
        scratch_shapes=[pltpu.VMEM((tm, d), BF16)],
        compiler_params=_cparams(("parallel", "parallel", "arbitrary")),
        name="norm_mod_matmul",
    )(*args)
    return out if with_side else out[0]


def _mm_kernel(a_ref, b_ref, o_ref):
    o_ref[...] = jnp.dot(a_ref[...], b_ref[...], preferred_element_type=F32).astype(o_ref.dtype)


def _matmul(a, b, out_dtype, tm, tn):
    m, k = a.shape
    n = b.shape[1]
    return pl.pallas_call(
        _mm_kernel,
        grid=(m // tm, n // tn),
        in_specs=[pl.BlockSpec((tm, k), lambda i, j: (i, 0)),
                  pl.BlockSpec((k, tn), lambda i, j: (0, j))],
        out_specs=pl.BlockSpec((tm, tn), lambda i, j: (i, j)),
        out_shape=jax.ShapeDtypeStruct((m, n), out_dtype),
        compiler_params=_cparams(("parallel", "parallel")),
        name="matmul",
    )(a, b)


def _mm_res_kernel(a_ref, b_ref, x_ref, g_ref, *rest, final_norm):
    o_ref = rest[-1]
    out = x_ref[0] + g_ref[0] * jnp.dot(a_ref[0], b_ref[...], preferred_element_type=F32)
    if final_norm:
        out = out * lax.rsqrt(jnp.mean(out * out, axis=-1, keepdims=True) + NORM_EPS) * rest[0][...]
    o_ref[0] = out


def _matmul_residual(a, w, x, gate, final_gain=None):
    bsz, n, k = a.shape
    d = w.shape[1]
    tm = min(n, 512)
    in_specs = [pl.BlockSpec((1, tm, k), lambda b, i: (b, i, 0)),
                pl.BlockSpec((k, d), lambda b, i: (0, 0)),
                pl.BlockSpec((1, tm, d), lambda b, i: (b, i, 0)),
                pl.BlockSpec((1, 1, d), lambda b, i: (b, 0, 0))]
    args = [a, w, x, gate]
    if final_gain is not None:
        in_specs.append(pl.BlockSpec((1, d), lambda b, i: (0, 0)))
        args.append(final_gain.reshape(1, d))
    return pl.pallas_call(
        functools.partial(_mm_res_kernel, final_norm=final_gain is not None),
        grid=(bsz, n // tm),
        in_specs=in_specs,
        out_specs=pl.BlockSpec((1, tm, d), lambda b, i: (b, i, 0)),
        out_shape=jax.ShapeDtypeStruct((bsz, n, d), F32),
        compiler_params=_cparams(("parallel", "parallel")),
        name="matmul_residual",
    )(*args)


def _dwconv_rows(x, w_ref, k):
    n = x.shape[0]
    left = (k - 1) // 2
    t = _row_iota((SUBLANE, x.shape[1]))
    acc = x * w_ref[left:left + 1, :]
    head_fix = jnp.zeros((SUBLANE, x.shape[1]), F32)
    tail_fix = jnp.zeros((SUBLANE, x.shape[1]), F32)
    for j in range(k):
        off = j - left
        if off == 0:
            continue
        term = pltpu.roll(x, (-off) % n, 0) * w_ref[j:j + 1, :]
        acc = acc + term
        if off < 0:
            head_fix = head_fix + jnp.where(t < -off, term[:SUBLANE, :], 0.0)
        else:
            tail_fix = tail_fix + jnp.where(t >= SUBLANE - off, term[n - SUBLANE:, :], 0.0)
    return jnp.concatenate([acc[:SUBLANE, :] - head_fix, acc[SUBLANE:n - SUBLANE, :],
                            acc[n - SUBLANE:, :] - tail_fix], axis=0)


def _dn_prep_kernel(p_ref, w_ref, rm_ref, tr_ref, *, n_tiles, heads):
    first = pl.program_id(1) * heads
    y_all = _silu(_dwconv_rows(p_ref[0].astype(F32), w_ref, 4))
    q_scale = jnp.where(first < DN_HEADS, DN_HEAD_DIM ** -0.5, 1.0)
    for h in range(heads):
        y = y_all[:, h * LANE:(h + 1) * LANE]
        unit = lax.rsqrt(jnp.sum(y * y, axis=-1, keepdims=True) + 1e-6) * q_scale
        y = y * jnp.where(first < 2 * DN_HEADS, unit, 1.0)
        rm_ref[0, :, h * LANE:(h + 1) * LANE] = y.astype(BF16)
        for t in range(n_tiles):
            tr_ref[0, h, t] = y[t * TILE:(t + 1) * TILE, :].T.astype(BF16)


def _dn_prep(proj, conv_w):
    bsz, n, _ = proj.shape
    nt = n // TILE
    nc = 3 * DN_HEADS
    heads = DN_HEADS if n <= DN_PREP_WIDE_ROWS else 2
    wide = heads * LANE
    return pl.pallas_call(
        functools.partial(_dn_prep_kernel, n_tiles=nt, heads=heads),
        grid=(bsz, nc // heads),
        in_specs=[pl.BlockSpec((1, n, wide), lambda b, c: (b, 0, QKV_BLK // heads + c)),
                  pl.BlockSpec((4, wide), lambda b, c: (0, c))],
        out_specs=[pl.BlockSpec((1, n, wide), lambda b, c: (b, 0, c)),
                   pl.BlockSpec((1, heads, nt, LANE, TILE), lambda b, c: (b, c, 0, 0, 0))],
        out_shape=[jax.ShapeDtypeStruct((bsz, n, nc * LANE), BF16),
                   jax.ShapeDtypeStruct((bsz, nc, nt, LANE, TILE), BF16)],
        compiler_params=_cparams(("parallel", "parallel")),
        name="dn_prep",
    )(proj, conv_w)


def _dn_gate_kernel(ab_ref, alog_ref, dtb_ref, isdec_ref, o_ref, *, tiles):
    s = _row_iota((TILE, TILE))
    t = _col_iota((TILE, TILE))
    same = _div_pow2(s, CHUNK) == _div_pow2(t, CHUNK)
    ones = [jnp.where(m, 1.0, 0.0).astype(BF16) for m in (same & (s <= t), same & (s >= t), same)]
    slot = _mod_pow2(_row_iota((LANE, TILE)), SUBLANE)
    for k in range(tiles):
        x = ab_ref[0, k * TILE:(k + 1) * TILE, :]
        dec = -jnp.exp(alog_ref[...]) * _softplus(x + dtb_ref[...])
        e = jnp.where(isdec_ref[...] > 0.5, dec, _sigmoid(x))
        et = e.T
        hi = et.astype(BF16)
        rest = et - hi.astype(F32)
        mid = rest.astype(BF16)
        lo = (rest - mid.astype(F32)).astype(BF16)
        pre, suf, tot = [_bdot(hi, m) + _bdot(mid, m) + _bdot(lo, m) for m in ones]
        o_ref[0, k] = jnp.where(slot == 0, pre, jnp.where(slot == 2, suf, jnp.where(slot >= 4, tot, et)))


def _dn_gates(ab, alog_c, dtb_c, isdec_c):
    bsz, n, _ = ab.shape
    nt = n // TILE
    tiles = min(nt, DN_GATE_TILES)
    vec = pl.BlockSpec((1, LANE), lambda b, i: (0, 0))
    return pl.pallas_call(
        functools.partial(_dn_gate_kernel, tiles=tiles),
        grid=(bsz, nt // tiles),
        in_specs=[pl.BlockSpec((1, tiles * TILE, LANE), lambda b, i: (b, i, 0)), vec, vec, vec],
        out_specs=pl.BlockSpec((1, tiles, LANE, TILE), lambda b, i: (b, i, 0, 0)),
        out_shape=jax.ShapeDtypeStruct((bsz, nt, LANE, TILE), F32),
        compiler_params=_cparams(("parallel", "parallel")),
        name="dn_gates",
    )(ab, alog_c, dtb_c, isdec_c)


def _dn_masks(mask_ref):
    a = _row_iota((TILE, TILE))
    b = _col_iota((TILE, TILE))
    apart = a ^ b
    n_levels = CHUNK.bit_length() - 2
    for lg in range(1, n_levels + 1):
        mask_ref[lg - 1] = jnp.where((apart >> lg) == 1, 1.0, 0.0).astype(BF16)
    for k, pack in enumerate((DN_PACK, 2 * DN_PACK)):
        mask_ref[n_levels + k] = jnp.where(_div_pow2(a, pack) == _div_pow2(b, pack), 1.0, 0.0).astype(BF16)


def _dn_tiles(chains, mask_ref):
    a = _row_iota((TILE, TILE))
    b = _col_iota((TILE, TILE))
    same = _div_pow2(a, CHUNK) == _div_pow2(b, CHUNK)
    apart = a ^ b
    n_chunks = TILE // CHUNK
    dot = functools.partial(jnp.dot, preferred_element_type=F32)

    def stack(v, reps):
        return jnp.concatenate([v] * reps, axis=0)

    kk = [dot(c["k_rm"], c["kt"]) for c in chains]
    kq = [dot(c["k_rm"], c["qt"]) for c in chains]
    xs, pw, attn = [], [], []
    for c, kk_c, kq_c in zip(chains, kk, kq):
        incl = same & ((a >= b) if c["backward"] else (a <= b))
        gcb = jnp.broadcast_to(c["gc"], (TILE, TILE))
        diff = gcb - gcb.T
        decay = jnp.where(incl, jnp.exp(jnp.where(incl, diff, 0.0)), 0.0)
        attn.append((kq_c * decay).astype(BF16))
        x = kk_c * decay * (-c["beta"])
        xs.append(x.astype(BF16))
        base = jnp.where(a == b, 1.0, jnp.where(apart == 1, x, 0.0))
        acc = base[0:DN_PACK, :]
        for r in range(1, TILE // DN_PACK):
            acc = acc + base[r * DN_PACK:(r + 1) * DN_PACK, :]
        pw.append(acc)
    pack = DN_PACK
    n_levels = CHUNK.bit_length() - 2
    block_mask = {DN_PACK: n_levels, 2 * DN_PACK: n_levels + 1}
    s = 2
    while s < CHUNK:
        if s == pack:
            keep = _div_pow2(_row_iota((2 * pack, TILE)), pack) == (_div_pow2(_col_iota((2 * pack, TILE)), pack) & 1)
            pw = [jnp.where(keep, stack(p, 2), 0.0) for p in pw]
            pack *= 2
        couple = mask_ref[s.bit_length() - 2]
        blocks = mask_ref[block_mask[pack]]
        pb = [p.astype(BF16) for p in pw]
        px = [dot(pb_c, x * couple) for pb_c, x in zip(pb, xs)]
        p_bd = [stack(pb_c, TILE // pack) * blocks for pb_c in pb]
        pw = [p + dot(px_c.astype(BF16), bd_c) for p, px_c, bd_c in zip(pw, px, p_bd)]
        s *= 2
    blocks = mask_ref[block_mask[pack]]
    t_inv = [stack(p.astype(BF16), TILE // pack) * blocks for p in pw]
    egc = [jnp.exp(c["gc"]) for c in chains]
    u_t = [dot((c["vt"].astype(F32) * c["beta"]).astype(BF16), t) for c, t in zip(chains, t_inv)]
    w_t = [dot((c["kt"].astype(F32) * (c["beta"] * e)).astype(BF16), t).astype(BF16)
           for c, e, t in zip(chains, egc, t_inv)]
    qd_t = [(c["qt"].astype(F32) * e).astype(BF16) for c, e in zip(chains, egc)]
    kdec = [jnp.exp(c["tot"] - c["gc"]) for c in chains]
    outs = [[None] * n_chunks for _ in chains]
    for step in range(n_chunks):
        cis = [(n_chunks - 1 - step) if c["backward"] else step for c in chains]
        sl = [slice(ci * CHUNK, (ci + 1) * CHUNK) for ci in cis]
        st = [c["st_ref"][...] for c in chains]
        stb = [s_c.astype(BF16) for s_c in st]
        swq = [dot(stb_c, jnp.concatenate([w_c[:, r], q_c[:, r]], axis=1))
               for stb_c, w_c, q_c, r in zip(stb, w_t, qd_t, sl)]
        sq = [v[:, CHUNK:] for v in swq]
        vn = [u_c[:, r] - v[:, :CHUNK] for u_c, v, r in zip(u_t, swq, sl)]
        av = [dot(vn_c.astype(BF16), at_c[r, r]) for vn_c, at_c, r in zip(vn, attn, sl)]
        upd = [dot((vn_c * kd_c[:, r]).astype(BF16), c["k_rm"][r, :])
               for vn_c, kd_c, c, r in zip(vn, kdec, chains, sl)]
        for idx, c in enumerate(chains):
            c["st_ref"][...] = st[idx] * jnp.exp(c["tot"][:, sl[idx]]) + upd[idx]
            outs[idx][cis[idx]] = sq[idx] + av[idx]
    return [jnp.concatenate(o, axis=1) for o in outs]


def _dn_kernel(k_ref, qt_ref, kt_ref, vt_ref, g_ref, s0f_ref, s0b_ref,
               o_ref, sf_ref, sb_ref, ot_ref, st_ref, mask_ref, *, n_tiles, with_output):
    hb = DN_HEADS_PER_STEP
    _dn_masks(mask_ref)
    for hh in range(hb):
        st_ref[2 * hh] = s0f_ref[0, hh]
        st_ref[2 * hh + 1] = s0b_ref[0, hh]

    def body(i, carry):
        nf = i
        nb = n_tiles - 1 - i
        chains = []
        for hh in range(hb):
            for backward, n in ((False, nf), (True, nb)):
                g = g_ref[0, n, hh * SUBLANE:(hh + 1) * SUBLANE, :]
                base = 2 if backward else 0
                chains.append(dict(
                    k_rm=k_ref[0, pl.ds(pl.multiple_of(n * TILE, TILE), TILE), hh * LANE:(hh + 1) * LANE],
                    qt=qt_ref[0, hh, n], kt=kt_ref[0, hh, n], vt=vt_ref[0, hh, n],
                    gc=g[base:base + 1, :], beta=g[base + 1:base + 2, :], tot=g[4 + base // 2:5 + base // 2, :],
                    st_ref=st_ref.at[2 * hh + (1 if backward else 0)], backward=backward, hh=hh, n=n))
        o_t = _dn_tiles(chains, mask_ref)
        if with_output:
            for c, o_c in zip(chains, o_t):
                ot_ref[c["hh"], c["n"]] = ot_ref[c["hh"], c["n"]] + o_c
        return carry

    if with_output:
        ot_ref[...] = jnp.zeros_like(ot_ref)
    lax.fori_loop(0, n_tiles, body, 0)
    for hh in range(hb):
        sf_ref[0, hh] = st_ref[2 * hh]
        sb_ref[0, hh] = st_ref[2 * hh + 1]
    if with_output:
        def finish(t, carry):
            rows = pl.ds(pl.multiple_of(t * TILE, TILE), TILE)
            for hh in range(hb):
                o_ref[0, rows, hh * LANE:(hh + 1) * LANE] = ot_ref[hh, t].T.astype(BF16)
            return carry

        lax.fori_loop(0, n_tiles, finish, 0)
    else:
        o_ref[...] = jnp.zeros_like(o_ref)


def _delta_net(qkv_rm, qkv_tr, gates, s0f, s0b, with_output):
    bsz, n, _ = qkv_rm.shape
    nt = n // TILE
    h = DN_HEADS
    hb = DN_HEADS_PER_STEP
    wide = hb * LANE
    once = pl.Buffered(1)
    tr_spec = lambda off: pl.BlockSpec((1, hb, nt, LANE, TILE), lambda b, j: (b, off + j, 0, 0, 0))
    st_spec = pl.BlockSpec((1, hb, LANE, LANE), lambda b, j: (b, j, 0, 0))
    n_out = n if with_output else SUBLANE
    return pl.pallas_call(
        functools.partial(_dn_kernel, n_tiles=nt, with_output=with_output),
        grid=(bsz, h // hb),
        in_specs=[pl.BlockSpec((1, n, wide), lambda b, j: (b, 0, h // hb + j), once),
                  tr_spec(0), tr_spec(h // hb), tr_spec(2 * h // hb),
                  pl.BlockSpec((1, nt, hb * SUBLANE, TILE), lambda b, j: (b, 0, j, 0)),
                  st_spec, st_spec],
        out_specs=[pl.BlockSpec((1, n_out, wide), lambda b, j: (b, 0, j)), st_spec, st_spec],
        out_shape=[jax.ShapeDtypeStruct((bsz, n_out, DN_WIDTH), BF16),
                   jax.ShapeDtypeStruct((bsz, h, LANE, LANE), F32),
                   jax.ShapeDtypeStruct((bsz, h, LANE, LANE), F32)],
        scratch_shapes=[pltpu.VMEM((hb, nt, LANE, TILE), F32),
                        pltpu.VMEM((2 * hb, LANE, LANE), F32),
                        pltpu.VMEM((CHUNK.bit_length(), TILE, TILE), BF16)],
        compiler_params=_cparams(("parallel", "parallel"), DN_VMEM_LIMIT_MIB),
        name="delta_net",
    )(qkv_rm, qkv_tr, qkv_tr, qkv_tr, gates, s0f, s0b)


def _lru_scan_block(x, wa, ba, wx, bx, spl, h_in, backward):
    rows, width = x.shape
    xb = x.astype(BF16)

    def gate(ws, bias):
        parts = [jnp.dot(xb[:, k * LRU_BLOCK:(k + 1) * LRU_BLOCK], w, preferred_element_type=F32)
                 for k, w in enumerate(ws)]
        return _sigmoid(jnp.concatenate(parts, axis=1) + bias)

    r = gate(wa, ba)
    gi = gate(wx, bx)
    log_a = -LRU_C * r * spl
    a = jnp.exp(log_a)
    b = jnp.sqrt(1.0 - jnp.exp(2.0 * log_a)) * (gi * x)
    groups = rows // SUBLANE
    a = a.reshape(groups, SUBLANE, width)
    b = b.reshape(groups, SUBLANE, width)
    sub = lax.broadcasted_iota(jnp.int32, a.shape, 1)
    s = 1
    while s < SUBLANE:
        if backward:
            keep = sub < SUBLANE - s
            a_sh = jnp.where(keep, pltpu.roll(a, SUBLANE - s, 1), 1.0)
            b_sh = jnp.where(keep, pltpu.roll(b, SUBLANE - s, 1), 0.0)
        else:
            keep = sub >= s
            a_sh = jnp.where(keep, pltpu.roll(a, s, 1), 1.0)
            b_sh = jnp.where(keep, pltpu.roll(b, s, 1), 0.0)
        b = a * b_sh + b
        a = a * a_sh
        s *= 2
    a = a.reshape(rows, width)
    b = b.reshape(rows, width)
    pieces = [None] * groups
    carry = h_in
    order = range(groups - 1, -1, -1) if backward else range(groups)
    edge = 0 if backward else SUBLANE - 1
    for gidx in order:
        lo = gidx * SUBLANE
        hgrp = b[lo:lo + SUBLANE, :] + a[lo:lo + SUBLANE, :] * carry
        pieces[gidx] = hgrp
        carry = hgrp[edge:edge + 1, :]
    return jnp.concatenate(pieces, axis=0), carry


def _lru_kernel(px_ref, py_ref, cw_ref, cb_ref, wa_ref, ba_ref, wx_ref, bx_ref, lam_ref, h0_ref,
                o_ref, last_ref, xs_ref, hs_ref, *, n_blocks, with_output):
    x = px_ref[0].astype(F32)
    xs_ref[...] = _dwconv_rows(x, cw_ref, 4) + cb_ref[...]
    spl = _softplus(-lam_ref[...])
    wa = [[wa_ref[d, k] for k in range(LRU_GROUP)] for d in range(2)]
    wx = [[wx_ref[d, k] for k in range(LRU_GROUP)] for d in range(2)]

    def body(i, carry):
        hf, hb = carry
        rf = pl.multiple_of(i * LRU_ROWS, LRU_ROWS)
        rb = pl.multiple_of((n_blocks - 1 - i) * LRU_ROWS, LRU_ROWS)
        h_f, hf = _lru_scan_block(xs_ref[pl.ds(rf, LRU_ROWS), :], wa[0], ba_ref[0:1, :],
                                  wx[0], bx_ref[0:1, :], spl[0:1, :], hf, False)
        h_b, hb = _lru_scan_block(xs_ref[pl.ds(rb, LRU_ROWS), :], wa[1], ba_ref[1:2, :],
                                  wx[1], bx_ref[1:2, :], spl[1:2, :], hb, True)
        if with_output:
            hs_ref[pl.ds(rf, LRU_ROWS), :] = hs_ref[pl.ds(rf, LRU_ROWS), :] + h_f
            hs_ref[pl.ds(rb, LRU_ROWS), :] = hs_ref[pl.ds(rb, LRU_ROWS), :] + h_b
        return hf, hb

    if with_output:
        hs_ref[...] = jnp.zeros_like(hs_ref)
    h0 = h0_ref[0]
    hf, hb = lax.fori_loop(0, n_blocks, body, (h0[0:1, :], h0[1:2, :]))
    last_ref[0] = jnp.concatenate([hf, hb], axis=0)
    if with_output:
        y = py_ref[0].astype(F32)
        gelu = 0.5 * y * (1.0 + jnp.tanh(math.sqrt(2.0 / math.pi) * (y + 0.044715 * (y * y * y))))
        o_ref[0] = (hs_ref[...] * gelu).astype(BF16)
    else:
        o_ref[...] = jnp.zeros_like(o_ref)


def _rglru(proj, conv_w, conv_b, w_a, b_a, w_x, b_x, lam, h0, with_output):
    bsz, n, _ = proj.shape
    rows = min(n, LRU_ROWS)
    nb = n // rows
    n_out = n if with_output else SUBLANE
    width = LRU_GROUP * LRU_BLOCK
    lx = LX_BLK * LANE // width
    ly = LY_BLK * LANE // width
    vec2 = pl.BlockSpec((2, width), lambda b, j: (0, j))
    wspec = pl.BlockSpec((2, LRU_GROUP, LRU_BLOCK, LRU_BLOCK), lambda b, j: (0, j, 0, 0))
    return pl.pallas_call(
        functools.partial(_lru_kernel, n_blocks=nb, with_output=with_output),
        grid=(bsz, LRU_BLOCKS // LRU_GROUP),
        in_specs=[pl.BlockSpec((1, n, width), lambda b, j: (b, 0, lx + j)),
                  pl.BlockSpec((1, n, width), lambda b, j: (b, 0, ly + j)),
                  pl.BlockSpec((4, width), lambda b, j: (0, j)),
                  pl.BlockSpec((1, width), lambda b, j: (0, j)),
                  wspec, vec2, wspec, vec2, vec2,
                  pl.BlockSpec((1, 2, width), lambda b, j: (b, 0, j))],
        out_specs=[pl.BlockSpec((1, n_out, width), lambda b, j: (b, 0, j)),
                   pl.BlockSpec((1, 2, width), lambda b, j: (b, 0, j))],
        out_shape=[jax.ShapeDtypeStruct((bsz, n_out, LRU_WIDTH), BF16),
                   jax.ShapeDtypeStruct((bsz, 2, LRU_WIDTH), F32)],
        scratch_shapes=[pltpu.VMEM((n, width), F32), pltpu.VMEM((n, width), F32)],
        compiler_params=_cparams(("parallel", "parallel")),
        name="rglru",
    )(proj, proj, conv_w, conv_b.reshape(1, LRU_WIDTH), w_a, b_a, w_x, b_x, lam, h0)


def _hy_prep_kernel(p0_ref, p1_ref, pv_ref, w0_ref, w1_ref, wv_ref, b0_ref, b1_ref, bv_ref, x0_ref, zz_ref):
    x0 = _dwconv_rows(p0_ref[0].astype(F32), w0_ref, 3) + b0_ref[...]
    x1 = _dwconv_rows(p1_ref[0].astype(F32), w1_ref, 3) + b1_ref[...]
    v = _dwconv_rows(pv_ref[0].astype(F32), wv_ref, 3) + bv_ref[...]
    x0_ref[0] = x0.astype(BF16)
    zz_ref[0] = (x1 * v).astype(BF16)


def _hy_prep(proj, conv_w, conv_b):
    bsz, n, _ = proj.shape
    nblk = HY_WIDTH // LANE
    pspec = lambda off: pl.BlockSpec((1, n, LANE), lambda b, j: (b, 0, HY_BLK + off + j))
    wspec = lambda off: pl.BlockSpec((3, LANE), lambda b, j: (0, off + j))
    bspec = lambda off: pl.BlockSpec((1, LANE), lambda b, j: (0, off + j))
    ospec = pl.BlockSpec((1, n, LANE), lambda b, j: (b, 0, j))
    cb = conv_b.reshape(1, 3 * HY_WIDTH)
    return pl.pallas_call(
        _hy_prep_kernel,
        grid=(bsz, nblk),
        in_specs=[pspec(0), pspec(nblk), pspec(2 * nblk), wspec(0), wspec(nblk), wspec(2 * nblk),
                  bspec(0), bspec(nblk), bspec(2 * nblk)],
        out_specs=[ospec, ospec],
        out_shape=[jax.ShapeDtypeStruct((bsz, n, HY_WIDTH), BF16)] * 2,
        compiler_params=_cparams(("parallel", "parallel")),
        name="hy_prep",
    )(proj, proj, proj, conv_w, conv_w, conv_w, cb, cb, cb)


def _hy_filter_kernel(w1_ref, b1_ref, f1_ref, w2_ref, b2_ref, f2_ref, w3_ref, band_ref, delta_ref,
                      k_ref, *, n, rows):
    def lag(shape):
        p = _row_iota(shape) + (pl.program_id(0) * rows - n)
        return p, jnp.abs(p).astype(F32)

    _, i = lag((rows, LANE))
    lane = _col_iota((rows, LANE))
    t = i * (1.0 / (n - 1))
    ang = band_ref[...] * (i * (2.0 * math.pi / n))
    feat = jnp.where(lane == 0, t,
                     jnp.where(lane <= HY_BANDS, jnp.cos(ang), jnp.where(lane < HY_EMB, -jnp.sin(ang), 0.0)))
    hid = jnp.sin(f1_ref[...] * (jnp.dot(feat, w1_ref[...], precision=HIGHEST,
                                         preferred_element_type=F32) + b1_ref[...]))
    hid = jnp.sin(f2_ref[...] * (jnp.dot(hid, w2_ref[...], precision=HIGHEST,
                                         preferred_element_type=F32) + b2_ref[...]))
    filt = _bdot(hid, w3_ref[...])
    p, iw = lag((rows, HY_WIDTH))
    dec = jnp.exp(-(iw * (1.0 / (n - 1))) * delta_ref[...])
    h_f = filt[:, :HY_WIDTH] * dec
    h_b = filt[:, HY_WIDTH:] * dec
    taps = jnp.where(p > 0, h_f, jnp.where(p < 0, h_b, h_f + h_b))
    k_ref[...] = jnp.where(p == -n, 0.0, taps).astype(BF16)


def _hy_filter(n, w1, b1, f1, w2, b2, f2, w3):
    rows = min(n, 256)
    hid = HY_FILTER_HIDDEN
    w1p = jnp.zeros((LANE, hid), F32).at[:HY_EMB].set(w1)
    bands = np.zeros((1, LANE), np.float32)
    base = np.linspace(1e-4, HY_BANDS - 1, HY_BANDS, dtype=np.float32)
    bands[0, 1:1 + HY_BANDS] = base
    bands[0, 1 + HY_BANDS:HY_EMB] = base
    log_target = math.log(HY_DECAY_TARGET)
    deltas = np.abs(np.linspace(log_target / HY_SLOW_DECAY_PCT, log_target / HY_FAST_DECAY_PCT, HY_WIDTH,
                                dtype=np.float32)).reshape(1, HY_WIDTH)
    full = lambda shape: pl.BlockSpec(shape, lambda i: (0,) * len(shape))
    return pl.pallas_call(
        functools.partial(_hy_filter_kernel, n=n, rows=rows),
        grid=(2 * n // rows,),
        in_specs=[full((LANE, hid)), full((1, hid)), full((1, hid)), full((hid, hid)), full((1, hid)),
                  full((1, hid)), full((hid, 2 * HY_WIDTH)), full((1, LANE)), full((1, HY_WIDTH))],
        out_specs=pl.BlockSpec((rows, HY_WIDTH), lambda i: (i, 0)),
        out_shape=jax.ShapeDtypeStruct((2 * n, HY_WIDTH), BF16),
        compiler_params=_cparams(("parallel",)),
        name="hy_filter",
    )(w1p, b1.reshape(1, hid), f1.reshape(1, hid), w2, b2.reshape(1, hid), f2.reshape(1, hid), w3,
      jnp.asarray(bands), jnp.asarray(deltas))


DFT_GROUP = 64


def _dft_table_kernel(c_ref, s_ref, cb_ref, sb_ref, *, m, ncols, col0, rows, transposed, blank_first):
    period = 4 * m
    scale = 2.0 * math.pi / period
    col = _col_iota((1, ncols)) + col0

    @pl.when(pl.program_id(0) == 0)
    def _():
        r2 = _row_iota((DFT_GROUP, ncols))
        c2 = _col_iota((DFT_GROUP, ncols)) + col0
        ph = (r2 * (2 * c2 + 1)) if transposed else ((2 * r2 + 1) * c2)
        ang = (ph & (period - 1)).astype(F32) * scale
        cb_ref[...] = jnp.cos(ang)
        sb_ref[...] = jnp.sin(ang)

    for g in range(rows // DFT_GROUP):
        r1 = pl.program_id(0) * (rows // DFT_GROUP) + g
        ph = (DFT_GROUP * r1) * (2 * col + 1) if transposed else (2 * DFT_GROUP * r1) * col
        ang = (ph & (period - 1)).astype(F32) * scale
        ca = jnp.cos(ang)
        sa = jnp.sin(ang)
        cb = cb_ref[...]
        sb = sb_ref[...]
        c_tile = ca * cb - sa * sb
        s_tile = sa * cb + ca * sb
        if blank_first:
            first = _col_iota((DFT_GROUP, ncols)) == 0
            c_tile = jnp.where(first, 0.0, c_tile)
            s_tile = jnp.where(first, 0.0, s_tile)
        c_ref[g * DFT_GROUP:(g + 1) * DFT_GROUP, :] = c_tile.astype(BF16)
        s_ref[g * DFT_GROUP:(g + 1) * DFT_GROUP, :] = s_tile.astype(BF16)


def _dft_tables(m, transposed=False, two_sided=False):
    rows = min(m, 256)
    ncols = 2 * m if two_sided else m
    spec = pl.BlockSpec((rows, ncols), lambda i: (i, 0))
    return pl.pallas_call(
        functools.partial(_dft_table_kernel, m=m, ncols=ncols, col0=3 * m if two_sided else 0, rows=rows,
                          transposed=transposed, blank_first=two_sided),
        grid=(m // rows,),
        in_specs=[],
        out_specs=[spec, spec],
        out_shape=[jax.ShapeDtypeStruct((m, ncols), BF16)] * 2,
        scratch_shapes=[pltpu.VMEM((DFT_GROUP, ncols), F32), pltpu.VMEM((DFT_GROUP, ncols), F32)],
        compiler_params=_cparams(("arbitrary",)),
        name="dft_tables",
    )()


def _hy_tables(m):
    return _dft_tables(m) + _dft_tables(m, transposed=True) + _dft_tables(m, two_sided=True)


def _hy_ktrans_kernel(cw_ref, sw_ref, lo_ref, hi_ref, kc_ref, ks_ref, *, m):
    dot = functools.partial(jnp.dot, preferred_element_type=F32)
    lo = lo_ref[...]
    hi = hi_ref[...]
    kc_ref[0] = dot(cw_ref[:, :m], lo) + dot(cw_ref[:, m:], hi)
    ks_ref[0] = dot(sw_ref[:, :m], lo) + dot(sw_ref[:, m:], hi)


def _hy_ktrans(cw, sw, taps, m):
    nd = taps.shape[0] // m - 1
    w = taps.shape[1]
    tn = 512
    tab = pl.BlockSpec((m, 2 * m), lambda e, j: (0, 0))
    ospec = pl.BlockSpec((1, m, tn), lambda e, j: (e, 0, j))
    return pl.pallas_call(
        functools.partial(_hy_ktrans_kernel, m=m),
        grid=(nd, w // tn),
        in_specs=[tab, tab, pl.BlockSpec((m, tn), lambda e, j: (e, j)), pl.BlockSpec((m, tn), lambda e, j: (e + 1, j))],
        out_specs=[ospec, ospec],
        out_shape=[jax.ShapeDtypeStruct((nd, m, w), F32)] * 2,
        compiler_params=_cparams(("parallel", "parallel")),
        name="hy_ktrans",
    )(cw, sw, taps, taps)


def _hy_fwd_kernel(c_ref, s_ref, zz_ref, kc_ref, ks_ref, a_ref, b_ref, *, m, nb):
    dot = functools.partial(jnp.dot, preferred_element_type=F32)
    c = c_ref[...]
    s = s_ref[...]
    acc_a = [None] * nb
    acc_b = [None] * nb
    for j in range(nb):
        zz = zz_ref[0, j * m:(j + 1) * m, :]
        uc = dot(c, zz)
        us = dot(s, zz)
        for i in range(nb):
            kc = kc_ref[i - j + nb - 1]
            ks = ks_ref[i - j + nb - 1]
            ta = uc * kc - us * ks
            tb = uc * ks + us * kc
            acc_a[i] = ta if acc_a[i] is None else acc_a[i] + ta
            acc_b[i] = tb if acc_b[i] is None else acc_b[i] + tb
    for i in range(nb):
        a_ref[0, i] = acc_a[i].astype(BF16)
        b_ref[0, i] = acc_b[i].astype(BF16)


def _hy_forward(ctab, stab, zz, kc, ks, m):
    bsz, n, w = zz.shape
    nb = n // m
    nd = 2 * nb - 1
    tf = min(m, 256)
    tn = 256
    tab = pl.BlockSpec((tf, m), lambda b, j, i: (i, 0))
    kspec = pl.BlockSpec((nd, tf, tn), lambda b, j, i: (0, i, j))
    ospec = pl.BlockSpec((1, nb, tf, tn), lambda b, j, i: (b, 0, i, j))
    return pl.pallas_call(
        functools.partial(_hy_fwd_kernel, m=m, nb=nb),
        grid=(bsz, w // tn, m // tf),
        in_specs=[tab, tab, pl.BlockSpec((1, n, tn), lambda b, j, i: (b, 0, j)), kspec, kspec],
        out_specs=[ospec, ospec],
        out_shape=[jax.ShapeDtypeStruct((bsz, nb, m, w), BF16)] * 2,
        compiler_params=_cparams(("parallel", "parallel", "parallel")),
        name="hy_forward",
    )(ctab, stab, zz, kc, ks)


def _hy_inv_kernel(ct_ref, st_ref, a_ref, b_ref, x0_ref, zz_ref, bias_ref, o_ref, *, m):
    y = (jnp.dot(ct_ref[...], a_ref[0, 0], preferred_element_type=F32)
         + jnp.dot(st_ref[...], b_ref[0, 0], preferred_element_type=F32)) * (1.0 / m)
    zz = zz_ref[0].astype(F32)
    o_ref[0] = (x0_ref[0].astype(F32) * (y + zz * bias_ref[...])).astype(BF16)


def _hy_inverse(cttab, sttab, a, bq, x0, zz, bias, m):
    bsz, n, w = zz.shape
    nb = n // m
    tn = 512
    tab = pl.BlockSpec((m, m), lambda b, i, j: (0, 0))
    spec4 = pl.BlockSpec((1, 1, m, tn), lambda b, i, j: (b, i, 0, j))
    tile = pl.BlockSpec((1, m, tn), lambda b, i, j: (b, i, j))
    return pl.pallas_call(
        functools.partial(_hy_inv_kernel, m=m),
        grid=(bsz, nb, w // tn),
        in_specs=[tab, tab, spec4, spec4, tile, tile, pl.BlockSpec((1, tn), lambda b, i, j: (0, j))],
        out_specs=tile,
        out_shape=jax.ShapeDtypeStruct((bsz, n, w), BF16),
        compiler_params=_cparams(("parallel", "parallel", "parallel")),
        name="hy_inverse",
    )(cttab, sttab, a, bq, x0, zz, bias.reshape(1, w))


def _merge_kernel(odn_ref, z_ref, ng_ref, ohy_ref, olru_ref, gdn_ref, ghy_ref, glru_ref, wdn_ref, why_ref,
                  wlru_ref, wout_ref, x_ref, gate_ref, o_ref):
    heads = []
    for h in range(DN_HEADS):
        o = odn_ref[0, :, h * LANE:(h + 1) * LANE].astype(F32)
        y = o * lax.rsqrt(jnp.mean(o * o, axis=-1, keepdims=True) + NORM_EPS) * ng_ref[...]
        heads.append((y * _silu(z_ref[0, :, h * LANE:(h + 1) * LANE].astype(F32))).astype(BF16))
    odn = jnp.concatenate(heads, axis=1)
    m = _sigmoid(gdn_ref[0].astype(F32)) * jnp.dot(odn, wdn_ref[...], preferred_element_type=F32)
    m = m + _sigmoid(ghy_ref[0].astype(F32)) * jnp.dot(ohy_ref[0], why_ref[...], preferred_element_type=F32)
    m = m + _sigmoid(glru_ref[0].astype(F32)) * jnp.dot(olru_ref[0], wlru_ref[...], preferred_element_type=F32)
    y = jnp.dot(m.astype(BF16), wout_ref[...], preferred_element_type=F32)
    o_ref[0] = x_ref[0] + gate_ref[0] * y


def _merge(o_dn, dn_norm_g, o_hy, o_lru, proj, w_dn, w_hy, w_lru, w_out, x, gate):
    bsz, n, d = x.shape
    tm = min(n, 256)
    nblk = d // 1024
    act = pl.BlockSpec((1, tm, d), lambda b, i: (b, i, 0))
    gspec = lambda k: pl.BlockSpec((1, tm, d), lambda b, i: (b, i, GATE_BLK * LANE // d + k * nblk))
    wspec = pl.BlockSpec((d, d), lambda b, i: (0, 0))
    return pl.pallas_call(
        _merge_kernel,
        grid=(bsz, n // tm),
        in_specs=[act, pl.BlockSpec((1, tm, d), lambda b, i: (b, i, Z_BLK * LANE // d)),
                  pl.BlockSpec((1, LANE), lambda b, i: (0, 0)),
                  act, act, gspec(0), gspec(1), gspec(2), wspec, wspec, wspec, wspec, act,
                  pl.BlockSpec((1, 1, d), lambda b, i: (b, 0, 0))],
        out_specs=act,
        out_shape=jax.ShapeDtypeStruct((bsz, n, d), F32),
        compiler_params=_cparams(("parallel", "parallel")),
        name="merge",
    )(o_dn, proj, dn_norm_g.reshape(1, LANE), o_hy, o_lru, proj, proj, proj, w_dn, w_hy, w_lru, w_out, x, gate)


def _ffn_act_kernel(ug_ref, uv_ref, wg_ref, wv_ref, o_ref, up_ref, mid_ref, dn_ref, *, rows, cols, n):
    tile = FFN_TILE
    a = _row_iota((tile, tile))
    b = _col_iota((tile, tile))
    c = _mod_pow2(a, cols)
    lmat = jnp.where((b == a - 1) & (c >= 1), 1.0, 0.0).astype(BF16)
    rmat = jnp.where((b == a + 1) & (c <= cols - 2), 1.0, 0.0).astype(BF16)
    taps = (-1, 0, 1) if rows > 1 else (0,)
    dst = {-1: up_ref, 0: mid_ref, 1: dn_ref}
    if rows > 1:
        pad = jnp.zeros((2, cols, FFN_WIDE), BF16)
        up_ref[:, 0:cols, :] = pad
        dn_ref[:, n + cols:n + 2 * cols, :] = pad

    def taps_of(i, carry):
        r0 = pl.multiple_of(i * tile, tile)
        for idx, (u_ref, w_ref) in enumerate(((ug_ref, wg_ref), (uv_ref, wv_ref))):
            u = u_ref[0, pl.ds(r0, tile), :]
            left = jnp.dot(lmat, u, preferred_element_type=F32).astype(BF16)
            right = jnp.dot(rmat, u, preferred_element_type=F32).astype(BF16)
            for di in taps:
                k = 3 * (di + 1)
                w = [w_ref[k + j:k + j + 1, :].astype(BF16) for j in range(3)]
                off = pl.multiple_of(r0 + (cols if di != 0 else 0), cols)
                dst[di][idx, pl.ds(off, tile), :] = left * w[0] + u * w[1] + right * w[2]
        return carry

    def combine(i, carry):
        r0 = pl.multiple_of(i * tile, tile)
        vals = []
        for idx in range(2):
            acc = mid_ref[idx, pl.ds(r0, tile), :]
            if rows > 1:
                acc = (acc + up_ref[idx, pl.ds(r0, tile), :]
                       + dn_ref[idx, pl.ds(pl.multiple_of(r0 + 2 * cols, cols), tile), :])
            vals.append(acc.astype(F32))
        o_ref[0, pl.ds(r0, tile), :] = (_silu(vals[0]) * vals[1]).astype(BF16)
        return carry

    lax.fori_loop(0, n // tile, taps_of, 0, unroll=min(4, n // tile))
    lax.fori_loop(0, n // tile, combine, 0)


def _ffn_act(u, conv_w, rows, cols):
    bsz, n, _ = u.shape
    nblk = FFN_HIDDEN // FFN_WIDE
    padded = n + 2 * cols if rows > 1 else SUBLANE * 2
    return pl.pallas_call(
        functools.partial(_ffn_act_kernel, rows=rows, cols=cols, n=n),
        grid=(bsz, nblk),
        in_specs=[pl.BlockSpec((1, n, FFN_WIDE), lambda b, j: (b, 0, j)),
                  pl.BlockSpec((1, n, FFN_WIDE), lambda b, j: (b, 0, nblk + j)),
                  pl.BlockSpec((9, FFN_WIDE), lambda b, j: (0, j)),
                  pl.BlockSpec((9, FFN_WIDE), lambda b, j: (0, nblk + j))],
        out_specs=pl.BlockSpec((1, n, FFN_WIDE), lambda b, j: (b, 0, j)),
        out_shape=jax.ShapeDtypeStruct((bsz, n, FFN_HIDDEN), BF16),
        scratch_shapes=[pltpu.VMEM((2, padded, FFN_WIDE), BF16), pltpu.VMEM((2, n, FFN_WIDE), BF16),
                        pltpu.VMEM((2, padded, FFN_WIDE), BF16)],
        compiler_params=_cparams(("parallel", "parallel")),
        name="ffn_act",
    )(u, u, conv_w, conv_w)


def _gate_column_map():
    src = np.full((LANE,), -1, np.int32)
    isdec = np.zeros((1, LANE), np.float32)
    dirs = np.zeros((LANE,), np.int32)
    for h in range(DN_HEADS):
        for slot, (d, kind) in enumerate(((0, 0), (0, 1), (1, 0), (1, 1), (0, 0), (1, 0))):
            src[h * SUBLANE + slot] = d * 2 * DN_HEADS + kind * DN_HEADS + h
            isdec[0, h * SUBLANE + slot] = 1.0 if kind == 0 else 0.0
            dirs[h * SUBLANE + slot] = d
    return src, isdec, dirs


def _split_in_proj(w_in):
    o = np.cumsum((3 * DN_WIDTH, DN_WIDTH, 4 * DN_HEADS, 3 * HY_WIDTH, LRU_WIDTH, LRU_WIDTH)).tolist()
    w_main = jnp.concatenate([w_in[:, :o[1]], w_in[:, o[2]:]], axis=1).astype(BF16)
    src, _, _ = _gate_column_map()
    w_ab = w_in[:, o[1]:o[2]]
    w_gate = jnp.where(jnp.asarray(src >= 0)[None, :], w_ab[:, np.maximum(src, 0)], 0.0).astype(BF16)
    return w_main, w_gate


def _gate_params(a_log, dt_bias):
    _, isdec, dirs = _gate_column_map()
    head = (np.arange(LANE) // SUBLANE).astype(np.int32)
    alog_c = a_log[dirs, head].reshape(1, LANE)
    dtb_c = dt_bias[dirs, head].reshape(1, LANE)
    return alog_c, dtb_c, jnp.asarray(isdec)


def _token_views(bsz, n, mods, shared_mod):
    if not shared_mod:
        same = lambda t: t
        return same, same, mods
    flat = lambda t: t.reshape(1, bsz * n, t.shape[-1])
    unflat = lambda t: t.reshape(bsz, n, t.shape[-1])
    return flat, unflat, [m[:1] for m in mods]


def _token_mixer(x, mods, lp, states, tables, with_output, shared_mod):
    bsz, n, _ = x.shape
    flat, unflat, tmods = _token_views(bsz, n, mods, shared_mod)
    proj, ab = _norm_mod_matmul(flat(x), lp["norm1_g"], tmods[0], tmods[1], lp["w_main"], BF16, 2048, 1024,
                                lp["w_gate"])
    proj, ab = unflat(proj), unflat(ab)

    qkv_rm, qkv_tr = _dn_prep(proj, lp["dn_conv_w"])
    gates = _dn_gates(ab, *lp["gate_params"])
    o_dn, s_f, s_b = _delta_net(qkv_rm, qkv_tr, gates, states[0], states[1], with_output)
    o_lru, h_last = _rglru(proj, lp["lru_conv_w"], lp["lru_conv_b"], lp["lru_w_a"], lp["lru_b_a"],
                           lp["lru_w_x"], lp["lru_b_x"], lp["lru_lambda"], states[2], with_output)
    new_states = (s_f, s_b, h_last)
    if not with_output:
        return None, new_states

    m = min(n, HY_BLOCK)
    ctab, stab, cttab, sttab, cwtab, swtab = tables
    taps = _hy_filter(n, lp["hy_w1"], lp["hy_b1"], lp["hy_f1"], lp["hy_w2"], lp["hy_b2"], lp["hy_f2"], lp["hy_w3"])
    kc, ks = _hy_ktrans(cwtab, swtab, taps, m)
    x0, zz = _hy_prep(proj, lp["hy_conv_w"], lp["hy_conv_b"])
    a, bq = _hy_forward(ctab, stab, zz, kc, ks, m)
    o_hy = _hy_inverse(cttab, sttab, a, bq, x0, zz, lp["hy_bias"], m)

    x = _merge(flat(o_dn), lp["dn_norm_g"], flat(o_hy), flat(o_lru), flat(proj), lp["w_proj_dn"], lp["w_proj_hy"], lp["w_proj_lru"],
               lp["w_out"], flat(x), tmods[2])
    return unflat(x), new_states


def _conv_ffn(x, mods, lp, rows, cols, shared_mod, final_gain=None):
    bsz, n, _ = x.shape
    flat, unflat, tmods = _token_views(bsz, n, mods, shared_mod)
    u = _norm_mod_matmul(flat(x), lp["norm2_g"], tmods[3], tmods[4], lp["ffn_up"], BF16, 1024, FFN_HIDDEN)
    act = _ffn_act(unflat(u), lp["ffn_conv_w"], rows, cols)
    return unflat(_matmul_residual(flat(act), lp["ffn_down"], flat(x), tmods[5], final_gain))


def kernel(x, c, ctx, c_ctx, w_mod, b_mod, norm1_g, norm2_g, w_in, dn_conv_w, dn_a_log, dn_dt_bias, dn_norm_g,
           hy_conv_w, hy_conv_b, hy_w1, hy_b1, hy_f1, hy_w2, hy_b2, hy_f2, hy_w3, hy_bias,
           lru_conv_w, lru_conv_b, lru_w_a, lru_b_a, lru_w_x, lru_b_x, lru_lambda,
           w_proj_dn, w_proj_hy, w_proj_lru, w_out, ffn_up, ffn_conv_w, ffn_down, final_norm_g):
    bsz, n_lat, d = x.shape
    n_ctx = ctx.shape[1]
    depth = w_in.shape[0]
    rows = n_lat // GRID_W

    cvec = jnp.zeros((SUBLANE, d), F32).at[:bsz].set(c).at[bsz].set(c_ctx)
    lat_tables = _hy_tables(min(n_lat, HY_BLOCK))
    ctx_tables = _hy_tables(min(n_ctx, HY_BLOCK))
    zero_states = (jnp.zeros((bsz, DN_HEADS, LANE, LANE), F32), jnp.zeros((bsz, DN_HEADS, LANE, LANE), F32),
                   jnp.zeros((bsz, 2, LRU_WIDTH), F32))

    xc = ctx
    for l in range(depth):
        ctx_needed = l < depth - 1
        w_main, w_gate = _split_in_proj(w_in[l])
        lp = dict(
            norm1_g=norm1_g[l], norm2_g=norm2_g[l], w_main=w_main, w_gate=w_gate,
            dn_conv_w=dn_conv_w[l], gate_params=_gate_params(dn_a_log[l], dn_dt_bias[l]), dn_norm_g=dn_norm_g[l],
            hy_conv_w=hy_conv_w[l], hy_conv_b=hy_conv_b[l], hy_w1=hy_w1[l], hy_b1=hy_b1[l], hy_f1=hy_f1[l],
            hy_w2=hy_w2[l], hy_b2=hy_b2[l], hy_f2=hy_f2[l], hy_w3=hy_w3[l], hy_bias=hy_bias[l],
            lru_conv_w=lru_conv_w[l], lru_conv_b=lru_conv_b[l], lru_w_a=lru_w_a[l].astype(BF16),
            lru_b_a=lru_b_a[l], lru_w_x=lru_w_x[l].astype(BF16), lru_b_x=lru_b_x[l], lru_lambda=lru_lambda[l],
            w_proj_dn=w_proj_dn[l].astype(BF16), w_proj_hy=w_proj_hy[l].astype(BF16),
            w_proj_lru=w_proj_lru[l].astype(BF16), w_out=w_out[l].astype(BF16),
            ffn_up=ffn_up[l].astype(BF16), ffn_conv_w=ffn_conv_w[l].reshape(9, 2 * FFN_HIDDEN),
            ffn_down=ffn_down[l].astype(BF16))
        mod = _modulation(cvec, w_mod[l].astype(BF16), b_mod[l])
        lat_mod = [mod[:bsz, k * d:(k + 1) * d].reshape(bsz, 1, d) for k in range(N_MOD)]
        ctx_mod = [jnp.broadcast_to(mod[bsz:bsz + 1, k * d:(k + 1) * d].reshape(1, 1, d), (bsz, 1, d))
                   for k in range(N_MOD)]

        xc_new, ctx_states = _token_mixer(xc, ctx_mod, lp, zero_states, ctx_tables, ctx_needed, True)
        x, _ = _token_mixer(x, lat_mod, lp, ctx_states, lat_tables, True, False)
        x = _conv_ffn(x, lat_mod, lp, rows, GRID_W, False, None if ctx_needed else final_norm_g)
        if ctx_needed:
            xc = _conv_ffn(xc_new, ctx_mod, lp, 1, n_ctx, True)
    return x
```

```python
import functools
import math

import numpy as np
import jax
import jax.numpy as jnp
from jax import lax
from jax.experimental import pallas as pl
from jax.experimental.pallas import tpu as pltpu

F32 = jnp.float32
BF16 = jnp.bfloat16
HIGHEST = lax.Precision.HIGHEST

D_MODEL = 1024
DEPTH = 2
GRID_W = 64
NORM_EPS = 1e-6
N_MOD = 6

DN_HEADS = 8
DN_HEAD_DIM = 128
DN_WIDTH = DN_HEADS * DN_HEAD_DIM
HY_WIDTH = 1024
HY_EMB = 33
HY_BANDS = (HY_EMB - 1) // 2
HY_FILTER_HIDDEN = 64
HY_FAST_DECAY_PCT = 0.3
HY_SLOW_DECAY_PCT = 1.5
HY_DECAY_TARGET = 1e-2
LRU_WIDTH = 1024
LRU_BLOCKS = 8
LRU_BLOCK = LRU_WIDTH // LRU_BLOCKS
LRU_C = 8.0
FFN_HIDDEN = 2816

LANE = 128
SUBLANE = 8
TILE = 256
CHUNK = 128
DN_HEADS_PER_STEP = 4
DN_GATE_TILES = 4
DN_PREP_WIDE_ROWS = 512
DN_PACK = 64
HY_BLOCK = 1024
LRU_ROWS = 256
LRU_GROUP = 4
NORM_CHUNK = 256
FFN_TILE = 256
FFN_WIDE = 256
MIB = 1024 * 1024
VMEM_LIMIT_MIB = 48
DN_VMEM_LIMIT_MIB = 58

QKV_BLK = 0
Z_BLK = 24
HY_BLK = 32
LX_BLK = 56
LY_BLK = 64
GATE_BLK = 72
N_MAIN = 96 * LANE


def _cparams(sem, vmem_mib=VMEM_LIMIT_MIB):
    return pltpu.CompilerParams(dimension_semantics=sem, vmem_limit_bytes=vmem_mib * MIB)


def _sigmoid(x):
    return 1.0 / (1.0 + jnp.exp(-x))


def _silu(x):
    return x * _sigmoid(x)


def _softplus(x):
    return jnp.maximum(x, 0.0) + jnp.log(1.0 + jnp.exp(-jnp.abs(x)))


def _row_iota(shape):
    return lax.broadcasted_iota(jnp.int32, shape, 0)


def _col_iota(shape):
    return lax.broadcasted_iota(jnp.int32, shape, 1)


def _div_pow2(x, k):
    assert k & (k - 1) == 0
    return x >> (k.bit_length() - 1)


def _mod_pow2(x, k):
    assert k & (k - 1) == 0
    return x & (k - 1)


def _bdot(a, b):
    return jnp.dot(a.astype(BF16), b.astype(BF16), preferred_element_type=F32)


def _mod_kernel(c_ref, w_ref, b_ref, o_ref):
    o_ref[...] = _bdot(_silu(c_ref[...]), w_ref[...]) + b_ref[...]


def _modulation(cvec, w_mod, b_mod):
    n = w_mod.shape[1]
    tn = 1024
    return pl.pallas_call(
        _mod_kernel,
        grid=(n // tn,),
        in_specs=[pl.BlockSpec((SUBLANE, D_MODEL), lambda j: (0, 0)),
                  pl.BlockSpec((D_MODEL, tn), lambda j: (0, j)),
                  pl.BlockSpec((1, tn), lambda j: (0, j))],
        out_specs=pl.BlockSpec((SUBLANE, tn), lambda j: (0, j)),
        out_shape=jax.ShapeDtypeStruct((SUBLANE, n), F32),
        compiler_params=_cparams(("parallel",)),
        name="modulation",
    )(cvec, w_mod, b_mod.reshape(1, n))


def _nmm_kernel(x_ref, g_ref, sh_ref, sc_ref, w_ref, *rest, with_side):
    o_ref = rest[1] if with_side else rest[0]
    h_ref = rest[-1]

    first = pl.program_id(2) == 0

    @pl.when(first)
    def _():
        rows = x_ref.shape[1]
        chunk = min(rows, NORM_CHUNK)
        for r in range(0, rows, chunk):
            x = x_ref[0, r:r + chunk, :]
            y = x * lax.rsqrt(jnp.mean(x * x, axis=-1, keepdims=True) + NORM_EPS) * g_ref[...]
            h = (y * (1.0 + sc_ref[0]) + sh_ref[0]).astype(BF16)
            h_ref[r:r + chunk, :] = h
            if with_side:
                rest[2][0, r:r + chunk, :] = jnp.dot(h, rest[0][...], preferred_element_type=F32)
            o_ref[0, r:r + chunk, :] = jnp.dot(h, w_ref[...], preferred_element_type=F32).astype(o_ref.dtype)

    @pl.when(jnp.logical_not(first))
    def _():
        o_ref[0] = jnp.dot(h_ref[...], w_ref[...], preferred_element_type=F32).astype(o_ref.dtype)


def _norm_mod_matmul(x, gain, shift, scale, w, out_dtype, tm, tn, w_side=None):
    bsz, n, d = x.shape
    nout = w.shape[1]
    tm = min(n, tm)
    with_side = w_side is not None
    in_specs = [pl.BlockSpec((1, tm, d), lambda b, i, j: (b, i, 0)),
                pl.BlockSpec((1, d), lambda b, i, j: (0, 0)),
                pl.BlockSpec((1, 1, d), lambda b, i, j: (b, 0, 0)),
                pl.BlockSpec((1, 1, d), lambda b, i, j: (b, 0, 0)),
                pl.BlockSpec((d, tn), lambda b, i, j: (0, j))]
    out_specs = [pl.BlockSpec((1, tm, tn), lambda b, i, j: (b, i, j))]
    out_shape = [jax.ShapeDtypeStruct((bsz, n, nout), out_dtype)]
    args = [x, gain.reshape(1, d), shift, scale, w]
    if with_side:
        ns = w_side.shape[1]
        in_specs.append(pl.BlockSpec((d, ns), lambda b, i, j: (0, 0)))
        out_specs.append(pl.BlockSpec((1, tm, ns), lambda b, i, j: (b, i, 0)))
        out_shape.append(jax.ShapeDtypeStruct((bsz, n, ns), F32))
        args.append(w_side)
    out = pl.pallas_call(
        functools.partial(_nmm_kernel, with_side=with_side),
        grid=(bsz, n // tm, nout // tn),
        in_specs=in_specs,
        out_specs=out_specs,
        out_shape=out_shape,
        scratch_shapes=[pltpu.VMEM((tm, d), BF16)],
        compiler_params=_cparams(("parallel", "parallel", "arbitrary")),
        name="norm_mod_matmul",
    )(*args)
    return out if with_side else out[0]


def _mm_kernel(a_ref, b_ref, o_ref):
    o_ref[...] = jnp.dot(a_ref[...], b_ref[...], preferred_element_type=F32).astype(o_ref.dtype)


def _matmul(a, b, out_dtype, tm, tn):
    m, k = a.shape
    n = b.shape[1]
    return pl.pallas_call(
        _mm_kernel,
        grid=(m // tm, n // tn),
        in_specs=[pl.BlockSpec((tm, k), lambda i, j: (i, 0)),
                  pl.BlockSpec((k, tn), lambda i, j: (0, j))],
        out_specs=pl.BlockSpec((tm, tn), lambda i, j: (i, j)),
        out_shape=jax.ShapeDtypeStruct((m, n), out_dtype),
        compiler_params=_cparams(("parallel", "parallel")),
        name="matmul",
    )(a, b)


def _mm_res_kernel(a_ref, b_ref, x_ref, g_ref, *rest, final_norm):
    o_ref = rest[-1]
    out = x_ref[0] + g_ref[0] * jnp.dot(a_ref[0], b_ref[...], preferred_element_type=F32)
    if final_norm:
        out = out * lax.rsqrt(jnp.mean(out * out, axis=-1, keepdims=True) + NORM_EPS) * rest[0][...]
    o_ref[0] = out


def _matmul_residual(a, w, x, gate, final_gain=None):
    bsz, n, k = a.shape
    d = w.shape[1]
    tm = min(n, 512)
    in_specs = [pl.BlockSpec((1, tm, k), lambda b, i: (b, i, 0)),
                pl.BlockSpec((k, d), lambda b, i: (0, 0)),
                pl.BlockSpec((1, tm, d), lambda b, i: (b, i, 0)),
                pl.BlockSpec((1, 1, d), lambda b, i: (b, 0, 0))]
    args = [a, w, x, gate]
    if final_gain is not None:
        in_specs.append(pl.BlockSpec((1, d), lambda b, i: (0, 0)))
        args.append(final_gain.reshape(1, d))
    return pl.pallas_call(
        functools.partial(_mm_res_kernel, final_norm=final_gain is not None),
        grid=(bsz, n // tm),
        in_specs=in_specs,
        out_specs=pl.BlockSpec((1, tm, d), lambda b, i: (b, i, 0)),
        out_shape=jax.ShapeDtypeStruct((bsz, n, d), F32),
        compiler_params=_cparams(("parallel", "parallel")),
        name="matmul_residual",
    )(*args)


def _dwconv_rows(x, w_ref, k):
    n = x.shape[0]
    left = (k - 1) // 2
    t = _row_iota((SUBLANE, x.shape[1]))
    acc = x * w_ref[left:left + 1, :]
    head_fix = jnp.zeros((SUBLANE, x.shape[1]), F32)
    tail_fix = jnp.zeros((SUBLANE, x.shape[1]), F32)
    for j in range(k):
        off = j - left
        if off == 0:
            continue
        term = pltpu.roll(x, (-off) % n, 0) * w_ref[j:j + 1, :]
        acc = acc + term
        if off < 0:
            head_fix = head_fix + jnp.where(t < -off, term[:SUBLANE, :], 0.0)
        else:
            tail_fix = tail_fix + jnp.where(t >= SUBLANE - off, term[n - SUBLANE:, :], 0.0)
    return jnp.concatenate([acc[:SUBLANE, :] - head_fix, acc[SUBLANE:n - SUBLANE, :],
                            acc[n - SUBLANE:, :] - tail_fix], axis=0)


def _dn_prep_kernel(p_ref, w_ref, rm_ref, tr_ref, *, n_tiles, heads):
    first = pl.program_id(1) * heads
    y_all = _silu(_dwconv_rows(p_ref[0].astype(F32), w_ref, 4))
    q_scale = jnp.where(first < DN_HEADS, DN_HEAD_DIM ** -0.5, 1.0)
    for h in range(heads):
        y = y_all[:, h * LANE:(h + 1) * LANE]
        unit = lax.rsqrt(jnp.sum(y * y, axis=-1, keepdims=True) + 1e-6) * q_scale
        y = y * jnp.where(first < 2 * DN_HEADS, unit, 1.0)
        rm_ref[0, :, h * LANE:(h + 1) * LANE] = y.astype(BF16)
        for t in range(n_tiles):
            tr_ref[0, h, t] = y[t * TILE:(t + 1) * TILE, :].T.astype(BF16)


def _dn_prep(proj, conv_w):
    bsz, n, _ = proj.shape
    nt = n // TILE
    nc = 3 * DN_HEADS
    heads = DN_HEADS if n <= DN_PREP_WIDE_ROWS else 2
    wide = heads * LANE
    return pl.pallas_call(
        functools.partial(_dn_prep_kernel, n_tiles=nt, heads=heads),
        grid=(bsz, nc // heads),
        in_specs=[pl.BlockSpec((1, n, wide), lambda b, c: (b, 0, QKV_BLK // heads + c)),
                  pl.BlockSpec((4, wide), lambda b, c: (0, c))],
        out_specs=[pl.BlockSpec((1, n, wide), lambda b, c: (b, 0, c)),
                   pl.BlockSpec((1, heads, nt, LANE, TILE), lambda b, c: (b, c, 0, 0, 0))],
        out_shape=[jax.ShapeDtypeStruct((bsz, n, nc * LANE), BF16),
                   jax.ShapeDtypeStruct((bsz, nc, nt, LANE, TILE), BF16)],
        compiler_params=_cparams(("parallel", "parallel")),
        name="dn_prep",
    )(proj, conv_w)


def _dn_gate_kernel(ab_ref, alog_ref, dtb_ref, isdec_ref, o_ref, *, tiles):
    s = _row_iota((TILE, TILE))
    t = _col_iota((TILE, TILE))
    same = _div_pow2(s, CHUNK) == _div_pow2(t, CHUNK)
    ones = [jnp.where(m, 1.0, 0.0).astype(BF16) for m in (same & (s <= t), same & (s >= t), same)]
    slot = _mod_pow2(_row_iota((LANE, TILE)), SUBLANE)
    for k in range(tiles):
        x = ab_ref[0, k * TILE:(k + 1) * TILE, :]
        dec = -jnp.exp(alog_ref[...]) * _softplus(x + dtb_ref[...])
        e = jnp.where(isdec_ref[...] > 0.5, dec, _sigmoid(x))
        et = e.T
        hi = et.astype(BF16)
        rest = et - hi.astype(F32)
        mid = rest.astype(BF16)
        lo = (rest - mid.astype(F32)).astype(BF16)
        pre, suf, tot = [_bdot(hi, m) + _bdot(mid, m) + _bdot(lo, m) for m in ones]
        o_ref[0, k] = jnp.where(slot == 0, pre, jnp.where(slot == 2, suf, jnp.where(slot >= 4, tot, et)))


def _dn_gates(ab, alog_c, dtb_c, isdec_c):
    bsz, n, _ = ab.shape
    nt = n // TILE
    tiles = min(nt, DN_GATE_TILES)
    vec = pl.BlockSpec((1, LANE), lambda b, i: (0, 0))
    return pl.pallas_call(
        functools.partial(_dn_gate_kernel, tiles=tiles),
        grid=(bsz, nt // tiles),
        in_specs=[pl.BlockSpec((1, tiles * TILE, LANE), lambda b, i: (b, i, 0)), vec, vec, vec],
        out_specs=pl.BlockSpec((1, tiles, LANE, TILE), lambda b, i: (b, i, 0, 0)),
        out_shape=jax.ShapeDtypeStruct((bsz, nt, LANE, TILE), F32),
        compiler_params=_cparams(("parallel", "parallel")),
        name="dn_gates",
    )(ab, alog_c, dtb_c, isdec_c)


def _dn_masks(mask_ref):
    a = _row_iota((TILE, TILE))
    b = _col_iota((TILE, TILE))
    apart = a ^ b
    n_levels = CHUNK.bit_length() - 2
    for lg in range(1, n_levels + 1):
        mask_ref[lg - 1] = jnp.where((apart >> lg) == 1, 1.0, 0.0).astype(BF16)
    for k, pack in enumerate((DN_PACK, 2 * DN_PACK)):
        mask_ref[n_levels + k] = jnp.where(_div_pow2(a, pack) == _div_pow2(b, pack), 1.0, 0.0).astype(BF16)


def _dn_tiles(chains, mask_ref):
    a = _row_iota((TILE, TILE))
    b = _col_iota((TILE, TILE))
    same = _div_pow2(a, CHUNK) == _div_pow2(b, CHUNK)
    apart = a ^ b
    n_chunks = TILE // CHUNK
    dot = functools.partial(jnp.dot, preferred_element_type=F32)

    def stack(v, reps):
        return jnp.concatenate([v] * reps, axis=0)

    kk = [dot(c["k_rm"], c["kt"]) for c in chains]
    kq = [dot(c["k_rm"], c["qt"]) for c in chains]
    xs, pw, attn = [], [], []
    for c, kk_c, kq_c in zip(chains, kk, kq):
        incl = same & ((a >= b) if c["backward"] else (a <= b))
        gcb = jnp.broadcast_to(c["gc"], (TILE, TILE))
        diff = gcb - gcb.T
        decay = jnp.where(incl, jnp.exp(jnp.where(incl, diff, 0.0)), 0.0)
        attn.append((kq_c * decay).astype(BF16))
        x = kk_c * decay * (-c["beta"])
        xs.append(x.astype(BF16))
        base = jnp.where(a == b, 1.0, jnp.where(apart == 1, x, 0.0))
        acc = base[0:DN_PACK, :]
        for r in range(1, TILE // DN_PACK):
            acc = acc + base[r * DN_PACK:(r + 1) * DN_PACK, :]
        pw.append(acc)
    pack = DN_PACK
    n_levels = CHUNK.bit_length() - 2
    block_mask = {DN_PACK: n_levels, 2 * DN_PACK: n_levels + 1}
    s = 2
    while s < CHUNK:
        if s == pack:
            keep = _div_pow2(_row_iota((2 * pack, TILE)), pack) == (_div_pow2(_col_iota((2 * pack, TILE)), pack) & 1)
            pw = [jnp.where(keep, stack(p, 2), 0.0) for p in pw]
            pack *= 2
        couple = mask_ref[s.bit_length() - 2]
        blocks = mask_ref[block_mask[pack]]
        pb = [p.astype(BF16) for p in pw]
        px = [dot(pb_c, x * couple) for pb_c, x in zip(pb, xs)]
        p_bd = [stack(pb_c, TILE // pack) * blocks for pb_c in pb]
        pw = [p + dot(px_c.astype(BF16), bd_c) for p, px_c, bd_c in zip(pw, px, p_bd)]
        s *= 2
    blocks = mask_ref[block_mask[pack]]
    t_inv = [stack(p.astype(BF16), TILE // pack) * blocks for p in pw]
    egc = [jnp.exp(c["gc"]) for c in chains]
    u_t = [dot((c["vt"].astype(F32) * c["beta"]).astype(BF16), t) for c, t in zip(chains, t_inv)]
    w_t = [dot((c["kt"].astype(F32) * (c["beta"] * e)).astype(BF16), t).astype(BF16)
           for c, e, t in zip(chains, egc, t_inv)]
    qd_t = [(c["qt"].astype(F32) * e).astype(BF16) for c, e in zip(chains, egc)]
    kdec = [jnp.exp(c["tot"] - c["gc"]) for c in chains]
    outs = [[None] * n_chunks for _ in chains]
    for step in range(n_chunks):
        cis = [(n_chunks - 1 - step) if c["backward"] else step for c in chains]
        sl = [slice(ci * CHUNK, (ci + 1) * CHUNK) for ci in cis]
        st = [c["st_ref"][...] for c in chains]
        stb = [s_c.astype(BF16) for s_c in st]
        swq = [dot(stb_c, jnp.concatenate([w_c[:, r], q_c[:, r]], axis=1))
               for stb_c, w_c, q_c, r in zip(stb, w_t, qd_t, sl)]
        sq = [v[:, CHUNK:] for v in swq]
        vn = [u_c[:, r] - v[:, :CHUNK] for u_c, v, r in zip(u_t, swq, sl)]
        av = [dot(vn_c.astype(BF16), at_c[r, r]) for vn_c, at_c, r in zip(vn, attn, sl)]
        upd = [dot((vn_c * kd_c[:, r]).astype(BF16), c["k_rm"][r, :])
               for vn_c, kd_c, c, r in zip(vn, kdec, chains, sl)]
        for idx, c in enumerate(chains):
            c["st_ref"][...] = st[idx] * jnp.exp(c["tot"][:, sl[idx]]) + upd[idx]
            outs[idx][cis[idx]] = sq[idx] + av[idx]
    return [jnp.concatenate(o, axis=1) for o in outs]


def _dn_kernel(k_ref, qt_ref, kt_ref, vt_ref, g_ref, s0f_ref, s0b_ref,
               o_ref, sf_ref, sb_ref, ot_ref, st_ref, mask_ref, *, n_tiles, with_output):
    hb = DN_HEADS_PER_STEP
    _dn_masks(mask_ref)
    for hh in range(hb):
        st_ref[2 * hh] = s0f_ref[0, hh]
        st_ref[2 * hh + 1] = s0b_ref[0, hh]

    def body(i, carry):
        nf = i
        nb = n_tiles - 1 - i
        chains = []
        for hh in range(hb):
            for backward, n in ((False, nf), (True, nb)):
                g = g_ref[0, n, hh * SUBLANE:(hh + 1) * SUBLANE, :]
                base = 2 if backward else 0
                chains.append(dict(
                    k_rm=k_ref[0, pl.ds(pl.multiple_of(n * TILE, TILE), TILE), hh * LANE:(hh + 1) * LANE],
                    qt=qt_ref[0, hh, n], kt=kt_ref[0, hh, n], vt=vt_ref[0, hh, n],
                    gc=g[base:base + 1, :], beta=g[base + 1:base + 2, :], tot=g[4 + base // 2:5 + base // 2, :],
                    st_ref=st_ref.at[2 * hh + (1 if backward else 0)], backward=backward, hh=hh, n=n))
        o_t = _dn_tiles(chains, mask_ref)
        if with_output:
            for c, o_c in zip(chains, o_t):
                ot_ref[c["hh"], c["n"]] = ot_ref[c["hh"], c["n"]] + o_c
        return carry

    if with_output:
        ot_ref[...] = jnp.zeros_like(ot_ref)
    lax.fori_loop(0, n_tiles, body, 0)
    for hh in range(hb):
        sf_ref[0, hh] = st_ref[2 * hh]
        sb_ref[0, hh] = st_ref[2 * hh + 1]
    if with_output:
        def finish(t, carry):
            rows = pl.ds(pl.multiple_of(t * TILE, TILE), TILE)
            for hh in range(hb):
                o_ref[0, rows, hh * LANE:(hh + 1) * LANE] = ot_ref[hh, t].T.astype(BF16)
            return carry

        lax.fori_loop(0, n_tiles, finish, 0)
    else:
        o_ref[...] = jnp.zeros_like(o_ref)


def _delta_net(qkv_rm, qkv_tr, gates, s0f, s0b, with_output):
    bsz, n, _ = qkv_rm.shape
    nt = n // TILE
    h = DN_HEADS
    hb = DN_HEADS_PER_STEP
    wide = hb * LANE
    once = pl.Buffered(1)
    tr_spec = lambda off: pl.BlockSpec((1, hb, nt, LANE, TILE), lambda b, j: (b, off + j, 0, 0, 0))
    st_spec = pl.BlockSpec((1, hb, LANE, LANE), lambda b, j: (b, j, 0, 0))
    n_out = n if with_output else SUBLANE
    return pl.pallas_call(
        functools.partial(_dn_kernel, n_tiles=nt, with_output=with_output),
        grid=(bsz, h // hb),
        in_specs=[pl.BlockSpec((1, n, wide), lambda b, j: (b, 0, h // hb + j), once),
                  tr_spec(0), tr_spec(h // hb), tr_spec(2 * h // hb),
                  pl.BlockSpec((1, nt, hb * SUBLANE, TILE), lambda b, j: (b, 0, j, 0)),
                  st_spec, st_spec],
        out_specs=[pl.BlockSpec((1, n_out, wide), lambda b, j: (b, 0, j)), st_spec, st_spec],
        out_shape=[jax.ShapeDtypeStruct((bsz, n_out, DN_WIDTH), BF16),
                   jax.ShapeDtypeStruct((bsz, h, LANE, LANE), F32),
                   jax.ShapeDtypeStruct((bsz, h, LANE, LANE), F32)],
        scratch_shapes=[pltpu.VMEM((hb, nt, LANE, TILE), F32),
                        pltpu.VMEM((2 * hb, LANE, LANE), F32),
                        pltpu.VMEM((CHUNK.bit_length(), TILE, TILE), BF16)],
        compiler_params=_cparams(("parallel", "parallel"), DN_VMEM_LIMIT_MIB),
        name="delta_net",
    )(qkv_rm, qkv_tr, qkv_tr, qkv_tr, gates, s0f, s0b)


def _lru_scan_block(x, wa, ba, wx, bx, spl, h_in, backward):
    rows, width = x.shape
    xb = x.astype(BF16)

    def gate(ws, bias):
        parts = [jnp.dot(xb[:, k * LRU_BLOCK:(k + 1) * LRU_BLOCK], w, preferred_element_type=F32)
                 for k, w in enumerate(ws)]
        return _sigmoid(jnp.concatenate(parts, axis=1) + bias)

    r = gate(wa, ba)
    gi = gate(wx, bx)
    log_a = -LRU_C * r * spl
    a = jnp.exp(log_a)
    b = jnp.sqrt(1.0 - jnp.exp(2.0 * log_a)) * (gi * x)
    groups = rows // SUBLANE
    a = a.reshape(groups, SUBLANE, width)
    b = b.reshape(groups, SUBLANE, width)
    sub = lax.broadcasted_iota(jnp.int32, a.shape, 1)
    s = 1
    while s < SUBLANE:
        if backward:
            keep = sub < SUBLANE - s
            a_sh = jnp.where(keep, pltpu.roll(a, SUBLANE - s, 1), 1.0)
            b_sh = jnp.where(keep, pltpu.roll(b, SUBLANE - s, 1), 0.0)
        else:
            keep = sub >= s
            a_sh = jnp.where(keep, pltpu.roll(a, s, 1), 1.0)
            b_sh = jnp.where(keep, pltpu.roll(b, s, 1), 0.0)
        b = a * b_sh + b
        a = a * a_sh
        s *= 2
    a = a.reshape(rows, width)
    b = b.reshape(rows, width)
    pieces = [None] * groups
    carry = h_in
    order = range(groups - 1, -1, -1) if backward else range(groups)
    edge = 0 if backward else SUBLANE - 1
    for gidx in order:
        lo = gidx * SUBLANE
        hgrp = b[lo:lo + SUBLANE, :] + a[lo:lo + SUBLANE, :] * carry
        pieces[gidx] = hgrp
        carry = hgrp[edge:edge + 1, :]
    return jnp.concatenate(pieces, axis=0), carry


def _lru_kernel(px_ref, py_ref, cw_ref, cb_ref, wa_ref, ba_ref, wx_ref, bx_ref, lam_ref, h0_ref,
                o_ref, last_ref, xs_ref, hs_ref, *, n_blocks, with_output):
    x = px_ref[0].astype(F32)
    xs_ref[...] = _dwconv_rows(x, cw_ref, 4) + cb_ref[...]
    spl = _softplus(-lam_ref[...])
    wa = [[wa_ref[d, k] for k in range(LRU_GROUP)] for d in range(2)]
    wx = [[wx_ref[d, k] for k in range(LRU_GROUP)] for d in range(2)]

    def body(i, carry):
        hf, hb = carry
        rf = pl.multiple_of(i * LRU_ROWS, LRU_ROWS)
        rb = pl.multiple_of((n_blocks - 1 - i) * LRU_ROWS, LRU_ROWS)
        h_f, hf = _lru_scan_block(xs_ref[pl.ds(rf, LRU_ROWS), :], wa[0], ba_ref[0:1, :],
                                  wx[0], bx_ref[0:1, :], spl[0:1, :], hf, False)
        h_b, hb = _lru_scan_block(xs_ref[pl.ds(rb, LRU_ROWS), :], wa[1], ba_ref[1:2, :],
                                  wx[1], bx_ref[1:2, :], spl[1:2, :], hb, True)
        if with_output:
            hs_ref[pl.ds(rf, LRU_ROWS), :] = hs_ref[pl.ds(rf, LRU_ROWS), :] + h_f
            hs_ref[pl.ds(rb, LRU_ROWS), :] = hs_ref[pl.ds(rb, LRU_ROWS), :] + h_b
        return hf, hb

    if with_output:
        hs_ref[...] = jnp.zeros_like(hs_ref)
    h0 = h0_ref[0]
    hf, hb = lax.fori_loop(0, n_blocks, body, (h0[0:1, :], h0[1:2, :]))
    last_ref[0] = jnp.concatenate([hf, hb], axis=0)
    if with_output:
        y = py_ref[0].astype(F32)
        gelu = 0.5 * y * (1.0 + jnp.tanh(math.sqrt(2.0 / math.pi) * (y + 0.044715 * (y * y * y))))
        o_ref[0] = (hs_ref[...] * gelu).astype(BF16)
    else:
        o_ref[...] = jnp.zeros_like(o_ref)


def _rglru(proj, conv_w, conv_b, w_a, b_a, w_x, b_x, lam, h0, with_output):
    bsz, n, _ = proj.shape
    rows = min(n, LRU_ROWS)
    nb = n // rows
    n_out = n if with_output else SUBLANE
    width = LRU_GROUP * LRU_BLOCK
    lx = LX_BLK * LANE // width
    ly = LY_BLK * LANE // width
    vec2 = pl.BlockSpec((2, width), lambda b, j: (0, j))
    wspec = pl.BlockSpec((2, LRU_GROUP, LRU_BLOCK, LRU_BLOCK), lambda b, j: (0, j, 0, 0))
    return pl.pallas_call(
        functools.partial(_lru_kernel, n_blocks=nb, with_output=with_output),
        grid=(bsz, LRU_BLOCKS // LRU_GROUP),
        in_specs=[pl.BlockSpec((1, n, width), lambda b, j: (b, 0, lx + j)),
                  pl.BlockSpec((1, n, width), lambda b, j: (b, 0, ly + j)),
                  pl.BlockSpec((4, width), lambda b, j: (0, j)),
                  pl.BlockSpec((1, width), lambda b, j: (0, j)),
                  wspec, vec2, wspec, vec2, vec2,
                  pl.BlockSpec((1, 2, width), lambda b, j: (b, 0, j))],
        out_specs=[pl.BlockSpec((1, n_out, width), lambda b, j: (b, 0, j)),
                   pl.BlockSpec((1, 2, width), lambda b, j: (b, 0, j))],
        out_shape=[jax.ShapeDtypeStruct((bsz, n_out, LRU_WIDTH), BF16),
                   jax.ShapeDtypeStruct((bsz, 2, LRU_WIDTH), F32)],
        scratch_shapes=[pltpu.VMEM((n, width), F32), pltpu.VMEM((n, width), F32)],
        compiler_params=_cparams(("parallel", "parallel")),
        name="rglru",
    )(proj, proj, conv_w, conv_b.reshape(1, LRU_WIDTH), w_a, b_a, w_x, b_x, lam, h0)


def _hy_prep_kernel(p0_ref, p1_ref, pv_ref, w0_ref, w1_ref, wv_ref, b0_ref, b1_ref, bv_ref, x0_ref, zz_ref):
    x0 = _dwconv_rows(p0_ref[0].astype(F32), w0_ref, 3) + b0_ref[...]
    x1 = _dwconv_rows(p1_ref[0].astype(F32), w1_ref, 3) + b1_ref[...]
    v = _dwconv_rows(pv_ref[0].astype(F32), wv_ref, 3) + bv_ref[...]
    x0_ref[0] = x0.astype(BF16)
    zz_ref[0] = (x1 * v).astype(BF16)


def _hy_prep(proj, conv_w, conv_b):
    bsz, n, _ = proj.shape
    nblk = HY_WIDTH // LANE
    pspec = lambda off: pl.BlockSpec((1, n, LANE), lambda b, j: (b, 0, HY_BLK + off + j))
    wspec = lambda off: pl.BlockSpec((3, LANE), lambda b, j: (0, off + j))
    bspec = lambda off: pl.BlockSpec((1, LANE), lambda b, j: (0, off + j))
    ospec = pl.BlockSpec((1, n, LANE), lambda b, j: (b, 0, j))
    cb = conv_b.reshape(1, 3 * HY_WIDTH)
    return pl.pallas_call(
        _hy_prep_kernel,
        grid=(bsz, nblk),
        in_specs=[pspec(0), pspec(nblk), pspec(2 * nblk), wspec(0), wspec(nblk), wspec(2 * nblk),
                  bspec(0), bspec(nblk), bspec(2 * nblk)],
        out_specs=[ospec, ospec],
        out_shape=[jax.ShapeDtypeStruct((bsz, n, HY_WIDTH), BF16)] * 2,
        compiler_params=_cparams(("parallel", "parallel")),
        name="hy_prep",
    )(proj, proj, proj, conv_w, conv_w, conv_w, cb, cb, cb)


def _hy_filter_kernel(w1_ref, b1_ref, f1_ref, w2_ref, b2_ref, f2_ref, w3_ref, band_ref, delta_ref,
                      k_ref, *, n, rows):
    def lag(shape):
        p = _row_iota(shape) + (pl.program_id(0) * rows - n)
        return p, jnp.abs(p).astype(F32)

    _, i = lag((rows, LANE))
    lane = _col_iota((rows, LANE))
    t = i * (1.0 / (n - 1))
    ang = band_ref[...] * (i * (2.0 * math.pi / n))
    feat = jnp.where(lane == 0, t,
                     jnp.where(lane <= HY_BANDS, jnp.cos(ang), jnp.where(lane < HY_EMB, -jnp.sin(ang), 0.0)))
    hid = jnp.sin(f1_ref[...] * (jnp.dot(feat, w1_ref[...], precision=HIGHEST,
                                         preferred_element_type=F32) + b1_ref[...]))
    hid = jnp.sin(f2_ref[...] * (jnp.dot(hid, w2_ref[...], precision=HIGHEST,
                                         preferred_element_type=F32) + b2_ref[...]))
    filt = _bdot(hid, w3_ref[...])
    p, iw = lag((rows, HY_WIDTH))
    dec = jnp.exp(-(iw * (1.0 / (n - 1))) * delta_ref[...])
    h_f = filt[:, :HY_WIDTH] * dec
    h_b = filt[:, HY_WIDTH:] * dec
    taps = jnp.where(p > 0, h_f, jnp.where(p < 0, h_b, h_f + h_b))
    k_ref[...] = jnp.where(p == -n, 0.0, taps).astype(BF16)


def _hy_filter(n, w1, b1, f1, w2, b2, f2, w3):
    rows = min(n, 256)
    hid = HY_FILTER_HIDDEN
    w1p = jnp.zeros((LANE, hid), F32).at[:HY_EMB].set(w1)
    bands = np.zeros((1, LANE), np.float32)
    base = np.linspace(1e-4, HY_BANDS - 1, HY_BANDS, dtype=np.float32)
    bands[0, 1:1 + HY_BANDS] = base
    bands[0, 1 + HY_BANDS:HY_EMB] = base
    log_target = math.log(HY_DECAY_TARGET)
    deltas = np.abs(np.linspace(log_target / HY_SLOW_DECAY_PCT, log_target / HY_FAST_DECAY_PCT, HY_WIDTH,
                                dtype=np.float32)).reshape(1, HY_WIDTH)
    full = lambda shape: pl.BlockSpec(shape, lambda i: (0,) * len(shape))
    return pl.pallas_call(
        functools.partial(_hy_filter_kernel, n=n, rows=rows),
        grid=(2 * n // rows,),
        in_specs=[full((LANE, hid)), full((1, hid)), full((1, hid)), full((hid, hid)), full((1, hid)),
                  full((1, hid)), full((hid, 2 * HY_WIDTH)), full((1, LANE)), full((1, HY_WIDTH))],
        out_specs=pl.BlockSpec((rows, HY_WIDTH), lambda i: (i, 0)),
        out_shape=jax.ShapeDtypeStruct((2 * n, HY_WIDTH), BF16),
        compiler_params=_cparams(("parallel",)),
        name="hy_filter",
    )(w1p, b1.reshape(1, hid), f1.reshape(1, hid), w2, b2.reshape(1, hid), f2.reshape(1, hid), w3,
      jnp.asarray(bands), jnp.asarray(deltas))


DFT_GROUP = 64


def _dft_table_kernel(c_ref, s_ref, cb_ref, sb_ref, *, m, ncols, col0, rows, transposed, blank_first):
    period = 4 * m
    scale = 2.0 * math.pi / period
    col = _col_iota((1, ncols)) + col0

    @pl.when(pl.program_id(0) == 0)
    def _():
        r2 = _row_iota((DFT_GROUP, ncols))
        c2 = _col_iota((DFT_GROUP, ncols)) + col0
        ph = (r2 * (2 * c2 + 1)) if transposed else ((2 * r2 + 1) * c2)
        ang = (ph & (period - 1)).astype(F32) * scale
        cb_ref[...] = jnp.cos(ang)
        sb_ref[...] = jnp.sin(ang)

    for g in range(rows // DFT_GROUP):
        r1 = pl.program_id(0) * (rows // DFT_GROUP) + g
        ph = (DFT_GROUP * r1) * (2 * col + 1) if transposed else (2 * DFT_GROUP * r1) * col
        ang = (ph & (period - 1)).astype(F32) * scale
        ca = jnp.cos(ang)
        sa = jnp.sin(ang)
        cb = cb_ref[...]
        sb = sb_ref[...]
        c_tile = ca * cb - sa * sb
        s_tile = sa * cb + ca * sb
        if blank_first:
            first = _col_iota((DFT_GROUP, ncols)) == 0
            c_tile = jnp.where(first, 0.0, c_tile)
            s_tile = jnp.where(first, 0.0, s_tile)
        c_ref[g * DFT_GROUP:(g + 1) * DFT_GROUP, :] = c_tile.astype(BF16)
        s_ref[g * DFT_GROUP:(g + 1) * DFT_GROUP, :] = s_tile.astype(BF16)


def _dft_tables(m, transposed=False, two_sided=False):
    rows = min(m, 256)
    ncols = 2 * m if two_sided else m
    spec = pl.BlockSpec((rows, ncols), lambda i: (i, 0))
    return pl.pallas_call(
        functools.partial(_dft_table_kernel, m=m, ncols=ncols, col0=3 * m if two_sided else 0, rows=rows,
                          transposed=transposed, blank_first=two_sided),
        grid=(m // rows,),
        in_specs=[],
        out_specs=[spec, spec],
        out_shape=[jax.ShapeDtypeStruct((m, ncols), BF16)] * 2,
        scratch_shapes=[pltpu.VMEM((DFT_GROUP, ncols), F32), pltpu.VMEM((DFT_GROUP, ncols), F32)],
        compiler_params=_cparams(("arbitrary",)),
        name="dft_tables",
    )()


def _hy_tables(m):
    return _dft_tables(m) + _dft_tables(m, transposed=True) + _dft_tables(m, two_sided=True)


def _hy_ktrans_kernel(cw_ref, sw_ref, lo_ref, hi_ref, kc_ref, ks_ref, *, m):
    dot = functools.partial(jnp.dot, preferred_element_type=F32)
    lo = lo_ref[...]
    hi = hi_ref[...]
    kc_ref[0] = dot(cw_ref[:, :m], lo) + dot(cw_ref[:, m:], hi)
    ks_ref[0] = dot(sw_ref[:, :m], lo) + dot(sw_ref[:, m:], hi)


def _hy_ktrans(cw, sw, taps, m):
    nd = taps.shape[0] // m - 1
    w = taps.shape[1]
    tn = 512
    tab = pl.BlockSpec((m, 2 * m), lambda e, j: (0, 0))
    ospec = pl.BlockSpec((1, m, tn), lambda e, j: (e, 0, j))
    return pl.pallas_call(
        functools.partial(_hy_ktrans_kernel, m=m),
        grid=(nd, w // tn),
        in_specs=[tab, tab, pl.BlockSpec((m, tn), lambda e, j: (e, j)), pl.BlockSpec((m, tn), lambda e, j: (e + 1, j))],
        out_specs=[ospec, ospec],
        out_shape=[jax.ShapeDtypeStruct((nd, m, w), F32)] * 2,
        compiler_params=_cparams(("parallel", "parallel")),
        name="hy_ktrans",
    )(cw, sw, taps, taps)


def _hy_fwd_kernel(c_ref, s_ref, zz_ref, kc_ref, ks_ref, a_ref, b_ref, *, m, nb):
    dot = functools.partial(jnp.dot, preferred_element_type=F32)
    c = c_ref[...]
    s = s_ref[...]
    acc_a = [None] * nb
    acc_b = [None] * nb
    for j in range(nb):
        zz = zz_ref[0, j * m:(j + 1) * m, :]
        uc = dot(c, zz)
        us = dot(s, zz)
        for i in range(nb):
            kc = kc_ref[i - j + nb - 1]
            ks = ks_ref[i - j + nb - 1]
            ta = uc * kc - us * ks
            tb = uc * ks + us * kc
            acc_a[i] = ta if acc_a[i] is None else acc_a[i] + ta
            acc_b[i] = tb if acc_b[i] is None else acc_b[i] + tb
    for i in range(nb):
        a_ref[0, i] = acc_a[i].astype(BF16)
        b_ref[0, i] = acc_b[i].astype(BF16)


def _hy_forward(ctab, stab, zz, kc, ks, m):
    bsz, n, w = zz.shape
    nb = n // m
    nd = 2 * nb - 1
    tf = min(m, 256)
    tn = 256
    tab = pl.BlockSpec((tf, m), lambda b, j, i: (i, 0))
    kspec = pl.BlockSpec((nd, tf, tn), lambda b, j, i: (0, i, j))
    ospec = pl.BlockSpec((1, nb, tf, tn), lambda b, j, i: (b, 0, i, j))
    return pl.pallas_call(
        functools.partial(_hy_fwd_kernel, m=m, nb=nb),
        grid=(bsz, w // tn, m // tf),
        in_specs=[tab, tab, pl.BlockSpec((1, n, tn), lambda b, j, i: (b, 0, j)), kspec, kspec],
        out_specs=[ospec, ospec],
        out_shape=[jax.ShapeDtypeStruct((bsz, nb, m, w), BF16)] * 2,
        compiler_params=_cparams(("parallel", "parallel", "parallel")),
        name="hy_forward",
    )(ctab, stab, zz, kc, ks)


def _hy_inv_kernel(ct_ref, st_ref, a_ref, b_ref, x0_ref, zz_ref, bias_ref, o_ref, *, m):
    y = (jnp.dot(ct_ref[...], a_ref[0, 0], preferred_element_type=F32)
         + jnp.dot(st_ref[...], b_ref[0, 0], preferred_element_type=F32)) * (1.0 / m)
    zz = zz_ref[0].astype(F32)
    o_ref[0] = (x0_ref[0].astype(F32) * (y + zz * bias_ref[...])).astype(BF16)


def _hy_inverse(cttab, sttab, a, bq, x0, zz, bias, m):
    bsz, n, w = zz.shape
    nb = n // m
    tn = 512
    tab = pl.BlockSpec((m, m), lambda b, i, j: (0, 0))
    spec4 = pl.BlockSpec((1, 1, m, tn), lambda b, i, j: (b, i, 0, j))
    tile = pl.BlockSpec((1, m, tn), lambda b, i, j: (b, i, j))
    return pl.pallas_call(
        functools.partial(_hy_inv_kernel, m=m),
        grid=(bsz, nb, w // tn),
        in_specs=[tab, tab, spec4, spec4, tile, tile, pl.BlockSpec((1, tn), lambda b, i, j: (0, j))],
        out_specs=tile,
        out_shape=jax.ShapeDtypeStruct((bsz, n, w), BF16),
        compiler_params=_cparams(("parallel", "parallel", "parallel")),
        name="hy_inverse",
    )(cttab, sttab, a, bq, x0, zz, bias.reshape(1, w))


def _merge_kernel(odn_ref, z_ref, ng_ref, ohy_ref, olru_ref, gdn_ref, ghy_ref, glru_ref, wdn_ref, why_ref,
                  wlru_ref, wout_ref, x_ref, gate_ref, o_ref):
    heads = []
    for h in range(DN_HEADS):
        o = odn_ref[0, :, h * LANE:(h + 1) * LANE].astype(F32)
        y = o * lax.rsqrt(jnp.mean(o * o, axis=-1, keepdims=True) + NORM_EPS) * ng_ref[...]
        heads.append((y * _silu(z_ref[0, :, h * LANE:(h + 1) * LANE].astype(F32))).astype(BF16))
    odn = jnp.concatenate(heads, axis=1)
    m = _sigmoid(gdn_ref[0].astype(F32)) * jnp.dot(odn, wdn_ref[...], preferred_element_type=F32)
    m = m + _sigmoid(ghy_ref[0].astype(F32)) * jnp.dot(ohy_ref[0], why_ref[...], preferred_element_type=F32)
    m = m + _sigmoid(glru_ref[0].astype(F32)) * jnp.dot(olru_ref[0], wlru_ref[...], preferred_element_type=F32)
    y = jnp.dot(m.astype(BF16), wout_ref[...], preferred_element_type=F32)
    o_ref[0] = x_ref[0] + gate_ref[0] * y


def _merge(o_dn, dn_norm_g, o_hy, o_lru, proj, w_dn, w_hy, w_lru, w_out, x, gate):
    bsz, n, d = x.shape
    tm = min(n, 256)
    nblk = d // 1024
    act = pl.BlockSpec((1, tm, d), lambda b, i: (b, i, 0))
    gspec = lambda k: pl.BlockSpec((1, tm, d), lambda b, i: (b, i, GATE_BLK * LANE // d + k * nblk))
    wspec = pl.BlockSpec((d, d), lambda b, i: (0, 0))
    return pl.pallas_call(
        _merge_kernel,
        grid=(bsz, n // tm),
        in_specs=[act, pl.BlockSpec((1, tm, d), lambda b, i: (b, i, Z_BLK * LANE // d)),
                  pl.BlockSpec((1, LANE), lambda b, i: (0, 0)),
                  act, act, gspec(0), gspec(1), gspec(2), wspec, wspec, wspec, wspec, act,
                  pl.BlockSpec((1, 1, d), lambda b, i: (b, 0, 0))],
        out_specs=act,
        out_shape=jax.ShapeDtypeStruct((bsz, n, d), F32),
        compiler_params=_cparams(("parallel", "parallel")),
        name="merge",
    )(o_dn, proj, dn_norm_g.reshape(1, LANE), o_hy, o_lru, proj, proj, proj, w_dn, w_hy, w_lru, w_out, x, gate)


def _ffn_act_kernel(ug_ref, uv_ref, wg_ref, wv_ref, o_ref, up_ref, mid_ref, dn_ref, *, rows, cols, n):
    tile = FFN_TILE
    a = _row_iota((tile, tile))
    b = _col_iota((tile, tile))
    c = _mod_pow2(a, cols)
    lmat = jnp.where((b == a - 1) & (c >= 1), 1.0, 0.0).astype(BF16)
    rmat = jnp.where((b == a + 1) & (c <= cols - 2), 1.0, 0.0).astype(BF16)
    taps = (-1, 0, 1) if rows > 1 else (0,)
    dst = {-1: up_ref, 0: mid_ref, 1: dn_ref}
    if rows > 1:
        pad = jnp.zeros((2, cols, FFN_WIDE), BF16)
        up_ref[:, 0:cols, :] = pad
        dn_ref[:, n + cols:n + 2 * cols, :] = pad

    def taps_of(i, carry):
        r0 = pl.multiple_of(i * tile, tile)
        for idx, (u_ref, w_ref) in enumerate(((ug_ref, wg_ref), (uv_ref, wv_ref))):
            u = u_ref[0, pl.ds(r0, tile), :]
            left = jnp.dot(lmat, u, preferred_element_type=F32).astype(BF16)
            right = jnp.dot(rmat, u, preferred_element_type=F32).astype(BF16)
            for di in taps:
                k = 3 * (di + 1)
                w = [w_ref[k + j:k + j + 1, :].astype(BF16) for j in range(3)]
                off = pl.multiple_of(r0 + (cols if di != 0 else 0), cols)
                dst[di][idx, pl.ds(off, tile), :] = left * w[0] + u * w[1] + right * w[2]
        return carry

    def combine(i, carry):
        r0 = pl.multiple_of(i * tile, tile)
        vals = []
        for idx in range(2):
            acc = mid_ref[idx, pl.ds(r0, tile), :]
            if rows > 1:
                acc = (acc + up_ref[idx, pl.ds(r0, tile), :]
                       + dn_ref[idx, pl.ds(pl.multiple_of(r0 + 2 * cols, cols), tile), :])
            vals.append(acc.astype(F32))
        o_ref[0, pl.ds(r0, tile), :] = (_silu(vals[0]) * vals[1]).astype(BF16)
        return carry

    lax.fori_loop(0, n // tile, taps_of, 0, unroll=min(4, n // tile))
    lax.fori_loop(0, n // tile, combine, 0)


def _ffn_act(u, conv_w, rows, cols):
    bsz, n, _ = u.shape
    nblk = FFN_HIDDEN // FFN_WIDE
    padded = n + 2 * cols if rows > 1 else SUBLANE * 2
    return pl.pallas_call(
        functools.partial(_ffn_act_kernel, rows=rows, cols=cols, n=n),
        grid=(bsz, nblk),
        in_specs=[pl.BlockSpec((1, n, FFN_WIDE), lambda b, j: (b, 0, j)),
                  pl.BlockSpec((1, n, FFN_WIDE), lambda b, j: (b, 0, nblk + j)),
                  pl.BlockSpec((9, FFN_WIDE), lambda b, j: (0, j)),
                  pl.BlockSpec((9, FFN_WIDE), lambda b, j: (0, nblk + j))],
        out_specs=pl.BlockSpec((1, n, FFN_WIDE), lambda b, j: (b, 0, j)),
        out_shape=jax.ShapeDtypeStruct((bsz, n, FFN_HIDDEN), BF16),
        scratch_shapes=[pltpu.VMEM((2, padded, FFN_WIDE), BF16), pltpu.VMEM((2, n, FFN_WIDE), BF16),
                        pltpu.VMEM((2, padded, FFN_WIDE), BF16)],
        compiler_params=_cparams(("parallel", "parallel")),
        name="ffn_act",
    )(u, u, conv_w, conv_w)


def _gate_column_map():
    src = np.full((LANE,), -1, np.int32)
    isdec = np.zeros((1, LANE), np.float32)
    dirs = np.zeros((LANE,), np.int32)
    for h in range(DN_HEADS):
        for slot, (d, kind) in enumerate(((0, 0), (0, 1), (1, 0), (1, 1), (0, 0), (1, 0))):
            src[h * SUBLANE + slot] = d * 2 * DN_HEADS + kind * DN_HEADS + h
            isdec[0, h * SUBLANE + slot] = 1.0 if kind == 0 else 0.0
            dirs[h * SUBLANE + slot] = d
    return src, isdec, dirs


def _split_in_proj(w_in):
    o = np.cumsum((3 * DN_WIDTH, DN_WIDTH, 4 * DN_HEADS, 3 * HY_WIDTH, LRU_WIDTH, LRU_WIDTH)).tolist()
    w_main = jnp.concatenate([w_in[:, :o[1]], w_in[:, o[2]:]], axis=1).astype(BF16)
    src, _, _ = _gate_column_map()
    w_ab = w_in[:, o[1]:o[2]]
    w_gate = jnp.where(jnp.asarray(src >= 0)[None, :], w_ab[:, np.maximum(src, 0)], 0.0).astype(BF16)
    return w_main, w_gate


def _gate_params(a_log, dt_bias):
    _, isdec, dirs = _gate_column_map()
    head = (np.arange(LANE) // SUBLANE).astype(np.int32)
    alog_c = a_log[dirs, head].reshape(1, LANE)
    dtb_c = dt_bias[dirs, head].reshape(1, LANE)
    return alog_c, dtb_c, jnp.asarray(isdec)


def _token_views(bsz, n, mods, shared_mod):
    if not shared_mod:
        same = lambda t: t
        return same, same, mods
    flat = lambda t: t.reshape(1, bsz * n, t.shape[-1])
    unflat = lambda t: t.reshape(bsz, n, t.shape[-1])
    return flat, unflat, [m[:1] for m in mods]


def _token_mixer(x, mods, lp, states, tables, with_output, shared_mod):
    bsz, n, _ = x.shape
    flat, unflat, tmods = _token_views(bsz, n, mods, shared_mod)
    proj, ab = _norm_mod_matmul(flat(x), lp["norm1_g"], tmods[0], tmods[1], lp["w_main"], BF16, 2048, 1024,
                                lp["w_gate"])
    proj, ab = unflat(proj), unflat(ab)

    qkv_rm, qkv_tr = _dn_prep(proj, lp["dn_conv_w"])
    gates = _dn_gates(ab, *lp["gate_params"])
    o_dn, s_f, s_b = _delta_net(qkv_rm, qkv_tr, gates, states[0], states[1], with_output)
    o_lru, h_last = _rglru(proj, lp["lru_conv_w"], lp["lru_conv_b"], lp["lru_w_a"], lp["lru_b_a"],
                           lp["lru_w_x"], lp["lru_b_x"], lp["lru_lambda"], states[2], with_output)
    new_states = (s_f, s_b, h_last)
    if not with_output:
        return None, new_states

    m = min(n, HY_BLOCK)
    ctab, stab, cttab, sttab, cwtab, swtab = tables
    taps = _hy_filter(n, lp["hy_w1"], lp["hy_b1"], lp["hy_f1"], lp["hy_w2"], lp["hy_b2"], lp["hy_f2"], lp["hy_w3"])
    kc, ks = _hy_ktrans(cwtab, swtab, taps, m)
    x0, zz = _hy_prep(proj, lp["hy_conv_w"], lp["hy_conv_b"])
    a, bq = _hy_forward(ctab, stab, zz, kc, ks, m)
    o_hy = _hy_inverse(cttab, sttab, a, bq, x0, zz, lp["hy_bias"], m)

    x = _merge(flat(o_dn), lp["dn_norm_g"], flat(o_hy), flat(o_lru), flat(proj), lp["w_proj_dn"], lp["w_proj_hy"], lp["w_proj_lru"],
               lp["w_out"], flat(x), tmods[2])
    return unflat(x), new_states


def _conv_ffn(x, mods, lp, rows, cols, shared_mod, final_gain=None):
    bsz, n, _ = x.shape
    flat, unflat, tmods = _token_views(bsz, n, mods, shared_mod)
    u = _norm_mod_matmul(flat(x), lp["norm2_g"], tmods[3], tmods[4], lp["ffn_up"], BF16, 1024, FFN_HIDDEN)
    act = _ffn_act(unflat(u), lp["ffn_conv_w"], rows, cols)
    return unflat(_matmul_residual(flat(act), lp["ffn_down"], flat(x), tmods[5], final_gain))


def kernel(x, c, ctx, c_ctx, w_mod, b_mod, norm1_g, norm2_g, w_in, dn_conv_w, dn_a_log, dn_dt_bias, dn_norm_g,
           hy_conv_w, hy_conv_b, hy_w1, hy_b1, hy_f1, hy_w2, hy_b2, hy_f2, hy_w3, hy_bias,
           lru_conv_w, lru_conv_b, lru_w_a, lru_b_a, lru_w_x, lru_b_x, lru_lambda,
           w_proj_dn, w_proj_hy, w_proj_lru, w_out, ffn_up, ffn_conv_w, ffn_down, final_norm_g):
    bsz, n_lat, d = x.shape
    n_ctx = ctx.shape[1]
    depth = w_in.shape[0]
    rows = n_lat // GRID_W

    cvec = jnp.zeros((SUBLANE, d), F32).at[:bsz].set(c).at[bsz].set(c_ctx)
    lat_tables = _hy_tables(min(n_lat, HY_BLOCK))
    ctx_tables = _hy_tables(min(n_ctx, HY_BLOCK))
    zero_states = (jnp.zeros((bsz, DN_HEADS, LANE, LANE), F32), jnp.zeros((bsz, DN_HEADS, LANE, LANE), F32),
                   jnp.zeros((bsz, 2, LRU_WIDTH), F32))

    xc = ctx
    for l in range(depth):
        ctx_needed = l < depth - 1
        w_main, w_gate = _split_in_proj(w_in[l])
        lp = dict(
            norm1_g=norm1_g[l], norm2_g=norm2_g[l], w_main=w_main, w_gate=w_gate,
            dn_conv_w=dn_conv_w[l], gate_params=_gate_params(dn_a_log[l], dn_dt_bias[l]), dn_norm_g=dn_norm_g[l],
            hy_conv_w=hy_conv_w[l], hy_conv_b=hy_conv_b[l], hy_w1=hy_w1[l], hy_b1=hy_b1[l], hy_f1=hy_f1[l],
            hy_w2=hy_w2[l], hy_b2=hy_b2[l], hy_f2=hy_f2[l], hy_w3=hy_w3[l], hy_bias=hy_bias[l],
            lru_conv_w=lru_conv_w[l], lru_conv_b=lru_conv_b[l], lru_w_a=lru_w_a[l].astype(BF16),
            lru_b_a=lru_b_a[l], lru_w_x=lru_w_x[l].astype(BF16), lru_b_x=lru_b_x[l], lru_lambda=lru_lambda[l],
            w_proj_dn=w_proj_dn[l].astype(BF16), w_proj_hy=w_proj_hy[l].astype(BF16),
            w_proj_lru=w_proj_lru[l].astype(BF16), w_out=w_out[l].astype(BF16),
            ffn_up=ffn_up[l].astype(BF16), ffn_conv_w=ffn_conv_w[l].reshape(9, 2 * FFN_HIDDEN),
            ffn_down=ffn_down[l].astype(BF16))
        mod = _modulation(cvec, w_mod[l].astype(BF16), b_mod[l])
        lat_mod = [mod[:bsz, k * d:(k + 1) * d].reshape(bsz, 1, d) for k in range(N_MOD)]
        ctx_mod = [jnp.broadcast_to(mod[bsz:bsz + 1, k * d:(k + 1) * d].reshape(1, 1, d), (bsz, 1, d))
                   for k in range(N_MOD)]

        xc_new, ctx_states = _token_mixer(xc, ctx_mod, lp, zero_states, ctx_tables, ctx_needed, True)
        x, _ = _token_mixer(x, lat_mod, lp, ctx_states, lat_tables, True, False)
        x = _conv_ffn(x, lat_mod, lp, rows, GRID_W, False, None if ctx_needed else final_norm_g)
        if ctx_needed:
            xc = _conv_ffn(xc_new, ctx_mod, lp, 1, n_ctx, True)
    return x
```
